```python
import jax, jax.numpy as jnp
from jax import lax
import numpy as np

D_MODEL = 1024
BATCH = 16
SEQ = 2048
DEPTH = 4

N_MIXERS = 4
GW = D_MODEL // N_MIXERS
HEAD_DIM = 64
SG_HEADS = GW // HEAD_DIM
SG_CHUNK = 128
CONV_WIDTH = 31
NSA_HEADS = GW // HEAD_DIM
CMP_BLOCK = 32
CMP_STRIDE = 16
SLC_BLOCK = 64
SLC_TOPK = 8
WIN = 512
Q_BLOCK = 128
FORCE_BONUS = 1e4
NEG = -1e30
POOL_WINDOWS = (2, 4, 8, 16)
POOL_GROUP = GW // len(POOL_WINDOWS)
FFN_HIDDEN = -(-8 * D_MODEL // (3 * 256)) * 256
RMS_EPS = 1e-6
LN_EPS = 1e-5
IN_SIZES = (GW, GW, GW, GW, GW, 6 * HEAD_DIM, 3 * NSA_HEADS, GW)
IN_COLS = sum(IN_SIZES)

kernel_name = 'hybrid_parallel_sgmlp_conformer_nsa_pool'


def rmsnorm(x, g):
    xf = x.astype(jnp.float32)
    r = lax.rsqrt(jnp.mean(xf * xf, axis=-1, keepdims=True) + RMS_EPS)
    return (xf * r).astype(x.dtype) * g


def layernorm(x, g, b):
    xf = x.astype(jnp.float32)
    mu = jnp.mean(xf, axis=-1, keepdims=True)
    var = jnp.mean(jnp.square(xf - mu), axis=-1, keepdims=True)
    return ((xf - mu) * lax.rsqrt(var + LN_EPS)).astype(x.dtype) * g + b


def spatial_gating(u, v, ln_g, w_s, b_s):
    B, S, _ = u.shape
    nc = S // SG_CHUNK
    v = layernorm(v, ln_g, jnp.zeros((), v.dtype))
    v = v.reshape(B, nc, SG_CHUNK, SG_HEADS, HEAD_DIM)
    w = jnp.tril(w_s)
    sv = jnp.einsum('hts,bcshd->bcthd', w, v) + b_s.T[None, None, :, :, None]
    return u * sv.reshape(B, S, GW)


def conformer_conv(a, gate, w_dw, b_dw, ln_g, ln_b, w_pw, b_pw):
    h = a * jax.nn.sigmoid(gate)
    h = lax.conv_general_dilated(
        h, w_dw[:, None, :], window_strides=(1,), padding=[(CONV_WIDTH - 1, 0)],
        dimension_numbers=('NWC', 'WIO', 'NWC'), feature_group_count=GW) + b_dw
    h = jax.nn.silu(layernorm(h, ln_g, ln_b))
    return h @ w_pw + b_pw


def nsa_mixer(q, kv, gate_logits, pos_k, pos_v, w1k, w2k, w1v, w2v):
    B, S, _ = q.shape
    H, D = NSA_HEADS, HEAD_DIM
    dt = q.dtype
    q = q.reshape(B, S, H, D) * (D ** -0.5)
    k_c, v_c, k_s, v_s, k_w, v_w = jnp.split(kv, 6, axis=-1)
    pos = np.arange(S)

    n_cmp = (S - CMP_BLOCK) // CMP_STRIDE + 1
    blk_idx = np.arange(n_cmp)[:, None] * CMP_STRIDE + np.arange(CMP_BLOCK)[None, :]

    def compress(t, pe, w1, w2):
        tb = (t[:, blk_idx] + pe).reshape(B, n_cmp, CMP_BLOCK * D)
        return jax.nn.silu(tb @ w1) @ w2

    kc = compress(k_c, pos_k, w1k, w2k)
    vc = compress(v_c, pos_v, w1v, w2v)
    cmp_end = np.arange(n_cmp) * CMP_STRIDE + CMP_BLOCK - 1
    cmask = cmp_end[None, :] <= pos[:, None]
    has_cmp = cmask.any(axis=-1)[:, None]
    s = jnp.einsum('bshd,bcd->bhsc', q, kc).astype(jnp.float32)
    p_cmp = jnp.where(has_cmp, jax.nn.softmax(jnp.where(cmask, s, NEG), axis=-1), 0.0)
    o_cmp = jnp.einsum('bhsc,bcd->bshd', p_cmp.astype(dt), vc)

    n_slc = S // SLC_BLOCK
    c_start = np.arange(n_cmp) * CMP_STRIDE
    j_start = np.arange(n_slc) * SLC_BLOCK
    overlap = ((c_start[:, None] <= j_start[None, :] + SLC_BLOCK - 1)
               & (c_start[:, None] + CMP_BLOCK - 1 >= j_start[None, :])).astype(np.float32)
    imp = jnp.einsum('bhsc,cj->bsj', p_cmp, overlap)
    cur = pos // SLC_BLOCK
    jj = np.arange(n_slc)[None, :]
    valid_blk = j_start[None, :] <= pos[:, None]
    forced = (jj == 0) | (jj == cur[:, None]) | (jj == cur[:, None] - 1)
    score = jnp.where(valid_blk, imp + jnp.where(forced, FORCE_BONUS, 0.0), NEG)
    k_sel = min(SLC_TOPK, n_slc)
    top_val, top_idx = lax.top_k(score, k_sel)
    sel_ok = top_val > NEG / 2

    nb = S // Q_BLOCK
    qb = q.reshape(B, nb, Q_BLOCK, H, D).transpose(1, 0, 2, 3, 4)
    ib = top_idx.reshape(B, nb, Q_BLOCK, k_sel).transpose(1, 0, 2, 3)
    okb = sel_ok.reshape(B, nb, Q_BLOCK, k_sel).transpose(1, 0, 2, 3)
    qposb = jnp.arange(S).reshape(nb, Q_BLOCK)

    def sel_block(args):
        qc, ic, okc, qp = args
        tok = ic[..., None] * SLC_BLOCK + jnp.arange(SLC_BLOCK)
        kg = jax.vmap(lambda t, i: t[i])(k_s, tok)
        vg = jax.vmap(lambda t, i: t[i])(v_s, tok)
        m = okc[..., None] & (tok <= qp[None, :, None, None])
        sc = jnp.einsum('bqhd,bqkld->bhqkl', qc, kg).astype(jnp.float32)
        sc = jnp.where(m[:, None], sc, NEG).reshape(B, H, Q_BLOCK, k_sel * SLC_BLOCK)
        pr = jax.nn.softmax(sc, axis=-1).reshape(B, H, Q_BLOCK, k_sel, SLC_BLOCK)
        return jnp.einsum('bhqkl,bqkld->bqhd', pr.astype(dt), vg)

    o_slc = lax.map(sel_block, (qb, ib, okb, qposb))
    o_slc = o_slc.transpose(1, 0, 2, 3, 4).reshape(B, S, H, D)

    kpad = jnp.pad(k_w, ((0, 0), (WIN, 0), (0, 0)))
    vpad = jnp.pad(v_w, ((0, 0), (WIN, 0), (0, 0)))
    band = np.arange(nb)[:, None] * Q_BLOCK + np.arange(WIN + Q_BLOCK)[None, :]
    kb = kpad[:, band]
    vb = vpad[:, band]
    rel = np.arange(WIN + Q_BLOCK)[None, :] - WIN - np.arange(Q_BLOCK)[:, None]
    wmask = ((rel <= 0) & (rel > -WIN))[None] & ((band - WIN) >= 0)[:, None, :]
    qw = q.reshape(B, nb, Q_BLOCK, H, D)
    sw = jnp.einsum('bnqhd,bnkd->bhnqk', qw, kb).astype(jnp.float32)
    pw = jax.nn.softmax(jnp.where(wmask, sw, NEG), axis=-1)
    o_win = jnp.einsum('bhnqk,bnkd->bnqhd', pw.astype(dt), vb).reshape(B, S, H, D)

    g = jax.nn.sigmoid(gate_logits.reshape(B, S, H, 3))
    o = g[..., 0:1] * o_cmp + g[..., 1:2] * o_slc + g[..., 2:3] * o_win
    return o.reshape(B, S, GW)


def pool_mixer(xd, w_pool, scale):
    B, S, _ = xd.shape
    xf = xd.astype(jnp.float32)
    cs = jnp.pad(jnp.cumsum(xf, axis=1), ((0, 0), (1, 0), (0, 0)))
    t1 = np.arange(1, S + 1)
    outs = []
    for gi, w in enumerate(POOL_WINDOWS):
        sl = slice(gi * POOL_GROUP, (gi + 1) * POOL_GROUP)
        lo = np.maximum(t1 - w, 0)
        cnt = np.minimum(t1, w).astype(np.float32)
        mean = (cs[:, 1:, sl] - cs[:, lo, sl]) / cnt[None, :, None]
        outs.append((mean - xf[..., sl]).astype(xd.dtype) @ w_pool[gi])
    return jnp.concatenate(outs, axis=-1) * scale


def hybrid_layer(x, g_pre_mix, g_post_mix, g_pre_ffn, g_post_ffn, w_in,
                 sg_ln_g, sg_w, sg_b, cv_w, cv_b, cv_ln_g, cv_ln_b, cv_pw, cv_pw_b,
                 cmp_pos_k, cmp_pos_v, cmp_w1_k, cmp_w2_k, cmp_w1_v, cmp_w2_v,
                 pool_w, pool_scale, w_out, ffn_w_gu, ffn_w_down):
    h = rmsnorm(x, g_pre_mix)
    z = h @ w_in
    offs = [int(o) for o in np.cumsum(IN_SIZES)[:-1]]
    a_u, a_v, b_a, b_g, c_q, c_kv, c_gate, d_in = jnp.split(z, offs, axis=-1)
    y_a = spatial_gating(a_u, a_v, sg_ln_g, sg_w, sg_b)
    y_b = conformer_conv(b_a, b_g, cv_w, cv_b, cv_ln_g, cv_ln_b, cv_pw, cv_pw_b)
    y_c = nsa_mixer(c_q, c_kv, c_gate, cmp_pos_k, cmp_pos_v, cmp_w1_k, cmp_w2_k, cmp_w1_v, cmp_w2_v)
    y_d = pool_mixer(d_in, pool_w, pool_scale)
    mix = jnp.concatenate([y_a, y_b, y_c, y_d], axis=-1) @ w_out
    x = x + rmsnorm(mix, g_post_mix)
    h = rmsnorm(x, g_pre_ffn)
    gate, up = jnp.split(h @ ffn_w_gu, 2, axis=-1)
    f = (jax.nn.silu(gate) * up) @ ffn_w_down
    return x + rmsnorm(f, g_post_ffn)


def _fwd_setup_inputs(seed: int = 0) -> dict:
    key = jax.random.key(seed)
    ks = jax.random.split(key, 32)
    f32 = jnp.float32

    def nrm(k, shape, scale):
        return jax.random.normal(k, shape, f32) * scale

    def gain(k, shape):
        return 1.0 + nrm(k, shape, 0.05)

    L = DEPTH
    return {
        'x': nrm(ks[0], (BATCH, SEQ, D_MODEL), 1.0),
        'g_pre_mix': gain(ks[1], (L, D_MODEL)),
        'g_post_mix': gain(ks[2], (L, D_MODEL)),
        'g_pre_ffn': gain(ks[3], (L, D_MODEL)),
        'g_post_ffn': gain(ks[4], (L, D_MODEL)),
        'w_in': nrm(ks[5], (L, D_MODEL, IN_COLS), D_MODEL ** -0.5),
        'sg_ln_g': gain(ks[6], (L, GW)),
        'sg_w': nrm(ks[7], (L, SG_HEADS, SG_CHUNK, SG_CHUNK), SG_CHUNK ** -0.5),
        'sg_b': 1.0 + nrm(ks[8], (L, SG_HEADS, SG_CHUNK), 0.01),
        'cv_w': nrm(ks[9], (L, CONV_WIDTH, GW), CONV_WIDTH ** -0.5),
        'cv_b': nrm(ks[10], (L, GW), 0.01),
        'cv_ln_g': gain(ks[11], (L, GW)),
        'cv_ln_b': nrm(ks[12], (L, GW), 0.01),
        'cv_pw': nrm(ks[13], (L, GW, GW), GW ** -0.5),
        'cv_pw_b': nrm(ks[14], (L, GW), 0.01),
        'cmp_pos_k': nrm(ks[15], (L, CMP_BLOCK, HEAD_DIM), 0.1),
        'cmp_pos_v': nrm(ks[16], (L, CMP_BLOCK, HEAD_DIM), 0.1),
        'cmp_w1_k': nrm(ks[17], (L, CMP_BLOCK * HEAD_DIM, HEAD_DIM), (CMP_BLOCK * HEAD_DIM) ** -0.5),
        'cmp_w2_k': nrm(ks[18], (L, HEAD_DIM, HEAD_DIM), HEAD_DIM ** -0.5),
        'cmp_w1_v': nrm(ks[19], (L, CMP_BLOCK * HEAD_DIM, HEAD_DIM), (CMP_BLOCK * HEAD_DIM) ** -0.5),
        'cmp_w2_v': nrm(ks[20], (L, HEAD_DIM, HEAD_DIM), HEAD_DIM ** -0.5),
        'pool_w': nrm(ks[21], (L, len(POOL_WINDOWS), POOL_GROUP, POOL_GROUP), POOL_GROUP ** -0.5),
        'pool_scale': gain(ks[22], (L, GW)),
        'w_out': nrm(ks[23], (L, D_MODEL, D_MODEL), D_MODEL ** -0.5),
        'ffn_w_gu': nrm(ks[24], (L, D_MODEL, 2 * FFN_HIDDEN), D_MODEL ** -0.5),
        'ffn_w_down': nrm(ks[25], (L, FFN_HIDDEN, D_MODEL), FFN_HIDDEN ** -0.5),
    }


def _fwd_reference(x, g_pre_mix, g_post_mix, g_pre_ffn, g_post_ffn, w_in,
              sg_ln_g, sg_w, sg_b, cv_w, cv_b, cv_ln_g, cv_ln_b, cv_pw, cv_pw_b,
              cmp_pos_k, cmp_pos_v, cmp_w1_k, cmp_w2_k, cmp_w1_v, cmp_w2_v,
              pool_w, pool_scale, w_out, ffn_w_gu, ffn_w_down):
    for l in range(DEPTH):
        x = hybrid_layer(x, g_pre_mix[l], g_post_mix[l], g_pre_ffn[l], g_post_ffn[l], w_in[l],
                         sg_ln_g[l], sg_w[l], sg_b[l], cv_w[l], cv_b[l], cv_ln_g[l], cv_ln_b[l],
                         cv_pw[l], cv_pw_b[l], cmp_pos_k[l], cmp_pos_v[l], cmp_w1_k[l], cmp_w2_k[l],
                         cmp_w1_v[l], cmp_w2_v[l], pool_w[l], pool_scale[l], w_out[l],
                         ffn_w_gu[l], ffn_w_down[l])
    return x


import jax as _jax
import jax.numpy as _jnp

TWIN_FORMAT = 'train_step'
FWD_PARAMS = ['x', 'g_pre_mix', 'g_post_mix', 'g_pre_ffn', 'g_post_ffn', 'w_in', 'sg_ln_g', 'sg_w', 'sg_b', 'cv_w', 'cv_b', 'cv_ln_g', 'cv_ln_b', 'cv_pw', 'cv_pw_b', 'cmp_pos_k', 'cmp_pos_v', 'cmp_w1_k', 'cmp_w2_k', 'cmp_w1_v', 'cmp_w2_v', 'pool_w', 'pool_scale', 'w_out', 'ffn_w_gu', 'ffn_w_down']
TWIN_WEIGHTS = ['g_pre_mix', 'g_post_mix', 'g_pre_ffn', 'g_post_ffn', 'w_in', 'sg_ln_g', 'sg_w', 'sg_b', 'cv_w', 'cv_b', 'cv_ln_g', 'cv_ln_b', 'cv_pw', 'cv_pw_b', 'cmp_pos_k', 'cmp_pos_v', 'cmp_w1_k', 'cmp_w2_k', 'cmp_w1_v', 'cmp_w2_v', 'pool_w', 'pool_scale', 'w_out', 'ffn_w_gu', 'ffn_w_down']
TWIN_DIFF_INPUT = 'x'
TWIN_INPUTS = ['x', 'g_pre_mix', 'g_post_mix', 'g_pre_ffn', 'g_post_ffn', 'w_in', 'sg_ln_g', 'sg_w', 'sg_b', 'cv_w', 'cv_b', 'cv_ln_g', 'cv_ln_b', 'cv_pw', 'cv_pw_b', 'cmp_pos_k', 'cmp_pos_v', 'cmp_w1_k', 'cmp_w2_k', 'cmp_w1_v', 'cmp_w2_v', 'pool_w', 'pool_scale', 'w_out', 'ffn_w_gu', 'ffn_w_down', 'loss_target', 'm_g_pre_mix', 'm_g_post_mix', 'm_g_pre_ffn', 'm_g_post_ffn', 'm_w_in', 'm_sg_ln_g', 'm_sg_w', 'm_sg_b', 'm_cv_w', 'm_cv_b', 'm_cv_ln_g', 'm_cv_ln_b', 'm_cv_pw', 'm_cv_pw_b', 'm_cmp_pos_k', 'm_cmp_pos_v', 'm_cmp_w1_k', 'm_cmp_w2_k', 'm_cmp_w1_v', 'm_cmp_w2_v', 'm_pool_w', 'm_pool_scale', 'm_w_out', 'm_ffn_w_gu', 'm_ffn_w_down', 'v_g_pre_mix', 'v_g_post_mix', 'v_g_pre_ffn', 'v_g_post_ffn', 'v_w_in', 'v_sg_ln_g', 'v_sg_w', 'v_sg_b', 'v_cv_w', 'v_cv_b', 'v_cv_ln_g', 'v_cv_ln_b', 'v_cv_pw', 'v_cv_pw_b', 'v_cmp_pos_k', 'v_cmp_pos_v', 'v_cmp_w1_k', 'v_cmp_w2_k', 'v_cmp_w1_v', 'v_cmp_w2_v', 'v_pool_w', 'v_pool_scale', 'v_w_out', 'v_ffn_w_gu', 'v_ffn_w_down']
TWIN_OUTPUTS = ['loss', 'grad_x', 'grad_g_pre_mix', 'grad_g_post_mix', 'grad_g_pre_ffn', 'grad_g_post_ffn', 'grad_w_in', 'grad_sg_ln_g', 'grad_sg_w', 'grad_sg_b', 'grad_cv_w', 'grad_cv_b', 'grad_cv_ln_g', 'grad_cv_ln_b', 'grad_cv_pw', 'grad_cv_pw_b', 'grad_cmp_pos_k', 'grad_cmp_pos_v', 'grad_cmp_w1_k', 'grad_cmp_w2_k', 'grad_cmp_w1_v', 'grad_cmp_w2_v', 'grad_pool_w', 'grad_pool_scale', 'grad_w_out', 'grad_ffn_w_gu', 'grad_ffn_w_down', 'delta_g_pre_mix', 'delta_g_post_mix', 'delta_g_pre_ffn', 'delta_g_post_ffn', 'delta_w_in', 'delta_sg_ln_g', 'delta_sg_w', 'delta_sg_b', 'delta_cv_w', 'delta_cv_b', 'delta_cv_ln_g', 'delta_cv_ln_b', 'delta_cv_pw', 'delta_cv_pw_b', 'delta_cmp_pos_k', 'delta_cmp_pos_v', 'delta_cmp_w1_k', 'delta_cmp_w2_k', 'delta_cmp_w1_v', 'delta_cmp_w2_v', 'delta_pool_w', 'delta_pool_scale', 'delta_w_out', 'delta_ffn_w_gu', 'delta_ffn_w_down', 'new_m_g_pre_mix', 'new_m_g_post_mix', 'new_m_g_pre_ffn', 'new_m_g_post_ffn', 'new_m_w_in', 'new_m_sg_ln_g', 'new_m_sg_w', 'new_m_sg_b', 'new_m_cv_w', 'new_m_cv_b', 'new_m_cv_ln_g', 'new_m_cv_ln_b', 'new_m_cv_pw', 'new_m_cv_pw_b', 'new_m_cmp_pos_k', 'new_m_cmp_pos_v', 'new_m_cmp_w1_k', 'new_m_cmp_w2_k', 'new_m_cmp_w1_v', 'new_m_cmp_w2_v', 'new_m_pool_w', 'new_m_pool_scale', 'new_m_w_out', 'new_m_ffn_w_gu', 'new_m_ffn_w_down', 'new_v_g_pre_mix', 'new_v_g_post_mix', 'new_v_g_pre_ffn', 'new_v_g_post_ffn', 'new_v_w_in', 'new_v_sg_ln_g', 'new_v_sg_w', 'new_v_sg_b', 'new_v_cv_w', 'new_v_cv_b', 'new_v_cv_ln_g', 'new_v_cv_ln_b', 'new_v_cv_pw', 'new_v_cv_pw_b', 'new_v_cmp_pos_k', 'new_v_cmp_pos_v', 'new_v_cmp_w1_k', 'new_v_cmp_w2_k', 'new_v_cmp_w1_v', 'new_v_cmp_w2_v', 'new_v_pool_w', 'new_v_pool_scale', 'new_v_w_out', 'new_v_ffn_w_gu', 'new_v_ffn_w_down']
TWIN_LEAF_KINDS = {'loss': 'loss', 'grad_x': 'grad_x', 'grad_g_pre_mix': 'grad_w', 'grad_g_post_mix': 'grad_w', 'grad_g_pre_ffn': 'grad_w', 'grad_g_post_ffn': 'grad_w', 'grad_w_in': 'grad_w', 'grad_sg_ln_g': 'grad_w', 'grad_sg_w': 'grad_w', 'grad_sg_b': 'grad_w', 'grad_cv_w': 'grad_w', 'grad_cv_b': 'grad_w', 'grad_cv_ln_g': 'grad_w', 'grad_cv_ln_b': 'grad_w', 'grad_cv_pw': 'grad_w', 'grad_cv_pw_b': 'grad_w', 'grad_cmp_pos_k': 'grad_w', 'grad_cmp_pos_v': 'grad_w', 'grad_cmp_w1_k': 'grad_w', 'grad_cmp_w2_k': 'grad_w', 'grad_cmp_w1_v': 'grad_w', 'grad_cmp_w2_v': 'grad_w', 'grad_pool_w': 'grad_w', 'grad_pool_scale': 'grad_w', 'grad_w_out': 'grad_w', 'grad_ffn_w_gu': 'grad_w', 'grad_ffn_w_down': 'grad_w', 'delta_g_pre_mix': 'delta_w', 'delta_g_post_mix': 'delta_w', 'delta_g_pre_ffn': 'delta_w', 'delta_g_post_ffn': 'delta_w', 'delta_w_in': 'delta_w', 'delta_sg_ln_g': 'delta_w', 'delta_sg_w': 'delta_w', 'delta_sg_b': 'delta_w', 'delta_cv_w': 'delta_w', 'delta_cv_b': 'delta_w', 'delta_cv_ln_g': 'delta_w', 'delta_cv_ln_b': 'delta_w', 'delta_cv_pw': 'delta_w', 'delta_cv_pw_b': 'delta_w', 'delta_cmp_pos_k': 'delta_w', 'delta_cmp_pos_v': 'delta_w', 'delta_cmp_w1_k': 'delta_w', 'delta_cmp_w2_k': 'delta_w', 'delta_cmp_w1_v': 'delta_w', 'delta_cmp_w2_v': 'delta_w', 'delta_pool_w': 'delta_w', 'delta_pool_scale': 'delta_w', 'delta_w_out': 'delta_w', 'delta_ffn_w_gu': 'delta_w', 'delta_ffn_w_down': 'delta_w', 'new_m_g_pre_mix': 'new_m', 'new_m_g_post_mix': 'new_m', 'new_m_g_pre_ffn': 'new_m', 'new_m_g_post_ffn': 'new_m', 'new_m_w_in': 'new_m', 'new_m_sg_ln_g': 'new_m', 'new_m_sg_w': 'new_m', 'new_m_sg_b': 'new_m', 'new_m_cv_w': 'new_m', 'new_m_cv_b': 'new_m', 'new_m_cv_ln_g': 'new_m', 'new_m_cv_ln_b': 'new_m', 'new_m_cv_pw': 'new_m', 'new_m_cv_pw_b': 'new_m', 'new_m_cmp_pos_k': 'new_m', 'new_m_cmp_pos_v': 'new_m', 'new_m_cmp_w1_k': 'new_m', 'new_m_cmp_w2_k': 'new_m', 'new_m_cmp_w1_v': 'new_m', 'new_m_cmp_w2_v': 'new_m', 'new_m_pool_w': 'new_m', 'new_m_pool_scale': 'new_m', 'new_m_w_out': 'new_m', 'new_m_ffn_w_gu': 'new_m', 'new_m_ffn_w_down': 'new_m', 'new_v_g_pre_mix': 'new_v', 'new_v_g_post_mix': 'new_v', 'new_v_g_pre_ffn': 'new_v', 'new_v_g_post_ffn': 'new_v', 'new_v_w_in': 'new_v', 'new_v_sg_ln_g': 'new_v', 'new_v_sg_w': 'new_v', 'new_v_sg_b': 'new_v', 'new_v_cv_w': 'new_v', 'new_v_cv_b': 'new_v', 'new_v_cv_ln_g': 'new_v', 'new_v_cv_ln_b': 'new_v', 'new_v_cv_pw': 'new_v', 'new_v_cv_pw_b': 'new_v', 'new_v_cmp_pos_k': 'new_v', 'new_v_cmp_pos_v': 'new_v', 'new_v_cmp_w1_k': 'new_v', 'new_v_cmp_w2_k': 'new_v', 'new_v_cmp_w1_v': 'new_v', 'new_v_cmp_w2_v': 'new_v', 'new_v_pool_w': 'new_v', 'new_v_pool_scale': 'new_v', 'new_v_w_out': 'new_v', 'new_v_ffn_w_gu': 'new_v', 'new_v_ffn_w_down': 'new_v'}


def _forward(args):
    return _fwd_reference(*[args[k] for k in FWD_PARAMS])


def _output_shape():
    out = _jax.eval_shape(lambda: _forward(_fwd_setup_inputs(0)))
    return out.shape, out.dtype

N_MICROBATCH = 1
ADAM_LR = 0.001
ADAM_B1 = 0.9
ADAM_B2 = 0.999
ADAM_EPS = 1e-08
ADAM_WD = 0.01
ADAM_STEP = 10
PER_EXAMPLE_BATCH_AXIS = {'x': 0, 'loss_target': 0}
SHARED_INPUTS = []
_WEIGHT_DTYPES = {'g_pre_mix': _jnp.float32, 'g_post_mix': _jnp.float32, 'g_pre_ffn': _jnp.float32, 'g_post_ffn': _jnp.float32, 'w_in': _jnp.float32, 'sg_ln_g': _jnp.float32, 'sg_w': _jnp.float32, 'sg_b': _jnp.float32, 'cv_w': _jnp.float32, 'cv_b': _jnp.float32, 'cv_ln_g': _jnp.float32, 'cv_ln_b': _jnp.float32, 'cv_pw': _jnp.float32, 'cv_pw_b': _jnp.float32, 'cmp_pos_k': _jnp.float32, 'cmp_pos_v': _jnp.float32, 'cmp_w1_k': _jnp.float32, 'cmp_w2_k': _jnp.float32, 'cmp_w1_v': _jnp.float32, 'cmp_w2_v': _jnp.float32, 'pool_w': _jnp.float32, 'pool_scale': _jnp.float32, 'w_out': _jnp.float32, 'ffn_w_gu': _jnp.float32, 'ffn_w_down': _jnp.float32}
MOMENT_SCALE = {'g_pre_mix': 1.939281e+00, 'g_post_mix': 3.176959e+01, 'g_pre_ffn': 1.507974e+00, 'g_post_ffn': 3.167239e+01, 'w_in': 1.371195e+00, 'sg_ln_g': 1.016964e+00, 'sg_w': 6.682936e-01, 'sg_b': 9.454790e-01, 'cv_w': 1.053600e+00, 'cv_b': 9.488786e+00, 'cv_ln_g': 4.085462e+00, 'cv_ln_b': 6.089205e+00, 'cv_pw': 2.144404e+00, 'cv_pw_b': 1.264089e+01, 'cmp_pos_k': 1.797471e-02, 'cmp_pos_v': 1.179141e+00, 'cmp_w1_k': 1.709897e-01, 'cmp_w2_k': 1.585172e-01, 'cmp_w1_v': 5.517915e-01, 'cmp_w2_v': 2.455493e+00, 'pool_w': 2.254142e+00, 'pool_scale': 2.597100e+00, 'w_out': 2.275925e+00, 'ffn_w_gu': 6.410792e-01, 'ffn_w_down': 1.209854e+00}


def _to_microbatches(a, axis):
    t = _jnp.moveaxis(a, axis, 0)
    t = t.reshape((N_MICROBATCH, t.shape[0] // N_MICROBATCH) + t.shape[1:])
    return _jnp.moveaxis(t, 1, axis + 1)


def setup_inputs(seed: int = 0) -> dict:
    inp = _fwd_setup_inputs(seed)
    key = _jax.random.fold_in(_jax.random.key(seed), 7919)
    shape, _ = _output_shape()
    out = dict(inp)
    out["loss_target"] = _jax.random.normal(_jax.random.fold_in(key, 0), shape, _jnp.float32)
    for i, name in enumerate(TWIN_WEIGHTS):
        w = inp[name].astype(_jnp.float32)
        if MOMENT_SCALE is None:
            s = _jnp.sqrt(_jnp.mean(_jnp.square(w)) + 1e-30)
        else:
            s = MOMENT_SCALE[name]
        km, kv = _jax.random.split(_jax.random.fold_in(key, i + 1))
        out[name] = w
        out["m_" + name] = s * _jax.random.normal(km, w.shape, _jnp.float32)
        out["v_" + name] = (s * s) * _jax.random.uniform(kv, w.shape, _jnp.float32, 0.5, 1.5)
    if N_MICROBATCH > 1:
        for name, axis in PER_EXAMPLE_BATCH_AXIS.items():
            out[name] = _to_microbatches(out[name], axis)
    return {'x': out['x'], 'g_pre_mix': out['g_pre_mix'], 'g_post_mix': out['g_post_mix'], 'g_pre_ffn': out['g_pre_ffn'], 'g_post_ffn': out['g_post_ffn'], 'w_in': out['w_in'], 'sg_ln_g': out['sg_ln_g'], 'sg_w': out['sg_w'], 'sg_b': out['sg_b'], 'cv_w': out['cv_w'], 'cv_b': out['cv_b'], 'cv_ln_g': out['cv_ln_g'], 'cv_ln_b': out['cv_ln_b'], 'cv_pw': out['cv_pw'], 'cv_pw_b': out['cv_pw_b'], 'cmp_pos_k': out['cmp_pos_k'], 'cmp_pos_v': out['cmp_pos_v'], 'cmp_w1_k': out['cmp_w1_k'], 'cmp_w2_k': out['cmp_w2_k'], 'cmp_w1_v': out['cmp_w1_v'], 'cmp_w2_v': out['cmp_w2_v'], 'pool_w': out['pool_w'], 'pool_scale': out['pool_scale'], 'w_out': out['w_out'], 'ffn_w_gu': out['ffn_w_gu'], 'ffn_w_down': out['ffn_w_down'], 'loss_target': out['loss_target'], 'm_g_pre_mix': out['m_g_pre_mix'], 'm_g_post_mix': out['m_g_post_mix'], 'm_g_pre_ffn': out['m_g_pre_ffn'], 'm_g_post_ffn': out['m_g_post_ffn'], 'm_w_in': out['m_w_in'], 'm_sg_ln_g': out['m_sg_ln_g'], 'm_sg_w': out['m_sg_w'], 'm_sg_b': out['m_sg_b'], 'm_cv_w': out['m_cv_w'], 'm_cv_b': out['m_cv_b'], 'm_cv_ln_g': out['m_cv_ln_g'], 'm_cv_ln_b': out['m_cv_ln_b'], 'm_cv_pw': out['m_cv_pw'], 'm_cv_pw_b': out['m_cv_pw_b'], 'm_cmp_pos_k': out['m_cmp_pos_k'], 'm_cmp_pos_v': out['m_cmp_pos_v'], 'm_cmp_w1_k': out['m_cmp_w1_k'], 'm_cmp_w2_k': out['m_cmp_w2_k'], 'm_cmp_w1_v': out['m_cmp_w1_v'], 'm_cmp_w2_v': out['m_cmp_w2_v'], 'm_pool_w': out['m_pool_w'], 'm_pool_scale': out['m_pool_scale'], 'm_w_out': out['m_w_out'], 'm_ffn_w_gu': out['m_ffn_w_gu'], 'm_ffn_w_down': out['m_ffn_w_down'], 'v_g_pre_mix': out['v_g_pre_mix'], 'v_g_post_mix': out['v_g_post_mix'], 'v_g_pre_ffn': out['v_g_pre_ffn'], 'v_g_post_ffn': out['v_g_post_ffn'], 'v_w_in': out['v_w_in'], 'v_sg_ln_g': out['v_sg_ln_g'], 'v_sg_w': out['v_sg_w'], 'v_sg_b': out['v_sg_b'], 'v_cv_w': out['v_cv_w'], 'v_cv_b': out['v_cv_b'], 'v_cv_ln_g': out['v_cv_ln_g'], 'v_cv_ln_b': out['v_cv_ln_b'], 'v_cv_pw': out['v_cv_pw'], 'v_cv_pw_b': out['v_cv_pw_b'], 'v_cmp_pos_k': out['v_cmp_pos_k'], 'v_cmp_pos_v': out['v_cmp_pos_v'], 'v_cmp_w1_k': out['v_cmp_w1_k'], 'v_cmp_w2_k': out['v_cmp_w2_k'], 'v_cmp_w1_v': out['v_cmp_w1_v'], 'v_cmp_w2_v': out['v_cmp_w2_v'], 'v_pool_w': out['v_pool_w'], 'v_pool_scale': out['v_pool_scale'], 'v_w_out': out['v_w_out'], 'v_ffn_w_gu': out['v_ffn_w_gu'], 'v_ffn_w_down': out['v_ffn_w_down']}


def _loss(weights, diff, rest, loss_target):
    with _jax.named_scope("forward"):
        args = {**rest, TWIN_DIFF_INPUT: diff, **{k: w.astype(_WEIGHT_DTYPES[k]) for k, w in weights.items()}}
        y = _forward(args)
    with _jax.named_scope("loss_head"):
        err = _jnp.square(y.astype(_jnp.float32) - loss_target)
        return 0.5 * _jnp.sum(_jnp.mean(err, axis=-1)) if err.ndim else 0.5 * err


def _adamw(w, g, m, v):
    m = ADAM_B1 * m + (1.0 - ADAM_B1) * g
    v = ADAM_B2 * v + (1.0 - ADAM_B2) * _jnp.square(g)
    m_hat = m / (1.0 - ADAM_B1 ** ADAM_STEP)
    v_hat = v / (1.0 - ADAM_B2 ** ADAM_STEP)
    delta = -ADAM_LR * (m_hat / (_jnp.sqrt(v_hat) + ADAM_EPS) + ADAM_WD * w)
    return delta, m, v


def reference(x, g_pre_mix, g_post_mix, g_pre_ffn, g_post_ffn, w_in, sg_ln_g, sg_w, sg_b, cv_w, cv_b, cv_ln_g, cv_ln_b, cv_pw, cv_pw_b, cmp_pos_k, cmp_pos_v, cmp_w1_k, cmp_w2_k, cmp_w1_v, cmp_w2_v, pool_w, pool_scale, w_out, ffn_w_gu, ffn_w_down, loss_target, m_g_pre_mix, m_g_post_mix, m_g_pre_ffn, m_g_post_ffn, m_w_in, m_sg_ln_g, m_sg_w, m_sg_b, m_cv_w, m_cv_b, m_cv_ln_g, m_cv_ln_b, m_cv_pw, m_cv_pw_b, m_cmp_pos_k, m_cmp_pos_v, m_cmp_w1_k, m_cmp_w2_k, m_cmp_w1_v, m_cmp_w2_v, m_pool_w, m_pool_scale, m_w_out, m_ffn_w_gu, m_ffn_w_down, v_g_pre_mix, v_g_post_mix, v_g_pre_ffn, v_g_post_ffn, v_w_in, v_sg_ln_g, v_sg_w, v_sg_b, v_cv_w, v_cv_b, v_cv_ln_g, v_cv_ln_b, v_cv_pw, v_cv_pw_b, v_cmp_pos_k, v_cmp_pos_v, v_cmp_w1_k, v_cmp_w2_k, v_cmp_w1_v, v_cmp_w2_v, v_pool_w, v_pool_scale, v_w_out, v_ffn_w_gu, v_ffn_w_down):
    given = dict(x=x, g_pre_mix=g_pre_mix, g_post_mix=g_post_mix, g_pre_ffn=g_pre_ffn, g_post_ffn=g_post_ffn, w_in=w_in, sg_ln_g=sg_ln_g, sg_w=sg_w, sg_b=sg_b, cv_w=cv_w, cv_b=cv_b, cv_ln_g=cv_ln_g, cv_ln_b=cv_ln_b, cv_pw=cv_pw, cv_pw_b=cv_pw_b, cmp_pos_k=cmp_pos_k, cmp_pos_v=cmp_pos_v, cmp_w1_k=cmp_w1_k, cmp_w2_k=cmp_w2_k, cmp_w1_v=cmp_w1_v, cmp_w2_v=cmp_w2_v, pool_w=pool_w, pool_scale=pool_scale, w_out=w_out, ffn_w_gu=ffn_w_gu, ffn_w_down=ffn_w_down, loss_target=loss_target, m_g_pre_mix=m_g_pre_mix, m_g_post_mix=m_g_post_mix, m_g_pre_ffn=m_g_pre_ffn, m_g_post_ffn=m_g_post_ffn, m_w_in=m_w_in, m_sg_ln_g=m_sg_ln_g, m_sg_w=m_sg_w, m_sg_b=m_sg_b, m_cv_w=m_cv_w, m_cv_b=m_cv_b, m_cv_ln_g=m_cv_ln_g, m_cv_ln_b=m_cv_ln_b, m_cv_pw=m_cv_pw, m_cv_pw_b=m_cv_pw_b, m_cmp_pos_k=m_cmp_pos_k, m_cmp_pos_v=m_cmp_pos_v, m_cmp_w1_k=m_cmp_w1_k, m_cmp_w2_k=m_cmp_w2_k, m_cmp_w1_v=m_cmp_w1_v, m_cmp_w2_v=m_cmp_w2_v, m_pool_w=m_pool_w, m_pool_scale=m_pool_scale, m_w_out=m_w_out, m_ffn_w_gu=m_ffn_w_gu, m_ffn_w_down=m_ffn_w_down, v_g_pre_mix=v_g_pre_mix, v_g_post_mix=v_g_post_mix, v_g_pre_ffn=v_g_pre_ffn, v_g_post_ffn=v_g_post_ffn, v_w_in=v_w_in, v_sg_ln_g=v_sg_ln_g, v_sg_w=v_sg_w, v_sg_b=v_sg_b, v_cv_w=v_cv_w, v_cv_b=v_cv_b, v_cv_ln_g=v_cv_ln_g, v_cv_ln_b=v_cv_ln_b, v_cv_pw=v_cv_pw, v_cv_pw_b=v_cv_pw_b, v_cmp_pos_k=v_cmp_pos_k, v_cmp_pos_v=v_cmp_pos_v, v_cmp_w1_k=v_cmp_w1_k, v_cmp_w2_k=v_cmp_w2_k, v_cmp_w1_v=v_cmp_w1_v, v_cmp_w2_v=v_cmp_w2_v, v_pool_w=v_pool_w, v_pool_scale=v_pool_scale, v_w_out=v_w_out, v_ffn_w_gu=v_ffn_w_gu, v_ffn_w_down=v_ffn_w_down)
    weights = {n: given[n] for n in TWIN_WEIGHTS}
    shared = {n: given[n] for n in SHARED_INPUTS}
    per_example = {n: given[n] for n in ['x']}
    grad_fn = _jax.value_and_grad(_loss, argnums=(0, 1))

    def one_microbatch(ex, loss_target):
        ex = dict(ex)
        diff = ex.pop(TWIN_DIFF_INPUT)
        return grad_fn(weights, diff, {**shared, **ex}, loss_target)

    if N_MICROBATCH == 1:
        loss, (grad_w, grad_x) = one_microbatch(per_example, given["loss_target"])
    else:
        def body(carry, xs):
            loss_sum, grad_sum = carry
            l_k, (gw_k, gx_k) = one_microbatch(xs[0], xs[1])
            with _jax.named_scope("update"):
                return (loss_sum + l_k, _jax.tree.map(_jnp.add, grad_sum, gw_k)), gx_k

        init = (_jnp.zeros((), _jnp.float32), _jax.tree.map(_jnp.zeros_like, weights))
        (loss, grad_w), grad_x = _jax.lax.scan(body, init, (per_example, given["loss_target"]))
    with _jax.named_scope("update"):
        delta_w, new_m, new_v = {}, {}, {}
        for n in TWIN_WEIGHTS:
            delta_w[n], new_m[n], new_v[n] = _adamw(weights[n], grad_w[n], given["m_" + n], given["v_" + n])
    return (loss, grad_x, *[grad_w[n] for n in TWIN_WEIGHTS], *[delta_w[n] for n in TWIN_WEIGHTS],
            *[new_m[n] for n in TWIN_WEIGHTS], *[new_v[n] for n in TWIN_WEIGHTS])
```

```python
import numpy as np
import jax
import jax.numpy as jnp
from jax import lax
from jax.experimental import pallas as pl
from jax.experimental.pallas import tpu as pltpu

F32 = jnp.float32
BF16 = jnp.bfloat16
HI = lax.Precision.HIGHEST

D_MODEL = 1024
GW = 256
HEAD_DIM = 64
ZW = 2048
SG_CHUNK = 128
CONV_WIDTH = 31
CONV_PAD = 32
CMP_STRIDE = 16
N_CMP = 128
SLC_BLOCK_SHIFT = 6
N_SLC = 32
SLC_TOPK = 8
WIN = 512
NEG = -1e30
FORCE_BONUS = 1e4
RMS_EPS = 1e-6
LN_EPS = 1e-5
FFN_HIDDEN = 2816
N_DEV = 8
FFN_BLK = 2 * FFN_HIDDEN // N_DEV
TQ = 256
ROW_TILE = 512
CONV_TILE = 256
VMEM_LIMIT = 56 * 1024 * 1024
MESH = pl.DeviceIdType.MESH

ADAM_LR, ADAM_B1, ADAM_B2, ADAM_EPS, ADAM_WD, ADAM_STEP = 0.001, 0.9, 0.999, 1e-08, 0.01, 10

COL_U, COL_V, COL_A, COL_G, COL_Q, COL_D, COL_KV, COL_GL = 0, 256, 512, 768, 1024, 1280, 1536, 1920


def _sds(shape, dtype):
    return jax.ShapeDtypeStruct(shape, dtype)


def _cp(sem=None):
    return pltpu.CompilerParams(dimension_semantics=sem, vmem_limit_bytes=VMEM_LIMIT)


def _tile(n, target, q=128):
    best = None
    for t in range(q, min(n, target) + 1, q):
        if n % t == 0:
            best = t
    return best or n


def _full(shape):
    nd = len(shape)
    return pl.BlockSpec(shape, lambda *_: (0,) * nd)


def _sigmoid(x):
    return jax.nn.sigmoid(x)


def _dot(a, b):
    return jnp.dot(a, b, preferred_element_type=F32)


def _dot_nt(a, b):
    return lax.dot_general(a, b, (((1,), (1,)), ((), ())), preferred_element_type=F32)


def _dot_tn(a, b):
    return lax.dot_general(a, b, (((0,), (0,)), ((), ())), preferred_element_type=F32)


def _lane_head(width=GW):
    return lax.shift_right_logical(lax.broadcasted_iota(jnp.int32, (1, width), 1), 6)


def _fold_heads(x):
    return x + pltpu.roll(x, 64, 1) + pltpu.roll(x, 128, 1) + pltpu.roll(x, 192, 1)


def pack_cols(w):
    pad = jnp.zeros(w.shape[:-1] + (ZW - 1932,), w.dtype)
    return jnp.concatenate([w[..., :1280], w[..., 1676:1932], w[..., 1280:1664], w[..., 1664:1676], pad], axis=-1)


def unpack_cols(wp):
    return jnp.concatenate([wp[..., :1280], wp[..., 1536:1920], wp[..., 1920:1932], wp[..., 1280:1536]], axis=-1)


def mm(a, b, *, ta=False, tb=False, blk=None, out_dtype=F32, name, tm=512, tn=1024, tk=512):
    a_dims = ("k", "m") if ta else ("m", "k")
    b_dims = ("n", "k") if tb else ("k", "n")
    a3, b3, o3 = blk in a_dims and blk is not None, blk in b_dims and blk is not None, blk in ("m", "n")
    size = {}
    size[a_dims[0]], size[a_dims[1]] = a.shape[-2:]
    size[b_dims[0]], size[b_dims[1]] = b.shape[-2:]
    nb = a.shape[0] if a3 else (b.shape[0] if b3 else 1)
    tile = {"m": _tile(size["m"], tm), "n": _tile(size["n"], tn), "k": _tile(size["k"], tk)}
    grid = {d: size[d] // tile[d] for d in "mnk"}
    if blk is not None:
        tile[blk] = size[blk]
        grid[blk] = nb
    nk = grid["k"]

    def spec(dims, is3):
        def im(i, j, k):
            g = {"m": i, "n": j, "k": k}
            idx = tuple(0 if d == blk else g[d] for d in dims)
            return ((g[blk],) + idx) if is3 else idx
        shape = (tile[dims[0]], tile[dims[1]])
        return pl.BlockSpec(((None,) + shape) if is3 else shape, im)

    dn = (((0 if ta else 1,), (1 if tb else 0,)), ((), ()))

    def body(a_ref, b_ref, o_ref, acc):
        k = pl.program_id(2)

        @pl.when(k == 0)
        def _():
            acc[...] = jnp.zeros_like(acc)

        acc[...] += lax.dot_general(a_ref[...].astype(BF16), b_ref[...].astype(BF16), dn, preferred_element_type=F32)

        @pl.when(k == nk - 1)
        def _():
            o_ref[...] = acc[...].astype(o_ref.dtype)

    oshape = ((nb,) if o3 else ()) + (size["m"], size["n"])
    return pl.pallas_call(
        body, grid=(grid["m"], grid["n"], nk),
        in_specs=[spec(a_dims, a3), spec(b_dims, b3)], out_specs=spec(("m", "n"), o3),
        out_shape=_sds(oshape, out_dtype), scratch_shapes=[pltpu.VMEM((tile["m"], tile["n"]), F32)],
        compiler_params=_cp(("parallel", "parallel", "arbitrary")), name=name)(a, b)


def _rows(tm, width):
    return pl.BlockSpec((tm, width), lambda i: (i, 0))


def rms_fwd(x, g, name):
    T = x.shape[0]

    def body(x_ref, g_ref, h_ref):
        x = x_ref[...]
        r = lax.rsqrt(jnp.mean(x * x, axis=-1, keepdims=True) + RMS_EPS)
        h_ref[...] = ((x * r) * g_ref[...]).astype(h_ref.dtype)

    return pl.pallas_call(body, grid=(T // ROW_TILE,), in_specs=[_rows(ROW_TILE, D_MODEL), _full((1, D_MODEL))],
                          out_specs=_rows(ROW_TILE, D_MODEL), out_shape=_sds((T, D_MODEL), BF16),
                          compiler_params=_cp(("parallel",)), name=name)(x, g)


def rms_post_fwd(xres, m, g, name):
    T = m.shape[0]

    def body(x_ref, m_ref, g_ref, o_ref):
        m = m_ref[...]
        r = lax.rsqrt(jnp.mean(m * m, axis=-1, keepdims=True) + RMS_EPS)
        o_ref[...] = x_ref[...] + (m * r) * g_ref[...]

    return pl.pallas_call(body, grid=(T // ROW_TILE,),
                          in_specs=[_rows(ROW_TILE, D_MODEL), _rows(ROW_TILE, D_MODEL), _full((1, D_MODEL))],
                          out_specs=_rows(ROW_TILE, D_MODEL), out_shape=_sds((T, D_MODEL), F32),
                          compiler_params=_cp(("parallel",)), name=name)(xres, m, g)


def rms_bwd(m, g, dy, dres, out_dtype, name):
    T = m.shape[0]
    has_res = dres is not None

    def body(*refs):
        if has_res:
            m_ref, g_ref, dy_ref, dres_ref, dm_ref, dg_ref = refs
        else:
            m_ref, g_ref, dy_ref, dm_ref, dg_ref = refs
        m = m_ref[...]
        dy = dy_ref[...].astype(F32)
        r = lax.rsqrt(jnp.mean(m * m, axis=-1, keepdims=True) + RMS_EPS)
        n = m * r
        dn = dy * g_ref[...]
        dm = r * (dn - n * jnp.mean(dn * n, axis=-1, keepdims=True))
        if has_res:
            dm = dm + dres_ref[...]
        dm_ref[...] = dm.astype(dm_ref.dtype)

        @pl.when(pl.program_id(0) == 0)
        def _():
            dg_ref[...] = jnp.zeros_like(dg_ref)

        dg_ref[...] += jnp.sum(dy * n, axis=0, keepdims=True)

    ins = [m, g, dy] + ([dres] if has_res else [])
    specs = [_rows(ROW_TILE, D_MODEL), _full((1, D_MODEL)), _rows(ROW_TILE, D_MODEL)] + ([_rows(ROW_TILE, D_MODEL)] if has_res else [])
    return pl.pallas_call(body, grid=(T // ROW_TILE,), in_specs=specs,
                          out_specs=[_rows(ROW_TILE, D_MODEL), _full((1, D_MODEL))],
                          out_shape=[_sds((T, D_MODEL), out_dtype), _sds((1, D_MODEL), F32)],
                          compiler_params=_cp(("arbitrary",)), name=name)(*ins)


def loss_fwd_bwd(y, tgt):
    T = y.shape[0]

    def body(y_ref, t_ref, dy_ref, l_ref):
        e = y_ref[...] - t_ref[...]
        dy_ref[...] = e * (1.0 / D_MODEL)

        @pl.when(pl.program_id(0) == 0)
        def _():
            l_ref[...] = jnp.zeros_like(l_ref)

        l_ref[...] += jnp.full(l_ref.shape, 0.5 * jnp.sum(jnp.mean(e * e, axis=-1, keepdims=True)), F32)

    return pl.pallas_call(body, grid=(T // ROW_TILE,), in_specs=[_rows(ROW_TILE, D_MODEL)] * 2,
                          out_specs=[_rows(ROW_TILE, D_MODEL), _full((8, 128))],
                          out_shape=[_sds((T, D_MODEL), F32), _sds((8, 128), F32)],
                          compiler_params=_cp(("arbitrary",)), name="loss")(y, tgt)


def _gu_spec():
    return pl.BlockSpec((2, None, ROW_TILE, FFN_BLK), lambda j, i: (0, j, i, 0))


def _a_spec():
    return pl.BlockSpec((None, ROW_TILE, FFN_BLK), lambda j, i: (j, i, 0))


def swiglu_fwd(gu4):
    T = gu4.shape[2]

    def body(gu_ref, a_ref):
        gate = gu_ref[0]
        a_ref[...] = (gate * _sigmoid(gate) * gu_ref[1]).astype(a_ref.dtype)

    return pl.pallas_call(body, grid=(4, T // ROW_TILE), in_specs=[_gu_spec()], out_specs=_a_spec(),
                          out_shape=_sds((4, T, FFN_BLK), BF16), compiler_params=_cp(("parallel", "parallel")),
                          name="swiglu_fwd")(gu4)


def swiglu_bwd(gu4, da3):
    T = gu4.shape[2]

    def body(gu_ref, da_ref, d_ref):
        gate, up, da = gu_ref[0], gu_ref[1], da_ref[...]
        sg = _sigmoid(gate)
        d_ref[0] = (da * up * (sg * (1.0 + gate * (1.0 - sg)))).astype(d_ref.dtype)
        d_ref[1] = (da * (gate * sg)).astype(d_ref.dtype)

    return pl.pallas_call(body, grid=(4, T // ROW_TILE), in_specs=[_gu_spec(), _a_spec()], out_specs=_gu_spec(),
                          out_shape=_sds((2, 4, T, FFN_BLK), BF16), compiler_params=_cp(("parallel", "parallel")),
                          name="swiglu_bwd")(gu4, da3)


def _zcol(tm, col):
    return pl.BlockSpec((tm, GW), lambda i: (i, col // GW))


def _sg_common(v, g):
    mu = jnp.mean(v, axis=-1, keepdims=True)
    xc = v - mu
    rstd = lax.rsqrt(jnp.mean(xc * xc, axis=-1, keepdims=True) + LN_EPS)
    vhat = xc * rstd
    return vhat, rstd, vhat * g


def _tril_weights(w_ref):
    tri = lax.broadcasted_iota(jnp.int32, (SG_CHUNK, SG_CHUNK), 0) >= lax.broadcasted_iota(jnp.int32, (SG_CHUNK, SG_CHUNK), 1)
    return tri, [jnp.where(tri, w_ref[h], 0.0).astype(BF16) for h in range(4)]


def mixa_fwd(z, ln_g, w, bexp):
    T = z.shape[0]
    nch = ROW_TILE // SG_CHUNK

    def body(u_ref, v_ref, g_ref, w_ref, be_ref, y_ref):
        _, _, vln = _sg_common(v_ref[...], g_ref[...])
        vb = vln.astype(BF16)
        head = _lane_head()
        _, wh = _tril_weights(w_ref)
        for c in range(nch):
            rows = slice(c * SG_CHUNK, (c + 1) * SG_CHUNK)
            sv = be_ref[...]
            for h in range(4):
                sv = sv + jnp.where(head == h, _dot(wh[h], vb[rows]), 0.0)
            y_ref[rows, :] = (u_ref[rows, :] * sv).astype(y_ref.dtype)

    return pl.pallas_call(body, grid=(T // ROW_TILE,),
                          in_specs=[_zcol(ROW_TILE, COL_U), _zcol(ROW_TILE, COL_V), _full((1, GW)), _full((4, SG_CHUNK, SG_CHUNK)),
                                    _full((SG_CHUNK, GW))],
                          out_specs=_rows(ROW_TILE, GW), out_shape=_sds((T, GW), BF16),
                          compiler_params=_cp(("parallel",)), name="mixa_fwd")(z, z, ln_g, w, bexp)


def mixa_bwd(z, dycat, ln_g, w, bexp):
    T = z.shape[0]
    nch = ROW_TILE // SG_CHUNK
    nsteps = T // ROW_TILE

    def body(u_ref, v_ref, dy_ref, g_ref, w_ref, be_ref, du_ref, dv_ref, dw_ref, db_ref, dg_ref, dbe_acc):
        step = pl.program_id(0)

        @pl.when(step == 0)
        def _():
            dw_ref[...] = jnp.zeros_like(dw_ref)
            dg_ref[...] = jnp.zeros_like(dg_ref)
            dbe_acc[...] = jnp.zeros_like(dbe_acc)

        g = g_ref[...]
        vhat, rstd, vln = _sg_common(v_ref[...], g)
        vb = vln.astype(BF16)
        head = _lane_head()
        tri, wh = _tril_weights(w_ref)
        dgsum = jnp.zeros((1, GW), F32)
        for c in range(nch):
            rows = slice(c * SG_CHUNK, (c + 1) * SG_CHUNK)
            sv = be_ref[...]
            for h in range(4):
                sv = sv + jnp.where(head == h, _dot(wh[h], vb[rows]), 0.0)
            dy = dy_ref[rows, :]
            du_ref[rows, :] = (dy * sv).astype(du_ref.dtype)
            dsv = dy * u_ref[rows, :]
            dbe_acc[...] += dsv
            dvln = jnp.zeros((SG_CHUNK, GW), F32)
            for h in range(4):
                dsvm = jnp.where(head == h, dsv, 0.0).astype(BF16)
                dw_ref[h] += _dot_nt(dsvm, vb[rows])
                dvln = dvln + _dot_tn(wh[h], dsvm)
            vh = vhat[rows]
            dgsum = dgsum + jnp.sum(dvln * vh, axis=0, keepdims=True)
            dvhat = dvln * g
            dv = rstd[rows] * (dvhat - jnp.mean(dvhat, axis=-1, keepdims=True) - vh * jnp.mean(dvhat * vh, axis=-1, keepdims=True))
            dv_ref[rows, :] = dv.astype(dv_ref.dtype)
        dg_ref[...] += dgsum

        @pl.when(step == nsteps - 1)
        def _():
            for h in range(4):
                dw_ref[h] = jnp.where(tri, dw_ref[h], 0.0)
            fold = (lax.shift_right_logical(lax.broadcasted_iota(jnp.int32, (GW, 128), 0), 6)
                    == lax.broadcasted_iota(jnp.int32, (GW, 128), 1)).astype(F32)
            db_ref[...] = jnp.dot(dbe_acc[...], fold, precision=HI, preferred_element_type=F32)

    return pl.pallas_call(
        body, grid=(nsteps,),
        in_specs=[_zcol(ROW_TILE, COL_U), _zcol(ROW_TILE, COL_V), pl.BlockSpec((ROW_TILE, GW), lambda i: (i, 0)),
                  _full((1, GW)), _full((4, SG_CHUNK, SG_CHUNK)), _full((SG_CHUNK, GW))],
        out_specs=[_rows(ROW_TILE, GW), _rows(ROW_TILE, GW), _full((4, SG_CHUNK, SG_CHUNK)), _full((SG_CHUNK, 128)), _full((1, GW))],
        out_shape=[_sds((T, GW), BF16), _sds((T, GW), BF16), _sds((4, SG_CHUNK, SG_CHUNK), F32), _sds((SG_CHUNK, 128), F32),
                   _sds((1, GW), F32)],
        scratch_shapes=[pltpu.VMEM((SG_CHUNK, GW), F32)],
        compiler_params=_cp(("arbitrary",)), name="mixa_bwd")(z, z, dycat, ln_g, w, bexp)


def _seq(S, col):
    return pl.BlockSpec((None, S, GW), lambda b: (b, 0, col // GW))


def _conv_ln(pad, r0, cw_ref, cb, lg, lb):
    acc = jnp.zeros((CONV_TILE, GW), F32) + cb
    for k in range(CONV_WIDTH):
        acc = acc + cw_ref[k:k + 1, :] * pad[pl.ds(r0 + CONV_PAD - (CONV_WIDTH - 1) + k, CONV_TILE), :]
    mu = jnp.mean(acc, axis=-1, keepdims=True)
    xc = acc - mu
    rstd = lax.rsqrt(jnp.mean(xc * xc, axis=-1, keepdims=True) + LN_EPS)
    hhat = xc * rstd
    return hhat, rstd, hhat * lg + lb


def mixb_fwd(z3, cw, cb, lg, lb, pw, pwb):
    B, S, _ = z3.shape

    def body(a_ref, gt_ref, cw_ref, cb_ref, lg_ref, lb_ref, pw_ref, pwb_ref, y_ref, pad):
        pad[0:CONV_PAD, :] = jnp.zeros((CONV_PAD, GW), F32)
        pad[CONV_PAD:CONV_PAD + S, :] = a_ref[...] * _sigmoid(gt_ref[...])
        pwv = pw_ref[...].astype(BF16)
        for r0 in range(0, S, CONV_TILE):
            _, _, ln = _conv_ln(pad, r0, cw_ref, cb_ref[...], lg_ref[...], lb_ref[...])
            s = ln * _sigmoid(ln)
            y_ref[r0:r0 + CONV_TILE, :] = (_dot(s.astype(BF16), pwv) + pwb_ref[...]).astype(y_ref.dtype)

    return pl.pallas_call(
        body, grid=(B,),
        in_specs=[_seq(S, COL_A), _seq(S, COL_G), _full((CONV_WIDTH, GW)), _full((1, GW)), _full((1, GW)), _full((1, GW)),
                  _full((GW, GW)), _full((1, GW))],
        out_specs=pl.BlockSpec((None, S, GW), lambda b: (b, 0, 0)), out_shape=_sds((B, S, GW), BF16),
        scratch_shapes=[pltpu.VMEM((S + CONV_PAD, GW), F32)],
        compiler_params=_cp(("parallel",)), name="mixb_fwd")(z3, z3, cw, cb, lg, lb, pw, pwb)


def mixb_bwd(z3, dycat3, cw, cb, lg, lb, pw, pwb):
    B, S, _ = z3.shape

    def body(a_ref, gt_ref, dy_ref, cw_ref, cb_ref, lg_ref, lb_ref, pw_ref, pwb_ref,
             da_ref, dgt_ref, dcw_ref, dcb_ref, dlg_ref, dlb_ref, dpw_ref, dpwb_ref, pad, dpad, dcw_acc):
        @pl.when(pl.program_id(0) == 0)
        def _():
            for r in (dcb_ref, dlg_ref, dlb_ref, dpw_ref, dpwb_ref, dcw_acc):
                r[...] = jnp.zeros_like(r)

        pad[0:CONV_PAD, :] = jnp.zeros((CONV_PAD, GW), F32)
        pad[CONV_PAD:CONV_PAD + S, :] = a_ref[...] * _sigmoid(gt_ref[...])
        dpad[S:S + CONV_PAD, :] = jnp.zeros((CONV_PAD, GW), F32)
        pwv = pw_ref[...].astype(BF16)
        lg = lg_ref[...]
        for r0 in range(0, S, CONV_TILE):
            hhat, rstd, ln = _conv_ln(pad, r0, cw_ref, cb_ref[...], lg, lb_ref[...])
            sg = _sigmoid(ln)
            s = ln * sg
            dy = dy_ref[r0:r0 + CONV_TILE, :]
            dyb = dy.astype(BF16)
            dpw_ref[...] += _dot_tn(s.astype(BF16), dyb)
            dpwb_ref[...] += jnp.sum(dy, axis=0, keepdims=True)
            dln = _dot_nt(dyb, pwv) * (sg * (1.0 + ln * (1.0 - sg)))
            dlg_ref[...] += jnp.sum(dln * hhat, axis=0, keepdims=True)
            dlb_ref[...] += jnp.sum(dln, axis=0, keepdims=True)
            dhh = dln * lg
            dhc = rstd * (dhh - jnp.mean(dhh, axis=-1, keepdims=True) - hhat * jnp.mean(dhh * hhat, axis=-1, keepdims=True))
            dpad[r0:r0 + CONV_TILE, :] = dhc
            dcb_ref[...] += jnp.sum(dhc, axis=0, keepdims=True)
            for k in range(CONV_WIDTH):
                prod = dhc * pad[pl.ds(r0 + CONV_PAD - (CONV_WIDTH - 1) + k, CONV_TILE), :]
                dcw_acc[k] += prod.reshape(CONV_TILE // 8, 8, GW).sum(axis=0)
        for r0 in range(0, S, CONV_TILE):
            dhg = jnp.zeros((CONV_TILE, GW), F32)
            for k in range(CONV_WIDTH):
                dhg = dhg + cw_ref[k:k + 1, :] * dpad[pl.ds(r0 + (CONV_WIDTH - 1) - k, CONV_TILE), :]
            a = a_ref[r0:r0 + CONV_TILE, :]
            sg = _sigmoid(gt_ref[r0:r0 + CONV_TILE, :])
            da_ref[r0:r0 + CONV_TILE, :] = (dhg * sg).astype(da_ref.dtype)
            dgt_ref[r0:r0 + CONV_TILE, :] = (dhg * a * sg * (1.0 - sg)).astype(dgt_ref.dtype)

        @pl.when(pl.program_id(0) == B - 1)
        def _():
            for k in range(CONV_WIDTH):
                dcw_ref[k:k + 1, :] = jnp.sum(dcw_acc[k], axis=0, keepdims=True)

    seq_out = pl.BlockSpec((None, S, GW), lambda b: (b, 0, 0))
    return pl.pallas_call(
        body, grid=(B,),
        in_specs=[_seq(S, COL_A), _seq(S, COL_G), pl.BlockSpec((None, S, GW), lambda b: (b, 0, 1)),
                  _full((CONV_WIDTH, GW)), _full((1, GW)), _full((1, GW)), _full((1, GW)), _full((GW, GW)), _full((1, GW))],
        out_specs=[seq_out, seq_out, _full((CONV_WIDTH, GW)), _full((1, GW)), _full((1, GW)), _full((1, GW)), _full((GW, GW)),
                   _full((1, GW))],
        out_shape=[_sds((B, S, GW), BF16), _sds((B, S, GW), BF16), _sds((CONV_WIDTH, GW), F32), _sds((1, GW), F32),
                   _sds((1, GW), F32), _sds((1, GW), F32), _sds((GW, GW), F32), _sds((1, GW), F32)],
        scratch_shapes=[pltpu.VMEM((S + CONV_PAD, GW), F32), pltpu.VMEM((S + CONV_PAD, GW), F32),
                        pltpu.VMEM((CONV_WIDTH, 8, GW), F32)],
        compiler_params=_cp(("arbitrary",)), name="mixb_bwd")(z3, z3, dycat3, cw, cb, lg, lb, pw, pwb)


POOL_PAD = 16


def _pool_window():
    lane = lax.broadcasted_iota(jnp.int32, (1, GW), 1)
    return jnp.where(lane < 64, 2, jnp.where(lane < 128, 4, jnp.where(lane < 192, 8, 16)))


def _pool_sums(pad, r0, base, sign):
    win = _pool_window()
    acc = pad[pl.ds(r0 + base, CONV_TILE), :]
    out = None
    for i in range(1, 16):
        acc = acc + pad[pl.ds(r0 + base + sign * i, CONV_TILE), :]
        if i + 1 in (2, 4, 8, 16):
            out = acc if out is None else jnp.where(win == i + 1, acc, out)
    return out


def _pool_cnt(r0):
    t1 = r0 + 1 + lax.broadcasted_iota(jnp.int32, (CONV_TILE, 1), 0)
    return jnp.minimum(t1, _pool_window()).astype(F32)


def mixd_fwd(z3, wbd, scale):
    B, S, _ = z3.shape

    def body(x_ref, w_ref, sc_ref, y_ref, pad):
        pad[0:POOL_PAD, :] = jnp.zeros((POOL_PAD, GW), F32)
        pad[POOL_PAD:POOL_PAD + S, :] = x_ref[...]
        wv = w_ref[...].astype(BF16)
        for r0 in range(0, S, CONV_TILE):
            mean = _pool_sums(pad, r0, POOL_PAD, -1) / _pool_cnt(r0)
            p = (mean - x_ref[r0:r0 + CONV_TILE, :]).astype(BF16)
            y_ref[r0:r0 + CONV_TILE, :] = (_dot(p, wv) * sc_ref[...]).astype(y_ref.dtype)

    return pl.pallas_call(
        body, grid=(B,), in_specs=[_seq(S, COL_D), _full((GW, GW)), _full((1, GW))],
        out_specs=pl.BlockSpec((None, S, GW), lambda b: (b, 0, 0)), out_shape=_sds((B, S, GW), BF16),
        scratch_shapes=[pltpu.VMEM((S + POOL_PAD, GW), F32)],
        compiler_params=_cp(("parallel",)), name="mixd_fwd")(z3, wbd, scale)


def mixd_bwd(z3, dycat3, wbd, scale):
    B, S, _ = z3.shape

    def body(x_ref, dy_ref, w_ref, sc_ref, dx_ref, dw_ref, dsc_ref, pad, qpad):
        @pl.when(pl.program_id(0) == 0)
        def _():
            dw_ref[...] = jnp.zeros_like(dw_ref)
            dsc_ref[...] = jnp.zeros_like(dsc_ref)

        pad[0:POOL_PAD, :] = jnp.zeros((POOL_PAD, GW), F32)
        pad[POOL_PAD:POOL_PAD + S, :] = x_ref[...]
        qpad[S:S + POOL_PAD, :] = jnp.zeros((POOL_PAD, GW), F32)
        wv = w_ref[...].astype(BF16)
        for r0 in range(0, S, CONV_TILE):
            cnt = _pool_cnt(r0)
            mean = _pool_sums(pad, r0, POOL_PAD, -1) / cnt
            p = (mean - x_ref[r0:r0 + CONV_TILE, :]).astype(BF16)
            dy = dy_ref[r0:r0 + CONV_TILE, :]
            dsc_ref[...] += jnp.sum(dy * _dot(p, wv), axis=0, keepdims=True)
            dyp = (dy * sc_ref[...]).astype(BF16)
            dw_ref[...] += _dot_tn(p, dyp)
            dp = _dot_nt(dyp, wv)
            dx_ref[r0:r0 + CONV_TILE, :] = (-dp).astype(dx_ref.dtype)
            qpad[r0:r0 + CONV_TILE, :] = dp / cnt
        for r0 in range(0, S, CONV_TILE):
            back = _pool_sums(qpad, r0, 0, 1)
            dx_ref[r0:r0 + CONV_TILE, :] = (dx_ref[r0:r0 + CONV_TILE, :].astype(F32) + back).astype(dx_ref.dtype)

    return pl.pallas_call(
        body, grid=(B,),
        in_specs=[_seq(S, COL_D), pl.BlockSpec((None, S, GW), lambda b: (b, 0, 3)), _full((GW, GW)), _full((1, GW))],
        out_specs=[pl.BlockSpec((None, S, GW), lambda b: (b, 0, 0)), _full((GW, GW)), _full((1, GW))],
        out_shape=[_sds((B, S, GW), F32), _sds((GW, GW), F32), _sds((1, GW), F32)],
        scratch_shapes=[pltpu.VMEM((S + POOL_PAD, GW), F32), pltpu.VMEM((S + POOL_PAD, GW), F32)],
        compiler_params=_cp(("arbitrary",)), name="mixd_bwd")(z3, dycat3, wbd, scale)


def cmp_kv_fwd(tbk, tbv, pek, pev, w1k, w2k, w1v, w2v):
    B = tbk.shape[0]

    def body(tbk_ref, tbv_ref, pek_ref, pev_ref, w1k_ref, w2k_ref, w1v_ref, w2v_ref, kc_ref, vc_ref):
        for tb_ref, pe_ref, w1_ref, w2_ref, o_ref in ((tbk_ref, pek_ref, w1k_ref, w2k_ref, kc_ref),
                                                      (tbv_ref, pev_ref, w1v_ref, w2v_ref, vc_ref)):
            pre = _dot((tb_ref[...] + pe_ref[...]).astype(BF16), w1_ref[...].astype(BF16))
            hm = pre * _sigmoid(pre)
            o_ref[...] = _dot(hm.astype(BF16), w2_ref[...].astype(BF16))

    tb_spec = pl.BlockSpec((None, N_CMP, 2048), lambda b: (b, 0, 0))
    o_spec = pl.BlockSpec((None, N_CMP, HEAD_DIM), lambda b: (b, 0, 0))
    return pl.pallas_call(
        body, grid=(B,),
        in_specs=[tb_spec, tb_spec, _full((1, 2048)), _full((1, 2048)), _full((2048, HEAD_DIM)), _full((HEAD_DIM, HEAD_DIM)),
                  _full((2048, HEAD_DIM)), _full((HEAD_DIM, HEAD_DIM))],
        out_specs=[o_spec, o_spec], out_shape=[_sds((B, N_CMP, HEAD_DIM), F32)] * 2,
        compiler_params=_cp(("parallel",)), name="cmp_kv_fwd")(tbk, tbv, pek, pev, w1k, w2k, w1v, w2v)


def cmp_kv_bwd(tbk, tbv, pek, pev, w1k, w2k, w1v, w2v, dkc, dvc):
    B = tbk.shape[0]

    def body(tbk_ref, tbv_ref, pek_ref, pev_ref, w1k_ref, w2k_ref, w1v_ref, w2v_ref, dkc_ref, dvc_ref,
             dk2_ref, dv2_ref, dpek_ref, dpev_ref, dw1k_ref, dw2k_ref, dw1v_ref, dw2v_ref):
        @pl.when(pl.program_id(0) == 0)
        def _():
            for r in (dpek_ref, dpev_ref, dw1k_ref, dw2k_ref, dw1v_ref, dw2v_ref):
                r[...] = jnp.zeros_like(r)

        row0 = lax.broadcasted_iota(jnp.int32, (N_CMP, 1), 0) == 0
        for tb_ref, pe_ref, w1_ref, w2_ref, do_ref, d2_ref, dpe_ref, dw1_ref, dw2_ref in (
                (tbk_ref, pek_ref, w1k_ref, w2k_ref, dkc_ref, dk2_ref, dpek_ref, dw1k_ref, dw2k_ref),
                (tbv_ref, pev_ref, w1v_ref, w2v_ref, dvc_ref, dv2_ref, dpev_ref, dw1v_ref, dw2v_ref)):
            tb = (tb_ref[...] + pe_ref[...]).astype(BF16)
            w1 = w1_ref[...].astype(BF16)
            pre = _dot(tb, w1)
            sg = _sigmoid(pre)
            hm = (pre * sg).astype(BF16)
            do = do_ref[...].astype(BF16)
            dw2_ref[...] += _dot_tn(hm, do)
            dpre = (_dot_nt(do, w2_ref[...].astype(BF16)) * (sg * (1.0 + pre * (1.0 - sg)))).astype(BF16)
            dw1_ref[...] += _dot_tn(tb, dpre)
            dtb = _dot_nt(dpre, w1)
            dpe_ref[...] += jnp.sum(dtb, axis=0, keepdims=True)
            down = jnp.where(row0, 0.0, pltpu.roll(dtb[:, 1024:], 1, 0))
            d2_ref[...] = dtb[:, :1024] + down

    tb_spec = pl.BlockSpec((None, N_CMP, 2048), lambda b: (b, 0, 0))
    c_spec = pl.BlockSpec((None, N_CMP, HEAD_DIM), lambda b: (b, 0, 0))
    d2_spec = pl.BlockSpec((None, N_CMP, 1024), lambda b: (b, 0, 0))
    return pl.pallas_call(
        body, grid=(B,),
        in_specs=[tb_spec, tb_spec, _full((1, 2048)), _full((1, 2048)), _full((2048, HEAD_DIM)), _full((HEAD_DIM, HEAD_DIM)),
                  _full((2048, HEAD_DIM)), _full((HEAD_DIM, HEAD_DIM)), c_spec, c_spec],
        out_specs=[d2_spec, d2_spec, _full((1, 2048)), _full((1, 2048)), _full((2048, HEAD_DIM)), _full((HEAD_DIM, HEAD_DIM)),
                   _full((2048, HEAD_DIM)), _full((HEAD_DIM, HEAD_DIM))],
        out_shape=[_sds((B, N_CMP, 1024), F32)] * 2 + [_sds((1, 2048), F32)] * 2
        + [_sds((2048, HEAD_DIM), F32), _sds((HEAD_DIM, HEAD_DIM), F32)] * 2,
        compiler_params=_cp(("arbitrary",)), name="cmp_kv_bwd")(tbk, tbv, pek, pev, w1k, w2k, w1v, w2v, dkc, dvc)


def _qtile(col):
    return pl.BlockSpec((None, TQ, GW), lambda b, i: (b, i, col // GW))


def _qtile0():
    return pl.BlockSpec((None, TQ, GW), lambda b, i: (b, i, 0))


def _cmp_probs(q, kc, qpos):
    head = _lane_head()
    cend = lax.broadcasted_iota(jnp.int32, (1, N_CMP), 1) * CMP_STRIDE + 31
    cmask = cend <= qpos
    has = qpos >= 31
    out = []
    for h in range(4):
        qm = jnp.where(head == h, q, 0.0).astype(BF16)
        s = jnp.where(cmask, _dot_nt(qm, kc), NEG)
        e = jnp.exp(s - jnp.max(s, axis=-1, keepdims=True))
        p = jnp.where(has, e / jnp.sum(e, axis=-1, keepdims=True), 0.0)
        out.append((qm, p))
    return out


def cmp_attn_fwd(z3, kc4, vc4):
    B, S, _ = z3.shape

    def body(q_ref, kc_ref, vc_ref, o_ref, sel_ref):
        t0 = pl.program_id(1) * TQ
        qpos = t0 + lax.broadcasted_iota(jnp.int32, (TQ, 1), 0)
        head = _lane_head()
        kc, vc = kc_ref[...], vc_ref[...]
        o = jnp.zeros((TQ, GW), F32)
        psum = jnp.zeros((TQ, N_CMP), F32)
        for h, (_, p) in enumerate(_cmp_probs(q_ref[...] * 0.125, kc, qpos)):
            o = o + jnp.where(head == h, _dot(p.astype(BF16), vc), 0.0)
            psum = psum + p
        o_ref[...] = o
        cst = lax.broadcasted_iota(jnp.int32, (N_SLC, N_CMP), 1) * CMP_STRIDE
        jst = lax.broadcasted_iota(jnp.int32, (N_SLC, N_CMP), 0) * 64
        overlap = ((cst <= jst + 63) & (cst + 31 >= jst)).astype(BF16)
        imp = _dot_nt(overlap, psum.astype(BF16))
        qp = t0 + lax.broadcasted_iota(jnp.int32, (1, TQ), 1)
        jj = lax.broadcasted_iota(jnp.int32, (N_SLC, 1), 0)
        cur = lax.shift_right_logical(qp, SLC_BLOCK_SHIFT)
        forced = (jj == 0) | (jj == cur) | (jj == cur - 1)
        score = jnp.where(jj * 64 <= qp, imp + jnp.where(forced, FORCE_BONUS, 0.0), NEG)
        rank = jnp.zeros((N_SLC, TQ), F32)
        for j2 in range(N_SLC):
            sj = score[j2:j2 + 1, :]
            rank = rank + jnp.where((sj > score) | ((sj == score) & (j2 < jj)), 1.0, 0.0)
        sel_ref[...] = jnp.where((rank < SLC_TOPK) & (score > NEG / 2), 1.0, 0.0)

    c_spec = pl.BlockSpec((None, N_CMP, GW), lambda b, i: (b, 0, 0))
    return pl.pallas_call(
        body, grid=(B, S // TQ), in_specs=[_qtile(COL_Q), c_spec, c_spec],
        out_specs=[_qtile0(), pl.BlockSpec((None, N_SLC, TQ), lambda b, i: (b, 0, i))],
        out_shape=[_sds((B, S, GW), F32), _sds((B, N_SLC, S), F32)],
        compiler_params=_cp(("parallel", "parallel")), name="cmp_attn_fwd")(z3, kc4, vc4)


def cmp_attn_bwd(z3, kc4, vc4, do):
    B, S, _ = z3.shape
    nq = S // TQ

    def body(q_ref, kc_ref, vc_ref, do_ref, dq_ref, dkc_ref, dvc_ref):
        qi = pl.program_id(1)

        @pl.when(qi == 0)
        def _():
            dkc_ref[...] = jnp.zeros_like(dkc_ref)
            dvc_ref[...] = jnp.zeros_like(dvc_ref)

        qpos = qi * TQ + lax.broadcasted_iota(jnp.int32, (TQ, 1), 0)
        head = _lane_head()
        kc, vc, do = kc_ref[...], vc_ref[...], do_ref[...]
        dq = jnp.zeros((TQ, GW), F32)
        for h, (qm, p) in enumerate(_cmp_probs(q_ref[...] * 0.125, kc, qpos)):
            dom = jnp.where(head == h, do, 0.0).astype(BF16)
            dp = _dot_nt(dom, vc)
            ds = (p * (dp - jnp.sum(p * dp, axis=-1, keepdims=True))).astype(BF16)
            dq = dq + jnp.where(head == h, _dot(ds, kc), 0.0)
            dkc_ref[...] += _dot_tn(ds, qm)
            dvc_ref[...] += _dot_tn(p.astype(BF16), dom)
        dq_ref[...] = dq * 0.125

        @pl.when(qi == nq - 1)
        def _():
            dkc_ref[...] = _fold_heads(dkc_ref[...])
            dvc_ref[...] = _fold_heads(dvc_ref[...])

    c_spec = pl.BlockSpec((None, N_CMP, GW), lambda b, i: (b, 0, 0))
    return pl.pallas_call(
        body, grid=(B, nq), in_specs=[_qtile(COL_Q), c_spec, c_spec, _qtile0()],
        out_specs=[_qtile0(), c_spec, c_spec],
        out_shape=[_sds((B, S, GW), F32), _sds((B, N_CMP, GW), F32), _sds((B, N_CMP, GW), F32)],
        compiler_params=_cp(("parallel", "arbitrary")), name="cmp_attn_bwd")(z3, kc4, vc4, do)


def _attn_mask(mode, qpos, k0, sel_b):
    kpos = k0 + lax.broadcasted_iota(jnp.int32, (1, TQ), 1)
    mask = kpos <= qpos
    if mode == "win":
        return mask & (kpos > qpos - WIN)
    blk = lax.shift_right_logical(k0 + lax.broadcasted_iota(jnp.int32, (N_SLC, TQ), 1), SLC_BLOCK_SHIFT)
    expand = (blk == lax.broadcasted_iota(jnp.int32, (N_SLC, TQ), 0)).astype(BF16)
    return mask & (_dot_tn(sel_b, expand) > 0.5)


def _attn_lo(mode, qi):
    return jnp.maximum(qi - WIN // TQ, 0) if mode == "win" else 0


def attn_fwd(mode, z3, k4, v4, selT):
    B, S, _ = z3.shape

    def body(q_ref, k_ref, v_ref, sel_ref, o_ref, lse_ref, m_s, l_s, acc_s):
        qi = pl.program_id(1)
        qpos = qi * TQ + lax.broadcasted_iota(jnp.int32, (TQ, 1), 0)
        head = _lane_head()
        q = q_ref[...] * 0.125
        qm = [jnp.where(head == h, q, 0.0).astype(BF16) for h in range(4)]
        sel_b = sel_ref[...].astype(BF16)
        m_s[...] = jnp.full(m_s.shape, NEG, F32)
        l_s[...] = jnp.zeros_like(l_s)
        acc_s[...] = jnp.zeros_like(acc_s)

        def step(kb, carry):
            k0 = pl.multiple_of(kb * TQ, TQ)
            kblk = k_ref[pl.ds(k0, TQ), :]
            vblk = v_ref[pl.ds(k0, TQ), :]
            mask = _attn_mask(mode, qpos, k0, sel_b)
            for h in range(4):
                s = jnp.where(mask, _dot_nt(qm[h], kblk), NEG)
                m_old = m_s[h]
                m_new = jnp.maximum(m_old, jnp.max(s, axis=-1, keepdims=True))
                p = jnp.where(mask, jnp.exp(s - m_new), 0.0)
                alpha = jnp.exp(m_old - m_new)
                l_s[h] = alpha * l_s[h] + jnp.sum(p, axis=-1, keepdims=True)
                acc_s[h] = alpha * acc_s[h] + _dot(p.astype(BF16), vblk)
                m_s[h] = m_new
            return carry

        lax.fori_loop(_attn_lo(mode, qi), qi + 1, step, 0)
        o = jnp.zeros((TQ, GW), F32)
        lse = jnp.zeros((TQ, 128), F32)
        lane = lax.broadcasted_iota(jnp.int32, (1, 128), 1)
        for h in range(4):
            o = o + jnp.where(head == h, acc_s[h] / l_s[h], 0.0)
            lse = jnp.where(lane == h, m_s[h] + jnp.log(l_s[h]), lse)
        o_ref[...] = o
        lse_ref[...] = lse

    kv_spec = pl.BlockSpec((None, S, GW), lambda b, i: (b, 0, 0))
    return pl.pallas_call(
        body, grid=(B, S // TQ),
        in_specs=[_qtile(COL_Q), kv_spec, kv_spec, pl.BlockSpec((None, N_SLC, TQ), lambda b, i: (b, 0, i))],
        out_specs=[_qtile0(), pl.BlockSpec((None, TQ, 128), lambda b, i: (b, i, 0))],
        out_shape=[_sds((B, S, GW), F32), _sds((B, S, 128), F32)],
        scratch_shapes=[pltpu.VMEM((4, TQ, 1), F32), pltpu.VMEM((4, TQ, 1), F32), pltpu.VMEM((4, TQ, GW), F32)],
        compiler_params=_cp(("parallel", "parallel")), name=mode + "_attn_fwd")(z3, k4, v4, selT)


def attn_bwd(mode, z3, k4, v4, selT, o, lse, do):
    B, S, _ = z3.shape
    nq = S // TQ

    def body(q_ref, k_ref, v_ref, sel_ref, o_ref, lse_ref, do_ref, dq_ref, dk_ref, dv_ref, dq_s):
        qi = pl.program_id(1)

        @pl.when(qi == 0)
        def _():
            dk_ref[...] = jnp.zeros_like(dk_ref)
            dv_ref[...] = jnp.zeros_like(dv_ref)

        qpos = qi * TQ + lax.broadcasted_iota(jnp.int32, (TQ, 1), 0)
        head = _lane_head()
        lane = lax.broadcasted_iota(jnp.int32, (1, 128), 1)
        q = q_ref[...] * 0.125
        do = do_ref[...]
        doo = do * o_ref[...]
        lse = lse_ref[...]
        qm = [jnp.where(head == h, q, 0.0).astype(BF16) for h in range(4)]
        dom = [jnp.where(head == h, do, 0.0).astype(BF16) for h in range(4)]
        delta = [jnp.sum(jnp.where(head == h, doo, 0.0), axis=-1, keepdims=True) for h in range(4)]
        lse_h = [jnp.max(jnp.where(lane == h, lse, NEG), axis=-1, keepdims=True) for h in range(4)]
        sel_b = sel_ref[...].astype(BF16)
        dq_s[...] = jnp.zeros_like(dq_s)

        def step(kb, carry):
            k0 = pl.multiple_of(kb * TQ, TQ)
            kblk = k_ref[pl.ds(k0, TQ), :]
            vblk = v_ref[pl.ds(k0, TQ), :]
            mask = _attn_mask(mode, qpos, k0, sel_b)
            for h in range(4):
                s = _dot_nt(qm[h], kblk)
                p = jnp.where(mask, jnp.exp(s - lse_h[h]), 0.0)
                dp = _dot_nt(dom[h], vblk)
                ds = (p * (dp - delta[h])).astype(BF16)
                dq_s[...] += jnp.where(head == h, _dot(ds, kblk), 0.0)
                dk_ref[pl.ds(k0, TQ), :] += _dot_tn(ds, qm[h])
                dv_ref[pl.ds(k0, TQ), :] += _dot_tn(p.astype(BF16), dom[h])
            return carry

        lax.fori_loop(_attn_lo(mode, qi), qi + 1, step, 0)
        dq_ref[...] = dq_s[...] * 0.125

        @pl.when(qi == nq - 1)
        def _():
            for r0 in range(0, S, TQ):
                dk_ref[r0:r0 + TQ, :] = _fold_heads(dk_ref[r0:r0 + TQ, :])
                dv_ref[r0:r0 + TQ, :] = _fold_heads(dv_ref[r0:r0 + TQ, :])

    kv_spec = pl.BlockSpec((None, S, GW), lambda b, i: (b, 0, 0))
    return pl.pallas_call(
        body, grid=(B, nq),
        in_specs=[_qtile(COL_Q), kv_spec, kv_spec, pl.BlockSpec((None, N_SLC, TQ), lambda b, i: (b, 0, i)), _qtile0(),
                  pl.BlockSpec((None, TQ, 128), lambda b, i: (b, i, 0)), _qtile0()],
        out_specs=[_qtile0(), kv_spec, kv_spec],
        out_shape=[_sds((B, S, GW), F32)] * 3,
        scratch_shapes=[pltpu.VMEM((TQ, GW), F32)],
        compiler_params=_cp(("parallel", "arbitrary")), name=mode + "_attn_bwd")(z3, k4, v4, selT, o, lse, do)


def _gate_expand(b):
    r = lax.broadcasted_iota(jnp.int32, (128, GW), 0)
    hl = lax.shift_right_logical(lax.broadcasted_iota(jnp.int32, (128, GW), 1), 6)
    return (r == 3 * hl + b).astype(F32)


def combine_fwd(z3, o_cmp, o_slc, o_win):
    B, S, _ = z3.shape

    def body(gl_ref, oc_ref, os_ref, ow_ref, y_ref):
        g = _sigmoid(gl_ref[...])
        y = jnp.zeros((TQ, GW), F32)
        for b, o_ref in enumerate((oc_ref, os_ref, ow_ref)):
            y = y + jnp.dot(g, _gate_expand(b), precision=HI, preferred_element_type=F32) * o_ref[...]
        y_ref[...] = y.astype(y_ref.dtype)

    return pl.pallas_call(
        body, grid=(B, S // TQ),
        in_specs=[pl.BlockSpec((None, TQ, 128), lambda b, i: (b, i, COL_GL // 128)), _qtile0(), _qtile0(), _qtile0()],
        out_specs=_qtile0(), out_shape=_sds((B, S, GW), BF16),
        compiler_params=_cp(("parallel", "parallel")), name="combine_fwd")(z3, o_cmp, o_slc, o_win)


def combine_bwd(z3, o_cmp, o_slc, o_win, dycat3):
    B, S, _ = z3.shape

    def body(gl_ref, oc_ref, os_ref, ow_ref, dy_ref, dc_ref, ds_ref, dw_ref, dgl_ref):
        g = _sigmoid(gl_ref[...])
        dy = dy_ref[...]
        dg = jnp.zeros((TQ, 128), F32)
        for b, (o_ref, d_ref) in enumerate(((oc_ref, dc_ref), (os_ref, ds_ref), (ow_ref, dw_ref))):
            ex = _gate_expand(b)
            d_ref[...] = jnp.dot(g, ex, precision=HI, preferred_element_type=F32) * dy
            dg = dg + lax.dot_general(dy * o_ref[...], ex, (((1,), (1,)), ((), ())), precision=HI, preferred_element_type=F32)
        dgl_ref[...] = dg * g * (1.0 - g)

    gl_spec = pl.BlockSpec((None, TQ, 128), lambda b, i: (b, i, COL_GL // 128))
    return pl.pallas_call(
        body, grid=(B, S // TQ),
        in_specs=[gl_spec, _qtile0(), _qtile0(), _qtile0(), pl.BlockSpec((None, TQ, GW), lambda b, i: (b, i, 2))],
        out_specs=[_qtile0(), _qtile0(), _qtile0(), pl.BlockSpec((None, TQ, 128), lambda b, i: (b, i, 0))],
        out_shape=[_sds((B, S, GW), F32)] * 3 + [_sds((B, S, 128), F32)],
        compiler_params=_cp(("parallel", "parallel")), name="combine_bwd")(z3, o_cmp, o_slc, o_win, dycat3)


def assemble_dz(du, dv, da, dgt, dq_c, dq_s, dq_w, dd, dkv, dgl):
    T = du.shape[0]

    def body(du_ref, dv_ref, da_ref, dgt_ref, dqc_ref, dqs_ref, dqw_ref, dd_ref, dkv_ref, dgl_ref, o_ref):
        o_ref[:, COL_U:COL_U + GW] = du_ref[...]
        o_ref[:, COL_V:COL_V + GW] = dv_ref[...]
        o_ref[:, COL_A:COL_A + GW] = da_ref[...]
        o_ref[:, COL_G:COL_G + GW] = dgt_ref[...]
        o_ref[:, COL_Q:COL_Q + GW] = (dqc_ref[...] + dqs_ref[...] + dqw_ref[...]).astype(BF16)
        o_ref[:, COL_D:COL_D + GW] = dd_ref[...].astype(BF16)
        o_ref[:, COL_KV:COL_KV + 384] = dkv_ref[...].astype(BF16)
        o_ref[:, COL_GL:COL_GL + 128] = dgl_ref[...].astype(BF16)

    specs = [_rows(ROW_TILE, GW)] * 8 + [_rows(ROW_TILE, 384), _rows(ROW_TILE, 128)]
    return pl.pallas_call(body, grid=(T // ROW_TILE,), in_specs=specs, out_specs=_rows(ROW_TILE, ZW),
                          out_shape=_sds((T, ZW), BF16), compiler_params=_cp(("parallel",)),
                          name="assemble_dz")(du, dv, da, dgt, dq_c, dq_s, dq_w, dd, dkv, dgl)


def _my_pos():
    return lax.axis_index("x"), lax.axis_index("y"), lax.axis_index("c")


def _peer(k):
    x, y, c = _my_pos()
    return ((1 - x) if k & 4 else x, (1 - y) if k & 2 else y, (1 - c) if k & 1 else c)


def _index(pos):
    return 4 * pos[0] + 2 * pos[1] + pos[2]


_HBM = pl.BlockSpec(memory_space=pltpu.HBM)


def all_gather_rows(arrs, name):
    n = len(arrs)

    def body(*refs):
        ins, outs = refs[:n], refs[n:2 * n]
        send, recv, loc = refs[2 * n:]
        me = _index(_my_pos())
        copies = []
        for a in range(n):
            r = ins[a].shape[1]
            mine = outs[a].at[:, pl.ds(me * r, r), :]
            lc = pltpu.make_async_copy(ins[a], mine, loc.at[a])
            lc.start()
            copies.append(lc)
            for k in range(1, N_DEV):
                cp = pltpu.make_async_remote_copy(src_ref=ins[a], dst_ref=mine, send_sem=send.at[a, k - 1],
                                                  recv_sem=recv.at[a, k - 1], device_id=_peer(k), device_id_type=MESH)
                cp.start()
                copies.append(cp)
        for cp in copies:
            cp.wait()

    return pl.pallas_call(
        body, in_specs=[_HBM] * n, out_specs=[_HBM] * n,
        out_shape=[_sds((a.shape[0], N_DEV * a.shape[1], a.shape[2]), a.dtype) for a in arrs],
        scratch_shapes=[pltpu.SemaphoreType.DMA((n, N_DEV - 1)), pltpu.SemaphoreType.DMA((n, N_DEV - 1)),
                        pltpu.SemaphoreType.DMA((n,))],
        name=name)(*arrs)


def scatter_rows(arrs, name):
    n = len(arrs)

    def body(*refs):
        ins, outs = refs[:n], refs[n:2 * n]
        send, recv, loc = refs[2 * n:]
        me = _index(_my_pos())
        copies = []
        for a in range(n):
            r = ins[a].shape[1] // N_DEV
            lc = pltpu.make_async_copy(ins[a].at[:, pl.ds(me * r, r), :], outs[a].at[me], loc.at[a])
            lc.start()
            copies.append(lc)
            for k in range(1, N_DEV):
                peer = _peer(k)
                cp = pltpu.make_async_remote_copy(src_ref=ins[a].at[:, pl.ds(_index(peer) * r, r), :], dst_ref=outs[a].at[me],
                                                  send_sem=send.at[a, k - 1], recv_sem=recv.at[a, k - 1],
                                                  device_id=peer, device_id_type=MESH)
                cp.start()
                copies.append(cp)
        for cp in copies:
            cp.wait()

    return pl.pallas_call(
        body, in_specs=[_HBM] * n, out_specs=[_HBM] * n,
        out_shape=[_sds((N_DEV, a.shape[0], a.shape[1] // N_DEV, a.shape[2]), a.dtype) for a in arrs],
        scratch_shapes=[pltpu.SemaphoreType.DMA((n, N_DEV - 1)), pltpu.SemaphoreType.DMA((n, N_DEV - 1)),
                        pltpu.SemaphoreType.DMA((n,))],
        name=name)(*arrs)


def sum_slots(land, name):
    _, L, r, C = land.shape
    tr = _tile(r, 256, 16)

    def body(x_ref, o_ref):
        acc = x_ref[0].astype(F32)
        for s in range(1, N_DEV):
            acc = acc + x_ref[s].astype(F32)
        o_ref[...] = acc

    return pl.pallas_call(
        body, grid=(L, r // tr), in_specs=[pl.BlockSpec((N_DEV, None, tr, C), lambda l, i: (0, l, i, 0))],
        out_specs=pl.BlockSpec((None, tr, C), lambda l, i: (l, i, 0)), out_shape=_sds((L, r, C), F32),
        compiler_params=_cp(("parallel", "parallel")), name=name)(land)


def all_reduce_flat(flat, name):
    R = flat.shape[0]
    (gathered,) = all_gather_rows([flat[None]], name + "_gather")
    return sum_slots(gathered.reshape(N_DEV, 1, R, 128), name + "_sum")[0]


def pack_flat(arrs):
    flat = jnp.concatenate([a.reshape(-1).astype(F32) for a in arrs])
    n = flat.shape[0]
    total = -(-n // 2048) * 2048
    return jnp.pad(flat, (0, total - n)).reshape(total // 128, 128)


def unpack_flat(flat, shapes):
    v = flat.reshape(-1)
    out, off = [], 0
    for s in shapes:
        n = int(np.prod(s))
        out.append(v[off:off + n].reshape(s))
        off += n
    return out


def adamw(w, g, m, v, name):
    shape = w.shape
    C = shape[-1]
    R = int(np.prod(shape)) // C
    tr = _tile(R, 128, 8)
    c1 = 1.0 - ADAM_B1 ** ADAM_STEP
    c2 = 1.0 - ADAM_B2 ** ADAM_STEP

    def body(w_ref, g_ref, m_ref, v_ref, d_ref, nm_ref, nv_ref):
        g = g_ref[...]
        m2 = ADAM_B1 * m_ref[...] + (1.0 - ADAM_B1) * g
        v2 = ADAM_B2 * v_ref[...] + (1.0 - ADAM_B2) * (g * g)
        nm_ref[...] = m2
        nv_ref[...] = v2
        d_ref[...] = -ADAM_LR * ((m2 / c1) / (jnp.sqrt(v2 / c2) + ADAM_EPS) + ADAM_WD * w_ref[...])

    spec = pl.BlockSpec((tr, C), lambda i: (i, 0))
    outs = pl.pallas_call(body, grid=(R // tr,), in_specs=[spec] * 4, out_specs=[spec] * 3,
                          out_shape=[_sds((R, C), F32)] * 3, compiler_params=_cp(("parallel",)), name=name)(
        w.reshape(R, C), g.reshape(R, C), m.reshape(R, C), v.reshape(R, C))
    return [o.reshape(shape) for o in outs]


def _bexp(sg_b):
    return jnp.repeat(sg_b.T, HEAD_DIM, axis=1)


def _block_diag(pool_w):
    out = jnp.zeros((GW, GW), F32)
    for i in range(4):
        out = out.at[i * 64:(i + 1) * 64, i * 64:(i + 1) * 64].set(pool_w[i])
    return out


def _cmp_rows(t):
    B, S, _ = t.shape
    t2 = t.reshape(B, S // CMP_STRIDE, CMP_STRIDE * HEAD_DIM)
    nxt = jnp.concatenate([t2[:, 1:], jnp.zeros_like(t2[:, :1])], axis=1)
    return jnp.concatenate([t2, nxt], axis=-1)


def _tile4(t):
    return jnp.tile(t, (1, 1, 4)).astype(BF16)


def layer_fwd(x, p, B, S):
    T = B * S
    sv = {"x0": x}
    h1 = rms_fwd(x, p["g_pre_mix"], "rms_pre_mix")
    z = mm(h1, p["w_in"], name="mm_in")
    z3 = z.reshape(B, S, ZW)
    ya = mixa_fwd(z, p["sg_ln_g"], p["sg_w"], p["bexp"])
    yb = mixb_fwd(z3, p["cv_w"], p["cv_b"], p["cv_ln_g"], p["cv_ln_b"], p["cv_pw"], p["cv_pw_b"])
    kv = z3[:, :, COL_KV:COL_KV + 384]
    ks = [kv[:, :, i * 64:(i + 1) * 64] for i in range(6)]
    tbk, tbv = _cmp_rows(ks[0]), _cmp_rows(ks[1])
    kc, vc = cmp_kv_fwd(tbk, tbv, p["cmp_pos_k"], p["cmp_pos_v"], p["cmp_w1_k"], p["cmp_w2_k"], p["cmp_w1_v"], p["cmp_w2_v"])
    kc4, vc4 = _tile4(kc), _tile4(vc)
    ks4, vs4, kw4, vw4 = _tile4(ks[2]), _tile4(ks[3]), _tile4(ks[4]), _tile4(ks[5])
    o_cmp, selT = cmp_attn_fwd(z3, kc4, vc4)
    o_slc, lse_slc = attn_fwd("slc", z3, ks4, vs4, selT)
    o_win, lse_win = attn_fwd("win", z3, kw4, vw4, selT)
    yc = combine_fwd(z3, o_cmp, o_slc, o_win)
    yd = mixd_fwd(z3, p["pool_bd"], p["pool_scale"])
    ycat = jnp.concatenate([ya, yb.reshape(T, GW), yc.reshape(T, GW), yd.reshape(T, GW)], axis=-1)
    mix = mm(ycat, p["w_out"], name="mm_out")
    x1 = rms_post_fwd(x, mix, p["g_post_mix"], "rms_post_mix")
    h2 = rms_fwd(x1, p["g_pre_ffn"], "rms_pre_ffn")
    gu = mm(h2, p["w_gu"], blk="n", name="mm_gu")
    gu4 = gu.reshape(2, 4, T, FFN_BLK)
    a3 = swiglu_fwd(gu4)
    f = mm(a3, p["w_down"], blk="k", name="mm_down")
    x2 = rms_post_fwd(x1, f, p["g_post_ffn"], "rms_post_ffn")
    sv.update(h1=h1, z=z, tbk=tbk, tbv=tbv, kc4=kc4, vc4=vc4, ks4=ks4, vs4=vs4, kw4=kw4, vw4=vw4, o_cmp=o_cmp, selT=selT,
              o_slc=o_slc, lse_slc=lse_slc, o_win=o_win, lse_win=lse_win, ycat=ycat, mix=mix, x1=x1, h2=h2, gu4=gu4, a3=a3, f=f)
    return x2, sv


def _unfold(t4):
    return t4[:, :, :HEAD_DIM]


def layer_bwd(dx2, p, sv, B, S):
    T = B * S
    gb, gs = {}, {}
    df, gs["g_post_ffn"] = rms_bwd(sv["f"], p["g_post_ffn"], dx2, None, BF16, "rms_post_ffn_bwd")
    da3 = mm(df, p["w_down"], tb=True, blk="n", name="mm_down_dx")
    gb["w_down"] = mm(sv["a3"], df, ta=True, blk="m", out_dtype=BF16, name="mm_down_dw")
    dgu4 = swiglu_bwd(sv["gu4"], da3)
    dgu = dgu4.reshape(N_DEV, T, FFN_BLK)
    dh2 = mm(dgu, p["w_gu"], tb=True, blk="k", name="mm_gu_dx")
    gb["w_gu"] = mm(sv["h2"], dgu, ta=True, blk="n", out_dtype=BF16, name="mm_gu_dw")
    dx1, gs["g_pre_ffn"] = rms_bwd(sv["x1"], p["g_pre_ffn"], dh2, dx2, F32, "rms_pre_ffn_bwd")
    dmix, gs["g_post_mix"] = rms_bwd(sv["mix"], p["g_post_mix"], dx1, None, BF16, "rms_post_mix_bwd")
    dycat = mm(dmix, p["w_out"], tb=True, name="mm_out_dx")
    gb["w_out"] = mm(sv["ycat"], dmix, ta=True, out_dtype=BF16, name="mm_out_dw")
    dycat3 = dycat.reshape(B, S, D_MODEL)
    z = sv["z"]
    z3 = z.reshape(B, S, ZW)
    du, dv, gs["sg_w"], db, gs["sg_ln_g"] = mixa_bwd(z, dycat, p["sg_ln_g"], p["sg_w"], p["bexp"])
    gs["sg_b"] = db[:, :4].T
    (da, dgt, gs["cv_w"], gs["cv_b"], gs["cv_ln_g"], gs["cv_ln_b"], gpw, gs["cv_pw_b"]) = mixb_bwd(
        z3, dycat3, p["cv_w"], p["cv_b"], p["cv_ln_g"], p["cv_ln_b"], p["cv_pw"], p["cv_pw_b"])
    gb["cv_pw"] = gpw.astype(BF16)
    dd, dwbd, gs["pool_scale"] = mixd_bwd(z3, dycat3, p["pool_bd"], p["pool_scale"])
    gs["pool_w"] = jnp.stack([dwbd[i * 64:(i + 1) * 64, i * 64:(i + 1) * 64] for i in range(4)])
    do_c, do_s, do_w, dgl = combine_bwd(z3, sv["o_cmp"], sv["o_slc"], sv["o_win"], dycat3)
    dq_s, dks4, dvs4 = attn_bwd("slc", z3, sv["ks4"], sv["vs4"], sv["selT"], sv["o_slc"], sv["lse_slc"], do_s)
    dq_w, dkw4, dvw4 = attn_bwd("win", z3, sv["kw4"], sv["vw4"], sv["selT"], sv["o_win"], sv["lse_win"], do_w)
    dq_c, dkc4, dvc4 = cmp_attn_bwd(z3, sv["kc4"], sv["vc4"], do_c)
    (dk2, dv2, gs["cmp_pos_k"], gs["cmp_pos_v"], gw1k, gs["cmp_w2_k"], gw1v, gs["cmp_w2_v"]) = cmp_kv_bwd(
        sv["tbk"], sv["tbv"], p["cmp_pos_k"], p["cmp_pos_v"], p["cmp_w1_k"], p["cmp_w2_k"], p["cmp_w1_v"], p["cmp_w2_v"],
        _unfold(dkc4), _unfold(dvc4))
    gb["cmp_w1_k"], gb["cmp_w1_v"] = gw1k.astype(BF16), gw1v.astype(BF16)
    dkv = jnp.concatenate([dk2.reshape(B, S, HEAD_DIM), dv2.reshape(B, S, HEAD_DIM), _unfold(dks4), _unfold(dvs4),
                           _unfold(dkw4), _unfold(dvw4)], axis=-1).reshape(T, 384)
    dz = assemble_dz(du, dv, da.reshape(T, GW), dgt.reshape(T, GW), dq_c.reshape(T, GW), dq_s.reshape(T, GW),
                     dq_w.reshape(T, GW), dd.reshape(T, GW), dkv, dgl.reshape(T, 128))
    dh1 = mm(dz, p["w_in"], tb=True, name="mm_in_dx")
    gb["w_in"] = mm(sv["h1"], dz, ta=True, out_dtype=BF16, name="mm_in_dw")
    dx0, gs["g_pre_mix"] = rms_bwd(sv["x0"], p["g_pre_mix"], dh1, dx1, F32, "rms_pre_mix_bwd")
    return dx0, gb, gs


SMALL = ["g_pre_mix", "g_post_mix", "g_pre_ffn", "g_post_ffn", "sg_ln_g", "sg_w", "sg_b", "cv_w", "cv_b", "cv_ln_g", "cv_ln_b",
         "cv_pw_b", "cmp_pos_k", "cmp_pos_v", "cmp_w2_k", "cmp_w2_v", "pool_w", "pool_scale"]
BIG = ["w_in", "w_out", "w_gu", "w_down", "cmp_w1_k", "cmp_w1_v", "cv_pw"]
NAMES = ["g_pre_mix", "g_post_mix", "g_pre_ffn", "g_post_ffn", "w_in", "sg_ln_g", "sg_w", "sg_b", "cv_w", "cv_b", "cv_ln_g",
         "cv_ln_b", "cv_pw", "cv_pw_b", "cmp_pos_k", "cmp_pos_v", "cmp_w1_k", "cmp_w2_k", "cmp_w1_v", "cmp_w2_v", "pool_w",
         "pool_scale", "w_out", "ffn_w_gu", "ffn_w_down"]


def kernel(x, g_pre_mix, g_post_mix, g_pre_ffn, g_post_ffn, w_in, sg_ln_g, sg_w, sg_b, cv_w, cv_b, cv_ln_g, cv_ln_b, cv_pw, cv_pw_b, cmp_pos_k, cmp_pos_v, cmp_w1_k, cmp_w2_k, cmp_w1_v, cmp_w2_v, pool_w, pool_scale, w_out, ffn_w_gu, ffn_w_down, loss_target, m_g_pre_mix, m_g_post_mix, m_g_pre_ffn, m_g_post_ffn, m_w_in, m_sg_ln_g, m_sg_w, m_sg_b, m_cv_w, m_cv_b, m_cv_ln_g, m_cv_ln_b, m_cv_pw, m_cv_pw_b, m_cmp_pos_k, m_cmp_pos_v, m_cmp_w1_k, m_cmp_w2_k, m_cmp_w1_v, m_cmp_w2_v, m_pool_w, m_pool_scale, m_w_out, m_ffn_w_gu, m_ffn_w_down, v_g_pre_mix, v_g_post_mix, v_g_pre_ffn, v_g_post_ffn, v_w_in, v_sg_ln_g, v_sg_w, v_sg_b, v_cv_w, v_cv_b, v_cv_ln_g, v_cv_ln_b, v_cv_pw, v_cv_pw_b, v_cmp_pos_k, v_cmp_pos_v, v_cmp_w1_k, v_cmp_w2_k, v_cmp_w1_v, v_cmp_w2_v, v_pool_w, v_pool_scale, v_w_out, v_ffn_w_gu, v_ffn_w_down):
    args = dict(locals())
    W = {n: args[n] for n in NAMES}
    M = {n: args["m_" + n] for n in NAMES}
    V = {n: args["v_" + n] for n in NAMES}
    B, S, _ = x.shape
    T = B * S
    L = w_in.shape[0]
    me = _index(_my_pos())
    cpd = GW // N_DEV

    shards = [pack_cols(w_in).astype(BF16), w_out.astype(BF16), ffn_w_gu.astype(BF16), ffn_w_down.astype(BF16),
              cmp_w1_k.astype(BF16), cmp_w1_v.astype(BF16), cv_pw.astype(BF16)]
    full = dict(zip(BIG, all_gather_rows(shards, "gather_weights")))
    full["w_gu"] = full["w_gu"].reshape(L, N_DEV, D_MODEL, FFN_BLK)
    full["w_down"] = full["w_down"].reshape(L, 4, FFN_BLK, D_MODEL)
    cvw_slab = lax.dynamic_update_slice(jnp.zeros((L, CONV_WIDTH, GW), F32), cv_w, (0, 0, me * cpd))
    cvw_full = all_reduce_flat(pack_flat([cvw_slab]), "gather_cvw")
    (cvw_full,) = unpack_flat(cvw_full, [(L, CONV_WIDTH, GW)])

    def layer_params(l):
        p = {n: full[n][l] for n in BIG}
        for n in ("g_pre_mix", "g_post_mix", "g_pre_ffn", "g_post_ffn", "sg_ln_g", "cv_b", "cv_ln_g", "cv_ln_b", "cv_pw_b",
                  "pool_scale"):
            p[n] = W[n][l][None, :]
        p["sg_w"] = sg_w[l]
        p["bexp"] = _bexp(sg_b[l])
        p["cv_w"] = cvw_full[l]
        p["cmp_pos_k"] = cmp_pos_k[l].reshape(1, 2048)
        p["cmp_pos_v"] = cmp_pos_v[l].reshape(1, 2048)
        p["cmp_w2_k"], p["cmp_w2_v"] = cmp_w2_k[l], cmp_w2_v[l]
        p["pool_bd"] = _block_diag(pool_w[l])
        return p

    xs = x.reshape(T, D_MODEL)
    params, saved = [], []
    for l in range(L):
        p = layer_params(l)
        xs, sv = layer_fwd(xs, p, B, S)
        params.append(p)
        saved.append(sv)
    dy, lpart = loss_fwd_bwd(xs, loss_target.reshape(T, D_MODEL))
    loss = lax.psum(lpart[0, 0], ("x", "y", "c"))

    small, lands = [None] * L, [None] * L
    for l in reversed(range(L)):
        dy, gb, gs = layer_bwd(dy, params[l], saved[l], B, S)
        gb["w_gu"] = gb["w_gu"].reshape(N_DEV * D_MODEL, FFN_BLK)
        gb["w_down"] = gb["w_down"].reshape(FFN_HIDDEN, D_MODEL)
        lands[l] = scatter_rows([gb[n][None] for n in BIG], "scatter_grads")
        small[l] = gs
    grad_x = dy.reshape(B, S, D_MODEL)

    grads = {}
    for i, n in enumerate(BIG):
        land = jnp.concatenate([lands[l][i] for l in range(L)], axis=1)
        grads[n] = sum_slots(land, "sum_" + n)
    grads["w_in"] = unpack_cols(grads["w_in"])
    grads["ffn_w_gu"], grads["ffn_w_down"] = grads.pop("w_gu"), grads.pop("w_down")

    small_shapes = [(L,) + tuple(small[0][n].shape) for n in SMALL]
    reduced = all_reduce_flat(pack_flat([jnp.stack([small[l][n] for l in range(L)]) for n in SMALL]), "reduce_small")
    for n, g in zip(SMALL, unpack_flat(reduced, small_shapes)):
        grads[n] = g.reshape(W[n].shape) if n != "cv_w" else g
    grads["cv_w"] = lax.dynamic_slice(grads["cv_w"], (0, 0, me * cpd), (L, CONV_WIDTH, cpd))

    delta, new_m, new_v = {}, {}, {}
    big_names = ["w_in", "w_out", "ffn_w_gu", "ffn_w_down", "cmp_w1_k", "cmp_w1_v", "cv_pw"]
    for n in big_names:
        delta[n], new_m[n], new_v[n] = adamw(W[n], grads[n], M[n], V[n], "adamw_" + n)
    shapes = [W[n].shape for n in SMALL]
    packed = adamw(pack_flat([W[n] for n in SMALL]), pack_flat([grads[n] for n in SMALL]),
                   pack_flat([M[n] for n in SMALL]), pack_flat([V[n] for n in SMALL]), "adamw_small")
    for out, flat in zip((delta, new_m, new_v), packed):
        for n, a in zip(SMALL, unpack_flat(flat, shapes)):
            out[n] = a

    return (loss, grad_x, *[grads[n] for n in NAMES], *[delta[n] for n in NAMES], *[new_m[n] for n in NAMES],
            *[new_v[n] for n in NAMES])
```

```python
import numpy as np
import jax
import jax.numpy as jnp
from jax import lax
from jax.experimental import pallas as pl
from jax.experimental.pallas import tpu as pltpu

F32 = jnp.float32
BF16 = jnp.bfloat16
HI = lax.Precision.HIGHEST

D_MODEL = 1024
GW = 256
HEAD_DIM = 64
ZW = 2048
SG_CHUNK = 128
CONV_WIDTH = 31
CONV_PAD = 32
CMP_STRIDE = 16
N_CMP = 128
SLC_BLOCK_SHIFT = 6
N_SLC = 32
SLC_TOPK = 8
WIN = 512
NEG = -1e30
FORCE_BONUS = 1e4
RMS_EPS = 1e-6
LN_EPS = 1e-5
FFN_HIDDEN = 2816
N_DEV = 8
FFN_BLK = 2 * FFN_HIDDEN // N_DEV
TQ = 256
ROW_TILE = 512
CONV_TILE = 256
VMEM_LIMIT = 56 * 1024 * 1024
MESH = pl.DeviceIdType.MESH

ADAM_LR, ADAM_B1, ADAM_B2, ADAM_EPS, ADAM_WD, ADAM_STEP = 0.001, 0.9, 0.999, 1e-08, 0.01, 10

COL_U, COL_V, COL_A, COL_G, COL_Q, COL_D, COL_KV, COL_GL = 0, 256, 512, 768, 1024, 1280, 1536, 1920


def _sds(shape, dtype):
    return jax.ShapeDtypeStruct(shape, dtype)


def _cp(sem=None):
    return pltpu.CompilerParams(dimension_semantics=sem, vmem_limit_bytes=VMEM_LIMIT)


def _tile(n, target, q=128):
    best = None
    for t in range(q, min(n, target) + 1, q):
        if n % t == 0:
            best = t
    return best or n


def _full(shape):
    nd = len(shape)
    return pl.BlockSpec(shape, lambda *_: (0,) * nd)


def _sigmoid(x):
    return jax.nn.sigmoid(x)


def _dot(a, b):
    return jnp.dot(a, b, preferred_element_type=F32)


def _dot_nt(a, b):
    return lax.dot_general(a, b, (((1,), (1,)), ((), ())), preferred_element_type=F32)


def _dot_tn(a, b):
    return lax.dot_general(a, b, (((0,), (0,)), ((), ())), preferred_element_type=F32)


def _lane_head(width=GW):
    return lax.shift_right_logical(lax.broadcasted_iota(jnp.int32, (1, width), 1), 6)


def _fold_heads(x):
    return x + pltpu.roll(x, 64, 1) + pltpu.roll(x, 128, 1) + pltpu.roll(x, 192, 1)


def pack_cols(w):
    pad = jnp.zeros(w.shape[:-1] + (ZW - 1932,), w.dtype)
    return jnp.concatenate([w[..., :1280], w[..., 1676:1932], w[..., 1280:1664], w[..., 1664:1676], pad], axis=-1)


def unpack_cols(wp):
    return jnp.concatenate([wp[..., :1280], wp[..., 1536:1920], wp[..., 1920:1932], wp[..., 1280:1536]], axis=-1)


def mm(a, b, *, ta=False, tb=False, blk=None, out_dtype=F32, name, tm=1024, tn=1024, tk=1024):
    a_dims = ("k", "m") if ta else ("m", "k")
    b_dims = ("n", "k") if tb else ("k", "n")
    a3, b3, o3 = blk in a_dims and blk is not None, blk in b_dims and blk is not None, blk in ("m", "n")
    size = {}
    size[a_dims[0]], size[a_dims[1]] = a.shape[-2:]
    size[b_dims[0]], size[b_dims[1]] = b.shape[-2:]
    nb = a.shape[0] if a3 else (b.shape[0] if b3 else 1)
    tile = {"m": _tile(size["m"], tm), "n": _tile(size["n"], tn), "k": _tile(size["k"], tk)}
    grid = {d: size[d] // tile[d] for d in "mnk"}
    if blk is not None:
        tile[blk] = size[blk]
        grid[blk] = nb
    nk = grid["k"]

    def spec(dims, is3):
        def im(i, j, k):
            g = {"m": i, "n": j, "k": k}
            idx = tuple(0 if d == blk else g[d] for d in dims)
            return ((g[blk],) + idx) if is3 else idx
        shape = (tile[dims[0]], tile[dims[1]])
        return pl.BlockSpec(((None,) + shape) if is3 else shape, im)

    dn = (((0 if ta else 1,), (1 if tb else 0,)), ((), ()))

    def partial(a_ref, b_ref):
        return lax.dot_general(a_ref[...].astype(BF16), b_ref[...].astype(BF16), dn, preferred_element_type=F32)

    def body_single(a_ref, b_ref, o_ref):
        o_ref[...] = partial(a_ref, b_ref).astype(o_ref.dtype)

    def body_acc(a_ref, b_ref, o_ref, acc):
        k = pl.program_id(2)

        @pl.when(k == 0)
        def _():
            acc[...] = partial(a_ref, b_ref)

        @pl.when((k > 0) & (k < nk - 1))
        def _():
            acc[...] += partial(a_ref, b_ref)

        @pl.when(k == nk - 1)
        def _():
            o_ref[...] = (acc[...] + partial(a_ref, b_ref)).astype(o_ref.dtype)

    oshape = ((nb,) if o3 else ()) + (size["m"], size["n"])
    return pl.pallas_call(
        body_single if nk == 1 else body_acc, grid=(grid["m"], grid["n"], nk),
        in_specs=[spec(a_dims, a3), spec(b_dims, b3)], out_specs=spec(("m", "n"), o3),
        out_shape=_sds(oshape, out_dtype),
        scratch_shapes=[] if nk == 1 else [pltpu.VMEM((tile["m"], tile["n"]), F32)],
        compiler_params=_cp(("parallel", "parallel", "arbitrary")), name=name)(a, b)


def _rows(tm, width):
    return pl.BlockSpec((tm, width), lambda i: (i, 0))


def rms_fwd(x, g, name):
    T = x.shape[0]

    def body(x_ref, g_ref, h_ref):
        x = x_ref[...]
        r = lax.rsqrt(jnp.mean(x * x, axis=-1, keepdims=True) + RMS_EPS)
        h_ref[...] = ((x * r) * g_ref[...]).astype(h_ref.dtype)

    return pl.pallas_call(body, grid=(T // ROW_TILE,), in_specs=[_rows(ROW_TILE, D_MODEL), _full((1, D_MODEL))],
                          out_specs=_rows(ROW_TILE, D_MODEL), out_shape=_sds((T, D_MODEL), BF16),
                          compiler_params=_cp(("parallel",)), name=name)(x, g)


def rms_post_fwd(xres, m, g, name):
    T = m.shape[0]

    def body(x_ref, m_ref, g_ref, o_ref):
        m = m_ref[...]
        r = lax.rsqrt(jnp.mean(m * m, axis=-1, keepdims=True) + RMS_EPS)
        o_ref[...] = x_ref[...] + (m * r) * g_ref[...]

    return pl.pallas_call(body, grid=(T // ROW_TILE,),
                          in_specs=[_rows(ROW_TILE, D_MODEL), _rows(ROW_TILE, D_MODEL), _full((1, D_MODEL))],
                          out_specs=_rows(ROW_TILE, D_MODEL), out_shape=_sds((T, D_MODEL), F32),
                          compiler_params=_cp(("parallel",)), name=name)(xres, m, g)


def rms_bwd(m, g, dy, dres, out_dtype, name):
    T = m.shape[0]
    has_res = dres is not None

    def body(*refs):
        if has_res:
            m_ref, g_ref, dy_ref, dres_ref, dm_ref, dg_ref = refs
        else:
            m_ref, g_ref, dy_ref, dm_ref, dg_ref = refs
        m = m_ref[...]
        dy = dy_ref[...].astype(F32)
        r = lax.rsqrt(jnp.mean(m * m, axis=-1, keepdims=True) + RMS_EPS)
        n = m * r
        dn = dy * g_ref[...]
        dm = r * (dn - n * jnp.mean(dn * n, axis=-1, keepdims=True))
        if has_res:
            dm = dm + dres_ref[...]
        dm_ref[...] = dm.astype(dm_ref.dtype)

        @pl.when(pl.program_id(0) == 0)
        def _():
            dg_ref[...] = jnp.zeros_like(dg_ref)

        dg_ref[...] += jnp.sum(dy * n, axis=0, keepdims=True)

    ins = [m, g, dy] + ([dres] if has_res else [])
    specs = [_rows(ROW_TILE, D_MODEL), _full((1, D_MODEL)), _rows(ROW_TILE, D_MODEL)] + ([_rows(ROW_TILE, D_MODEL)] if has_res else [])
    return pl.pallas_call(body, grid=(T // ROW_TILE,), in_specs=specs,
                          out_specs=[_rows(ROW_TILE, D_MODEL), _full((1, D_MODEL))],
                          out_shape=[_sds((T, D_MODEL), out_dtype), _sds((1, D_MODEL), F32)],
                          compiler_params=_cp(("arbitrary",)), name=name)(*ins)


def loss_fwd_bwd(y, tgt):
    T = y.shape[0]

    def body(y_ref, t_ref, dy_ref, l_ref):
        e = y_ref[...] - t_ref[...]
        dy_ref[...] = e * (1.0 / D_MODEL)

        @pl.when(pl.program_id(0) == 0)
        def _():
            l_ref[...] = jnp.zeros_like(l_ref)

        l_ref[...] += jnp.full(l_ref.shape, 0.5 * jnp.sum(jnp.mean(e * e, axis=-1, keepdims=True)), F32)

    return pl.pallas_call(body, grid=(T // ROW_TILE,), in_specs=[_rows(ROW_TILE, D_MODEL)] * 2,
                          out_specs=[_rows(ROW_TILE, D_MODEL), _full((8, 128))],
                          out_shape=[_sds((T, D_MODEL), F32), _sds((8, 128), F32)],
                          compiler_params=_cp(("arbitrary",)), name="loss")(y, tgt)


def _gu_spec():
    return pl.BlockSpec((2, None, ROW_TILE, FFN_BLK), lambda j, i: (0, j, i, 0))


def _a_spec():
    return pl.BlockSpec((None, ROW_TILE, FFN_BLK), lambda j, i: (j, i, 0))


def swiglu_fwd(gu4):
    T = gu4.shape[2]

    def body(gu_ref, a_ref):
        gate = gu_ref[0]
        a_ref[...] = (gate * _sigmoid(gate) * gu_ref[1]).astype(a_ref.dtype)

    return pl.pallas_call(body, grid=(4, T // ROW_TILE), in_specs=[_gu_spec()], out_specs=_a_spec(),
                          out_shape=_sds((4, T, FFN_BLK), BF16), compiler_params=_cp(("parallel", "parallel")),
                          name="swiglu_fwd")(gu4)


def swiglu_bwd(gu4, da3):
    T = gu4.shape[2]

    def body(gu_ref, da_ref, d_ref):
        gate, up, da = gu_ref[0], gu_ref[1], da_ref[...]
        sg = _sigmoid(gate)
        d_ref[0] = (da * up * (sg * (1.0 + gate * (1.0 - sg)))).astype(d_ref.dtype)
        d_ref[1] = (da * (gate * sg)).astype(d_ref.dtype)

    return pl.pallas_call(body, grid=(4, T // ROW_TILE), in_specs=[_gu_spec(), _a_spec()], out_specs=_gu_spec(),
                          out_shape=_sds((2, 4, T, FFN_BLK), BF16), compiler_params=_cp(("parallel", "parallel")),
                          name="swiglu_bwd")(gu4, da3)


def _zcol(tm, col):
    return pl.BlockSpec((tm, GW), lambda i: (i, col // GW))


def _sg_common(v, g):
    mu = jnp.mean(v, axis=-1, keepdims=True)
    xc = v - mu
    rstd = lax.rsqrt(jnp.mean(xc * xc, axis=-1, keepdims=True) + LN_EPS)
    vhat = xc * rstd
    return vhat, rstd, vhat * g


def _tril_weights(w_ref):
    tri = lax.broadcasted_iota(jnp.int32, (SG_CHUNK, SG_CHUNK), 0) >= lax.broadcasted_iota(jnp.int32, (SG_CHUNK, SG_CHUNK), 1)
    return tri, [jnp.where(tri, w_ref[h], 0.0).astype(BF16) for h in range(4)]


def mixa_fwd(z, ln_g, w, bexp):
    T = z.shape[0]
    nch = ROW_TILE // SG_CHUNK

    def body(u_ref, v_ref, g_ref, w_ref, be_ref, y_ref):
        _, _, vln = _sg_common(v_ref[...], g_ref[...])
        vb = vln.astype(BF16)
        head = _lane_head()
        _, wh = _tril_weights(w_ref)
        for c in range(nch):
            rows = slice(c * SG_CHUNK, (c + 1) * SG_CHUNK)
            sv = be_ref[...]
            for h in range(4):
                sv = sv + jnp.where(head == h, _dot(wh[h], vb[rows]), 0.0)
            y_ref[rows, :] = (u_ref[rows, :] * sv).astype(y_ref.dtype)

    return pl.pallas_call(body, grid=(T // ROW_TILE,),
                          in_specs=[_zcol(ROW_TILE, COL_U), _zcol(ROW_TILE, COL_V), _full((1, GW)), _full((4, SG_CHUNK, SG_CHUNK)),
                                    _full((SG_CHUNK, GW))],
                          out_specs=_rows(ROW_TILE, GW), out_shape=_sds((T, GW), BF16),
                          compiler_params=_cp(("parallel",)), name="mixa_fwd")(z, z, ln_g, w, bexp)


def mixa_bwd(z, dycat, ln_g, w, bexp):
    T = z.shape[0]
    nch = ROW_TILE // SG_CHUNK
    nsteps = T // ROW_TILE

    def body(u_ref, v_ref, dy_ref, g_ref, w_ref, be_ref, du_ref, dv_ref, dw_ref, db_ref, dg_ref, dbe_acc):
        step = pl.program_id(0)

        @pl.when(step == 0)
        def _():
            dw_ref[...] = jnp.zeros_like(dw_ref)
            dg_ref[...] = jnp.zeros_like(dg_ref)
            dbe_acc[...] = jnp.zeros_like(dbe_acc)

        g = g_ref[...]
        vhat, rstd, vln = _sg_common(v_ref[...], g)
        vb = vln.astype(BF16)
        head = _lane_head()
        tri, wh = _tril_weights(w_ref)
        dgsum = jnp.zeros((1, GW), F32)
        for c in range(nch):
            rows = slice(c * SG_CHUNK, (c + 1) * SG_CHUNK)
            sv = be_ref[...]
            for h in range(4):
                sv = sv + jnp.where(head == h, _dot(wh[h], vb[rows]), 0.0)
            dy = dy_ref[rows, :]
            du_ref[rows, :] = (dy * sv).astype(du_ref.dtype)
            dsv = dy * u_ref[rows, :]
            dbe_acc[...] += dsv
            dvln = jnp.zeros((SG_CHUNK, GW), F32)
            for h in range(4):
                dsvm = jnp.where(head == h, dsv, 0.0).astype(BF16)
                dw_ref[h] += _dot_nt(dsvm, vb[rows])
                dvln = dvln + _dot_tn(wh[h], dsvm)
            vh = vhat[rows]
            dgsum = dgsum + jnp.sum(dvln * vh, axis=0, keepdims=True)
            dvhat = dvln * g
            dv = rstd[rows] * (dvhat - jnp.mean(dvhat, axis=-1, keepdims=True) - vh * jnp.mean(dvhat * vh, axis=-1, keepdims=True))
            dv_ref[rows, :] = dv.astype(dv_ref.dtype)
        dg_ref[...] += dgsum

        @pl.when(step == nsteps - 1)
        def _():
            for h in range(4):
                dw_ref[h] = jnp.where(tri, dw_ref[h], 0.0)
            fold = (lax.shift_right_logical(lax.broadcasted_iota(jnp.int32, (GW, 128), 0), 6)
                    == lax.broadcasted_iota(jnp.int32, (GW, 128), 1)).astype(F32)
            db_ref[...] = jnp.dot(dbe_acc[...], fold, precision=HI, preferred_element_type=F32)

    return pl.pallas_call(
        body, grid=(nsteps,),
        in_specs=[_zcol(ROW_TILE, COL_U), _zcol(ROW_TILE, COL_V), pl.BlockSpec((ROW_TILE, GW), lambda i: (i, 0)),
                  _full((1, GW)), _full((4, SG_CHUNK, SG_CHUNK)), _full((SG_CHUNK, GW))],
        out_specs=[_rows(ROW_TILE, GW), _rows(ROW_TILE, GW), _full((4, SG_CHUNK, SG_CHUNK)), _full((SG_CHUNK, 128)), _full((1, GW))],
        out_shape=[_sds((T, GW), BF16), _sds((T, GW), BF16), _sds((4, SG_CHUNK, SG_CHUNK), F32), _sds((SG_CHUNK, 128), F32),
                   _sds((1, GW), F32)],
        scratch_shapes=[pltpu.VMEM((SG_CHUNK, GW), F32)],
        compiler_params=_cp(("arbitrary",)), name="mixa_bwd")(z, z, dycat, ln_g, w, bexp)


def _seq(S, col):
    return pl.BlockSpec((None, S, GW), lambda b: (b, 0, col // GW))


def _conv_ln(pad, r0, cw_ref, cb, lg, lb):
    acc = jnp.zeros((CONV_TILE, GW), F32) + cb
    for k in range(CONV_WIDTH):
        acc = acc + cw_ref[k:k + 1, :] * pad[pl.ds(r0 + CONV_PAD - (CONV_WIDTH - 1) + k, CONV_TILE), :]
    mu = jnp.mean(acc, axis=-1, keepdims=True)
    xc = acc - mu
    rstd = lax.rsqrt(jnp.mean(xc * xc, axis=-1, keepdims=True) + LN_EPS)
    hhat = xc * rstd
    return hhat, rstd, hhat * lg + lb


def mixb_fwd(z3, cw, cb, lg, lb, pw, pwb):
    B, S, _ = z3.shape

    def body(a_ref, gt_ref, cw_ref, cb_ref, lg_ref, lb_ref, pw_ref, pwb_ref, y_ref, pad):
        pad[0:CONV_PAD, :] = jnp.zeros((CONV_PAD, GW), F32)
        pad[CONV_PAD:CONV_PAD + S, :] = a_ref[...] * _sigmoid(gt_ref[...])
        pwv = pw_ref[...].astype(BF16)
        for r0 in range(0, S, CONV_TILE):
            _, _, ln = _conv_ln(pad, r0, cw_ref, cb_ref[...], lg_ref[...], lb_ref[...])
            s = ln * _sigmoid(ln)
            y_ref[r0:r0 + CONV_TILE, :] = (_dot(s.astype(BF16), pwv) + pwb_ref[...]).astype(y_ref.dtype)

    return pl.pallas_call(
        body, grid=(B,),
        in_specs=[_seq(S, COL_A), _seq(S, COL_G), _full((CONV_WIDTH, GW)), _full((1, GW)), _full((1, GW)), _full((1, GW)),
                  _full((GW, GW)), _full((1, GW))],
        out_specs=pl.BlockSpec((None, S, GW), lambda b: (b, 0, 0)), out_shape=_sds((B, S, GW), BF16),
        scratch_shapes=[pltpu.VMEM((S + CONV_PAD, GW), F32)],
        compiler_params=_cp(("parallel",)), name="mixb_fwd")(z3, z3, cw, cb, lg, lb, pw, pwb)


def mixb_bwd(z3, dycat3, cw, cb, lg, lb, pw, pwb):
    B, S, _ = z3.shape

    def body(a_ref, gt_ref, dy_ref, cw_ref, cb_ref, lg_ref, lb_ref, pw_ref, pwb_ref,
             da_ref, dgt_ref, dcw_ref, dcb_ref, dlg_ref, dlb_ref, dpw_ref, dpwb_ref, pad, dpad, dcw_acc):
        @pl.when(pl.program_id(0) == 0)
        def _():
            for r in (dcb_ref, dlg_ref, dlb_ref, dpw_ref, dpwb_ref, dcw_acc):
                r[...] = jnp.zeros_like(r)

        pad[0:CONV_PAD, :] = jnp.zeros((CONV_PAD, GW), F32)
        pad[CONV_PAD:CONV_PAD + S, :] = a_ref[...] * _sigmoid(gt_ref[...])
        dpad[S:S + CONV_PAD, :] = jnp.zeros((CONV_PAD, GW), F32)
        pwv = pw_ref[...].astype(BF16)
        lg = lg_ref[...]
        for r0 in range(0, S, CONV_TILE):
            hhat, rstd, ln = _conv_ln(pad, r0, cw_ref, cb_ref[...], lg, lb_ref[...])
            sg = _sigmoid(ln)
            s = ln * sg
            dy = dy_ref[r0:r0 + CONV_TILE, :]
            dyb = dy.astype(BF16)
            dpw_ref[...] += _dot_tn(s.astype(BF16), dyb)
            dpwb_ref[...] += jnp.sum(dy, axis=0, keepdims=True)
            dln = _dot_nt(dyb, pwv) * (sg * (1.0 + ln * (1.0 - sg)))
            dlg_ref[...] += jnp.sum(dln * hhat, axis=0, keepdims=True)
            dlb_ref[...] += jnp.sum(dln, axis=0, keepdims=True)
            dhh = dln * lg
            dhc = rstd * (dhh - jnp.mean(dhh, axis=-1, keepdims=True) - hhat * jnp.mean(dhh * hhat, axis=-1, keepdims=True))
            dpad[r0:r0 + CONV_TILE, :] = dhc
            dcb_ref[...] += jnp.sum(dhc, axis=0, keepdims=True)
            for k in range(CONV_WIDTH):
                prod = dhc * pad[pl.ds(r0 + CONV_PAD - (CONV_WIDTH - 1) + k, CONV_TILE), :]
                dcw_acc[k] += prod.reshape(CONV_TILE // 8, 8, GW).sum(axis=0)
        for r0 in range(0, S, CONV_TILE):
            dhg = jnp.zeros((CONV_TILE, GW), F32)
            for k in range(CONV_WIDTH):
                dhg = dhg + cw_ref[k:k + 1, :] * dpad[pl.ds(r0 + (CONV_WIDTH - 1) - k, CONV_TILE), :]
            a = a_ref[r0:r0 + CONV_TILE, :]
            sg = _sigmoid(gt_ref[r0:r0 + CONV_TILE, :])
            da_ref[r0:r0 + CONV_TILE, :] = (dhg * sg).astype(da_ref.dtype)
            dgt_ref[r0:r0 + CONV_TILE, :] = (dhg * a * sg * (1.0 - sg)).astype(dgt_ref.dtype)

        @pl.when(pl.program_id(0) == B - 1)
        def _():
            for k in range(CONV_WIDTH):
                dcw_ref[k:k + 1, :] = jnp.sum(dcw_acc[k], axis=0, keepdims=True)

    seq_out = pl.BlockSpec((None, S, GW), lambda b: (b, 0, 0))
    return pl.pallas_call(
        body, grid=(B,),
        in_specs=[_seq(S, COL_A), _seq(S, COL_G), pl.BlockSpec((None, S, GW), lambda b: (b, 0, 1)),
                  _full((CONV_WIDTH, GW)), _full((1, GW)), _full((1, GW)), _full((1, GW)), _full((GW, GW)), _full((1, GW))],
        out_specs=[seq_out, seq_out, _full((CONV_WIDTH, GW)), _full((1, GW)), _full((1, GW)), _full((1, GW)), _full((GW, GW)),
                   _full((1, GW))],
        out_shape=[_sds((B, S, GW), BF16), _sds((B, S, GW), BF16), _sds((CONV_WIDTH, GW), F32), _sds((1, GW), F32),
                   _sds((1, GW), F32), _sds((1, GW), F32), _sds((GW, GW), F32), _sds((1, GW), F32)],
        scratch_shapes=[pltpu.VMEM((S + CONV_PAD, GW), F32), pltpu.VMEM((S + CONV_PAD, GW), F32),
                        pltpu.VMEM((CONV_WIDTH, 8, GW), F32)],
        compiler_params=_cp(("arbitrary",)), name="mixb_bwd")(z3, z3, dycat3, cw, cb, lg, lb, pw, pwb)


POOL_PAD = 16


def _pool_window():
    lane = lax.broadcasted_iota(jnp.int32, (1, GW), 1)
    return jnp.where(lane < 64, 2, jnp.where(lane < 128, 4, jnp.where(lane < 192, 8, 16)))


def _pool_sums(pad, r0, base, sign):
    win = _pool_window()
    acc = pad[pl.ds(r0 + base, CONV_TILE), :]
    out = None
    for i in range(1, 16):
        acc = acc + pad[pl.ds(r0 + base + sign * i, CONV_TILE), :]
        if i + 1 in (2, 4, 8, 16):
            out = acc if out is None else jnp.where(win == i + 1, acc, out)
    return out


def _pool_cnt(r0):
    t1 = r0 + 1 + lax.broadcasted_iota(jnp.int32, (CONV_TILE, 1), 0)
    return jnp.minimum(t1, _pool_window()).astype(F32)


def mixd_fwd(z3, wbd, scale):
    B, S, _ = z3.shape

    def body(x_ref, w_ref, sc_ref, y_ref, pad):
        pad[0:POOL_PAD, :] = jnp.zeros((POOL_PAD, GW), F32)
        pad[POOL_PAD:POOL_PAD + S, :] = x_ref[...]
        wv = w_ref[...].astype(BF16)
        for r0 in range(0, S, CONV_TILE):
            mean = _pool_sums(pad, r0, POOL_PAD, -1) / _pool_cnt(r0)
            p = (mean - x_ref[r0:r0 + CONV_TILE, :]).astype(BF16)
            y_ref[r0:r0 + CONV_TILE, :] = (_dot(p, wv) * sc_ref[...]).astype(y_ref.dtype)

    return pl.pallas_call(
        body, grid=(B,), in_specs=[_seq(S, COL_D), _full((GW, GW)), _full((1, GW))],
        out_specs=pl.BlockSpec((None, S, GW), lambda b: (b, 0, 0)), out_shape=_sds((B, S, GW), BF16),
        scratch_shapes=[pltpu.VMEM((S + POOL_PAD, GW), F32)],
        compiler_params=_cp(("parallel",)), name="mixd_fwd")(z3, wbd, scale)


def mixd_bwd(z3, dycat3, wbd, scale):
    B, S, _ = z3.shape

    def body(x_ref, dy_ref, w_ref, sc_ref, dx_ref, dw_ref, dsc_ref, pad, qpad):
        @pl.when(pl.program_id(0) == 0)
        def _():
            dw_ref[...] = jnp.zeros_like(dw_ref)
            dsc_ref[...] = jnp.zeros_like(dsc_ref)

        pad[0:POOL_PAD, :] = jnp.zeros((POOL_PAD, GW), F32)
        pad[POOL_PAD:POOL_PAD + S, :] = x_ref[...]
        qpad[S:S + POOL_PAD, :] = jnp.zeros((POOL_PAD, GW), F32)
        wv = w_ref[...].astype(BF16)
        for r0 in range(0, S, CONV_TILE):
            cnt = _pool_cnt(r0)
            mean = _pool_sums(pad, r0, POOL_PAD, -1) / cnt
            p = (mean - x_ref[r0:r0 + CONV_TILE, :]).astype(BF16)
            dy = dy_ref[r0:r0 + CONV_TILE, :]
            dsc_ref[...] += jnp.sum(dy * _dot(p, wv), axis=0, keepdims=True)
            dyp = (dy * sc_ref[...]).astype(BF16)
            dw_ref[...] += _dot_tn(p, dyp)
            dp = _dot_nt(dyp, wv)
            dx_ref[r0:r0 + CONV_TILE, :] = (-dp).astype(dx_ref.dtype)
            qpad[r0:r0 + CONV_TILE, :] = dp / cnt
        for r0 in range(0, S, CONV_TILE):
            back = _pool_sums(qpad, r0, 0, 1)
            dx_ref[r0:r0 + CONV_TILE, :] = (dx_ref[r0:r0 + CONV_TILE, :].astype(F32) + back).astype(dx_ref.dtype)

    return pl.pallas_call(
        body, grid=(B,),
        in_specs=[_seq(S, COL_D), pl.BlockSpec((None, S, GW), lambda b: (b, 0, 3)), _full((GW, GW)), _full((1, GW))],
        out_specs=[pl.BlockSpec((None, S, GW), lambda b: (b, 0, 0)), _full((GW, GW)), _full((1, GW))],
        out_shape=[_sds((B, S, GW), F32), _sds((GW, GW), F32), _sds((1, GW), F32)],
        scratch_shapes=[pltpu.VMEM((S + POOL_PAD, GW), F32), pltpu.VMEM((S + POOL_PAD, GW), F32)],
        compiler_params=_cp(("arbitrary",)), name="mixd_bwd")(z3, dycat3, wbd, scale)


def cmp_kv_fwd(tbk, tbv, pek, pev, w1k, w2k, w1v, w2v):
    B = tbk.shape[0]

    def body(tbk_ref, tbv_ref, pek_ref, pev_ref, w1k_ref, w2k_ref, w1v_ref, w2v_ref, kc_ref, vc_ref):
        for tb_ref, pe_ref, w1_ref, w2_ref, o_ref in ((tbk_ref, pek_ref, w1k_ref, w2k_ref, kc_ref),
                                                      (tbv_ref, pev_ref, w1v_ref, w2v_ref, vc_ref)):
            pre = _dot((tb_ref[...] + pe_ref[...]).astype(BF16), w1_ref[...].astype(BF16))
            hm = pre * _sigmoid(pre)
            o_ref[...] = _dot(hm.astype(BF16), w2_ref[...].astype(BF16))

    tb_spec = pl.BlockSpec((None, N_CMP, 2048), lambda b: (b, 0, 0))
    o_spec = pl.BlockSpec((None, N_CMP, HEAD_DIM), lambda b: (b, 0, 0))
    return pl.pallas_call(
        body, grid=(B,),
        in_specs=[tb_spec, tb_spec, _full((1, 2048)), _full((1, 2048)), _full((2048, HEAD_DIM)), _full((HEAD_DIM, HEAD_DIM)),
                  _full((2048, HEAD_DIM)), _full((HEAD_DIM, HEAD_DIM))],
        out_specs=[o_spec, o_spec], out_shape=[_sds((B, N_CMP, HEAD_DIM), F32)] * 2,
        compiler_params=_cp(("parallel",)), name="cmp_kv_fwd")(tbk, tbv, pek, pev, w1k, w2k, w1v, w2v)


def cmp_kv_bwd(tbk, tbv, pek, pev, w1k, w2k, w1v, w2v, dkc, dvc):
    B = tbk.shape[0]

    def body(tbk_ref, tbv_ref, pek_ref, pev_ref, w1k_ref, w2k_ref, w1v_ref, w2v_ref, dkc_ref, dvc_ref,
             dk2_ref, dv2_ref, dpek_ref, dpev_ref, dw1k_ref, dw2k_ref, dw1v_ref, dw2v_ref):
        @pl.when(pl.program_id(0) == 0)
        def _():
            for r in (dpek_ref, dpev_ref, dw1k_ref, dw2k_ref, dw1v_ref, dw2v_ref):
                r[...] = jnp.zeros_like(r)

        row0 = lax.broadcasted_iota(jnp.int32, (N_CMP, 1), 0) == 0
        for tb_ref, pe_ref, w1_ref, w2_ref, do_ref, d2_ref, dpe_ref, dw1_ref, dw2_ref in (
                (tbk_ref, pek_ref, w1k_ref, w2k_ref, dkc_ref, dk2_ref, dpek_ref, dw1k_ref, dw2k_ref),
                (tbv_ref, pev_ref, w1v_ref, w2v_ref, dvc_ref, dv2_ref, dpev_ref, dw1v_ref, dw2v_ref)):
            tb = (tb_ref[...] + pe_ref[...]).astype(BF16)
            w1 = w1_ref[...].astype(BF16)
            pre = _dot(tb, w1)
            sg = _sigmoid(pre)
            hm = (pre * sg).astype(BF16)
            do = do_ref[...].astype(BF16)
            dw2_ref[...] += _dot_tn(hm, do)
            dpre = (_dot_nt(do, w2_ref[...].astype(BF16)) * (sg * (1.0 + pre * (1.0 - sg)))).astype(BF16)
            dw1_ref[...] += _dot_tn(tb, dpre)
            dtb = _dot_nt(dpre, w1)
            dpe_ref[...] += jnp.sum(dtb, axis=0, keepdims=True)
            down = jnp.where(row0, 0.0, pltpu.roll(dtb[:, 1024:], 1, 0))
            d2_ref[...] = dtb[:, :1024] + down

    tb_spec = pl.BlockSpec((None, N_CMP, 2048), lambda b: (b, 0, 0))
    c_spec = pl.BlockSpec((None, N_CMP, HEAD_DIM), lambda b: (b, 0, 0))
    d2_spec = pl.BlockSpec((None, N_CMP, 1024), lambda b: (b, 0, 0))
    return pl.pallas_call(
        body, grid=(B,),
        in_specs=[tb_spec, tb_spec, _full((1, 2048)), _full((1, 2048)), _full((2048, HEAD_DIM)), _full((HEAD_DIM, HEAD_DIM)),
                  _full((2048, HEAD_DIM)), _full((HEAD_DIM, HEAD_DIM)), c_spec, c_spec],
        out_specs=[d2_spec, d2_spec, _full((1, 2048)), _full((1, 2048)), _full((2048, HEAD_DIM)), _full((HEAD_DIM, HEAD_DIM)),
                   _full((2048, HEAD_DIM)), _full((HEAD_DIM, HEAD_DIM))],
        out_shape=[_sds((B, N_CMP, 1024), F32)] * 2 + [_sds((1, 2048), F32)] * 2
        + [_sds((2048, HEAD_DIM), F32), _sds((HEAD_DIM, HEAD_DIM), F32)] * 2,
        compiler_params=_cp(("arbitrary",)), name="cmp_kv_bwd")(tbk, tbv, pek, pev, w1k, w2k, w1v, w2v, dkc, dvc)


def _qtile(col):
    return pl.BlockSpec((None, TQ, GW), lambda b, i: (b, i, col // GW))


def _qtile0():
    return pl.BlockSpec((None, TQ, GW), lambda b, i: (b, i, 0))


def _cmp_probs(q, kc, qpos):
    head = _lane_head()
    cend = lax.broadcasted_iota(jnp.int32, (1, N_CMP), 1) * CMP_STRIDE + 31
    cmask = cend <= qpos
    has = qpos >= 31
    out = []
    for h in range(4):
        qm = jnp.where(head == h, q, 0.0).astype(BF16)
        s = jnp.where(cmask, _dot_nt(qm, kc), NEG)
        e = jnp.exp(s - jnp.max(s, axis=-1, keepdims=True))
        p = jnp.where(has, e / jnp.sum(e, axis=-1, keepdims=True), 0.0)
        out.append((qm, p))
    return out


def cmp_attn_fwd(z3, kc4, vc4):
    B, S, _ = z3.shape

    def body(q_ref, kc_ref, vc_ref, o_ref, sel_ref):
        t0 = pl.program_id(1) * TQ
        qpos = t0 + lax.broadcasted_iota(jnp.int32, (TQ, 1), 0)
        head = _lane_head()
        kc, vc = kc_ref[...], vc_ref[...]
        o = jnp.zeros((TQ, GW), F32)
        psum = jnp.zeros((TQ, N_CMP), F32)
        for h, (_, p) in enumerate(_cmp_probs(q_ref[...] * 0.125, kc, qpos)):
            o = o + jnp.where(head == h, _dot(p.astype(BF16), vc), 0.0)
            psum = psum + p
        o_ref[...] = o
        cst = lax.broadcasted_iota(jnp.int32, (N_SLC, N_CMP), 1) * CMP_STRIDE
        jst = lax.broadcasted_iota(jnp.int32, (N_SLC, N_CMP), 0) * 64
        overlap = ((cst <= jst + 63) & (cst + 31 >= jst)).astype(BF16)
        imp = _dot_nt(overlap, psum.astype(BF16))
        qp = t0 + lax.broadcasted_iota(jnp.int32, (1, TQ), 1)
        jj = lax.broadcasted_iota(jnp.int32, (N_SLC, 1), 0)
        cur = lax.shift_right_logical(qp, SLC_BLOCK_SHIFT)
        forced = (jj == 0) | (jj == cur) | (jj == cur - 1)
        score = jnp.where(jj * 64 <= qp, imp + jnp.where(forced, FORCE_BONUS, 0.0), NEG)
        rank = jnp.zeros((N_SLC, TQ), F32)
        for j2 in range(N_SLC):
            sj = score[j2:j2 + 1, :]
            rank = rank + jnp.where((sj > score) | ((sj == score) & (j2 < jj)), 1.0, 0.0)
        sel_ref[...] = jnp.where((rank < SLC_TOPK) & (score > NEG / 2), 1.0, 0.0)

    c_spec = pl.BlockSpec((None, N_CMP, GW), lambda b, i: (b, 0, 0))
    return pl.pallas_call(
        body, grid=(B, S // TQ), in_specs=[_qtile(COL_Q), c_spec, c_spec],
        out_specs=[_qtile0(), pl.BlockSpec((None, N_SLC, TQ), lambda b, i: (b, 0, i))],
        out_shape=[_sds((B, S, GW), F32), _sds((B, N_SLC, S), F32)],
        compiler_params=_cp(("parallel", "parallel")), name="cmp_attn_fwd")(z3, kc4, vc4)


def cmp_attn_bwd(z3, kc4, vc4, do):
    B, S, _ = z3.shape
    nq = S // TQ

    def body(q_ref, kc_ref, vc_ref, do_ref, dq_ref, dkc_ref, dvc_ref):
        qi = pl.program_id(1)

        @pl.when(qi == 0)
        def _():
            dkc_ref[...] = jnp.zeros_like(dkc_ref)
            dvc_ref[...] = jnp.zeros_like(dvc_ref)

        qpos = qi * TQ + lax.broadcasted_iota(jnp.int32, (TQ, 1), 0)
        head = _lane_head()
        kc, vc, do = kc_ref[...], vc_ref[...], do_ref[...]
        dq = jnp.zeros((TQ, GW), F32)
        for h, (qm, p) in enumerate(_cmp_probs(q_ref[...] * 0.125, kc, qpos)):
            dom = jnp.where(head == h, do, 0.0).astype(BF16)
            dp = _dot_nt(dom, vc)
            ds = (p * (dp - jnp.sum(p * dp, axis=-1, keepdims=True))).astype(BF16)
            dq = dq + jnp.where(head == h, _dot(ds, kc), 0.0)
            dkc_ref[...] += _dot_tn(ds, qm)
            dvc_ref[...] += _dot_tn(p.astype(BF16), dom)
        dq_ref[...] = dq * 0.125

        @pl.when(qi == nq - 1)
        def _():
            dkc_ref[...] = _fold_heads(dkc_ref[...])
            dvc_ref[...] = _fold_heads(dvc_ref[...])

    c_spec = pl.BlockSpec((None, N_CMP, GW), lambda b, i: (b, 0, 0))
    return pl.pallas_call(
        body, grid=(B, nq), in_specs=[_qtile(COL_Q), c_spec, c_spec, _qtile0()],
        out_specs=[_qtile0(), c_spec, c_spec],
        out_shape=[_sds((B, S, GW), F32), _sds((B, N_CMP, GW), F32), _sds((B, N_CMP, GW), F32)],
        compiler_params=_cp(("parallel", "arbitrary")), name="cmp_attn_bwd")(z3, kc4, vc4, do)


def _attn_mask(mode, qpos, k0, sel_b):
    kpos = k0 + lax.broadcasted_iota(jnp.int32, (1, TQ), 1)
    mask = kpos <= qpos
    if mode == "win":
        return mask & (kpos > qpos - WIN)
    blk = lax.shift_right_logical(k0 + lax.broadcasted_iota(jnp.int32, (N_SLC, TQ), 1), SLC_BLOCK_SHIFT)
    expand = (blk == lax.broadcasted_iota(jnp.int32, (N_SLC, TQ), 0)).astype(BF16)
    return mask & (_dot_tn(sel_b, expand) > 0.5)


def _attn_lo(mode, qi):
    return jnp.maximum(qi - WIN // TQ, 0) if mode == "win" else 0


def attn_fwd(mode, z3, k4, v4, selT):
    B, S, _ = z3.shape

    def body(q_ref, k_ref, v_ref, sel_ref, o_ref, lse_ref, s_all, m_acc, l_acc, o_acc):
        qi = pl.program_id(1)
        qpos = qi * TQ + lax.broadcasted_iota(jnp.int32, (TQ, 1), 0)
        head = _lane_head()
        q = q_ref[...] * 0.125
        qm = [jnp.where(head == h, q, 0.0).astype(BF16) for h in range(4)]
        sel_b = sel_ref[...].astype(BF16)
        lo, hi = _attn_lo(mode, qi), qi + 1
        m_acc[...] = jnp.full(m_acc.shape, NEG, F32)

        def scores(kb, carry):
            k0 = pl.multiple_of(kb * TQ, TQ)
            kblk = k_ref[pl.ds(k0, TQ), :]
            mask = _attn_mask(mode, qpos, k0, sel_b)
            for h in range(4):
                s = jnp.where(mask, _dot_nt(qm[h], kblk), NEG)
                s_all[h, kb] = s
                m_acc[h] = jnp.maximum(m_acc[h], s)
            return carry

        lax.fori_loop(lo, hi, scores, 0)
        for h in range(4):
            m_acc[h] = jnp.broadcast_to(jnp.max(m_acc[h], axis=-1, keepdims=True), (TQ, TQ))
        l_acc[...] = jnp.zeros_like(l_acc)
        o_acc[...] = jnp.zeros_like(o_acc)

        def weights(kb, carry):
            vblk = v_ref[pl.ds(pl.multiple_of(kb * TQ, TQ), TQ), :]
            for h in range(4):
                p = jnp.exp(s_all[h, kb] - m_acc[h])
                l_acc[h] += p
                o_acc[h] += _dot(p.astype(BF16), vblk)
            return carry

        lax.fori_loop(lo, hi, weights, 0)
        o = jnp.zeros((TQ, GW), F32)
        lse = jnp.zeros((TQ, 128), F32)
        lane = lax.broadcasted_iota(jnp.int32, (1, 128), 1)
        for h in range(4):
            l = jnp.sum(l_acc[h], axis=-1, keepdims=True)
            o = o + jnp.where(head == h, o_acc[h] / l, 0.0)
            lse = jnp.where(lane == h, jnp.max(m_acc[h], axis=-1, keepdims=True) + jnp.log(l), lse)
        o_ref[...] = o
        lse_ref[...] = lse

    kv_spec = pl.BlockSpec((None, S, GW), lambda b, i: (b, 0, 0))
    return pl.pallas_call(
        body, grid=(B, S // TQ),
        in_specs=[_qtile(COL_Q), kv_spec, kv_spec, pl.BlockSpec((None, N_SLC, TQ), lambda b, i: (b, 0, i))],
        out_specs=[_qtile0(), pl.BlockSpec((None, TQ, 128), lambda b, i: (b, i, 0))],
        out_shape=[_sds((B, S, GW), F32), _sds((B, S, 128), F32)],
        scratch_shapes=[pltpu.VMEM((4, S // TQ, TQ, TQ), F32), pltpu.VMEM((4, TQ, TQ), F32), pltpu.VMEM((4, TQ, TQ), F32),
                        pltpu.VMEM((4, TQ, GW), F32)],
        compiler_params=_cp(("parallel", "parallel")), name=mode + "_attn_fwd")(z3, k4, v4, selT)


def attn_bwd(mode, z3, k4, v4, selT, o, lse, do):
    B, S, _ = z3.shape
    nq = S // TQ

    def body(q_ref, k_ref, v_ref, sel_ref, o_ref, lse_ref, do_ref, dq_ref, dk_ref, dv_ref, dq_s):
        qi = pl.program_id(1)

        @pl.when(qi == 0)
        def _():
            dk_ref[...] = jnp.zeros_like(dk_ref)
            dv_ref[...] = jnp.zeros_like(dv_ref)

        qpos = qi * TQ + lax.broadcasted_iota(jnp.int32, (TQ, 1), 0)
        head = _lane_head()
        lane = lax.broadcasted_iota(jnp.int32, (1, 128), 1)
        q = q_ref[...] * 0.125
        do = do_ref[...]
        doo = do * o_ref[...]
        lse = lse_ref[...]
        qm = [jnp.where(head == h, q, 0.0).astype(BF16) for h in range(4)]
        dom = [jnp.where(head == h, do, 0.0).astype(BF16) for h in range(4)]
        delta = [jnp.sum(jnp.where(head == h, doo, 0.0), axis=-1, keepdims=True) for h in range(4)]
        lse_h = [jnp.max(jnp.where(lane == h, lse, NEG), axis=-1, keepdims=True) for h in range(4)]
        sel_b = sel_ref[...].astype(BF16)
        dq_s[...] = jnp.zeros_like(dq_s)

        def step(kb, carry):
            k0 = pl.multiple_of(kb * TQ, TQ)
            kblk = k_ref[pl.ds(k0, TQ), :]
            vblk = v_ref[pl.ds(k0, TQ), :]
            mask = _attn_mask(mode, qpos, k0, sel_b)
            for h in range(4):
                s = _dot_nt(qm[h], kblk)
                p = jnp.where(mask, jnp.exp(s - lse_h[h]), 0.0)
                dp = _dot_nt(dom[h], vblk)
                ds = (p * (dp - delta[h])).astype(BF16)
                dq_s[...] += jnp.where(head == h, _dot(ds, kblk), 0.0)
                dk_ref[pl.ds(k0, TQ), :] += _dot_tn(ds, qm[h])
                dv_ref[pl.ds(k0, TQ), :] += _dot_tn(p.astype(BF16), dom[h])
            return carry

        lax.fori_loop(_attn_lo(mode, qi), qi + 1, step, 0)
        dq_ref[...] = dq_s[...] * 0.125

        @pl.when(qi == nq - 1)
        def _():
            for r0 in range(0, S, TQ):
                dk_ref[r0:r0 + TQ, :] = _fold_heads(dk_ref[r0:r0 + TQ, :])
                dv_ref[r0:r0 + TQ, :] = _fold_heads(dv_ref[r0:r0 + TQ, :])

    kv_spec = pl.BlockSpec((None, S, GW), lambda b, i: (b, 0, 0))
    return pl.pallas_call(
        body, grid=(B, nq),
        in_specs=[_qtile(COL_Q), kv_spec, kv_spec, pl.BlockSpec((None, N_SLC, TQ), lambda b, i: (b, 0, i)), _qtile0(),
                  pl.BlockSpec((None, TQ, 128), lambda b, i: (b, i, 0)), _qtile0()],
        out_specs=[_qtile0(), kv_spec, kv_spec],
        out_shape=[_sds((B, S, GW), F32)] * 3,
        scratch_shapes=[pltpu.VMEM((TQ, GW), F32)],
        compiler_params=_cp(("parallel", "arbitrary")), name=mode + "_attn_bwd")(z3, k4, v4, selT, o, lse, do)


def _gate_expand(b):
    r = lax.broadcasted_iota(jnp.int32, (128, GW), 0)
    hl = lax.shift_right_logical(lax.broadcasted_iota(jnp.int32, (128, GW), 1), 6)
    return (r == 3 * hl + b).astype(F32)


def combine_fwd(z3, o_cmp, o_slc, o_win):
    B, S, _ = z3.shape

    def body(gl_ref, oc_ref, os_ref, ow_ref, y_ref):
        g = _sigmoid(gl_ref[...])
        y = jnp.zeros((TQ, GW), F32)
        for b, o_ref in enumerate((oc_ref, os_ref, ow_ref)):
            y = y + jnp.dot(g, _gate_expand(b), precision=HI, preferred_element_type=F32) * o_ref[...]
        y_ref[...] = y.astype(y_ref.dtype)

    return pl.pallas_call(
        body, grid=(B, S // TQ),
        in_specs=[pl.BlockSpec((None, TQ, 128), lambda b, i: (b, i, COL_GL // 128)), _qtile0(), _qtile0(), _qtile0()],
        out_specs=_qtile0(), out_shape=_sds((B, S, GW), BF16),
        compiler_params=_cp(("parallel", "parallel")), name="combine_fwd")(z3, o_cmp, o_slc, o_win)


def combine_bwd(z3, o_cmp, o_slc, o_win, dycat3):
    B, S, _ = z3.shape

    def body(gl_ref, oc_ref, os_ref, ow_ref, dy_ref, dc_ref, ds_ref, dw_ref, dgl_ref):
        g = _sigmoid(gl_ref[...])
        dy = dy_ref[...]
        dg = jnp.zeros((TQ, 128), F32)
        for b, (o_ref, d_ref) in enumerate(((oc_ref, dc_ref), (os_ref, ds_ref), (ow_ref, dw_ref))):
            ex = _gate_expand(b)
            d_ref[...] = jnp.dot(g, ex, precision=HI, preferred_element_type=F32) * dy
            dg = dg + lax.dot_general(dy * o_ref[...], ex, (((1,), (1,)), ((), ())), precision=HI, preferred_element_type=F32)
        dgl_ref[...] = dg * g * (1.0 - g)

    gl_spec = pl.BlockSpec((None, TQ, 128), lambda b, i: (b, i, COL_GL // 128))
    return pl.pallas_call(
        body, grid=(B, S // TQ),
        in_specs=[gl_spec, _qtile0(), _qtile0(), _qtile0(), pl.BlockSpec((None, TQ, GW), lambda b, i: (b, i, 2))],
        out_specs=[_qtile0(), _qtile0(), _qtile0(), pl.BlockSpec((None, TQ, 128), lambda b, i: (b, i, 0))],
        out_shape=[_sds((B, S, GW), F32)] * 3 + [_sds((B, S, 128), F32)],
        compiler_params=_cp(("parallel", "parallel")), name="combine_bwd")(z3, o_cmp, o_slc, o_win, dycat3)


def assemble_dz(du, dv, da, dgt, dq_c, dq_s, dq_w, dd, dkv, dgl):
    T = du.shape[0]

    def body(du_ref, dv_ref, da_ref, dgt_ref, dqc_ref, dqs_ref, dqw_ref, dd_ref, dkv_ref, dgl_ref, o_ref):
        o_ref[:, COL_U:COL_U + GW] = du_ref[...]
        o_ref[:, COL_V:COL_V + GW] = dv_ref[...]
        o_ref[:, COL_A:COL_A + GW] = da_ref[...]
        o_ref[:, COL_G:COL_G + GW] = dgt_ref[...]
        o_ref[:, COL_Q:COL_Q + GW] = (dqc_ref[...] + dqs_ref[...] + dqw_ref[...]).astype(BF16)
        o_ref[:, COL_D:COL_D + GW] = dd_ref[...].astype(BF16)
        o_ref[:, COL_KV:COL_KV + 384] = dkv_ref[...].astype(BF16)
        o_ref[:, COL_GL:COL_GL + 128] = dgl_ref[...].astype(BF16)

    specs = [_rows(ROW_TILE, GW)] * 8 + [_rows(ROW_TILE, 384), _rows(ROW_TILE, 128)]
    return pl.pallas_call(body, grid=(T // ROW_TILE,), in_specs=specs, out_specs=_rows(ROW_TILE, ZW),
                          out_shape=_sds((T, ZW), BF16), compiler_params=_cp(("parallel",)),
                          name="assemble_dz")(du, dv, da, dgt, dq_c, dq_s, dq_w, dd, dkv, dgl)


def _my_pos():
    return lax.axis_index("x"), lax.axis_index("y"), lax.axis_index("c")


def _peer(k):
    x, y, c = _my_pos()
    return ((1 - x) if k & 4 else x, (1 - y) if k & 2 else y, (1 - c) if k & 1 else c)


def _index(pos):
    return 4 * pos[0] + 2 * pos[1] + pos[2]


_HBM = pl.BlockSpec(memory_space=pltpu.HBM)


def all_gather_rows(arrs, name):
    n = len(arrs)

    def body(*refs):
        ins, outs = refs[:n], refs[n:2 * n]
        send, recv, loc = refs[2 * n:]
        me = _index(_my_pos())
        copies = []
        for a in range(n):
            r = ins[a].shape[1]
            mine = outs[a].at[:, pl.ds(me * r, r), :]
            lc = pltpu.make_async_copy(ins[a], mine, loc.at[a])
            lc.start()
            copies.append(lc)
            for k in range(1, N_DEV):
                cp = pltpu.make_async_remote_copy(src_ref=ins[a], dst_ref=mine, send_sem=send.at[a, k - 1],
                                                  recv_sem=recv.at[a, k - 1], device_id=_peer(k), device_id_type=MESH)
                cp.start()
                copies.append(cp)
        for cp in copies:
            cp.wait()

    return pl.pallas_call(
        body, in_specs=[_HBM] * n, out_specs=[_HBM] * n,
        out_shape=[_sds((a.shape[0], N_DEV * a.shape[1], a.shape[2]), a.dtype) for a in arrs],
        scratch_shapes=[pltpu.SemaphoreType.DMA((n, N_DEV - 1)), pltpu.SemaphoreType.DMA((n, N_DEV - 1)),
                        pltpu.SemaphoreType.DMA((n,))],
        name=name)(*arrs)


_SEM = pl.BlockSpec(memory_space=pltpu.SEMAPHORE)
_EFFECT = pltpu.SideEffectType.DATAFLOW_SIDE_EFFECTING


def _exchange_copies(kinds, srcs, lands, send, recv):
    me = _index(_my_pos())
    out = []
    for a, kind in enumerate(kinds):
        for k in range(1, N_DEV):
            peer = _peer(k)
            if kind == "gather":
                r = srcs[a].shape[1]
                src, dst = srcs[a], lands[a].at[:, pl.ds(me * r, r), :]
            else:
                r = srcs[a].shape[1] // N_DEV
                src, dst = srcs[a].at[:, pl.ds(_index(peer) * r, r), :], lands[a].at[me]
            sem = a * (N_DEV - 1) + k - 1
            out.append(pltpu.make_async_remote_copy(src_ref=src, dst_ref=dst, send_sem=send.at[sem], recv_sem=recv.at[sem],
                                                    device_id=peer, device_id_type=MESH))
    return out


def _land_with_own(kind, src):
    me = _index(_my_pos())
    if kind == "gather":
        _, r, C = src.shape
        return lax.dynamic_update_slice(lax.empty((1, N_DEV * r, C), src.dtype), src, (0, me * r, 0))
    _, r8, C = src.shape
    r = r8 // N_DEV
    own = lax.dynamic_slice(src, (0, me * r, 0), (1, r, C))
    return lax.dynamic_update_slice(lax.empty((N_DEV, 1, r, C), src.dtype), own[None], (me, 0, 0, 0))


def exchange_start(kinds, srcs, name):
    n = len(srcs)
    lands = [_land_with_own(k, s) for k, s in zip(kinds, srcs)]

    def body(*refs):
        s, l = refs[:n], refs[n:2 * n]
        send, recv = refs[2 * n], refs[2 * n + 1]
        for cp in _exchange_copies(kinds, s, l, send, recv):
            cp.start()
        refs[-1][...] = jnp.zeros((8, 128), F32)

    hbm = [pltpu.HBM(a.shape, a.dtype) for a in srcs + lands]
    outs = pl.pallas_call(
        body, name=name,
        out_shape=(pltpu.SemaphoreType.DMA((n * (N_DEV - 1),)), pltpu.SemaphoreType.DMA((n * (N_DEV - 1),)), *hbm,
                   _sds((8, 128), F32)),
        in_specs=[_HBM] * (2 * n), out_specs=(_SEM, _SEM, *([_HBM] * (2 * n)), pl.BlockSpec(memory_space=pltpu.VMEM)),
        input_output_aliases={i: 2 + i for i in range(2 * n)},
        compiler_params=pltpu.CompilerParams(has_side_effects=_EFFECT),
    )(*[pltpu.with_memory_space_constraint(a, pltpu.HBM) for a in srcs + lands])
    return outs[0], outs[1], list(outs[2:2 + n]), list(outs[2 + n:2 + 2 * n]), outs[-1]


def exchange_wait(kinds, started, after, name):
    send, recv, srcs, lands, _ = started
    n = len(srcs)

    def body(*refs):
        s, l = refs[:n], refs[n:2 * n]
        for cp in _exchange_copies(kinds, s, l, refs[2 * n], refs[2 * n + 1]):
            cp.wait_send()
            cp.wait_recv()

    outs = pl.pallas_call(
        body, name=name, out_shape=[pltpu.HBM(a.shape, a.dtype) for a in srcs + lands],
        in_specs=[_HBM] * (2 * n) + [_SEM, _SEM, pl.BlockSpec(memory_space=pl.ANY)], out_specs=[_HBM] * (2 * n),
        input_output_aliases={i: i for i in range(2 * n)},
        compiler_params=pltpu.CompilerParams(has_side_effects=_EFFECT),
    )(*srcs, *lands, send, recv, after)
    return list(outs[n:])


def sum_slots(land, name):
    _, L, r, C = land.shape
    tr = _tile(r, 256, 16)

    def body(x_ref, o_ref):
        acc = x_ref[0].astype(F32)
        for s in range(1, N_DEV):
            acc = acc + x_ref[s].astype(F32)
        o_ref[...] = acc

    return pl.pallas_call(
        body, grid=(L, r // tr), in_specs=[pl.BlockSpec((N_DEV, None, tr, C), lambda l, i: (0, l, i, 0))],
        out_specs=pl.BlockSpec((None, tr, C), lambda l, i: (l, i, 0)), out_shape=_sds((L, r, C), F32),
        compiler_params=_cp(("parallel", "parallel")), name=name)(land)


def all_reduce_flat(flat, name):
    R = flat.shape[0]
    (gathered,) = all_gather_rows([flat[None]], name + "_gather")
    return sum_slots(gathered.reshape(N_DEV, 1, R, 128), name + "_sum")[0]


def pack_flat(arrs):
    flat = jnp.concatenate([a.reshape(-1).astype(F32) for a in arrs])
    n = flat.shape[0]
    total = -(-n // 2048) * 2048
    return jnp.pad(flat, (0, total - n)).reshape(total // 128, 128)


def unpack_flat(flat, shapes):
    v = flat.reshape(-1)
    out, off = [], 0
    for s in shapes:
        n = int(np.prod(s))
        out.append(v[off:off + n].reshape(s))
        off += n
    return out


def adamw(w, g, m, v, name):
    shape = w.shape
    C = shape[-1]
    R = int(np.prod(shape)) // C
    tr = _tile(R, 128, 8)
    c1 = 1.0 - ADAM_B1 ** ADAM_STEP
    c2 = 1.0 - ADAM_B2 ** ADAM_STEP

    def body(w_ref, g_ref, m_ref, v_ref, d_ref, nm_ref, nv_ref):
        g = g_ref[...]
        m2 = ADAM_B1 * m_ref[...] + (1.0 - ADAM_B1) * g
        v2 = ADAM_B2 * v_ref[...] + (1.0 - ADAM_B2) * (g * g)
        nm_ref[...] = m2
        nv_ref[...] = v2
        d_ref[...] = -ADAM_LR * ((m2 / c1) / (jnp.sqrt(v2 / c2) + ADAM_EPS) + ADAM_WD * w_ref[...])

    spec = pl.BlockSpec((tr, C), lambda i: (i, 0))
    outs = pl.pallas_call(body, grid=(R // tr,), in_specs=[spec] * 4, out_specs=[spec] * 3,
                          out_shape=[_sds((R, C), F32)] * 3, compiler_params=_cp(("parallel",)), name=name)(
        w.reshape(R, C), g.reshape(R, C), m.reshape(R, C), v.reshape(R, C))
    return [o.reshape(shape) for o in outs]


def _bexp(sg_b):
    return jnp.repeat(sg_b.T, HEAD_DIM, axis=1)


def _block_diag(pool_w):
    out = jnp.zeros((GW, GW), F32)
    for i in range(4):
        out = out.at[i * 64:(i + 1) * 64, i * 64:(i + 1) * 64].set(pool_w[i])
    return out


def _cmp_rows(t):
    B, S, _ = t.shape
    t2 = t.reshape(B, S // CMP_STRIDE, CMP_STRIDE * HEAD_DIM)
    nxt = jnp.concatenate([t2[:, 1:], jnp.zeros_like(t2[:, :1])], axis=1)
    return jnp.concatenate([t2, nxt], axis=-1)


def _tile4(t):
    return jnp.tile(t, (1, 1, 4)).astype(BF16)


def layer_fwd(x, p, B, S):
    T = B * S
    sv = {"x0": x}
    h1 = rms_fwd(x, p["g_pre_mix"], "rms_pre_mix")
    z = mm(h1, p["w_in"], name="mm_in")
    z3 = z.reshape(B, S, ZW)
    ya = mixa_fwd(z, p["sg_ln_g"], p["sg_w"], p["bexp"])
    yb = mixb_fwd(z3, p["cv_w"], p["cv_b"], p["cv_ln_g"], p["cv_ln_b"], p["cv_pw"], p["cv_pw_b"])
    kv = z3[:, :, COL_KV:COL_KV + 384]
    ks = [kv[:, :, i * 64:(i + 1) * 64] for i in range(6)]
    tbk, tbv = _cmp_rows(ks[0]), _cmp_rows(ks[1])
    kc, vc = cmp_kv_fwd(tbk, tbv, p["cmp_pos_k"], p["cmp_pos_v"], p["cmp_w1_k"], p["cmp_w2_k"], p["cmp_w1_v"], p["cmp_w2_v"])
    kc4, vc4 = _tile4(kc), _tile4(vc)
    ks4, vs4, kw4, vw4 = _tile4(ks[2]), _tile4(ks[3]), _tile4(ks[4]), _tile4(ks[5])
    o_cmp, selT = cmp_attn_fwd(z3, kc4, vc4)
    o_slc, lse_slc = attn_fwd("slc", z3, ks4, vs4, selT)
    o_win, lse_win = attn_fwd("win", z3, kw4, vw4, selT)
    yc = combine_fwd(z3, o_cmp, o_slc, o_win)
    yd = mixd_fwd(z3, p["pool_bd"], p["pool_scale"])
    ycat = jnp.concatenate([ya, yb.reshape(T, GW), yc.reshape(T, GW), yd.reshape(T, GW)], axis=-1)
    mix = mm(ycat, p["w_out"], name="mm_out")
    x1 = rms_post_fwd(x, mix, p["g_post_mix"], "rms_post_mix")
    h2 = rms_fwd(x1, p["g_pre_ffn"], "rms_pre_ffn")
    gu = mm(h2, p["w_gu"], blk="n", name="mm_gu")
    gu4 = gu.reshape(2, 4, T, FFN_BLK)
    a3 = swiglu_fwd(gu4)
    f = mm(a3, p["w_down"], blk="k", name="mm_down")
    x2 = rms_post_fwd(x1, f, p["g_post_ffn"], "rms_post_ffn")
    sv.update(h1=h1, z=z, tbk=tbk, tbv=tbv, kc4=kc4, vc4=vc4, ks4=ks4, vs4=vs4, kw4=kw4, vw4=vw4, o_cmp=o_cmp, selT=selT,
              o_slc=o_slc, lse_slc=lse_slc, o_win=o_win, lse_win=lse_win, ycat=ycat, mix=mix, x1=x1, h2=h2, gu4=gu4, a3=a3, f=f)
    return x2, sv


def _unfold(t4):
    return t4[:, :, :HEAD_DIM]


def layer_bwd_ffn(dx2, p, sv, B, S):
    T = B * S
    gb, gs = {}, {}
    df, gs["g_post_ffn"] = rms_bwd(sv["f"], p["g_post_ffn"], dx2, None, BF16, "rms_post_ffn_bwd")
    da3 = mm(df, p["w_down"], tb=True, blk="n", name="mm_down_dx")
    gb["w_down"] = mm(sv["a3"], df, ta=True, blk="m", out_dtype=BF16, name="mm_down_dw")
    dgu4 = swiglu_bwd(sv["gu4"], da3)
    dgu = dgu4.reshape(N_DEV, T, FFN_BLK)
    dh2 = mm(dgu, p["w_gu"], tb=True, blk="k", name="mm_gu_dx")
    gb["w_gu"] = mm(sv["h2"], dgu, ta=True, blk="n", out_dtype=BF16, name="mm_gu_dw")
    dx1, gs["g_pre_ffn"] = rms_bwd(sv["x1"], p["g_pre_ffn"], dh2, dx2, F32, "rms_pre_ffn_bwd")
    gb["w_gu"] = gb["w_gu"].reshape(1, N_DEV * D_MODEL, FFN_BLK)
    gb["w_down"] = gb["w_down"].reshape(1, FFN_HIDDEN, D_MODEL)
    return dx1, gb, gs


def layer_bwd_mix(dx1, p, sv, B, S):
    T = B * S
    gb, gs = {}, {}
    dmix, gs["g_post_mix"] = rms_bwd(sv["mix"], p["g_post_mix"], dx1, None, BF16, "rms_post_mix_bwd")
    dycat = mm(dmix, p["w_out"], tb=True, name="mm_out_dx")
    gb["w_out"] = mm(sv["ycat"], dmix, ta=True, out_dtype=BF16, name="mm_out_dw")
    dycat3 = dycat.reshape(B, S, D_MODEL)
    z = sv["z"]
    z3 = z.reshape(B, S, ZW)
    du, dv, gs["sg_w"], db, gs["sg_ln_g"] = mixa_bwd(z, dycat, p["sg_ln_g"], p["sg_w"], p["bexp"])
    gs["sg_b"] = db[:, :4].T
    (da, dgt, gs["cv_w"], gs["cv_b"], gs["cv_ln_g"], gs["cv_ln_b"], gpw, gs["cv_pw_b"]) = mixb_bwd(
        z3, dycat3, p["cv_w"], p["cv_b"], p["cv_ln_g"], p["cv_ln_b"], p["cv_pw"], p["cv_pw_b"])
    gb["cv_pw"] = gpw.astype(BF16)
    dd, dwbd, gs["pool_scale"] = mixd_bwd(z3, dycat3, p["pool_bd"], p["pool_scale"])
    gs["pool_w"] = jnp.stack([dwbd[i * 64:(i + 1) * 64, i * 64:(i + 1) * 64] for i in range(4)])
    do_c, do_s, do_w, dgl = combine_bwd(z3, sv["o_cmp"], sv["o_slc"], sv["o_win"], dycat3)
    dq_s, dks4, dvs4 = attn_bwd("slc", z3, sv["ks4"], sv["vs4"], sv["selT"], sv["o_slc"], sv["lse_slc"], do_s)
    dq_w, dkw4, dvw4 = attn_bwd("win", z3, sv["kw4"], sv["vw4"], sv["selT"], sv["o_win"], sv["lse_win"], do_w)
    dq_c, dkc4, dvc4 = cmp_attn_bwd(z3, sv["kc4"], sv["vc4"], do_c)
    (dk2, dv2, gs["cmp_pos_k"], gs["cmp_pos_v"], gw1k, gs["cmp_w2_k"], gw1v, gs["cmp_w2_v"]) = cmp_kv_bwd(
        sv["tbk"], sv["tbv"], p["cmp_pos_k"], p["cmp_pos_v"], p["cmp_w1_k"], p["cmp_w2_k"], p["cmp_w1_v"], p["cmp_w2_v"],
        _unfold(dkc4), _unfold(dvc4))
    gb["cmp_w1_k"], gb["cmp_w1_v"] = gw1k.astype(BF16), gw1v.astype(BF16)
    dkv = jnp.concatenate([dk2.reshape(B, S, HEAD_DIM), dv2.reshape(B, S, HEAD_DIM), _unfold(dks4), _unfold(dvs4),
                           _unfold(dkw4), _unfold(dvw4)], axis=-1).reshape(T, 384)
    dz = assemble_dz(du, dv, da.reshape(T, GW), dgt.reshape(T, GW), dq_c.reshape(T, GW), dq_s.reshape(T, GW),
                     dq_w.reshape(T, GW), dd.reshape(T, GW), dkv, dgl.reshape(T, 128))
    dh1 = mm(dz, p["w_in"], tb=True, name="mm_in_dx")
    gb["w_in"] = mm(sv["h1"], dz, ta=True, out_dtype=BF16, name="mm_in_dw")
    dx0, gs["g_pre_mix"] = rms_bwd(sv["x0"], p["g_pre_mix"], dh1, dx1, F32, "rms_pre_mix_bwd")
    return dx0, gb, gs


SMALL = ["g_pre_mix", "g_post_mix", "g_pre_ffn", "g_post_ffn", "sg_ln_g", "sg_w", "sg_b", "cv_w", "cv_b", "cv_ln_g", "cv_ln_b",
         "cv_pw_b", "cmp_pos_k", "cmp_pos_v", "cmp_w2_k", "cmp_w2_v", "pool_w", "pool_scale"]
BIG = ["w_in", "w_out", "w_gu", "w_down", "cmp_w1_k", "cmp_w1_v", "cv_pw"]
NAMES = ["g_pre_mix", "g_post_mix", "g_pre_ffn", "g_post_ffn", "w_in", "sg_ln_g", "sg_w", "sg_b", "cv_w", "cv_b", "cv_ln_g",
         "cv_ln_b", "cv_pw", "cv_pw_b", "cmp_pos_k", "cmp_pos_v", "cmp_w1_k", "cmp_w2_k", "cmp_w1_v", "cmp_w2_v", "pool_w",
         "pool_scale", "w_out", "ffn_w_gu", "ffn_w_down"]


def kernel(x, g_pre_mix, g_post_mix, g_pre_ffn, g_post_ffn, w_in, sg_ln_g, sg_w, sg_b, cv_w, cv_b, cv_ln_g, cv_ln_b, cv_pw, cv_pw_b, cmp_pos_k, cmp_pos_v, cmp_w1_k, cmp_w2_k, cmp_w1_v, cmp_w2_v, pool_w, pool_scale, w_out, ffn_w_gu, ffn_w_down, loss_target, m_g_pre_mix, m_g_post_mix, m_g_pre_ffn, m_g_post_ffn, m_w_in, m_sg_ln_g, m_sg_w, m_sg_b, m_cv_w, m_cv_b, m_cv_ln_g, m_cv_ln_b, m_cv_pw, m_cv_pw_b, m_cmp_pos_k, m_cmp_pos_v, m_cmp_w1_k, m_cmp_w2_k, m_cmp_w1_v, m_cmp_w2_v, m_pool_w, m_pool_scale, m_w_out, m_ffn_w_gu, m_ffn_w_down, v_g_pre_mix, v_g_post_mix, v_g_pre_ffn, v_g_post_ffn, v_w_in, v_sg_ln_g, v_sg_w, v_sg_b, v_cv_w, v_cv_b, v_cv_ln_g, v_cv_ln_b, v_cv_pw, v_cv_pw_b, v_cmp_pos_k, v_cmp_pos_v, v_cmp_w1_k, v_cmp_w2_k, v_cmp_w1_v, v_cmp_w2_v, v_pool_w, v_pool_scale, v_w_out, v_ffn_w_gu, v_ffn_w_down):
    args = dict(locals())
    W = {n: args[n] for n in NAMES}
    M = {n: args["m_" + n] for n in NAMES}
    V = {n: args["v_" + n] for n in NAMES}
    B, S, _ = x.shape
    T = B * S
    L = w_in.shape[0]
    me = _index(_my_pos())
    cpd = GW // N_DEV

    cvw_slab = lax.dynamic_update_slice(jnp.zeros((L, CONV_WIDTH, GW), F32), cv_w, (0, 0, me * cpd))
    cvw_full = all_reduce_flat(pack_flat([cvw_slab]), "gather_cvw")
    (cvw_full,) = unpack_flat(cvw_full, [(L, CONV_WIDTH, GW)])
    shards = dict(zip(BIG, [pack_cols(w_in).astype(BF16), w_out.astype(BF16), ffn_w_gu.astype(BF16), ffn_w_down.astype(BF16),
                            cmp_w1_k.astype(BF16), cmp_w1_v.astype(BF16), cv_pw.astype(BF16)]))
    gather_kinds = ["gather"] * len(BIG)

    def start_gather(l):
        return exchange_start(gather_kinds, [shards[n][l][None] for n in BIG], "gather_start_%d" % l)

    def layer_params(l, full):
        p = {n: full[n] for n in BIG}
        for n in ("g_pre_mix", "g_post_mix", "g_pre_ffn", "g_post_ffn", "sg_ln_g", "cv_b", "cv_ln_g", "cv_ln_b", "cv_pw_b",
                  "pool_scale"):
            p[n] = W[n][l][None, :]
        p["sg_w"] = sg_w[l]
        p["bexp"] = _bexp(sg_b[l])
        p["cv_w"] = cvw_full[l]
        p["cmp_pos_k"] = cmp_pos_k[l].reshape(1, 2048)
        p["cmp_pos_v"] = cmp_pos_v[l].reshape(1, 2048)
        p["cmp_w2_k"], p["cmp_w2_v"] = cmp_w2_k[l], cmp_w2_v[l]
        p["pool_bd"] = _block_diag(pool_w[l])
        return p

    xs = x.reshape(T, D_MODEL)
    params, saved = [], []
    started = start_gather(0)
    for l in range(L):
        ahead = start_gather(l + 1) if l + 1 < L else None
        arrived = exchange_wait(gather_kinds, started, (ahead or started)[4], "gather_wait_%d" % l)
        full = {n: a[0] for n, a in zip(BIG, arrived)}
        full["w_gu"] = full["w_gu"].reshape(N_DEV, D_MODEL, FFN_BLK)
        full["w_down"] = full["w_down"].reshape(4, FFN_BLK, D_MODEL)
        p = layer_params(l, full)
        xs, sv = layer_fwd(xs, p, B, S)
        params.append(p)
        saved.append(sv)
        started = ahead
    dy, lpart = loss_fwd_bwd(xs, loss_target.reshape(T, D_MODEL))
    loss = lax.psum(lpart[0, 0], ("x", "y", "c"))

    ffn_big, mix_big = ["w_gu", "w_down"], ["w_in", "w_out", "cmp_w1_k", "cmp_w1_v", "cv_pw"]
    ffn_kinds, mix_kinds = ["scatter"] * len(ffn_big), ["scatter"] * len(mix_big) + ["gather"]
    pending, token = [], None
    for l in reversed(range(L)):
        p = dict(params[l])
        if token is not None:
            p["g_post_ffn"] = p["g_post_ffn"] + token[0, 0]
        dy, gb_ffn, gs = layer_bwd_ffn(dy, p, saved[l], B, S)
        st_ffn = exchange_start(ffn_kinds, [gb_ffn[n] for n in ffn_big], "scatter_ffn_start_%d" % l)
        p["g_post_mix"] = p["g_post_mix"] + st_ffn[4][0, 0]
        dy, gb_mix, gs_mix = layer_bwd_mix(dy, p, saved[l], B, S)
        gs.update(gs_mix)
        small_shapes = [tuple(gs[n].shape) for n in SMALL]
        st_mix = exchange_start(mix_kinds, [gb_mix[n][None] for n in mix_big] + [pack_flat([gs[n] for n in SMALL])[None]],
                                "scatter_mix_start_%d" % l)
        token = st_mix[4]
        pending.append((l, st_ffn, st_mix))
    grad_x = dy.reshape(B, S, D_MODEL)

    lands = {}
    for l, st_ffn, st_mix in pending:
        got = exchange_wait(ffn_kinds, st_ffn, dy, "scatter_ffn_wait_%d" % l) \
            + exchange_wait(mix_kinds, st_mix, dy, "scatter_mix_wait_%d" % l)
        lands[l] = dict(zip(ffn_big + mix_big + ["small"], got))
    grads = {}
    for n in BIG:
        grads[n] = sum_slots(jnp.concatenate([lands[l][n] for l in range(L)], axis=1), "sum_" + n)
    grads["w_in"] = unpack_cols(grads["w_in"])
    grads["ffn_w_gu"], grads["ffn_w_down"] = grads.pop("w_gu"), grads.pop("w_down")
    rows = lands[0]["small"].shape[1] // N_DEV
    reduced = sum_slots(jnp.concatenate([lands[l]["small"].reshape(N_DEV, 1, rows, 128) for l in range(L)], axis=1), "sum_small")
    per_layer = [unpack_flat(reduced[l], small_shapes) for l in range(L)]
    for i, n in enumerate(SMALL):
        g = jnp.stack([per_layer[l][i] for l in range(L)])
        grads[n] = g.reshape(W[n].shape) if n != "cv_w" else g
    grads["cv_w"] = lax.dynamic_slice(grads["cv_w"], (0, 0, me * cpd), (L, CONV_WIDTH, cpd))

    delta, new_m, new_v = {}, {}, {}
    big_names = ["w_in", "w_out", "ffn_w_gu", "ffn_w_down", "cmp_w1_k", "cmp_w1_v", "cv_pw"]
    for n in big_names:
        delta[n], new_m[n], new_v[n] = adamw(W[n], grads[n], M[n], V[n], "adamw_" + n)
    shapes = [W[n].shape for n in SMALL]
    packed = adamw(pack_flat([W[n] for n in SMALL]), pack_flat([grads[n] for n in SMALL]),
                   pack_flat([M[n] for n in SMALL]), pack_flat([V[n] for n in SMALL]), "adamw_small")
    for out, flat in zip((delta, new_m, new_v), packed):
        for n, a in zip(SMALL, unpack_flat(flat, shapes)):
            out[n] = a

    return (loss, grad_x, *[grads[n] for n in NAMES], *[delta[n] for n in NAMES], *[new_m[n] for n in NAMES],
            *[new_v[n] for n in NAMES])
```

```python
import numpy as np
import jax
import jax.numpy as jnp
from jax import lax
from jax.experimental import pallas as pl
from jax.experimental.pallas import tpu as pltpu

F32 = jnp.float32
BF16 = jnp.bfloat16
HI = lax.Precision.HIGHEST

D_MODEL = 1024
GW = 256
HEAD_DIM = 64
ZW = 2048
SG_CHUNK = 128
CONV_WIDTH = 31
CONV_PAD = 32
CMP_STRIDE = 16
N_CMP = 128
SLC_BLOCK_SHIFT = 6
N_SLC = 32
SLC_TOPK = 8
WIN = 512
NEG = -1e30
FORCE_BONUS = 1e4
RMS_EPS = 1e-6
LN_EPS = 1e-5
FFN_HIDDEN = 2816
N_DEV = 8
FFN_BLK = 2 * FFN_HIDDEN // N_DEV
TQ = 256
ROW_TILE = 512
CONV_TILE = 256
VMEM_LIMIT = 56 * 1024 * 1024
MESH = pl.DeviceIdType.MESH

ADAM_LR, ADAM_B1, ADAM_B2, ADAM_EPS, ADAM_WD, ADAM_STEP = 0.001, 0.9, 0.999, 1e-08, 0.01, 10

COL_U, COL_V, COL_A, COL_G, COL_Q, COL_D, COL_KV, COL_GL = 0, 256, 512, 768, 1024, 1280, 1536, 1920


def _sds(shape, dtype):
    return jax.ShapeDtypeStruct(shape, dtype)


def _cp(sem=None):
    return pltpu.CompilerParams(dimension_semantics=sem, vmem_limit_bytes=VMEM_LIMIT)


def _tile(n, target, q=128):
    best = None
    for t in range(q, min(n, target) + 1, q):
        if n % t == 0:
            best = t
    return best or n


def _full(shape):
    nd = len(shape)
    return pl.BlockSpec(shape, lambda *_: (0,) * nd)


def _sigmoid(x):
    return jax.nn.sigmoid(x)


def _dot(a, b):
    return jnp.dot(a, b, preferred_element_type=F32)


def _dot_nt(a, b):
    return lax.dot_general(a, b, (((1,), (1,)), ((), ())), preferred_element_type=F32)


def _dot_tn(a, b):
    return lax.dot_general(a, b, (((0,), (0,)), ((), ())), preferred_element_type=F32)


def _lane_head(width=GW):
    return lax.shift_right_logical(lax.broadcasted_iota(jnp.int32, (1, width), 1), 6)


def _fold_heads(x):
    return x + pltpu.roll(x, 64, 1) + pltpu.roll(x, 128, 1) + pltpu.roll(x, 192, 1)


def pack_cols(w):
    pad = jnp.zeros(w.shape[:-1] + (ZW - 1932,), w.dtype)
    return jnp.concatenate([w[..., :1280], w[..., 1676:1932], w[..., 1280:1664], w[..., 1664:1676], pad], axis=-1)


def unpack_cols(wp):
    return jnp.concatenate([wp[..., :1280], wp[..., 1536:1920], wp[..., 1920:1932], wp[..., 1280:1536]], axis=-1)


def mm(a, b, *, ta=False, tb=False, blk=None, out_dtype=F32, name, tm=1024, tn=1024, tk=1024):
    a_dims = ("k", "m") if ta else ("m", "k")
    b_dims = ("n", "k") if tb else ("k", "n")
    a3, b3, o3 = blk in a_dims and blk is not None, blk in b_dims and blk is not None, blk in ("m", "n")
    size = {}
    size[a_dims[0]], size[a_dims[1]] = a.shape[-2:]
    size[b_dims[0]], size[b_dims[1]] = b.shape[-2:]
    nb = a.shape[0] if a3 else (b.shape[0] if b3 else 1)
    tile = {"m": _tile(size["m"], tm), "n": _tile(size["n"], tn), "k": _tile(size["k"], tk)}
    grid = {d: size[d] // tile[d] for d in "mnk"}
    if blk is not None:
        tile[blk] = size[blk]
        grid[blk] = nb
    nk = grid["k"]

    def spec(dims, is3):
        def im(i, j, k):
            g = {"m": i, "n": j, "k": k}
            idx = tuple(0 if d == blk else g[d] for d in dims)
            return ((g[blk],) + idx) if is3 else idx
        shape = (tile[dims[0]], tile[dims[1]])
        return pl.BlockSpec(((None,) + shape) if is3 else shape, im)

    dn = (((0 if ta else 1,), (1 if tb else 0,)), ((), ()))

    def partial(a_ref, b_ref):
        return lax.dot_general(a_ref[...].astype(BF16), b_ref[...].astype(BF16), dn, preferred_element_type=F32)

    def body_single(a_ref, b_ref, o_ref):
        o_ref[...] = partial(a_ref, b_ref).astype(o_ref.dtype)

    def body_acc(a_ref, b_ref, o_ref, acc):
        k = pl.program_id(2)

        @pl.when(k == 0)
        def _():
            acc[...] = partial(a_ref, b_ref)

        @pl.when((k > 0) & (k < nk - 1))
        def _():
            acc[...] += partial(a_ref, b_ref)

        @pl.when(k == nk - 1)
        def _():
            o_ref[...] = (acc[...] + partial(a_ref, b_ref)).astype(o_ref.dtype)

    oshape = ((nb,) if o3 else ()) + (size["m"], size["n"])
    return pl.pallas_call(
        body_single if nk == 1 else body_acc, grid=(grid["m"], grid["n"], nk),
        in_specs=[spec(a_dims, a3), spec(b_dims, b3)], out_specs=spec(("m", "n"), o3),
        out_shape=_sds(oshape, out_dtype),
        scratch_shapes=[] if nk == 1 else [pltpu.VMEM((tile["m"], tile["n"]), F32)],
        compiler_params=_cp(("parallel", "parallel", "arbitrary")), name=name)(a, b)


def _rows(tm, width):
    return pl.BlockSpec((tm, width), lambda i: (i, 0))


def rms_fwd(x, g, name):
    T = x.shape[0]

    def body(x_ref, g_ref, h_ref, ht_ref):
        x = x_ref[...]
        r = lax.rsqrt(jnp.mean(x * x, axis=-1, keepdims=True) + RMS_EPS)
        h = (x * r) * g_ref[...]
        h_ref[...] = h.astype(h_ref.dtype)
        ht_ref[...] = h.T.astype(ht_ref.dtype)

    return pl.pallas_call(body, grid=(T // ROW_TILE,), in_specs=[_rows(ROW_TILE, D_MODEL), _full((1, D_MODEL))],
                          out_specs=[_rows(ROW_TILE, D_MODEL), pl.BlockSpec((D_MODEL, ROW_TILE), lambda i: (0, i))],
                          out_shape=[_sds((T, D_MODEL), BF16), _sds((D_MODEL, T), BF16)],
                          compiler_params=_cp(("parallel",)), name=name)(x, g)


def rms_post_fwd(xres, m, g, name):
    T = m.shape[0]

    def body(x_ref, m_ref, g_ref, o_ref):
        m = m_ref[...]
        r = lax.rsqrt(jnp.mean(m * m, axis=-1, keepdims=True) + RMS_EPS)
        o_ref[...] = x_ref[...] + (m * r) * g_ref[...]

    return pl.pallas_call(body, grid=(T // ROW_TILE,),
                          in_specs=[_rows(ROW_TILE, D_MODEL), _rows(ROW_TILE, D_MODEL), _full((1, D_MODEL))],
                          out_specs=_rows(ROW_TILE, D_MODEL), out_shape=_sds((T, D_MODEL), F32),
                          compiler_params=_cp(("parallel",)), name=name)(xres, m, g)


def rms_bwd(m, g, dy, dres, out_dtype, name):
    T = m.shape[0]
    has_res = dres is not None

    def body(*refs):
        if has_res:
            m_ref, g_ref, dy_ref, dres_ref, dm_ref, dg_ref = refs
        else:
            m_ref, g_ref, dy_ref, dm_ref, dg_ref = refs
        m = m_ref[...]
        dy = dy_ref[...].astype(F32)
        r = lax.rsqrt(jnp.mean(m * m, axis=-1, keepdims=True) + RMS_EPS)
        n = m * r
        dn = dy * g_ref[...]
        dm = r * (dn - n * jnp.mean(dn * n, axis=-1, keepdims=True))
        if has_res:
            dm = dm + dres_ref[...]
        dm_ref[...] = dm.astype(dm_ref.dtype)

        @pl.when(pl.program_id(0) == 0)
        def _():
            dg_ref[...] = jnp.zeros_like(dg_ref)

        dg_ref[...] += jnp.sum(dy * n, axis=0, keepdims=True)

    ins = [m, g, dy] + ([dres] if has_res else [])
    specs = [_rows(ROW_TILE, D_MODEL), _full((1, D_MODEL)), _rows(ROW_TILE, D_MODEL)] + ([_rows(ROW_TILE, D_MODEL)] if has_res else [])
    return pl.pallas_call(body, grid=(T // ROW_TILE,), in_specs=specs,
                          out_specs=[_rows(ROW_TILE, D_MODEL), _full((1, D_MODEL))],
                          out_shape=[_sds((T, D_MODEL), out_dtype), _sds((1, D_MODEL), F32)],
                          compiler_params=_cp(("arbitrary",)), name=name)(*ins)


def loss_fwd_bwd(y, tgt):
    T = y.shape[0]

    def body(y_ref, t_ref, dy_ref, l_ref):
        e = y_ref[...] - t_ref[...]
        dy_ref[...] = e * (1.0 / D_MODEL)

        @pl.when(pl.program_id(0) == 0)
        def _():
            l_ref[...] = jnp.zeros_like(l_ref)

        l_ref[...] += jnp.full(l_ref.shape, 0.5 * jnp.sum(jnp.mean(e * e, axis=-1, keepdims=True)), F32)

    return pl.pallas_call(body, grid=(T // ROW_TILE,), in_specs=[_rows(ROW_TILE, D_MODEL)] * 2,
                          out_specs=[_rows(ROW_TILE, D_MODEL), _full((8, 128))],
                          out_shape=[_sds((T, D_MODEL), F32), _sds((8, 128), F32)],
                          compiler_params=_cp(("arbitrary",)), name="loss")(y, tgt)


FFN_TILE = 1024


def _gu_spec():
    return pl.BlockSpec((2, None, FFN_TILE, FFN_BLK), lambda i, j: (0, j, i, 0))


def ffn_up_fwd(h, w_gu):
    T = h.shape[0]

    def body(h_ref, wg_ref, wu_ref, gu_ref, a_ref):
        h = h_ref[...]
        gate = _dot(h, wg_ref[...])
        up = _dot(h, wu_ref[...])
        gu_ref[0] = gate
        gu_ref[1] = up
        a_ref[...] = (gate * _sigmoid(gate) * up).astype(a_ref.dtype)

    return pl.pallas_call(
        body, grid=(T // FFN_TILE, 4),
        in_specs=[pl.BlockSpec((FFN_TILE, D_MODEL), lambda i, j: (i, 0)),
                  pl.BlockSpec((None, D_MODEL, FFN_BLK), lambda i, j: (j, 0, 0)),
                  pl.BlockSpec((None, D_MODEL, FFN_BLK), lambda i, j: (j + 4, 0, 0))],
        out_specs=[_gu_spec(), pl.BlockSpec((None, FFN_TILE, FFN_BLK), lambda i, j: (j, i, 0))],
        out_shape=[_sds((2, 4, T, FFN_BLK), F32), _sds((4, T, FFN_BLK), BF16)],
        compiler_params=_cp(("parallel", "parallel")), name="ffn_up_fwd")(h, w_gu, w_gu)


def ffn_down_dx(df, w_down, gu4):
    T = df.shape[0]

    def body(df_ref, w_ref, gu_ref, d_ref):
        da = _dot_nt(df_ref[...], w_ref[...])
        gate, up = gu_ref[0], gu_ref[1]
        sg = _sigmoid(gate)
        d_ref[0] = (da * up * (sg * (1.0 + gate * (1.0 - sg)))).astype(d_ref.dtype)
        d_ref[1] = (da * (gate * sg)).astype(d_ref.dtype)

    return pl.pallas_call(
        body, grid=(T // FFN_TILE, 4),
        in_specs=[pl.BlockSpec((FFN_TILE, D_MODEL), lambda i, j: (i, 0)),
                  pl.BlockSpec((None, FFN_BLK, D_MODEL), lambda i, j: (j, 0, 0)), _gu_spec()],
        out_specs=_gu_spec(), out_shape=_sds((2, 4, T, FFN_BLK), BF16),
        compiler_params=_cp(("parallel", "parallel")), name="ffn_down_dx")(df, w_down, gu4)


def _zcol(tm, col):
    return pl.BlockSpec((tm, GW), lambda i: (i, col // GW))


def _sg_common(v, g):
    mu = jnp.mean(v, axis=-1, keepdims=True)
    xc = v - mu
    rstd = lax.rsqrt(jnp.mean(xc * xc, axis=-1, keepdims=True) + LN_EPS)
    vhat = xc * rstd
    return vhat, rstd, vhat * g


def _tril_weights(w_ref):
    tri = lax.broadcasted_iota(jnp.int32, (SG_CHUNK, SG_CHUNK), 0) >= lax.broadcasted_iota(jnp.int32, (SG_CHUNK, SG_CHUNK), 1)
    return tri, [jnp.where(tri, w_ref[h], 0.0).astype(BF16) for h in range(4)]


def mixa_fwd(z, ln_g, w, bexp):
    T = z.shape[0]
    nch = ROW_TILE // SG_CHUNK

    def body(u_ref, v_ref, g_ref, w_ref, be_ref, y_ref):
        _, _, vln = _sg_common(v_ref[...], g_ref[...])
        vb = vln.astype(BF16)
        head = _lane_head()
        _, wh = _tril_weights(w_ref)
        for c in range(nch):
            rows = slice(c * SG_CHUNK, (c + 1) * SG_CHUNK)
            sv = be_ref[...]
            for h in range(4):
                sv = sv + jnp.where(head == h, _dot(wh[h], vb[rows]), 0.0)
            y_ref[rows, :] = (u_ref[rows, :] * sv).astype(y_ref.dtype)

    return pl.pallas_call(body, grid=(T // ROW_TILE,),
                          in_specs=[_zcol(ROW_TILE, COL_U), _zcol(ROW_TILE, COL_V), _full((1, GW)), _full((4, SG_CHUNK, SG_CHUNK)),
                                    _full((SG_CHUNK, GW))],
                          out_specs=_rows(ROW_TILE, GW), out_shape=_sds((T, GW), BF16),
                          compiler_params=_cp(("parallel",)), name="mixa_fwd")(z, z, ln_g, w, bexp)


def mixa_bwd(z, dycat, ln_g, w, bexp):
    T = z.shape[0]
    nch = ROW_TILE // SG_CHUNK
    nsteps = T // ROW_TILE

    def body(u_ref, v_ref, dy_ref, g_ref, w_ref, be_ref, du_ref, dv_ref, dw_ref, db_ref, dg_ref, dbe_acc):
        step = pl.program_id(0)

        @pl.when(step == 0)
        def _():
            dw_ref[...] = jnp.zeros_like(dw_ref)
            dg_ref[...] = jnp.zeros_like(dg_ref)
            dbe_acc[...] = jnp.zeros_like(dbe_acc)

        g = g_ref[...]
        vhat, rstd, vln = _sg_common(v_ref[...], g)
        vb = vln.astype(BF16)
        head = _lane_head()
        tri, wh = _tril_weights(w_ref)
        dgsum = jnp.zeros((1, GW), F32)
        for c in range(nch):
            rows = slice(c * SG_CHUNK, (c + 1) * SG_CHUNK)
            sv = be_ref[...]
            for h in range(4):
                sv = sv + jnp.where(head == h, _dot(wh[h], vb[rows]), 0.0)
            dy = dy_ref[rows, :]
            du_ref[rows, :] = (dy * sv).astype(du_ref.dtype)
            dsv = dy * u_ref[rows, :]
            dbe_acc[...] += dsv
            dvln = jnp.zeros((SG_CHUNK, GW), F32)
            for h in range(4):
                dsvm = jnp.where(head == h, dsv, 0.0).astype(BF16)
                dw_ref[h] += _dot_nt(dsvm, vb[rows])
                dvln = dvln + _dot_tn(wh[h], dsvm)
            vh = vhat[rows]
            dgsum = dgsum + jnp.sum(dvln * vh, axis=0, keepdims=True)
            dvhat = dvln * g
            dv = rstd[rows] * (dvhat - jnp.mean(dvhat, axis=-1, keepdims=True) - vh * jnp.mean(dvhat * vh, axis=-1, keepdims=True))
            dv_ref[rows, :] = dv.astype(dv_ref.dtype)
        dg_ref[...] += dgsum

        @pl.when(step == nsteps - 1)
        def _():
            for h in range(4):
                dw_ref[h] = jnp.where(tri, dw_ref[h], 0.0)
            fold = (lax.shift_right_logical(lax.broadcasted_iota(jnp.int32, (GW, 128), 0), 6)
                    == lax.broadcasted_iota(jnp.int32, (GW, 128), 1)).astype(F32)
            db_ref[...] = jnp.dot(dbe_acc[...], fold, precision=HI, preferred_element_type=F32)

    return pl.pallas_call(
        body, grid=(nsteps,),
        in_specs=[_zcol(ROW_TILE, COL_U), _zcol(ROW_TILE, COL_V), pl.BlockSpec((ROW_TILE, GW), lambda i: (i, 0)),
                  _full((1, GW)), _full((4, SG_CHUNK, SG_CHUNK)), _full((SG_CHUNK, GW))],
        out_specs=[_rows(ROW_TILE, GW), _rows(ROW_TILE, GW), _full((4, SG_CHUNK, SG_CHUNK)), _full((SG_CHUNK, 128)), _full((1, GW))],
        out_shape=[_sds((T, GW), BF16), _sds((T, GW), BF16), _sds((4, SG_CHUNK, SG_CHUNK), F32), _sds((SG_CHUNK, 128), F32),
                   _sds((1, GW), F32)],
        scratch_shapes=[pltpu.VMEM((SG_CHUNK, GW), F32)],
        compiler_params=_cp(("arbitrary",)), name="mixa_bwd")(z, z, dycat, ln_g, w, bexp)


def _seq(S, col):
    return pl.BlockSpec((None, S, GW), lambda b: (b, 0, col // GW))


def _taps(buf, r0, offsets):
    by_phase = {}
    for k, off in enumerate(offsets):
        by_phase.setdefault(off % 8, []).append((k, off))
    for phase, items in sorted(by_phase.items()):
        span = max(off for _, off in items) - phase
        win = buf[pl.ds(r0 + phase, CONV_TILE + span), :]
        for k, off in items:
            yield k, win[off - phase:off - phase + CONV_TILE]


_CONV_FWD_OFFSETS = [CONV_PAD - (CONV_WIDTH - 1) + k for k in range(CONV_WIDTH)]
_CONV_BWD_OFFSETS = [(CONV_WIDTH - 1) - k for k in range(CONV_WIDTH)]


def _conv_ln(pad, r0, cw_ref, cb, lg, lb):
    acc = jnp.zeros((CONV_TILE, GW), F32) + cb
    for k, rows in _taps(pad, r0, _CONV_FWD_OFFSETS):
        acc = acc + cw_ref[k:k + 1, :] * rows
    mu = jnp.mean(acc, axis=-1, keepdims=True)
    xc = acc - mu
    rstd = lax.rsqrt(jnp.mean(xc * xc, axis=-1, keepdims=True) + LN_EPS)
    hhat = xc * rstd
    return hhat, rstd, hhat * lg + lb


def mixb_fwd(z3, cw, cb, lg, lb, pw, pwb):
    B, S, _ = z3.shape

    def body(a_ref, gt_ref, cw_ref, cb_ref, lg_ref, lb_ref, pw_ref, pwb_ref, y_ref, pad):
        pad[0:CONV_PAD, :] = jnp.zeros((CONV_PAD, GW), F32)
        pad[CONV_PAD:CONV_PAD + S, :] = a_ref[...] * _sigmoid(gt_ref[...])
        pwv = pw_ref[...].astype(BF16)
        for r0 in range(0, S, CONV_TILE):
            _, _, ln = _conv_ln(pad, r0, cw_ref, cb_ref[...], lg_ref[...], lb_ref[...])
            s = ln * _sigmoid(ln)
            y_ref[r0:r0 + CONV_TILE, :] = (_dot(s.astype(BF16), pwv) + pwb_ref[...]).astype(y_ref.dtype)

    return pl.pallas_call(
        body, grid=(B,),
        in_specs=[_seq(S, COL_A), _seq(S, COL_G), _full((CONV_WIDTH, GW)), _full((1, GW)), _full((1, GW)), _full((1, GW)),
                  _full((GW, GW)), _full((1, GW))],
        out_specs=pl.BlockSpec((None, S, GW), lambda b: (b, 0, 0)), out_shape=_sds((B, S, GW), BF16),
        scratch_shapes=[pltpu.VMEM((S + CONV_PAD, GW), F32)],
        compiler_params=_cp(("parallel",)), name="mixb_fwd")(z3, z3, cw, cb, lg, lb, pw, pwb)


def mixb_bwd(z3, dycat3, cw, cb, lg, lb, pw, pwb):
    B, S, _ = z3.shape

    def body(a_ref, gt_ref, dy_ref, cw_ref, cb_ref, lg_ref, lb_ref, pw_ref, pwb_ref,
             da_ref, dgt_ref, dcw_ref, dcb_ref, dlg_ref, dlb_ref, dpw_ref, dpwb_ref, pad, dpad, dcw_acc):
        @pl.when(pl.program_id(0) == 0)
        def _():
            for r in (dcb_ref, dlg_ref, dlb_ref, dpw_ref, dpwb_ref, dcw_acc):
                r[...] = jnp.zeros_like(r)

        pad[0:CONV_PAD, :] = jnp.zeros((CONV_PAD, GW), F32)
        pad[CONV_PAD:CONV_PAD + S, :] = a_ref[...] * _sigmoid(gt_ref[...])
        dpad[S:S + CONV_PAD, :] = jnp.zeros((CONV_PAD, GW), F32)
        pwv = pw_ref[...].astype(BF16)
        lg = lg_ref[...]
        for r0 in range(0, S, CONV_TILE):
            hhat, rstd, ln = _conv_ln(pad, r0, cw_ref, cb_ref[...], lg, lb_ref[...])
            sg = _sigmoid(ln)
            s = ln * sg
            dy = dy_ref[r0:r0 + CONV_TILE, :]
            dyb = dy.astype(BF16)
            dpw_ref[...] += _dot_tn(s.astype(BF16), dyb)
            dpwb_ref[...] += jnp.sum(dy, axis=0, keepdims=True)
            dln = _dot_nt(dyb, pwv) * (sg * (1.0 + ln * (1.0 - sg)))
            dlg_ref[...] += jnp.sum(dln * hhat, axis=0, keepdims=True)
            dlb_ref[...] += jnp.sum(dln, axis=0, keepdims=True)
            dhh = dln * lg
            dhc = rstd * (dhh - jnp.mean(dhh, axis=-1, keepdims=True) - hhat * jnp.mean(dhh * hhat, axis=-1, keepdims=True))
            dpad[r0:r0 + CONV_TILE, :] = dhc
            dcb_ref[...] += jnp.sum(dhc, axis=0, keepdims=True)
            for k, rows in _taps(pad, r0, _CONV_FWD_OFFSETS):
                dcw_acc[k] += (dhc * rows).reshape(CONV_TILE // 8, 8, GW).sum(axis=0)
        for r0 in range(0, S, CONV_TILE):
            dhg = jnp.zeros((CONV_TILE, GW), F32)
            for k, rows in _taps(dpad, r0, _CONV_BWD_OFFSETS):
                dhg = dhg + cw_ref[k:k + 1, :] * rows
            a = a_ref[r0:r0 + CONV_TILE, :]
            sg = _sigmoid(gt_ref[r0:r0 + CONV_TILE, :])
            da_ref[r0:r0 + CONV_TILE, :] = (dhg * sg).astype(da_ref.dtype)
            dgt_ref[r0:r0 + CONV_TILE, :] = (dhg * a * sg * (1.0 - sg)).astype(dgt_ref.dtype)

        @pl.when(pl.program_id(0) == B - 1)
        def _():
            for k in range(CONV_WIDTH):
                dcw_ref[k:k + 1, :] = jnp.sum(dcw_acc[k], axis=0, keepdims=True)

    seq_out = pl.BlockSpec((None, S, GW), lambda b: (b, 0, 0))
    return pl.pallas_call(
        body, grid=(B,),
        in_specs=[_seq(S, COL_A), _seq(S, COL_G), pl.BlockSpec((None, S, GW), lambda b: (b, 0, 1)),
                  _full((CONV_WIDTH, GW)), _full((1, GW)), _full((1, GW)), _full((1, GW)), _full((GW, GW)), _full((1, GW))],
        out_specs=[seq_out, seq_out, _full((CONV_WIDTH, GW)), _full((1, GW)), _full((1, GW)), _full((1, GW)), _full((GW, GW)),
                   _full((1, GW))],
        out_shape=[_sds((B, S, GW), BF16), _sds((B, S, GW), BF16), _sds((CONV_WIDTH, GW), F32), _sds((1, GW), F32),
                   _sds((1, GW), F32), _sds((1, GW), F32), _sds((GW, GW), F32), _sds((1, GW), F32)],
        scratch_shapes=[pltpu.VMEM((S + CONV_PAD, GW), F32), pltpu.VMEM((S + CONV_PAD, GW), F32),
                        pltpu.VMEM((CONV_WIDTH, 8, GW), F32)],
        compiler_params=_cp(("arbitrary",)), name="mixb_bwd")(z3, z3, dycat3, cw, cb, lg, lb, pw, pwb)


POOL_PAD = 16


def _pool_window():
    lane = lax.broadcasted_iota(jnp.int32, (1, GW), 1)
    return jnp.where(lane < 64, 2, jnp.where(lane < 128, 4, jnp.where(lane < 192, 8, 16)))


def _pool_sums(pad, r0, base, sign):
    win = _pool_window()
    acc = pad[pl.ds(r0 + base, CONV_TILE), :]
    out = None
    for i in range(1, 16):
        acc = acc + pad[pl.ds(r0 + base + sign * i, CONV_TILE), :]
        if i + 1 in (2, 4, 8, 16):
            out = acc if out is None else jnp.where(win == i + 1, acc, out)
    return out


def _pool_cnt(r0):
    t1 = r0 + 1 + lax.broadcasted_iota(jnp.int32, (CONV_TILE, 1), 0)
    return jnp.minimum(t1, _pool_window()).astype(F32)


def mixd_fwd(z3, wbd, scale):
    B, S, _ = z3.shape

    def body(x_ref, w_ref, sc_ref, y_ref, pad):
        pad[0:POOL_PAD, :] = jnp.zeros((POOL_PAD, GW), F32)
        pad[POOL_PAD:POOL_PAD + S, :] = x_ref[...]
        wv = w_ref[...].astype(BF16)
        for r0 in range(0, S, CONV_TILE):
            mean = _pool_sums(pad, r0, POOL_PAD, -1) / _pool_cnt(r0)
            p = (mean - x_ref[r0:r0 + CONV_TILE, :]).astype(BF16)
            y_ref[r0:r0 + CONV_TILE, :] = (_dot(p, wv) * sc_ref[...]).astype(y_ref.dtype)

    return pl.pallas_call(
        body, grid=(B,), in_specs=[_seq(S, COL_D), _full((GW, GW)), _full((1, GW))],
        out_specs=pl.BlockSpec((None, S, GW), lambda b: (b, 0, 0)), out_shape=_sds((B, S, GW), BF16),
        scratch_shapes=[pltpu.VMEM((S + POOL_PAD, GW), F32)],
        compiler_params=_cp(("parallel",)), name="mixd_fwd")(z3, wbd, scale)


def mixd_bwd(z3, dycat3, wbd, scale):
    B, S, _ = z3.shape

    def body(x_ref, dy_ref, w_ref, sc_ref, dx_ref, dw_ref, dsc_ref, pad, qpad):
        @pl.when(pl.program_id(0) == 0)
        def _():
            dw_ref[...] = jnp.zeros_like(dw_ref)
            dsc_ref[...] = jnp.zeros_like(dsc_ref)

        pad[0:POOL_PAD, :] = jnp.zeros((POOL_PAD, GW), F32)
        pad[POOL_PAD:POOL_PAD + S, :] = x_ref[...]
        qpad[S:S + POOL_PAD, :] = jnp.zeros((POOL_PAD, GW), F32)
        wv = w_ref[...].astype(BF16)
        for r0 in range(0, S, CONV_TILE):
            cnt = _pool_cnt(r0)
            mean = _pool_sums(pad, r0, POOL_PAD, -1) / cnt
            p = (mean - x_ref[r0:r0 + CONV_TILE, :]).astype(BF16)
            dy = dy_ref[r0:r0 + CONV_TILE, :]
            dsc_ref[...] += jnp.sum(dy * _dot(p, wv), axis=0, keepdims=True)
            dyp = (dy * sc_ref[...]).astype(BF16)
            dw_ref[...] += _dot_tn(p, dyp)
            dp = _dot_nt(dyp, wv)
            dx_ref[r0:r0 + CONV_TILE, :] = (-dp).astype(dx_ref.dtype)
            qpad[r0:r0 + CONV_TILE, :] = dp / cnt
        for r0 in range(0, S, CONV_TILE):
            back = _pool_sums(qpad, r0, 0, 1)
            dx_ref[r0:r0 + CONV_TILE, :] = (dx_ref[r0:r0 + CONV_TILE, :].astype(F32) + back).astype(dx_ref.dtype)

    return pl.pallas_call(
        body, grid=(B,),
        in_specs=[_seq(S, COL_D), pl.BlockSpec((None, S, GW), lambda b: (b, 0, 3)), _full((GW, GW)), _full((1, GW))],
        out_specs=[pl.BlockSpec((None, S, GW), lambda b: (b, 0, 0)), _full((GW, GW)), _full((1, GW))],
        out_shape=[_sds((B, S, GW), F32), _sds((GW, GW), F32), _sds((1, GW), F32)],
        scratch_shapes=[pltpu.VMEM((S + POOL_PAD, GW), F32), pltpu.VMEM((S + POOL_PAD, GW), F32)],
        compiler_params=_cp(("arbitrary",)), name="mixd_bwd")(z3, dycat3, wbd, scale)


def cmp_kv_fwd(tbk, tbv, pek, pev, w1k, w2k, w1v, w2v):
    B = tbk.shape[0]

    def body(tbk_ref, tbv_ref, pek_ref, pev_ref, w1k_ref, w2k_ref, w1v_ref, w2v_ref, kc_ref, vc_ref):
        for tb_ref, pe_ref, w1_ref, w2_ref, o_ref in ((tbk_ref, pek_ref, w1k_ref, w2k_ref, kc_ref),
                                                      (tbv_ref, pev_ref, w1v_ref, w2v_ref, vc_ref)):
            pre = _dot((tb_ref[...] + pe_ref[...]).astype(BF16), w1_ref[...].astype(BF16))
            hm = pre * _sigmoid(pre)
            o_ref[...] = _dot(hm.astype(BF16), w2_ref[...].astype(BF16))

    tb_spec = pl.BlockSpec((None, N_CMP, 2048), lambda b: (b, 0, 0))
    o_spec = pl.BlockSpec((None, N_CMP, HEAD_DIM), lambda b: (b, 0, 0))
    return pl.pallas_call(
        body, grid=(B,),
        in_specs=[tb_spec, tb_spec, _full((1, 2048)), _full((1, 2048)), _full((2048, HEAD_DIM)), _full((HEAD_DIM, HEAD_DIM)),
                  _full((2048, HEAD_DIM)), _full((HEAD_DIM, HEAD_DIM))],
        out_specs=[o_spec, o_spec], out_shape=[_sds((B, N_CMP, HEAD_DIM), F32)] * 2,
        compiler_params=_cp(("parallel",)), name="cmp_kv_fwd")(tbk, tbv, pek, pev, w1k, w2k, w1v, w2v)


def cmp_kv_bwd(tbk, tbv, pek, pev, w1k, w2k, w1v, w2v, dkc, dvc):
    B = tbk.shape[0]

    def body(tbk_ref, tbv_ref, pek_ref, pev_ref, w1k_ref, w2k_ref, w1v_ref, w2v_ref, dkc_ref, dvc_ref,
             dk2_ref, dv2_ref, dpek_ref, dpev_ref, dw1k_ref, dw2k_ref, dw1v_ref, dw2v_ref):
        @pl.when(pl.program_id(0) == 0)
        def _():
            for r in (dpek_ref, dpev_ref, dw1k_ref, dw2k_ref, dw1v_ref, dw2v_ref):
                r[...] = jnp.zeros_like(r)

        row0 = lax.broadcasted_iota(jnp.int32, (N_CMP, 1), 0) == 0
        for tb_ref, pe_ref, w1_ref, w2_ref, do_ref, d2_ref, dpe_ref, dw1_ref, dw2_ref in (
                (tbk_ref, pek_ref, w1k_ref, w2k_ref, dkc_ref, dk2_ref, dpek_ref, dw1k_ref, dw2k_ref),
                (tbv_ref, pev_ref, w1v_ref, w2v_ref, dvc_ref, dv2_ref, dpev_ref, dw1v_ref, dw2v_ref)):
            tb = (tb_ref[...] + pe_ref[...]).astype(BF16)
            w1 = w1_ref[...].astype(BF16)
            pre = _dot(tb, w1)
            sg = _sigmoid(pre)
            hm = (pre * sg).astype(BF16)
            do = do_ref[...].astype(BF16)
            dw2_ref[...] += _dot_tn(hm, do)
            dpre = (_dot_nt(do, w2_ref[...].astype(BF16)) * (sg * (1.0 + pre * (1.0 - sg)))).astype(BF16)
            dw1_ref[...] += _dot_tn(tb, dpre)
            dtb = _dot_nt(dpre, w1)
            dpe_ref[...] += jnp.sum(dtb, axis=0, keepdims=True)
            down = jnp.where(row0, 0.0, pltpu.roll(dtb[:, 1024:], 1, 0))
            d2_ref[...] = dtb[:, :1024] + down

    tb_spec = pl.BlockSpec((None, N_CMP, 2048), lambda b: (b, 0, 0))
    c_spec = pl.BlockSpec((None, N_CMP, HEAD_DIM), lambda b: (b, 0, 0))
    d2_spec = pl.BlockSpec((None, N_CMP, 1024), lambda b: (b, 0, 0))
    return pl.pallas_call(
        body, grid=(B,),
        in_specs=[tb_spec, tb_spec, _full((1, 2048)), _full((1, 2048)), _full((2048, HEAD_DIM)), _full((HEAD_DIM, HEAD_DIM)),
                  _full((2048, HEAD_DIM)), _full((HEAD_DIM, HEAD_DIM)), c_spec, c_spec],
        out_specs=[d2_spec, d2_spec, _full((1, 2048)), _full((1, 2048)), _full((2048, HEAD_DIM)), _full((HEAD_DIM, HEAD_DIM)),
                   _full((2048, HEAD_DIM)), _full((HEAD_DIM, HEAD_DIM))],
        out_shape=[_sds((B, N_CMP, 1024), F32)] * 2 + [_sds((1, 2048), F32)] * 2
        + [_sds((2048, HEAD_DIM), F32), _sds((HEAD_DIM, HEAD_DIM), F32)] * 2,
        compiler_params=_cp(("arbitrary",)), name="cmp_kv_bwd")(tbk, tbv, pek, pev, w1k, w2k, w1v, w2v, dkc, dvc)


def _qtile(col):
    return pl.BlockSpec((None, TQ, GW), lambda b, i: (b, i, col // GW))


def _qtile0():
    return pl.BlockSpec((None, TQ, GW), lambda b, i: (b, i, 0))


def _cmp_probs(q, kc, qpos):
    head = _lane_head()
    cend = lax.broadcasted_iota(jnp.int32, (1, N_CMP), 1) * CMP_STRIDE + 31
    cmask = cend <= qpos
    has = qpos >= 31
    out = []
    for h in range(4):
        qm = jnp.where(head == h, q, 0.0).astype(BF16)
        s = jnp.where(cmask, _dot_nt(qm, kc), NEG)
        e = jnp.exp(s - jnp.max(s, axis=-1, keepdims=True))
        p = jnp.where(has, e / jnp.sum(e, axis=-1, keepdims=True), 0.0)
        out.append((qm, p))
    return out


def cmp_attn_fwd(z3, kc4, vc4):
    B, S, _ = z3.shape

    def body(q_ref, kc_ref, vc_ref, o_ref, sel_ref):
        t0 = pl.program_id(1) * TQ
        qpos = t0 + lax.broadcasted_iota(jnp.int32, (TQ, 1), 0)
        head = _lane_head()
        kc, vc = kc_ref[...], vc_ref[...]
        o = jnp.zeros((TQ, GW), F32)
        psum = jnp.zeros((TQ, N_CMP), F32)
        for h, (_, p) in enumerate(_cmp_probs(q_ref[...] * 0.125, kc, qpos)):
            o = o + jnp.where(head == h, _dot(p.astype(BF16), vc), 0.0)
            psum = psum + p
        o_ref[...] = o
        cst = lax.broadcasted_iota(jnp.int32, (N_SLC, N_CMP), 1) * CMP_STRIDE
        jst = lax.broadcasted_iota(jnp.int32, (N_SLC, N_CMP), 0) * 64
        overlap = ((cst <= jst + 63) & (cst + 31 >= jst)).astype(BF16)
        imp = _dot_nt(overlap, psum.astype(BF16))
        qp = t0 + lax.broadcasted_iota(jnp.int32, (1, TQ), 1)
        jj = lax.broadcasted_iota(jnp.int32, (N_SLC, 1), 0)
        cur = lax.shift_right_logical(qp, SLC_BLOCK_SHIFT)
        forced = (jj == 0) | (jj == cur) | (jj == cur - 1)
        score = jnp.where(jj * 64 <= qp, imp + jnp.where(forced, FORCE_BONUS, 0.0), NEG)
        rank = jnp.zeros((N_SLC, TQ), F32)
        for j2 in range(N_SLC):
            sj = score[j2:j2 + 1, :]
            rank = rank + jnp.where((sj > score) | ((sj == score) & (j2 < jj)), 1.0, 0.0)
        sel_ref[...] = jnp.where((rank < SLC_TOPK) & (score > NEG / 2), 1.0, 0.0)

    c_spec = pl.BlockSpec((None, N_CMP, GW), lambda b, i: (b, 0, 0))
    return pl.pallas_call(
        body, grid=(B, S // TQ), in_specs=[_qtile(COL_Q), c_spec, c_spec],
        out_specs=[_qtile0(), pl.BlockSpec((None, N_SLC, TQ), lambda b, i: (b, 0, i))],
        out_shape=[_sds((B, S, GW), F32), _sds((B, N_SLC, S), F32)],
        compiler_params=_cp(("parallel", "parallel")), name="cmp_attn_fwd")(z3, kc4, vc4)


def cmp_attn_bwd(z3, kc4, vc4, do):
    B, S, _ = z3.shape
    nq = S // TQ

    def body(q_ref, kc_ref, vc_ref, do_ref, dq_ref, dkc_ref, dvc_ref):
        qi = pl.program_id(1)

        @pl.when(qi == 0)
        def _():
            dkc_ref[...] = jnp.zeros_like(dkc_ref)
            dvc_ref[...] = jnp.zeros_like(dvc_ref)

        qpos = qi * TQ + lax.broadcasted_iota(jnp.int32, (TQ, 1), 0)
        head = _lane_head()
        kc, vc, do = kc_ref[...], vc_ref[...], do_ref[...]
        dq = jnp.zeros((TQ, GW), F32)
        for h, (qm, p) in enumerate(_cmp_probs(q_ref[...] * 0.125, kc, qpos)):
            dom = jnp.where(head == h, do, 0.0).astype(BF16)
            dp = _dot_nt(dom, vc)
            ds = (p * (dp - jnp.sum(p * dp, axis=-1, keepdims=True))).astype(BF16)
            dq = dq + jnp.where(head == h, _dot(ds, kc), 0.0)
            dkc_ref[...] += _dot_tn(ds, qm)
            dvc_ref[...] += _dot_tn(p.astype(BF16), dom)
        dq_ref[...] = dq * 0.125

        @pl.when(qi == nq - 1)
        def _():
            dkc_ref[...] = _fold_heads(dkc_ref[...])
            dvc_ref[...] = _fold_heads(dvc_ref[...])

    c_spec = pl.BlockSpec((None, N_CMP, GW), lambda b, i: (b, 0, 0))
    return pl.pallas_call(
        body, grid=(B, nq), in_specs=[_qtile(COL_Q), c_spec, c_spec, _qtile0()],
        out_specs=[_qtile0(), c_spec, c_spec],
        out_shape=[_sds((B, S, GW), F32), _sds((B, N_CMP, GW), F32), _sds((B, N_CMP, GW), F32)],
        compiler_params=_cp(("parallel", "arbitrary")), name="cmp_attn_bwd")(z3, kc4, vc4, do)


def _attn_mask(mode, qpos, k0, sel_b):
    kpos = k0 + lax.broadcasted_iota(jnp.int32, (1, TQ), 1)
    mask = kpos <= qpos
    if mode == "win":
        return mask & (kpos > qpos - WIN)
    blk = lax.shift_right_logical(k0 + lax.broadcasted_iota(jnp.int32, (N_SLC, TQ), 1), SLC_BLOCK_SHIFT)
    expand = (blk == lax.broadcasted_iota(jnp.int32, (N_SLC, TQ), 0)).astype(BF16)
    return mask & (_dot_tn(sel_b, expand) > 0.5)


def _attn_lo(mode, qi):
    return jnp.maximum(qi - WIN // TQ, 0) if mode == "win" else 0


def attn_fwd(mode, z3, k4, v4, selT):
    B, S, _ = z3.shape

    def body(q_ref, k_ref, v_ref, sel_ref, o_ref, lse_ref, s_all, m_acc, l_acc, o_acc):
        qi = pl.program_id(1)
        qpos = qi * TQ + lax.broadcasted_iota(jnp.int32, (TQ, 1), 0)
        head = _lane_head()
        q = q_ref[...] * 0.125
        qm = [jnp.where(head == h, q, 0.0).astype(BF16) for h in range(4)]
        sel_b = sel_ref[...].astype(BF16)
        lo, hi = _attn_lo(mode, qi), qi + 1
        m_acc[...] = jnp.full(m_acc.shape, NEG, F32)

        def scores(kb, carry):
            k0 = pl.multiple_of(kb * TQ, TQ)
            kblk = k_ref[pl.ds(k0, TQ), :]
            mask = _attn_mask(mode, qpos, k0, sel_b)
            for h in range(4):
                s = jnp.where(mask, _dot_nt(qm[h], kblk), NEG)
                s_all[h, kb] = s
                m_acc[h] = jnp.maximum(m_acc[h], s)
            return carry

        lax.fori_loop(lo, hi, scores, 0)
        for h in range(4):
            m_acc[h] = jnp.broadcast_to(jnp.max(m_acc[h], axis=-1, keepdims=True), (TQ, TQ))
        l_acc[...] = jnp.zeros_like(l_acc)
        o_acc[...] = jnp.zeros_like(o_acc)

        def weights(kb, carry):
            vblk = v_ref[pl.ds(pl.multiple_of(kb * TQ, TQ), TQ), :]
            for h in range(4):
                p = jnp.exp(s_all[h, kb] - m_acc[h])
                l_acc[h] += p
                o_acc[h] += _dot(p.astype(BF16), vblk)
            return carry

        lax.fori_loop(lo, hi, weights, 0)
        o = jnp.zeros((TQ, GW), F32)
        lse = jnp.zeros((TQ, 128), F32)
        lane = lax.broadcasted_iota(jnp.int32, (1, 128), 1)
        for h in range(4):
            l = jnp.sum(l_acc[h], axis=-1, keepdims=True)
            o = o + jnp.where(head == h, o_acc[h] / l, 0.0)
            lse = jnp.where(lane == h, jnp.max(m_acc[h], axis=-1, keepdims=True) + jnp.log(l), lse)
        o_ref[...] = o
        lse_ref[...] = lse

    kv_spec = pl.BlockSpec((None, S, GW), lambda b, i: (b, 0, 0))
    return pl.pallas_call(
        body, grid=(B, S // TQ),
        in_specs=[_qtile(COL_Q), kv_spec, kv_spec, pl.BlockSpec((None, N_SLC, TQ), lambda b, i: (b, 0, i))],
        out_specs=[_qtile0(), pl.BlockSpec((None, TQ, 128), lambda b, i: (b, i, 0))],
        out_shape=[_sds((B, S, GW), F32), _sds((B, S, 128), F32)],
        scratch_shapes=[pltpu.VMEM((4, S // TQ, TQ, TQ), F32), pltpu.VMEM((4, TQ, TQ), F32), pltpu.VMEM((4, TQ, TQ), F32),
                        pltpu.VMEM((4, TQ, GW), F32)],
        compiler_params=_cp(("parallel", "parallel")), name=mode + "_attn_fwd")(z3, k4, v4, selT)


def attn_bwd(mode, z3, k4, v4, selT, o, lse, do):
    B, S, _ = z3.shape
    nq = S // TQ

    def body(q_ref, k_ref, v_ref, sel_ref, o_ref, lse_ref, do_ref, dq_ref, dk_ref, dv_ref, dq_s):
        qi = pl.program_id(1)

        @pl.when(qi == 0)
        def _():
            dk_ref[...] = jnp.zeros_like(dk_ref)
            dv_ref[...] = jnp.zeros_like(dv_ref)

        qpos = qi * TQ + lax.broadcasted_iota(jnp.int32, (TQ, 1), 0)
        head = _lane_head()
        lane = lax.broadcasted_iota(jnp.int32, (1, 128), 1)
        q = q_ref[...] * 0.125
        do = do_ref[...]
        doo = do * o_ref[...]
        lse = lse_ref[...]
        qm = [jnp.where(head == h, q, 0.0).astype(BF16) for h in range(4)]
        dom = [jnp.where(head == h, do, 0.0).astype(BF16) for h in range(4)]
        delta = [jnp.sum(jnp.where(head == h, doo, 0.0), axis=-1, keepdims=True) for h in range(4)]
        lse_h = [jnp.max(jnp.where(lane == h, lse, NEG), axis=-1, keepdims=True) for h in range(4)]
        sel_b = sel_ref[...].astype(BF16)
        dq_s[...] = jnp.zeros_like(dq_s)

        def step(kb, carry):
            k0 = pl.multiple_of(kb * TQ, TQ)
            kblk = k_ref[pl.ds(k0, TQ), :]
            vblk = v_ref[pl.ds(k0, TQ), :]
            mask = _attn_mask(mode, qpos, k0, sel_b)
            for h in range(4):
                s = _dot_nt(qm[h], kblk)
                p = jnp.where(mask, jnp.exp(s - lse_h[h]), 0.0)
                dp = _dot_nt(dom[h], vblk)
                ds = (p * (dp - delta[h])).astype(BF16)
                dq_s[...] += jnp.where(head == h, _dot(ds, kblk), 0.0)
                dk_ref[pl.ds(k0, TQ), :] += _dot_tn(ds, qm[h])
                dv_ref[pl.ds(k0, TQ), :] += _dot_tn(p.astype(BF16), dom[h])
            return carry

        lax.fori_loop(_attn_lo(mode, qi), qi + 1, step, 0)
        dq_ref[...] = dq_s[...] * 0.125

        @pl.when(qi == nq - 1)
        def _():
            for r0 in range(0, S, TQ):
                dk_ref[r0:r0 + TQ, :] = _fold_heads(dk_ref[r0:r0 + TQ, :])
                dv_ref[r0:r0 + TQ, :] = _fold_heads(dv_ref[r0:r0 + TQ, :])

    kv_spec = pl.BlockSpec((None, S, GW), lambda b, i: (b, 0, 0))
    return pl.pallas_call(
        body, grid=(B, nq),
        in_specs=[_qtile(COL_Q), kv_spec, kv_spec, pl.BlockSpec((None, N_SLC, TQ), lambda b, i: (b, 0, i)), _qtile0(),
                  pl.BlockSpec((None, TQ, 128), lambda b, i: (b, i, 0)), _qtile0()],
        out_specs=[_qtile0(), kv_spec, kv_spec],
        out_shape=[_sds((B, S, GW), F32)] * 3,
        scratch_shapes=[pltpu.VMEM((TQ, GW), F32)],
        compiler_params=_cp(("parallel", "arbitrary")), name=mode + "_attn_bwd")(z3, k4, v4, selT, o, lse, do)


def _gate_expand(b):
    r = lax.broadcasted_iota(jnp.int32, (128, GW), 0)
    hl = lax.shift_right_logical(lax.broadcasted_iota(jnp.int32, (128, GW), 1), 6)
    return (r == 3 * hl + b).astype(F32)


def combine_fwd(z3, o_cmp, o_slc, o_win):
    B, S, _ = z3.shape

    def body(gl_ref, oc_ref, os_ref, ow_ref, y_ref):
        g = _sigmoid(gl_ref[...])
        y = jnp.zeros((TQ, GW), F32)
        for b, o_ref in enumerate((oc_ref, os_ref, ow_ref)):
            y = y + jnp.dot(g, _gate_expand(b), precision=HI, preferred_element_type=F32) * o_ref[...]
        y_ref[...] = y.astype(y_ref.dtype)

    return pl.pallas_call(
        body, grid=(B, S // TQ),
        in_specs=[pl.BlockSpec((None, TQ, 128), lambda b, i: (b, i, COL_GL // 128)), _qtile0(), _qtile0(), _qtile0()],
        out_specs=_qtile0(), out_shape=_sds((B, S, GW), BF16),
        compiler_params=_cp(("parallel", "parallel")), name="combine_fwd")(z3, o_cmp, o_slc, o_win)


def combine_bwd(z3, o_cmp, o_slc, o_win, dycat3):
    B, S, _ = z3.shape

    def body(gl_ref, oc_ref, os_ref, ow_ref, dy_ref, dc_ref, ds_ref, dw_ref, dgl_ref):
        g = _sigmoid(gl_ref[...])
        dy = dy_ref[...]
        dg = jnp.zeros((TQ, 128), F32)
        for b, (o_ref, d_ref) in enumerate(((oc_ref, dc_ref), (os_ref, ds_ref), (ow_ref, dw_ref))):
            ex = _gate_expand(b)
            d_ref[...] = jnp.dot(g, ex, precision=HI, preferred_element_type=F32) * dy
            dg = dg + lax.dot_general(dy * o_ref[...], ex, (((1,), (1,)), ((), ())), precision=HI, preferred_element_type=F32)
        dgl_ref[...] = dg * g * (1.0 - g)

    gl_spec = pl.BlockSpec((None, TQ, 128), lambda b, i: (b, i, COL_GL // 128))
    return pl.pallas_call(
        body, grid=(B, S // TQ),
        in_specs=[gl_spec, _qtile0(), _qtile0(), _qtile0(), pl.BlockSpec((None, TQ, GW), lambda b, i: (b, i, 2))],
        out_specs=[_qtile0(), _qtile0(), _qtile0(), pl.BlockSpec((None, TQ, 128), lambda b, i: (b, i, 0))],
        out_shape=[_sds((B, S, GW), F32)] * 3 + [_sds((B, S, 128), F32)],
        compiler_params=_cp(("parallel", "parallel")), name="combine_bwd")(z3, o_cmp, o_slc, o_win, dycat3)


def assemble_dz(du, dv, da, dgt, dq_c, dq_s, dq_w, dd, dkv, dgl):
    T = du.shape[0]

    def body(du_ref, dv_ref, da_ref, dgt_ref, dqc_ref, dqs_ref, dqw_ref, dd_ref, dkv_ref, dgl_ref, o_ref):
        o_ref[:, COL_U:COL_U + GW] = du_ref[...]
        o_ref[:, COL_V:COL_V + GW] = dv_ref[...]
        o_ref[:, COL_A:COL_A + GW] = da_ref[...]
        o_ref[:, COL_G:COL_G + GW] = dgt_ref[...]
        o_ref[:, COL_Q:COL_Q + GW] = (dqc_ref[...] + dqs_ref[...] + dqw_ref[...]).astype(BF16)
        o_ref[:, COL_D:COL_D + GW] = dd_ref[...].astype(BF16)
        o_ref[:, COL_KV:COL_KV + 384] = dkv_ref[...].astype(BF16)
        o_ref[:, COL_GL:COL_GL + 128] = dgl_ref[...].astype(BF16)

    specs = [_rows(ROW_TILE, GW)] * 8 + [_rows(ROW_TILE, 384), _rows(ROW_TILE, 128)]
    return pl.pallas_call(body, grid=(T // ROW_TILE,), in_specs=specs, out_specs=_rows(ROW_TILE, ZW),
                          out_shape=_sds((T, ZW), BF16), compiler_params=_cp(("parallel",)),
                          name="assemble_dz")(du, dv, da, dgt, dq_c, dq_s, dq_w, dd, dkv, dgl)


def _my_pos():
    return lax.axis_index("x"), lax.axis_index("y"), lax.axis_index("c")


def _peer(k):
    x, y, c = _my_pos()
    return ((1 - x) if k & 4 else x, (1 - y) if k & 2 else y, (1 - c) if k & 1 else c)


def _index(pos):
    return 4 * pos[0] + 2 * pos[1] + pos[2]


_HBM = pl.BlockSpec(memory_space=pltpu.HBM)


_SEM = pl.BlockSpec(memory_space=pltpu.SEMAPHORE)
_EFFECT = pltpu.SideEffectType.DATAFLOW_SIDE_EFFECTING


def _exchange_copies(kinds, srcs, lands, send, recv):
    me = _index(_my_pos())
    out = []
    for a, kind in enumerate(kinds):
        for k in range(1, N_DEV):
            peer = _peer(k)
            if kind == "gather":
                r = srcs[a].shape[1]
                src, dst = srcs[a], lands[a].at[:, pl.ds(me * r, r), :]
            else:
                r = srcs[a].shape[1] // N_DEV
                src, dst = srcs[a].at[:, pl.ds(_index(peer) * r, r), :], lands[a].at[me]
            sem = a * (N_DEV - 1) + k - 1
            out.append(pltpu.make_async_remote_copy(src_ref=src, dst_ref=dst, send_sem=send.at[sem], recv_sem=recv.at[sem],
                                                    device_id=peer, device_id_type=MESH))
    return out


def _land_with_own(kind, src):
    me = _index(_my_pos())
    if kind == "gather":
        _, r, C = src.shape
        return lax.dynamic_update_slice(lax.empty((1, N_DEV * r, C), src.dtype), src, (0, me * r, 0))
    _, r8, C = src.shape
    r = r8 // N_DEV
    own = lax.dynamic_slice(src, (0, me * r, 0), (1, r, C))
    return lax.dynamic_update_slice(lax.empty((N_DEV, 1, r, C), src.dtype), own[None], (me, 0, 0, 0))


def exchange_start(kinds, srcs, name):
    n = len(srcs)
    lands = [_land_with_own(k, s) for k, s in zip(kinds, srcs)]

    def body(*refs):
        s, l = refs[:n], refs[n:2 * n]
        send, recv = refs[2 * n], refs[2 * n + 1]
        for cp in _exchange_copies(kinds, s, l, send, recv):
            cp.start()
        refs[-1][...] = jnp.zeros((8, 128), F32)

    hbm = [pltpu.HBM(a.shape, a.dtype) for a in srcs + lands]
    outs = pl.pallas_call(
        body, name=name,
        out_shape=(pltpu.SemaphoreType.DMA((n * (N_DEV - 1),)), pltpu.SemaphoreType.DMA((n * (N_DEV - 1),)), *hbm,
                   _sds((8, 128), F32)),
        in_specs=[_HBM] * (2 * n), out_specs=(_SEM, _SEM, *([_HBM] * (2 * n)), pl.BlockSpec(memory_space=pltpu.VMEM)),
        input_output_aliases={i: 2 + i for i in range(2 * n)},
        compiler_params=pltpu.CompilerParams(has_side_effects=_EFFECT),
    )(*[pltpu.with_memory_space_constraint(a, pltpu.HBM) for a in srcs + lands])
    return outs[0], outs[1], list(outs[2:2 + n]), list(outs[2 + n:2 + 2 * n]), outs[-1]


def exchange_wait(kinds, started, after, name):
    send, recv, srcs, lands, _ = started
    n = len(srcs)

    def body(*refs):
        s, l = refs[:n], refs[n:2 * n]
        for cp in _exchange_copies(kinds, s, l, refs[2 * n], refs[2 * n + 1]):
            cp.wait_send()
            cp.wait_recv()

    outs = pl.pallas_call(
        body, name=name, out_shape=[pltpu.HBM(a.shape, a.dtype) for a in srcs + lands],
        in_specs=[_HBM] * (2 * n) + [_SEM, _SEM, pl.BlockSpec(memory_space=pl.ANY)], out_specs=[_HBM] * (2 * n),
        input_output_aliases={i: i for i in range(2 * n)},
        compiler_params=pltpu.CompilerParams(has_side_effects=_EFFECT),
    )(*srcs, *lands, send, recv, after)
    return list(outs[n:])


def sum_slots(lands, name):
    L = len(lands)
    _, _, r, C = lands[0].shape
    tr = _tile(r, 256, 16)

    def body(*refs):
        o_ref = refs[L]
        for l in range(L):
            @pl.when(pl.program_id(0) == l)
            def _(x_ref=refs[l]):
                acc = x_ref[0].astype(F32)
                for s in range(1, N_DEV):
                    acc = acc + x_ref[s].astype(F32)
                o_ref[...] = acc

    specs = [pl.BlockSpec((N_DEV, None, tr, C), lambda g, i, l=l: (0, 0, jnp.where(g == l, i, 0), 0)) for l in range(L)]
    return pl.pallas_call(
        body, grid=(L, r // tr), in_specs=specs,
        out_specs=pl.BlockSpec((None, tr, C), lambda g, i: (g, i, 0)), out_shape=_sds((L, r, C), F32),
        compiler_params=_cp(("arbitrary", "arbitrary")), name=name)(*lands)


def pack_flat(arrs):
    flat = jnp.concatenate([a.reshape(-1).astype(F32) for a in arrs])
    n = flat.shape[0]
    total = -(-n // 32768) * 32768
    return jnp.pad(flat, (0, total - n)).reshape(total // 128, 128)


def unpack_flat(flat, shapes):
    v = flat.reshape(-1)
    out, off = [], 0
    for s in shapes:
        n = int(np.prod(s))
        out.append(v[off:off + n].reshape(s))
        off += n
    return out


def adamw(w, g, m, v, name):
    shape = w.shape
    C = shape[-1]
    R = int(np.prod(shape)) // C
    tr = _tile(R, 128, 8)
    c1 = 1.0 - ADAM_B1 ** ADAM_STEP
    c2 = 1.0 - ADAM_B2 ** ADAM_STEP

    def body(w_ref, g_ref, m_ref, v_ref, d_ref, nm_ref, nv_ref):
        g = g_ref[...]
        m2 = ADAM_B1 * m_ref[...] + (1.0 - ADAM_B1) * g
        v2 = ADAM_B2 * v_ref[...] + (1.0 - ADAM_B2) * (g * g)
        nm_ref[...] = m2
        nv_ref[...] = v2
        d_ref[...] = -ADAM_LR * ((m2 / c1) / (jnp.sqrt(v2 / c2) + ADAM_EPS) + ADAM_WD * w_ref[...])

    spec = pl.BlockSpec((tr, C), lambda i: (i, 0))
    outs = pl.pallas_call(body, grid=(R // tr,), in_specs=[spec] * 4, out_specs=[spec] * 3,
                          out_shape=[_sds((R, C), F32)] * 3, compiler_params=_cp(("parallel",)), name=name)(
        w.reshape(R, C), g.reshape(R, C), m.reshape(R, C), v.reshape(R, C))
    return [o.reshape(shape) for o in outs]


def _bexp(sg_b):
    return jnp.repeat(sg_b.T, HEAD_DIM, axis=1)


def _block_diag(pool_w):
    out = jnp.zeros((GW, GW), F32)
    for i in range(4):
        out = out.at[i * 64:(i + 1) * 64, i * 64:(i + 1) * 64].set(pool_w[i])
    return out


def _cmp_rows(t):
    B, S, _ = t.shape
    t2 = t.reshape(B, S // CMP_STRIDE, CMP_STRIDE * HEAD_DIM)
    nxt = jnp.concatenate([t2[:, 1:], jnp.zeros_like(t2[:, :1])], axis=1)
    return jnp.concatenate([t2, nxt], axis=-1)


def _tile4(t):
    return jnp.tile(t, (1, 1, 4)).astype(BF16)


def layer_fwd(x, p, late, B, S):
    T = B * S
    sv = {"x0": x}
    h1, h1t = rms_fwd(x, p["g_pre_mix"], "rms_pre_mix")
    z = mm(h1, p["w_in"], name="mm_in")
    z3 = z.reshape(B, S, ZW)
    ya = mixa_fwd(z, p["sg_ln_g"], p["sg_w"], p["bexp"])
    yb = mixb_fwd(z3, p["cv_w"], p["cv_b"], p["cv_ln_g"], p["cv_ln_b"], p["cv_pw"], p["cv_pw_b"])
    kv = z3[:, :, COL_KV:COL_KV + 384]
    ks = [kv[:, :, i * 64:(i + 1) * 64] for i in range(6)]
    tbk, tbv = _cmp_rows(ks[0]), _cmp_rows(ks[1])
    kc, vc = cmp_kv_fwd(tbk, tbv, p["cmp_pos_k"], p["cmp_pos_v"], p["cmp_w1_k"], p["cmp_w2_k"], p["cmp_w1_v"], p["cmp_w2_v"])
    kc4, vc4 = _tile4(kc), _tile4(vc)
    ks4, vs4, kw4, vw4 = _tile4(ks[2]), _tile4(ks[3]), _tile4(ks[4]), _tile4(ks[5])
    o_cmp, selT = cmp_attn_fwd(z3, kc4, vc4)
    o_slc, lse_slc = attn_fwd("slc", z3, ks4, vs4, selT)
    o_win, lse_win = attn_fwd("win", z3, kw4, vw4, selT)
    yc = combine_fwd(z3, o_cmp, o_slc, o_win)
    yd = mixd_fwd(z3, p["pool_bd"], p["pool_scale"])
    ycat = jnp.concatenate([ya, yb.reshape(T, GW), yc.reshape(T, GW), yd.reshape(T, GW)], axis=-1)
    p.update(late(ycat))
    mix = mm(ycat, p["w_out"], name="mm_out")
    x1 = rms_post_fwd(x, mix, p["g_post_mix"], "rms_post_mix")
    h2, h2t = rms_fwd(x1, p["g_pre_ffn"], "rms_pre_ffn")
    gu4, a3 = ffn_up_fwd(h2, p["w_gu"])
    f = mm(a3, p["w_down"], blk="k", name="mm_down")
    x2 = rms_post_fwd(x1, f, p["g_post_ffn"], "rms_post_ffn")
    sv.update(h1t=h1t, z=z, tbk=tbk, tbv=tbv, kc4=kc4, vc4=vc4, ks4=ks4, vs4=vs4, kw4=kw4, vw4=vw4, o_cmp=o_cmp, selT=selT,
              o_slc=o_slc, lse_slc=lse_slc, o_win=o_win, lse_win=lse_win, ycat=ycat, mix=mix, x1=x1, h2t=h2t, gu4=gu4, a3=a3, f=f)
    return x2, sv


def _unfold(t4):
    return t4[:, :, :HEAD_DIM]


def layer_bwd_ffn(dx2, p, sv, B, S):
    T = B * S
    gb, gs = {}, {}
    df, gs["g_post_ffn"] = rms_bwd(sv["f"], p["g_post_ffn"], dx2, None, BF16, "rms_post_ffn_bwd")
    dgu = ffn_down_dx(df, p["w_down"], sv["gu4"]).reshape(N_DEV, T, FFN_BLK)
    gb["w_down"] = mm(sv["a3"], df, ta=True, blk="m", out_dtype=BF16, name="mm_down_dw")
    dh2 = mm(dgu, p["w_gu"], tb=True, blk="k", name="mm_gu_dx")
    gb["w_gu"] = mm(sv["h2t"], dgu, blk="n", out_dtype=BF16, name="mm_gu_dw")
    dx1, gs["g_pre_ffn"] = rms_bwd(sv["x1"], p["g_pre_ffn"], dh2, dx2, F32, "rms_pre_ffn_bwd")
    gb["w_gu"] = gb["w_gu"].reshape(1, N_DEV * D_MODEL, FFN_BLK)
    gb["w_down"] = gb["w_down"].reshape(1, FFN_HIDDEN, D_MODEL)
    return dx1, gb, gs


def layer_bwd_mix(dx1, p, sv, B, S):
    T = B * S
    gb, gs = {}, {}
    dmix, gs["g_post_mix"] = rms_bwd(sv["mix"], p["g_post_mix"], dx1, None, BF16, "rms_post_mix_bwd")
    dycat = mm(dmix, p["w_out"], tb=True, name="mm_out_dx")
    gb["w_out"] = mm(sv["ycat"], dmix, ta=True, out_dtype=BF16, name="mm_out_dw")
    dycat3 = dycat.reshape(B, S, D_MODEL)
    z = sv["z"]
    z3 = z.reshape(B, S, ZW)
    du, dv, gs["sg_w"], db, gs["sg_ln_g"] = mixa_bwd(z, dycat, p["sg_ln_g"], p["sg_w"], p["bexp"])
    gs["sg_b"] = db[:, :4].T
    (da, dgt, gs["cv_w"], gs["cv_b"], gs["cv_ln_g"], gs["cv_ln_b"], gpw, gs["cv_pw_b"]) = mixb_bwd(
        z3, dycat3, p["cv_w"], p["cv_b"], p["cv_ln_g"], p["cv_ln_b"], p["cv_pw"], p["cv_pw_b"])
    gb["cv_pw"] = gpw.astype(BF16)
    dd, dwbd, gs["pool_scale"] = mixd_bwd(z3, dycat3, p["pool_bd"], p["pool_scale"])
    gs["pool_w"] = jnp.stack([dwbd[i * 64:(i + 1) * 64, i * 64:(i + 1) * 64] for i in range(4)])
    do_c, do_s, do_w, dgl = combine_bwd(z3, sv["o_cmp"], sv["o_slc"], sv["o_win"], dycat3)
    dq_s, dks4, dvs4 = attn_bwd("slc", z3, sv["ks4"], sv["vs4"], sv["selT"], sv["o_slc"], sv["lse_slc"], do_s)
    dq_w, dkw4, dvw4 = attn_bwd("win", z3, sv["kw4"], sv["vw4"], sv["selT"], sv["o_win"], sv["lse_win"], do_w)
    dq_c, dkc4, dvc4 = cmp_attn_bwd(z3, sv["kc4"], sv["vc4"], do_c)
    (dk2, dv2, gs["cmp_pos_k"], gs["cmp_pos_v"], gw1k, gs["cmp_w2_k"], gw1v, gs["cmp_w2_v"]) = cmp_kv_bwd(
        sv["tbk"], sv["tbv"], p["cmp_pos_k"], p["cmp_pos_v"], p["cmp_w1_k"], p["cmp_w2_k"], p["cmp_w1_v"], p["cmp_w2_v"],
        _unfold(dkc4), _unfold(dvc4))
    gb["cmp_w1_k"], gb["cmp_w1_v"] = gw1k.astype(BF16), gw1v.astype(BF16)
    dkv = jnp.concatenate([dk2.reshape(B, S, HEAD_DIM), dv2.reshape(B, S, HEAD_DIM), _unfold(dks4), _unfold(dvs4),
                           _unfold(dkw4), _unfold(dvw4)], axis=-1).reshape(T, 384)
    dz = assemble_dz(du, dv, da.reshape(T, GW), dgt.reshape(T, GW), dq_c.reshape(T, GW), dq_s.reshape(T, GW),
                     dq_w.reshape(T, GW), dd.reshape(T, GW), dkv, dgl.reshape(T, 128))
    dh1 = mm(dz, p["w_in"], tb=True, name="mm_in_dx")
    gb["w_in"] = mm(sv["h1t"], dz, out_dtype=BF16, name="mm_in_dw")
    dx0, gs["g_pre_mix"] = rms_bwd(sv["x0"], p["g_pre_mix"], dh1, dx1, F32, "rms_pre_mix_bwd")
    return dx0, gb, gs


SMALL = ["g_pre_mix", "g_post_mix", "g_pre_ffn", "g_post_ffn", "sg_ln_g", "sg_w", "sg_b", "cv_w", "cv_b", "cv_ln_g", "cv_ln_b",
         "cv_pw_b", "cmp_pos_k", "cmp_pos_v", "cmp_w2_k", "cmp_w2_v", "pool_w", "pool_scale"]
BIG = ["w_in", "w_out", "w_gu", "w_down", "cmp_w1_k", "cmp_w1_v", "cv_pw"]
NAMES = ["g_pre_mix", "g_post_mix", "g_pre_ffn", "g_post_ffn", "w_in", "sg_ln_g", "sg_w", "sg_b", "cv_w", "cv_b", "cv_ln_g",
         "cv_ln_b", "cv_pw", "cv_pw_b", "cmp_pos_k", "cmp_pos_v", "cmp_w1_k", "cmp_w2_k", "cmp_w1_v", "cmp_w2_v", "pool_w",
         "pool_scale", "w_out", "ffn_w_gu", "ffn_w_down"]


def kernel(x, g_pre_mix, g_post_mix, g_pre_ffn, g_post_ffn, w_in, sg_ln_g, sg_w, sg_b, cv_w, cv_b, cv_ln_g, cv_ln_b, cv_pw, cv_pw_b, cmp_pos_k, cmp_pos_v, cmp_w1_k, cmp_w2_k, cmp_w1_v, cmp_w2_v, pool_w, pool_scale, w_out, ffn_w_gu, ffn_w_down, loss_target, m_g_pre_mix, m_g_post_mix, m_g_pre_ffn, m_g_post_ffn, m_w_in, m_sg_ln_g, m_sg_w, m_sg_b, m_cv_w, m_cv_b, m_cv_ln_g, m_cv_ln_b, m_cv_pw, m_cv_pw_b, m_cmp_pos_k, m_cmp_pos_v, m_cmp_w1_k, m_cmp_w2_k, m_cmp_w1_v, m_cmp_w2_v, m_pool_w, m_pool_scale, m_w_out, m_ffn_w_gu, m_ffn_w_down, v_g_pre_mix, v_g_post_mix, v_g_pre_ffn, v_g_post_ffn, v_w_in, v_sg_ln_g, v_sg_w, v_sg_b, v_cv_w, v_cv_b, v_cv_ln_g, v_cv_ln_b, v_cv_pw, v_cv_pw_b, v_cmp_pos_k, v_cmp_pos_v, v_cmp_w1_k, v_cmp_w2_k, v_cmp_w1_v, v_cmp_w2_v, v_pool_w, v_pool_scale, v_w_out, v_ffn_w_gu, v_ffn_w_down):
    args = dict(locals())
    W = {n: args[n] for n in NAMES}
    M = {n: args["m_" + n] for n in NAMES}
    V = {n: args["v_" + n] for n in NAMES}
    B, S, _ = x.shape
    T = B * S
    L = w_in.shape[0]
    me = _index(_my_pos())
    cpd = GW // N_DEV

    shards = {"w_in": pack_cols(w_in).astype(BF16), "w_out": w_out.astype(BF16), "w_gu": ffn_w_gu.astype(BF16),
              "w_down": ffn_w_down.astype(BF16), "cmp_w1_k": cmp_w1_k.astype(BF16), "cmp_w1_v": cmp_w1_v.astype(BF16),
              "cv_pw": cv_pw.astype(BF16), "cv_w": jnp.swapaxes(cv_w, 1, 2)}
    early, later = ["w_in", "cmp_w1_k", "cmp_w1_v", "cv_pw", "cv_w"], ["w_out", "w_gu", "w_down"]

    def start_gather(names, l, tag):
        return exchange_start(["gather"] * len(names), [shards[n][l][None] for n in names], "gather_%s_start_%d" % (tag, l))

    def wait_gather(names, started, after, l, tag):
        arrived = exchange_wait(["gather"] * len(names), started, after, "gather_%s_wait_%d" % (tag, l))
        full = {n: a[0] for n, a in zip(names, arrived)}
        if "w_gu" in full:
            full["w_gu"] = full["w_gu"].reshape(N_DEV, D_MODEL, FFN_BLK)
            full["w_down"] = full["w_down"].reshape(4, FFN_BLK, D_MODEL)
        if "cv_w" in full:
            full["cv_w"] = full["cv_w"].T
        return full

    def layer_params(l, full):
        p = dict(full)
        for n in ("g_pre_mix", "g_post_mix", "g_pre_ffn", "g_post_ffn", "sg_ln_g", "cv_b", "cv_ln_g", "cv_ln_b", "cv_pw_b",
                  "pool_scale"):
            p[n] = W[n][l][None, :]
        p["sg_w"] = sg_w[l]
        p["bexp"] = _bexp(sg_b[l])
        p["cmp_pos_k"] = cmp_pos_k[l].reshape(1, 2048)
        p["cmp_pos_v"] = cmp_pos_v[l].reshape(1, 2048)
        p["cmp_w2_k"], p["cmp_w2_v"] = cmp_w2_k[l], cmp_w2_v[l]
        p["pool_bd"] = _block_diag(pool_w[l])
        return p

    xs = x.reshape(T, D_MODEL)
    params, saved = [], []
    started = (start_gather(early, 0, "early"), start_gather(later, 0, "later"))
    for l in range(L):
        ahead = (start_gather(early, l + 1, "early"), start_gather(later, l + 1, "later")) if l + 1 < L else None
        p = layer_params(l, wait_gather(early, started[0], (ahead or started)[1][4], l, "early"))
        xs, sv = layer_fwd(xs, p, lambda after, st=started[1], l=l: wait_gather(later, st, after, l, "later"), B, S)
        params.append(p)
        saved.append(sv)
        started = ahead
    dy, lpart = loss_fwd_bwd(xs, loss_target.reshape(T, D_MODEL))
    loss = lax.psum(lpart[0, 0], ("x", "y", "c"))

    ffn_big, mix_big = ["w_gu", "w_down"], ["w_in", "w_out", "cmp_w1_k", "cmp_w1_v", "cv_pw"]
    ffn_kinds, mix_kinds = ["scatter"] * len(ffn_big), ["scatter"] * len(mix_big) + ["gather"]
    pending, token = [], None
    for l in reversed(range(L)):
        p = dict(params[l])
        if token is not None:
            p["g_post_ffn"] = p["g_post_ffn"] + token[0, 0]
        dy, gb_ffn, gs = layer_bwd_ffn(dy, p, saved[l], B, S)
        st_ffn = exchange_start(ffn_kinds, [gb_ffn[n] for n in ffn_big], "scatter_ffn_start_%d" % l)
        p["g_post_mix"] = p["g_post_mix"] + st_ffn[4][0, 0]
        dy, gb_mix, gs_mix = layer_bwd_mix(dy, p, saved[l], B, S)
        gs.update(gs_mix)
        small_shapes = [tuple(gs[n].shape) for n in SMALL]
        st_mix = exchange_start(mix_kinds, [gb_mix[n][None] for n in mix_big] + [pack_flat([gs[n] for n in SMALL])[None]],
                                "scatter_mix_start_%d" % l)
        token = st_mix[4]
        pending.append((l, st_ffn, st_mix))
    grad_x = dy.reshape(B, S, D_MODEL)

    lands = {}
    for l, st_ffn, st_mix in pending:
        got = exchange_wait(ffn_kinds, st_ffn, dy, "scatter_ffn_wait_%d" % l) \
            + exchange_wait(mix_kinds, st_mix, dy, "scatter_mix_wait_%d" % l)
        lands[l] = dict(zip(ffn_big + mix_big + ["small"], got))
    grads = {}
    for n in BIG:
        grads[n] = sum_slots([lands[l][n] for l in range(L)], "sum_" + n)
    grads["w_in"] = unpack_cols(grads["w_in"])
    grads["ffn_w_gu"], grads["ffn_w_down"] = grads.pop("w_gu"), grads.pop("w_down")
    rows = lands[0]["small"].shape[1] // N_DEV
    reduced = sum_slots([lands[l]["small"].reshape(N_DEV, 1, rows, 128) for l in range(L)], "sum_small")
    per_layer = [unpack_flat(reduced[l], small_shapes) for l in range(L)]
    for i, n in enumerate(SMALL):
        g = jnp.stack([per_layer[l][i] for l in range(L)])
        grads[n] = g.reshape(W[n].shape) if n != "cv_w" else g
    grads["cv_w"] = lax.dynamic_slice(grads["cv_w"], (0, 0, me * cpd), (L, CONV_WIDTH, cpd))

    delta, new_m, new_v = {}, {}, {}
    big_names = ["w_in", "w_out", "ffn_w_gu", "ffn_w_down", "cmp_w1_k", "cmp_w1_v", "cv_pw"]
    for n in big_names:
        delta[n], new_m[n], new_v[n] = adamw(W[n], grads[n], M[n], V[n], "adamw_" + n)
    shapes = [W[n].shape for n in SMALL]
    packed = adamw(pack_flat([W[n] for n in SMALL]), pack_flat([grads[n] for n in SMALL]),
                   pack_flat([M[n] for n in SMALL]), pack_flat([V[n] for n in SMALL]), "adamw_small")
    for out, flat in zip((delta, new_m, new_v), packed):
        for n, a in zip(SMALL, unpack_flat(flat, shapes)):
            out[n] = a

    return (loss, grad_x, *[grads[n] for n in NAMES], *[delta[n] for n in NAMES], *[new_m[n] for n in NAMES],
            *[new_v[n] for n in NAMES])
```

```python
import numpy as np
import jax
import jax.numpy as jnp
from jax import lax
from jax.experimental import pallas as pl
from jax.experimental.pallas import tpu as pltpu

F32 = jnp.float32
BF16 = jnp.bfloat16
HI = lax.Precision.HIGHEST

D_MODEL = 1024
GW = 256
HEAD_DIM = 64
ZW = 2048
SG_CHUNK = 128
CONV_WIDTH = 31
CONV_PAD = 32
CMP_STRIDE = 16
N_CMP = 128
SLC_BLOCK_SHIFT = 6
N_SLC = 32
SLC_TOPK = 8
WIN = 512
NEG = -1e30
FORCE_BONUS = 1e4
RMS_EPS = 1e-6
LN_EPS = 1e-5
FFN_HIDDEN = 2816
N_DEV = 8
FFN_BLK = 2 * FFN_HIDDEN // N_DEV
TQ = 256
ROW_TILE = 512
CONV_TILE = 256
VMEM_LIMIT = 56 * 1024 * 1024
MESH = pl.DeviceIdType.MESH

ADAM_LR, ADAM_B1, ADAM_B2, ADAM_EPS, ADAM_WD, ADAM_STEP = 0.001, 0.9, 0.999, 1e-08, 0.01, 10

COL_U, COL_V, COL_A, COL_G, COL_Q, COL_D, COL_KV, COL_GL = 0, 256, 512, 768, 1024, 1280, 1536, 1920


def _sds(shape, dtype):
    return jax.ShapeDtypeStruct(shape, dtype)


def _cp(sem=None):
    return pltpu.CompilerParams(dimension_semantics=sem, vmem_limit_bytes=VMEM_LIMIT)


def _tile(n, target, q=128):
    best = None
    for t in range(q, min(n, target) + 1, q):
        if n % t == 0:
            best = t
    return best or n


def _full(shape):
    nd = len(shape)
    return pl.BlockSpec(shape, lambda *_: (0,) * nd)


def _sigmoid(x):
    return jax.nn.sigmoid(x)


def _dot(a, b):
    return jnp.dot(a, b, preferred_element_type=F32)


def _dot_nt(a, b):
    return lax.dot_general(a, b, (((1,), (1,)), ((), ())), preferred_element_type=F32)


def _dot_tn(a, b):
    return lax.dot_general(a, b, (((0,), (0,)), ((), ())), preferred_element_type=F32)


def _lane_head(width=GW):
    return lax.shift_right_logical(lax.broadcasted_iota(jnp.int32, (1, width), 1), 6)


def _fold_heads(x):
    return x + pltpu.roll(x, 64, 1) + pltpu.roll(x, 128, 1) + pltpu.roll(x, 192, 1)


def pack_cols(w):
    pad = jnp.zeros(w.shape[:-1] + (ZW - 1932,), w.dtype)
    return jnp.concatenate([w[..., :1280], w[..., 1676:1932], w[..., 1280:1664], w[..., 1664:1676], pad], axis=-1)


def unpack_cols(wp):
    return jnp.concatenate([wp[..., :1280], wp[..., 1536:1920], wp[..., 1920:1932], wp[..., 1280:1536]], axis=-1)


def mm(a, b, *, ta=False, tb=False, blk=None, out_dtype=F32, name, tm=1024, tn=1024, tk=1024):
    a_dims = ("k", "m") if ta else ("m", "k")
    b_dims = ("n", "k") if tb else ("k", "n")
    a3, b3, o3 = blk in a_dims and blk is not None, blk in b_dims and blk is not None, blk in ("m", "n")
    size = {}
    size[a_dims[0]], size[a_dims[1]] = a.shape[-2:]
    size[b_dims[0]], size[b_dims[1]] = b.shape[-2:]
    nb = a.shape[0] if a3 else (b.shape[0] if b3 else 1)
    tile = {"m": _tile(size["m"], tm), "n": _tile(size["n"], tn), "k": _tile(size["k"], tk)}
    grid = {d: size[d] // tile[d] for d in "mnk"}
    if blk is not None:
        tile[blk] = size[blk]
        grid[blk] = nb
    nk = grid["k"]

    def spec(dims, is3):
        def im(i, j, k):
            g = {"m": i, "n": j, "k": k}
            idx = tuple(0 if d == blk else g[d] for d in dims)
            return ((g[blk],) + idx) if is3 else idx
        shape = (tile[dims[0]], tile[dims[1]])
        return pl.BlockSpec(((None,) + shape) if is3 else shape, im)

    dn = (((0 if ta else 1,), (1 if tb else 0,)), ((), ()))

    def partial(a_ref, b_ref):
        return lax.dot_general(a_ref[...].astype(BF16), b_ref[...].astype(BF16), dn, preferred_element_type=F32)

    def body_single(a_ref, b_ref, o_ref):
        o_ref[...] = partial(a_ref, b_ref).astype(o_ref.dtype)

    def body_acc(a_ref, b_ref, o_ref, acc):
        k = pl.program_id(2)

        @pl.when(k == 0)
        def _():
            acc[...] = partial(a_ref, b_ref)

        @pl.when((k > 0) & (k < nk - 1))
        def _():
            acc[...] += partial(a_ref, b_ref)

        @pl.when(k == nk - 1)
        def _():
            o_ref[...] = (acc[...] + partial(a_ref, b_ref)).astype(o_ref.dtype)

    oshape = ((nb,) if o3 else ()) + (size["m"], size["n"])
    return pl.pallas_call(
        body_single if nk == 1 else body_acc, grid=(grid["m"], grid["n"], nk),
        in_specs=[spec(a_dims, a3), spec(b_dims, b3)], out_specs=spec(("m", "n"), o3),
        out_shape=_sds(oshape, out_dtype),
        scratch_shapes=[] if nk == 1 else [pltpu.VMEM((tile["m"], tile["n"]), F32)],
        compiler_params=_cp(("parallel", "parallel", "arbitrary")), name=name)(a, b)


def _rows(tm, width):
    return pl.BlockSpec((tm, width), lambda i: (i, 0))


def rms_fwd(x, g, name):
    T = x.shape[0]

    def body(x_ref, g_ref, h_ref, ht_ref):
        x = x_ref[...]
        r = lax.rsqrt(jnp.mean(x * x, axis=-1, keepdims=True) + RMS_EPS)
        h = (x * r) * g_ref[...]
        h_ref[...] = h.astype(h_ref.dtype)
        ht_ref[...] = h.T.astype(ht_ref.dtype)

    return pl.pallas_call(body, grid=(T // ROW_TILE,), in_specs=[_rows(ROW_TILE, D_MODEL), _full((1, D_MODEL))],
                          out_specs=[_rows(ROW_TILE, D_MODEL), pl.BlockSpec((D_MODEL, ROW_TILE), lambda i: (0, i))],
                          out_shape=[_sds((T, D_MODEL), BF16), _sds((D_MODEL, T), BF16)],
                          compiler_params=_cp(("parallel",)), name=name)(x, g)


def rms_post_fwd(xres, m, g, name):
    T = m.shape[0]

    def body(x_ref, m_ref, g_ref, o_ref):
        m = m_ref[...]
        r = lax.rsqrt(jnp.mean(m * m, axis=-1, keepdims=True) + RMS_EPS)
        o_ref[...] = x_ref[...] + (m * r) * g_ref[...]

    return pl.pallas_call(body, grid=(T // ROW_TILE,),
                          in_specs=[_rows(ROW_TILE, D_MODEL), _rows(ROW_TILE, D_MODEL), _full((1, D_MODEL))],
                          out_specs=_rows(ROW_TILE, D_MODEL), out_shape=_sds((T, D_MODEL), F32),
                          compiler_params=_cp(("parallel",)), name=name)(xres, m, g)


def rms_bwd(m, g, dy, dres, out_dtype, name):
    T = m.shape[0]
    has_res = dres is not None

    def body(*refs):
        if has_res:
            m_ref, g_ref, dy_ref, dres_ref, dm_ref, dg_ref = refs
        else:
            m_ref, g_ref, dy_ref, dm_ref, dg_ref = refs
        m = m_ref[...]
        dy = dy_ref[...].astype(F32)
        r = lax.rsqrt(jnp.mean(m * m, axis=-1, keepdims=True) + RMS_EPS)
        n = m * r
        dn = dy * g_ref[...]
        dm = r * (dn - n * jnp.mean(dn * n, axis=-1, keepdims=True))
        if has_res:
            dm = dm + dres_ref[...]
        dm_ref[...] = dm.astype(dm_ref.dtype)

        @pl.when(pl.program_id(0) == 0)
        def _():
            dg_ref[...] = jnp.zeros_like(dg_ref)

        dg_ref[...] += jnp.sum(dy * n, axis=0, keepdims=True)

    ins = [m, g, dy] + ([dres] if has_res else [])
    specs = [_rows(ROW_TILE, D_MODEL), _full((1, D_MODEL)), _rows(ROW_TILE, D_MODEL)] + ([_rows(ROW_TILE, D_MODEL)] if has_res else [])
    return pl.pallas_call(body, grid=(T // ROW_TILE,), in_specs=specs,
                          out_specs=[_rows(ROW_TILE, D_MODEL), _full((1, D_MODEL))],
                          out_shape=[_sds((T, D_MODEL), out_dtype), _sds((1, D_MODEL), F32)],
                          compiler_params=_cp(("arbitrary",)), name=name)(*ins)


def loss_fwd_bwd(y, tgt):
    T = y.shape[0]

    def body(y_ref, t_ref, dy_ref, l_ref):
        e = y_ref[...] - t_ref[...]
        dy_ref[...] = e * (1.0 / D_MODEL)

        @pl.when(pl.program_id(0) == 0)
        def _():
            l_ref[...] = jnp.zeros_like(l_ref)

        l_ref[...] += jnp.full(l_ref.shape, 0.5 * jnp.sum(jnp.mean(e * e, axis=-1, keepdims=True)), F32)

    return pl.pallas_call(body, grid=(T // ROW_TILE,), in_specs=[_rows(ROW_TILE, D_MODEL)] * 2,
                          out_specs=[_rows(ROW_TILE, D_MODEL), _full((8, 128))],
                          out_shape=[_sds((T, D_MODEL), F32), _sds((8, 128), F32)],
                          compiler_params=_cp(("arbitrary",)), name="loss")(y, tgt)


FFN_TILE = 1024


def _gu_spec():
    return pl.BlockSpec((2, None, FFN_TILE, FFN_BLK), lambda i, j: (0, j, i, 0))


def ffn_up_fwd(h, w_gu):
    T = h.shape[0]

    def body(h_ref, wg_ref, wu_ref, gu_ref, a_ref):
        h = h_ref[...]
        gate = _dot(h, wg_ref[...])
        up = _dot(h, wu_ref[...])
        gu_ref[0] = gate
        gu_ref[1] = up
        a_ref[...] = (gate * _sigmoid(gate) * up).astype(a_ref.dtype)

    return pl.pallas_call(
        body, grid=(T // FFN_TILE, 4),
        in_specs=[pl.BlockSpec((FFN_TILE, D_MODEL), lambda i, j: (i, 0)),
                  pl.BlockSpec((None, D_MODEL, FFN_BLK), lambda i, j: (j, 0, 0)),
                  pl.BlockSpec((None, D_MODEL, FFN_BLK), lambda i, j: (j + 4, 0, 0))],
        out_specs=[_gu_spec(), pl.BlockSpec((None, FFN_TILE, FFN_BLK), lambda i, j: (j, i, 0))],
        out_shape=[_sds((2, 4, T, FFN_BLK), F32), _sds((4, T, FFN_BLK), BF16)],
        compiler_params=_cp(("parallel", "parallel")), name="ffn_up_fwd")(h, w_gu, w_gu)


def ffn_down_dx(df, w_down, gu4):
    T = df.shape[0]

    def body(df_ref, w_ref, gu_ref, d_ref):
        da = _dot_nt(df_ref[...], w_ref[...])
        gate, up = gu_ref[0], gu_ref[1]
        sg = _sigmoid(gate)
        d_ref[0] = (da * up * (sg * (1.0 + gate * (1.0 - sg)))).astype(d_ref.dtype)
        d_ref[1] = (da * (gate * sg)).astype(d_ref.dtype)

    return pl.pallas_call(
        body, grid=(T // FFN_TILE, 4),
        in_specs=[pl.BlockSpec((FFN_TILE, D_MODEL), lambda i, j: (i, 0)),
                  pl.BlockSpec((None, FFN_BLK, D_MODEL), lambda i, j: (j, 0, 0)), _gu_spec()],
        out_specs=_gu_spec(), out_shape=_sds((2, 4, T, FFN_BLK), BF16),
        compiler_params=_cp(("parallel", "parallel")), name="ffn_down_dx")(df, w_down, gu4)


def _zcol(tm, col):
    return pl.BlockSpec((tm, GW), lambda i: (i, col // GW))


def _sg_common(v, g):
    mu = jnp.mean(v, axis=-1, keepdims=True)
    xc = v - mu
    rstd = lax.rsqrt(jnp.mean(xc * xc, axis=-1, keepdims=True) + LN_EPS)
    vhat = xc * rstd
    return vhat, rstd, vhat * g


def _tril_weights(w_ref):
    tri = lax.broadcasted_iota(jnp.int32, (SG_CHUNK, SG_CHUNK), 0) >= lax.broadcasted_iota(jnp.int32, (SG_CHUNK, SG_CHUNK), 1)
    return tri, [jnp.where(tri, w_ref[h], 0.0).astype(BF16) for h in range(4)]


def mixa_fwd(z, ln_g, w, bexp):
    T = z.shape[0]
    nch = ROW_TILE // SG_CHUNK

    def body(u_ref, v_ref, g_ref, w_ref, be_ref, y_ref):
        _, _, vln = _sg_common(v_ref[...], g_ref[...])
        vb = vln.astype(BF16)
        head = _lane_head()
        _, wh = _tril_weights(w_ref)
        for c in range(nch):
            rows = slice(c * SG_CHUNK, (c + 1) * SG_CHUNK)
            sv = be_ref[...]
            for h in range(4):
                sv = sv + jnp.where(head == h, _dot(wh[h], vb[rows]), 0.0)
            y_ref[rows, :] = (u_ref[rows, :] * sv).astype(y_ref.dtype)

    return pl.pallas_call(body, grid=(T // ROW_TILE,),
                          in_specs=[_zcol(ROW_TILE, COL_U), _zcol(ROW_TILE, COL_V), _full((1, GW)), _full((4, SG_CHUNK, SG_CHUNK)),
                                    _full((SG_CHUNK, GW))],
                          out_specs=_rows(ROW_TILE, GW), out_shape=_sds((T, GW), BF16),
                          compiler_params=_cp(("parallel",)), name="mixa_fwd")(z, z, ln_g, w, bexp)


def mixa_bwd(z, dycat, ln_g, w, bexp):
    T = z.shape[0]
    nch = ROW_TILE // SG_CHUNK
    nsteps = T // ROW_TILE

    def body(u_ref, v_ref, dy_ref, g_ref, w_ref, be_ref, du_ref, dv_ref, dw_ref, db_ref, dg_ref, dbe_acc):
        step = pl.program_id(0)

        @pl.when(step == 0)
        def _():
            dw_ref[...] = jnp.zeros_like(dw_ref)
            dg_ref[...] = jnp.zeros_like(dg_ref)
            dbe_acc[...] = jnp.zeros_like(dbe_acc)

        g = g_ref[...]
        vhat, rstd, vln = _sg_common(v_ref[...], g)
        vb = vln.astype(BF16)
        head = _lane_head()
        tri, wh = _tril_weights(w_ref)
        dgsum = jnp.zeros((1, GW), F32)
        for c in range(nch):
            rows = slice(c * SG_CHUNK, (c + 1) * SG_CHUNK)
            sv = be_ref[...]
            for h in range(4):
                sv = sv + jnp.where(head == h, _dot(wh[h], vb[rows]), 0.0)
            dy = dy_ref[rows, :]
            du_ref[rows, :] = (dy * sv).astype(du_ref.dtype)
            dsv = dy * u_ref[rows, :]
            dbe_acc[...] += dsv
            dvln = jnp.zeros((SG_CHUNK, GW), F32)
            for h in range(4):
                dsvm = jnp.where(head == h, dsv, 0.0).astype(BF16)
                dw_ref[h] += _dot_nt(dsvm, vb[rows])
                dvln = dvln + _dot_tn(wh[h], dsvm)
            vh = vhat[rows]
            dgsum = dgsum + jnp.sum(dvln * vh, axis=0, keepdims=True)
            dvhat = dvln * g
            dv = rstd[rows] * (dvhat - jnp.mean(dvhat, axis=-1, keepdims=True) - vh * jnp.mean(dvhat * vh, axis=-1, keepdims=True))
            dv_ref[rows, :] = dv.astype(dv_ref.dtype)
        dg_ref[...] += dgsum

        @pl.when(step == nsteps - 1)
        def _():
            for h in range(4):
                dw_ref[h] = jnp.where(tri, dw_ref[h], 0.0)
            fold = (lax.shift_right_logical(lax.broadcasted_iota(jnp.int32, (GW, 128), 0), 6)
                    == lax.broadcasted_iota(jnp.int32, (GW, 128), 1)).astype(F32)
            db_ref[...] = jnp.dot(dbe_acc[...], fold, precision=HI, preferred_element_type=F32)

    return pl.pallas_call(
        body, grid=(nsteps,),
        in_specs=[_zcol(ROW_TILE, COL_U), _zcol(ROW_TILE, COL_V), pl.BlockSpec((ROW_TILE, GW), lambda i: (i, 0)),
                  _full((1, GW)), _full((4, SG_CHUNK, SG_CHUNK)), _full((SG_CHUNK, GW))],
        out_specs=[_rows(ROW_TILE, GW), _rows(ROW_TILE, GW), _full((4, SG_CHUNK, SG_CHUNK)), _full((SG_CHUNK, 128)), _full((1, GW))],
        out_shape=[_sds((T, GW), BF16), _sds((T, GW), BF16), _sds((4, SG_CHUNK, SG_CHUNK), F32), _sds((SG_CHUNK, 128), F32),
                   _sds((1, GW), F32)],
        scratch_shapes=[pltpu.VMEM((SG_CHUNK, GW), F32)],
        compiler_params=_cp(("arbitrary",)), name="mixa_bwd")(z, z, dycat, ln_g, w, bexp)


def _seq(S, col):
    return pl.BlockSpec((None, S, GW), lambda b: (b, 0, col // GW))


def _taps(buf, r0, offsets):
    by_phase = {}
    for k, off in enumerate(offsets):
        by_phase.setdefault(off % 8, []).append((k, off))
    for phase, items in sorted(by_phase.items()):
        span = max(off for _, off in items) - phase
        win = buf[pl.ds(r0 + phase, CONV_TILE + span), :]
        for k, off in items:
            yield k, win[off - phase:off - phase + CONV_TILE]


_CONV_FWD_OFFSETS = [CONV_PAD - (CONV_WIDTH - 1) + k for k in range(CONV_WIDTH)]
_CONV_BWD_OFFSETS = [(CONV_WIDTH - 1) - k for k in range(CONV_WIDTH)]


def _conv(pad, r0, cw_ref, cb):
    acc = jnp.zeros((CONV_TILE, GW), F32) + cb
    for k, rows in _taps(pad, r0, _CONV_FWD_OFFSETS):
        acc = acc + cw_ref[k:k + 1, :] * rows
    return acc


def _conv_ln(acc, lg, lb):
    mu = jnp.mean(acc, axis=-1, keepdims=True)
    xc = acc - mu
    rstd = lax.rsqrt(jnp.mean(xc * xc, axis=-1, keepdims=True) + LN_EPS)
    hhat = xc * rstd
    return hhat, rstd, hhat * lg + lb


def mixb_fwd(z3, cw, cb, lg, lb, pw, pwb):
    B, S, _ = z3.shape

    def body(a_ref, gt_ref, cw_ref, cb_ref, lg_ref, lb_ref, pw_ref, pwb_ref, y_ref, hc_ref, pad):
        pad[0:CONV_PAD, :] = jnp.zeros((CONV_PAD, GW), F32)
        pad[CONV_PAD:CONV_PAD + S, :] = a_ref[...] * _sigmoid(gt_ref[...])
        pwv = pw_ref[...].astype(BF16)
        for r0 in range(0, S, CONV_TILE):
            hc = _conv(pad, r0, cw_ref, cb_ref[...])
            hc_ref[r0:r0 + CONV_TILE, :] = hc
            _, _, ln = _conv_ln(hc, lg_ref[...], lb_ref[...])
            s = ln * _sigmoid(ln)
            y_ref[r0:r0 + CONV_TILE, :] = (_dot(s.astype(BF16), pwv) + pwb_ref[...]).astype(y_ref.dtype)

    seq_out = pl.BlockSpec((None, S, GW), lambda b: (b, 0, 0))
    return pl.pallas_call(
        body, grid=(B,),
        in_specs=[_seq(S, COL_A), _seq(S, COL_G), _full((CONV_WIDTH, GW)), _full((1, GW)), _full((1, GW)), _full((1, GW)),
                  _full((GW, GW)), _full((1, GW))],
        out_specs=[seq_out, seq_out], out_shape=[_sds((B, S, GW), BF16), _sds((B, S, GW), F32)],
        scratch_shapes=[pltpu.VMEM((S + CONV_PAD, GW), F32)],
        compiler_params=_cp(("parallel",)), name="mixb_fwd")(z3, z3, cw, cb, lg, lb, pw, pwb)


def mixb_bwd(z3, hc3, dycat3, cw, lg, lb, pw):
    B, S, _ = z3.shape

    def body(a_ref, gt_ref, hc_ref, dy_ref, cw_ref, lg_ref, lb_ref, pw_ref,
             da_ref, dgt_ref, dcw_ref, dcb_ref, dlg_ref, dlb_ref, dpw_ref, dpwb_ref, pad, dpad, dcw_acc):
        @pl.when(pl.program_id(0) == 0)
        def _():
            for r in (dcb_ref, dlg_ref, dlb_ref, dpw_ref, dpwb_ref, dcw_acc):
                r[...] = jnp.zeros_like(r)

        pad[0:CONV_PAD, :] = jnp.zeros((CONV_PAD, GW), F32)
        pad[CONV_PAD:CONV_PAD + S, :] = a_ref[...] * _sigmoid(gt_ref[...])
        dpad[S:S + CONV_PAD, :] = jnp.zeros((CONV_PAD, GW), F32)
        pwv = pw_ref[...].astype(BF16)
        lg = lg_ref[...]
        for r0 in range(0, S, CONV_TILE):
            hhat, rstd, ln = _conv_ln(hc_ref[r0:r0 + CONV_TILE, :], lg, lb_ref[...])
            sg = _sigmoid(ln)
            s = ln * sg
            dy = dy_ref[r0:r0 + CONV_TILE, :]
            dyb = dy.astype(BF16)
            dpw_ref[...] += _dot_tn(s.astype(BF16), dyb)
            dpwb_ref[...] += jnp.sum(dy, axis=0, keepdims=True)
            dln = _dot_nt(dyb, pwv) * (sg * (1.0 + ln * (1.0 - sg)))
            dlg_ref[...] += jnp.sum(dln * hhat, axis=0, keepdims=True)
            dlb_ref[...] += jnp.sum(dln, axis=0, keepdims=True)
            dhh = dln * lg
            dhc = rstd * (dhh - jnp.mean(dhh, axis=-1, keepdims=True) - hhat * jnp.mean(dhh * hhat, axis=-1, keepdims=True))
            dpad[r0:r0 + CONV_TILE, :] = dhc
            dcb_ref[...] += jnp.sum(dhc, axis=0, keepdims=True)
            for k, rows in _taps(pad, r0, _CONV_FWD_OFFSETS):
                dcw_acc[k] += (dhc * rows).reshape(CONV_TILE // 8, 8, GW).sum(axis=0)
        for r0 in range(0, S, CONV_TILE):
            dhg = jnp.zeros((CONV_TILE, GW), F32)
            for k, rows in _taps(dpad, r0, _CONV_BWD_OFFSETS):
                dhg = dhg + cw_ref[k:k + 1, :] * rows
            a = a_ref[r0:r0 + CONV_TILE, :]
            sg = _sigmoid(gt_ref[r0:r0 + CONV_TILE, :])
            da_ref[r0:r0 + CONV_TILE, :] = (dhg * sg).astype(da_ref.dtype)
            dgt_ref[r0:r0 + CONV_TILE, :] = (dhg * a * sg * (1.0 - sg)).astype(dgt_ref.dtype)

        @pl.when(pl.program_id(0) == B - 1)
        def _():
            for k in range(CONV_WIDTH):
                dcw_ref[k:k + 1, :] = jnp.sum(dcw_acc[k], axis=0, keepdims=True)

    seq_out = pl.BlockSpec((None, S, GW), lambda b: (b, 0, 0))
    return pl.pallas_call(
        body, grid=(B,),
        in_specs=[_seq(S, COL_A), _seq(S, COL_G), seq_out, pl.BlockSpec((None, S, GW), lambda b: (b, 0, 1)),
                  _full((CONV_WIDTH, GW)), _full((1, GW)), _full((1, GW)), _full((GW, GW))],
        out_specs=[seq_out, seq_out, _full((CONV_WIDTH, GW)), _full((1, GW)), _full((1, GW)), _full((1, GW)), _full((GW, GW)),
                   _full((1, GW))],
        out_shape=[_sds((B, S, GW), BF16), _sds((B, S, GW), BF16), _sds((CONV_WIDTH, GW), F32), _sds((1, GW), F32),
                   _sds((1, GW), F32), _sds((1, GW), F32), _sds((GW, GW), F32), _sds((1, GW), F32)],
        scratch_shapes=[pltpu.VMEM((S + CONV_PAD, GW), F32), pltpu.VMEM((S + CONV_PAD, GW), F32),
                        pltpu.VMEM((CONV_WIDTH, 8, GW), F32)],
        compiler_params=_cp(("arbitrary",)), name="mixb_bwd")(z3, z3, hc3, dycat3, cw, lg, lb, pw)


POOL_PAD = 16


def _pool_window():
    lane = lax.broadcasted_iota(jnp.int32, (1, GW), 1)
    return jnp.where(lane < 64, 2, jnp.where(lane < 128, 4, jnp.where(lane < 192, 8, 16)))


def _pool_sums(pad, r0, base, sign):
    win = _pool_window()
    acc = pad[pl.ds(r0 + base, CONV_TILE), :]
    out = None
    for i in range(1, 16):
        acc = acc + pad[pl.ds(r0 + base + sign * i, CONV_TILE), :]
        if i + 1 in (2, 4, 8, 16):
            out = acc if out is None else jnp.where(win == i + 1, acc, out)
    return out


def _pool_cnt(r0):
    t1 = r0 + 1 + lax.broadcasted_iota(jnp.int32, (CONV_TILE, 1), 0)
    return jnp.minimum(t1, _pool_window()).astype(F32)


def mixd_fwd(z3, wbd, scale):
    B, S, _ = z3.shape

    def body(x_ref, w_ref, sc_ref, y_ref, pad):
        pad[0:POOL_PAD, :] = jnp.zeros((POOL_PAD, GW), F32)
        pad[POOL_PAD:POOL_PAD + S, :] = x_ref[...]
        wv = w_ref[...].astype(BF16)
        for r0 in range(0, S, CONV_TILE):
            mean = _pool_sums(pad, r0, POOL_PAD, -1) / _pool_cnt(r0)
            p = (mean - x_ref[r0:r0 + CONV_TILE, :]).astype(BF16)
            y_ref[r0:r0 + CONV_TILE, :] = (_dot(p, wv) * sc_ref[...]).astype(y_ref.dtype)

    return pl.pallas_call(
        body, grid=(B,), in_specs=[_seq(S, COL_D), _full((GW, GW)), _full((1, GW))],
        out_specs=pl.BlockSpec((None, S, GW), lambda b: (b, 0, 0)), out_shape=_sds((B, S, GW), BF16),
        scratch_shapes=[pltpu.VMEM((S + POOL_PAD, GW), F32)],
        compiler_params=_cp(("parallel",)), name="mixd_fwd")(z3, wbd, scale)


def mixd_bwd(z3, dycat3, wbd, scale):
    B, S, _ = z3.shape

    def body(x_ref, dy_ref, w_ref, sc_ref, dx_ref, dw_ref, dsc_ref, pad, qpad):
        @pl.when(pl.program_id(0) == 0)
        def _():
            dw_ref[...] = jnp.zeros_like(dw_ref)
            dsc_ref[...] = jnp.zeros_like(dsc_ref)

        pad[0:POOL_PAD, :] = jnp.zeros((POOL_PAD, GW), F32)
        pad[POOL_PAD:POOL_PAD + S, :] = x_ref[...]
        qpad[S:S + POOL_PAD, :] = jnp.zeros((POOL_PAD, GW), F32)
        wv = w_ref[...].astype(BF16)
        for r0 in range(0, S, CONV_TILE):
            cnt = _pool_cnt(r0)
            mean = _pool_sums(pad, r0, POOL_PAD, -1) / cnt
            p = (mean - x_ref[r0:r0 + CONV_TILE, :]).astype(BF16)
            dy = dy_ref[r0:r0 + CONV_TILE, :]
            dsc_ref[...] += jnp.sum(dy * _dot(p, wv), axis=0, keepdims=True)
            dyp = (dy * sc_ref[...]).astype(BF16)
            dw_ref[...] += _dot_tn(p, dyp)
            dp = _dot_nt(dyp, wv)
            dx_ref[r0:r0 + CONV_TILE, :] = (-dp).astype(dx_ref.dtype)
            qpad[r0:r0 + CONV_TILE, :] = dp / cnt
        for r0 in range(0, S, CONV_TILE):
            back = _pool_sums(qpad, r0, 0, 1)
            dx_ref[r0:r0 + CONV_TILE, :] = (dx_ref[r0:r0 + CONV_TILE, :].astype(F32) + back).astype(dx_ref.dtype)

    return pl.pallas_call(
        body, grid=(B,),
        in_specs=[_seq(S, COL_D), pl.BlockSpec((None, S, GW), lambda b: (b, 0, 3)), _full((GW, GW)), _full((1, GW))],
        out_specs=[pl.BlockSpec((None, S, GW), lambda b: (b, 0, 0)), _full((GW, GW)), _full((1, GW))],
        out_shape=[_sds((B, S, GW), F32), _sds((GW, GW), F32), _sds((1, GW), F32)],
        scratch_shapes=[pltpu.VMEM((S + POOL_PAD, GW), F32), pltpu.VMEM((S + POOL_PAD, GW), F32)],
        compiler_params=_cp(("arbitrary",)), name="mixd_bwd")(z3, dycat3, wbd, scale)


def cmp_kv_fwd(tbk, tbv, pek, pev, w1k, w2k, w1v, w2v):
    B = tbk.shape[0]

    def body(tbk_ref, tbv_ref, pek_ref, pev_ref, w1k_ref, w2k_ref, w1v_ref, w2v_ref, kc_ref, vc_ref):
        for tb_ref, pe_ref, w1_ref, w2_ref, o_ref in ((tbk_ref, pek_ref, w1k_ref, w2k_ref, kc_ref),
                                                      (tbv_ref, pev_ref, w1v_ref, w2v_ref, vc_ref)):
            pre = _dot((tb_ref[...] + pe_ref[...]).astype(BF16), w1_ref[...].astype(BF16))
            hm = pre * _sigmoid(pre)
            o_ref[...] = _dot(hm.astype(BF16), w2_ref[...].astype(BF16))

    tb_spec = pl.BlockSpec((None, N_CMP, 2048), lambda b: (b, 0, 0))
    o_spec = pl.BlockSpec((None, N_CMP, HEAD_DIM), lambda b: (b, 0, 0))
    return pl.pallas_call(
        body, grid=(B,),
        in_specs=[tb_spec, tb_spec, _full((1, 2048)), _full((1, 2048)), _full((2048, HEAD_DIM)), _full((HEAD_DIM, HEAD_DIM)),
                  _full((2048, HEAD_DIM)), _full((HEAD_DIM, HEAD_DIM))],
        out_specs=[o_spec, o_spec], out_shape=[_sds((B, N_CMP, HEAD_DIM), F32)] * 2,
        compiler_params=_cp(("parallel",)), name="cmp_kv_fwd")(tbk, tbv, pek, pev, w1k, w2k, w1v, w2v)


def cmp_kv_bwd(tbk, tbv, pek, pev, w1k, w2k, w1v, w2v, dkc, dvc):
    B = tbk.shape[0]

    def body(tbk_ref, tbv_ref, pek_ref, pev_ref, w1k_ref, w2k_ref, w1v_ref, w2v_ref, dkc_ref, dvc_ref,
             dk2_ref, dv2_ref, dpek_ref, dpev_ref, dw1k_ref, dw2k_ref, dw1v_ref, dw2v_ref):
        @pl.when(pl.program_id(0) == 0)
        def _():
            for r in (dpek_ref, dpev_ref, dw1k_ref, dw2k_ref, dw1v_ref, dw2v_ref):
                r[...] = jnp.zeros_like(r)

        row0 = lax.broadcasted_iota(jnp.int32, (N_CMP, 1), 0) == 0
        for tb_ref, pe_ref, w1_ref, w2_ref, do_ref, d2_ref, dpe_ref, dw1_ref, dw2_ref in (
                (tbk_ref, pek_ref, w1k_ref, w2k_ref, dkc_ref, dk2_ref, dpek_ref, dw1k_ref, dw2k_ref),
                (tbv_ref, pev_ref, w1v_ref, w2v_ref, dvc_ref, dv2_ref, dpev_ref, dw1v_ref, dw2v_ref)):
            tb = (tb_ref[...] + pe_ref[...]).astype(BF16)
            w1 = w1_ref[...].astype(BF16)
            pre = _dot(tb, w1)
            sg = _sigmoid(pre)
            hm = (pre * sg).astype(BF16)
            do = do_ref[...].astype(BF16)
            dw2_ref[...] += _dot_tn(hm, do)
            dpre = (_dot_nt(do, w2_ref[...].astype(BF16)) * (sg * (1.0 + pre * (1.0 - sg)))).astype(BF16)
            dw1_ref[...] += _dot_tn(tb, dpre)
            dtb = _dot_nt(dpre, w1)
            dpe_ref[...] += jnp.sum(dtb, axis=0, keepdims=True)
            down = jnp.where(row0, 0.0, pltpu.roll(dtb[:, 1024:], 1, 0))
            d2_ref[...] = dtb[:, :1024] + down

    tb_spec = pl.BlockSpec((None, N_CMP, 2048), lambda b: (b, 0, 0))
    c_spec = pl.BlockSpec((None, N_CMP, HEAD_DIM), lambda b: (b, 0, 0))
    d2_spec = pl.BlockSpec((None, N_CMP, 1024), lambda b: (b, 0, 0))
    return pl.pallas_call(
        body, grid=(B,),
        in_specs=[tb_spec, tb_spec, _full((1, 2048)), _full((1, 2048)), _full((2048, HEAD_DIM)), _full((HEAD_DIM, HEAD_DIM)),
                  _full((2048, HEAD_DIM)), _full((HEAD_DIM, HEAD_DIM)), c_spec, c_spec],
        out_specs=[d2_spec, d2_spec, _full((1, 2048)), _full((1, 2048)), _full((2048, HEAD_DIM)), _full((HEAD_DIM, HEAD_DIM)),
                   _full((2048, HEAD_DIM)), _full((HEAD_DIM, HEAD_DIM))],
        out_shape=[_sds((B, N_CMP, 1024), F32)] * 2 + [_sds((1, 2048), F32)] * 2
        + [_sds((2048, HEAD_DIM), F32), _sds((HEAD_DIM, HEAD_DIM), F32)] * 2,
        compiler_params=_cp(("arbitrary",)), name="cmp_kv_bwd")(tbk, tbv, pek, pev, w1k, w2k, w1v, w2v, dkc, dvc)


def _qtile(col):
    return pl.BlockSpec((None, TQ, GW), lambda b, i: (b, i, col // GW))


def _qtile0():
    return pl.BlockSpec((None, TQ, GW), lambda b, i: (b, i, 0))


def _cmp_probs(q, kc, qpos):
    head = _lane_head()
    cend = lax.broadcasted_iota(jnp.int32, (1, N_CMP), 1) * CMP_STRIDE + 31
    cmask = cend <= qpos
    has = qpos >= 31
    out = []
    for h in range(4):
        qm = jnp.where(head == h, q, 0.0).astype(BF16)
        s = jnp.where(cmask, _dot_nt(qm, kc), NEG)
        e = jnp.exp(s - jnp.max(s, axis=-1, keepdims=True))
        p = jnp.where(has, e / jnp.sum(e, axis=-1, keepdims=True), 0.0)
        out.append((qm, p))
    return out


def cmp_attn_fwd(z3, kc4, vc4):
    B, S, _ = z3.shape

    def body(q_ref, kc_ref, vc_ref, o_ref, sel_ref):
        t0 = pl.program_id(1) * TQ
        qpos = t0 + lax.broadcasted_iota(jnp.int32, (TQ, 1), 0)
        head = _lane_head()
        kc, vc = kc_ref[...], vc_ref[...]
        o = jnp.zeros((TQ, GW), F32)
        psum = jnp.zeros((TQ, N_CMP), F32)
        for h, (_, p) in enumerate(_cmp_probs(q_ref[...] * 0.125, kc, qpos)):
            o = o + jnp.where(head == h, _dot(p.astype(BF16), vc), 0.0)
            psum = psum + p
        o_ref[...] = o
        cst = lax.broadcasted_iota(jnp.int32, (N_SLC, N_CMP), 1) * CMP_STRIDE
        jst = lax.broadcasted_iota(jnp.int32, (N_SLC, N_CMP), 0) * 64
        overlap = ((cst <= jst + 63) & (cst + 31 >= jst)).astype(BF16)
        imp = _dot_nt(overlap, psum.astype(BF16))
        qp = t0 + lax.broadcasted_iota(jnp.int32, (1, TQ), 1)
        jj = lax.broadcasted_iota(jnp.int32, (N_SLC, 1), 0)
        cur = lax.shift_right_logical(qp, SLC_BLOCK_SHIFT)
        forced = (jj == 0) | (jj == cur) | (jj == cur - 1)
        score = jnp.where(jj * 64 <= qp, imp + jnp.where(forced, FORCE_BONUS, 0.0), NEG)
        rank = jnp.zeros((N_SLC, TQ), F32)
        for j2 in range(N_SLC):
            sj = score[j2:j2 + 1, :]
            rank = rank + jnp.where((sj > score) | ((sj == score) & (j2 < jj)), 1.0, 0.0)
        sel_ref[...] = jnp.where((rank < SLC_TOPK) & (score > NEG / 2), 1.0, 0.0)

    c_spec = pl.BlockSpec((None, N_CMP, GW), lambda b, i: (b, 0, 0))
    return pl.pallas_call(
        body, grid=(B, S // TQ), in_specs=[_qtile(COL_Q), c_spec, c_spec],
        out_specs=[_qtile0(), pl.BlockSpec((None, N_SLC, TQ), lambda b, i: (b, 0, i))],
        out_shape=[_sds((B, S, GW), F32), _sds((B, N_SLC, S), F32)],
        compiler_params=_cp(("parallel", "parallel")), name="cmp_attn_fwd")(z3, kc4, vc4)


def cmp_attn_bwd(z3, kc4, vc4, do):
    B, S, _ = z3.shape
    nq = S // TQ

    def body(q_ref, kc_ref, vc_ref, do_ref, dq_ref, dkc_ref, dvc_ref):
        qi = pl.program_id(1)

        @pl.when(qi == 0)
        def _():
            dkc_ref[...] = jnp.zeros_like(dkc_ref)
            dvc_ref[...] = jnp.zeros_like(dvc_ref)

        qpos = qi * TQ + lax.broadcasted_iota(jnp.int32, (TQ, 1), 0)
        head = _lane_head()
        kc, vc, do = kc_ref[...], vc_ref[...], do_ref[...]
        dq = jnp.zeros((TQ, GW), F32)
        for h, (qm, p) in enumerate(_cmp_probs(q_ref[...] * 0.125, kc, qpos)):
            dom = jnp.where(head == h, do, 0.0).astype(BF16)
            dp = _dot_nt(dom, vc)
            ds = (p * (dp - jnp.sum(p * dp, axis=-1, keepdims=True))).astype(BF16)
            dq = dq + jnp.where(head == h, _dot(ds, kc), 0.0)
            dkc_ref[...] += _dot_tn(ds, qm)
            dvc_ref[...] += _dot_tn(p.astype(BF16), dom)
        dq_ref[...] = dq * 0.125

        @pl.when(qi == nq - 1)
        def _():
            dkc_ref[...] = _fold_heads(dkc_ref[...])
            dvc_ref[...] = _fold_heads(dvc_ref[...])

    c_spec = pl.BlockSpec((None, N_CMP, GW), lambda b, i: (b, 0, 0))
    return pl.pallas_call(
        body, grid=(B, nq), in_specs=[_qtile(COL_Q), c_spec, c_spec, _qtile0()],
        out_specs=[_qtile0(), c_spec, c_spec],
        out_shape=[_sds((B, S, GW), F32), _sds((B, N_CMP, GW), F32), _sds((B, N_CMP, GW), F32)],
        compiler_params=_cp(("parallel", "arbitrary")), name="cmp_attn_bwd")(z3, kc4, vc4, do)


def _attn_mask(mode, qpos, k0, sel_b):
    kpos = k0 + lax.broadcasted_iota(jnp.int32, (1, TQ), 1)
    mask = kpos <= qpos
    if mode == "win":
        return mask & (kpos > qpos - WIN)
    blk = lax.shift_right_logical(k0 + lax.broadcasted_iota(jnp.int32, (N_SLC, TQ), 1), SLC_BLOCK_SHIFT)
    expand = (blk == lax.broadcasted_iota(jnp.int32, (N_SLC, TQ), 0)).astype(BF16)
    return mask & (_dot_tn(sel_b, expand) > 0.5)


def _attn_lo(mode, qi):
    return jnp.maximum(qi - WIN // TQ, 0) if mode == "win" else 0


def attn_fwd(mode, z3, k4, v4, selT):
    B, S, _ = z3.shape

    def body(q_ref, k_ref, v_ref, sel_ref, o_ref, lse_ref, s_all, m_acc, l_acc, o_acc):
        qi = pl.program_id(1)
        qpos = qi * TQ + lax.broadcasted_iota(jnp.int32, (TQ, 1), 0)
        head = _lane_head()
        q = q_ref[...] * 0.125
        qm = [jnp.where(head == h, q, 0.0).astype(BF16) for h in range(4)]
        sel_b = sel_ref[...].astype(BF16)
        lo, hi = _attn_lo(mode, qi), qi + 1
        m_acc[...] = jnp.full(m_acc.shape, NEG, F32)

        def scores(kb, carry):
            k0 = pl.multiple_of(kb * TQ, TQ)
            kblk = k_ref[pl.ds(k0, TQ), :]
            mask = _attn_mask(mode, qpos, k0, sel_b)
            for h in range(4):
                s = jnp.where(mask, _dot_nt(qm[h], kblk), NEG)
                s_all[h, kb] = s
                m_acc[h] = jnp.maximum(m_acc[h], s)
            return carry

        lax.fori_loop(lo, hi, scores, 0)
        for h in range(4):
            m_acc[h] = jnp.broadcast_to(jnp.max(m_acc[h], axis=-1, keepdims=True), (TQ, TQ))
        l_acc[...] = jnp.zeros_like(l_acc)
        o_acc[...] = jnp.zeros_like(o_acc)

        def weights(kb, carry):
            vblk = v_ref[pl.ds(pl.multiple_of(kb * TQ, TQ), TQ), :]
            for h in range(4):
                p = jnp.exp(s_all[h, kb] - m_acc[h])
                l_acc[h] += p
                o_acc[h] += _dot(p.astype(BF16), vblk)
            return carry

        lax.fori_loop(lo, hi, weights, 0)
        o = jnp.zeros((TQ, GW), F32)
        lse = jnp.zeros((TQ, 128), F32)
        lane = lax.broadcasted_iota(jnp.int32, (1, 128), 1)
        for h in range(4):
            l = jnp.sum(l_acc[h], axis=-1, keepdims=True)
            o = o + jnp.where(head == h, o_acc[h] / l, 0.0)
            lse = jnp.where(lane == h, jnp.max(m_acc[h], axis=-1, keepdims=True) + jnp.log(l), lse)
        o_ref[...] = o
        lse_ref[...] = lse

    kv_spec = pl.BlockSpec((None, S, GW), lambda b, i: (b, 0, 0))
    return pl.pallas_call(
        body, grid=(B, S // TQ),
        in_specs=[_qtile(COL_Q), kv_spec, kv_spec, pl.BlockSpec((None, N_SLC, TQ), lambda b, i: (b, 0, i))],
        out_specs=[_qtile0(), pl.BlockSpec((None, TQ, 128), lambda b, i: (b, i, 0))],
        out_shape=[_sds((B, S, GW), F32), _sds((B, S, 128), F32)],
        scratch_shapes=[pltpu.VMEM((4, S // TQ, TQ, TQ), F32), pltpu.VMEM((4, TQ, TQ), F32), pltpu.VMEM((4, TQ, TQ), F32),
                        pltpu.VMEM((4, TQ, GW), F32)],
        compiler_params=_cp(("parallel", "parallel")), name=mode + "_attn_fwd")(z3, k4, v4, selT)


def attn_bwd(mode, z3, k4, v4, selT, o, lse, do):
    B, S, _ = z3.shape
    nq = S // TQ

    def body(q_ref, k_ref, v_ref, sel_ref, o_ref, lse_ref, do_ref, dq_ref, dk_ref, dv_ref, dq_s):
        qi = pl.program_id(1)

        @pl.when(qi == 0)
        def _():
            dk_ref[...] = jnp.zeros_like(dk_ref)
            dv_ref[...] = jnp.zeros_like(dv_ref)

        qpos = qi * TQ + lax.broadcasted_iota(jnp.int32, (TQ, 1), 0)
        head = _lane_head()
        lane = lax.broadcasted_iota(jnp.int32, (1, 128), 1)
        q = q_ref[...] * 0.125
        do = do_ref[...]
        doo = do * o_ref[...]
        lse = lse_ref[...]
        qm = [jnp.where(head == h, q, 0.0).astype(BF16) for h in range(4)]
        dom = [jnp.where(head == h, do, 0.0).astype(BF16) for h in range(4)]
        delta = [jnp.sum(jnp.where(head == h, doo, 0.0), axis=-1, keepdims=True) for h in range(4)]
        lse_h = [jnp.max(jnp.where(lane == h, lse, NEG), axis=-1, keepdims=True) for h in range(4)]
        sel_b = sel_ref[...].astype(BF16)
        dq_s[...] = jnp.zeros_like(dq_s)

        def step(kb, carry):
            k0 = pl.multiple_of(kb * TQ, TQ)
            kblk = k_ref[pl.ds(k0, TQ), :]
            vblk = v_ref[pl.ds(k0, TQ), :]
            mask = _attn_mask(mode, qpos, k0, sel_b)
            for h in range(4):
                s = _dot_nt(qm[h], kblk)
                p = jnp.where(mask, jnp.exp(s - lse_h[h]), 0.0)
                dp = _dot_nt(dom[h], vblk)
                ds = (p * (dp - delta[h])).astype(BF16)
                dq_s[...] += jnp.where(head == h, _dot(ds, kblk), 0.0)
                dk_ref[pl.ds(k0, TQ), :] += _dot_tn(ds, qm[h])
                dv_ref[pl.ds(k0, TQ), :] += _dot_tn(p.astype(BF16), dom[h])
            return carry

        lax.fori_loop(_attn_lo(mode, qi), qi + 1, step, 0)
        dq_ref[...] = dq_s[...] * 0.125

        @pl.when(qi == nq - 1)
        def _():
            for r0 in range(0, S, TQ):
                dk_ref[r0:r0 + TQ, :] = _fold_heads(dk_ref[r0:r0 + TQ, :])
                dv_ref[r0:r0 + TQ, :] = _fold_heads(dv_ref[r0:r0 + TQ, :])

    kv_spec = pl.BlockSpec((None, S, GW), lambda b, i: (b, 0, 0))
    return pl.pallas_call(
        body, grid=(B, nq),
        in_specs=[_qtile(COL_Q), kv_spec, kv_spec, pl.BlockSpec((None, N_SLC, TQ), lambda b, i: (b, 0, i)), _qtile0(),
                  pl.BlockSpec((None, TQ, 128), lambda b, i: (b, i, 0)), _qtile0()],
        out_specs=[_qtile0(), kv_spec, kv_spec],
        out_shape=[_sds((B, S, GW), F32)] * 3,
        scratch_shapes=[pltpu.VMEM((TQ, GW), F32)],
        compiler_params=_cp(("parallel", "arbitrary")), name=mode + "_attn_bwd")(z3, k4, v4, selT, o, lse, do)


def _gate_expand(b):
    r = lax.broadcasted_iota(jnp.int32, (128, GW), 0)
    hl = lax.shift_right_logical(lax.broadcasted_iota(jnp.int32, (128, GW), 1), 6)
    return (r == 3 * hl + b).astype(F32)


def combine_fwd(z3, o_cmp, o_slc, o_win):
    B, S, _ = z3.shape

    def body(gl_ref, oc_ref, os_ref, ow_ref, y_ref):
        g = _sigmoid(gl_ref[...])
        y = jnp.zeros((TQ, GW), F32)
        for b, o_ref in enumerate((oc_ref, os_ref, ow_ref)):
            y = y + jnp.dot(g, _gate_expand(b), precision=HI, preferred_element_type=F32) * o_ref[...]
        y_ref[...] = y.astype(y_ref.dtype)

    return pl.pallas_call(
        body, grid=(B, S // TQ),
        in_specs=[pl.BlockSpec((None, TQ, 128), lambda b, i: (b, i, COL_GL // 128)), _qtile0(), _qtile0(), _qtile0()],
        out_specs=_qtile0(), out_shape=_sds((B, S, GW), BF16),
        compiler_params=_cp(("parallel", "parallel")), name="combine_fwd")(z3, o_cmp, o_slc, o_win)


def combine_bwd(z3, o_cmp, o_slc, o_win, dycat3):
    B, S, _ = z3.shape

    def body(gl_ref, oc_ref, os_ref, ow_ref, dy_ref, dc_ref, ds_ref, dw_ref, dgl_ref):
        g = _sigmoid(gl_ref[...])
        dy = dy_ref[...]
        dg = jnp.zeros((TQ, 128), F32)
        for b, (o_ref, d_ref) in enumerate(((oc_ref, dc_ref), (os_ref, ds_ref), (ow_ref, dw_ref))):
            ex = _gate_expand(b)
            d_ref[...] = jnp.dot(g, ex, precision=HI, preferred_element_type=F32) * dy
            dg = dg + lax.dot_general(dy * o_ref[...], ex, (((1,), (1,)), ((), ())), precision=HI, preferred_element_type=F32)
        dgl_ref[...] = dg * g * (1.0 - g)

    gl_spec = pl.BlockSpec((None, TQ, 128), lambda b, i: (b, i, COL_GL // 128))
    return pl.pallas_call(
        body, grid=(B, S // TQ),
        in_specs=[gl_spec, _qtile0(), _qtile0(), _qtile0(), pl.BlockSpec((None, TQ, GW), lambda b, i: (b, i, 2))],
        out_specs=[_qtile0(), _qtile0(), _qtile0(), pl.BlockSpec((None, TQ, 128), lambda b, i: (b, i, 0))],
        out_shape=[_sds((B, S, GW), F32)] * 3 + [_sds((B, S, 128), F32)],
        compiler_params=_cp(("parallel", "parallel")), name="combine_bwd")(z3, o_cmp, o_slc, o_win, dycat3)


def assemble_dz(du, dv, da, dgt, dq_c, dq_s, dq_w, dd, dkv, dgl):
    T = du.shape[0]

    def body(du_ref, dv_ref, da_ref, dgt_ref, dqc_ref, dqs_ref, dqw_ref, dd_ref, dkv_ref, dgl_ref, o_ref):
        o_ref[:, COL_U:COL_U + GW] = du_ref[...]
        o_ref[:, COL_V:COL_V + GW] = dv_ref[...]
        o_ref[:, COL_A:COL_A + GW] = da_ref[...]
        o_ref[:, COL_G:COL_G + GW] = dgt_ref[...]
        o_ref[:, COL_Q:COL_Q + GW] = (dqc_ref[...] + dqs_ref[...] + dqw_ref[...]).astype(BF16)
        o_ref[:, COL_D:COL_D + GW] = dd_ref[...].astype(BF16)
        o_ref[:, COL_KV:COL_KV + 384] = dkv_ref[...].astype(BF16)
        o_ref[:, COL_GL:COL_GL + 128] = dgl_ref[...].astype(BF16)

    specs = [_rows(ROW_TILE, GW)] * 8 + [_rows(ROW_TILE, 384), _rows(ROW_TILE, 128)]
    return pl.pallas_call(body, grid=(T // ROW_TILE,), in_specs=specs, out_specs=_rows(ROW_TILE, ZW),
                          out_shape=_sds((T, ZW), BF16), compiler_params=_cp(("parallel",)),
                          name="assemble_dz")(du, dv, da, dgt, dq_c, dq_s, dq_w, dd, dkv, dgl)


def _my_pos():
    return lax.axis_index("x"), lax.axis_index("y"), lax.axis_index("c")


def _peer(k):
    x, y, c = _my_pos()
    return ((1 - x) if k & 4 else x, (1 - y) if k & 2 else y, (1 - c) if k & 1 else c)


def _index(pos):
    return 4 * pos[0] + 2 * pos[1] + pos[2]


_HBM = pl.BlockSpec(memory_space=pltpu.HBM)


_SEM = pl.BlockSpec(memory_space=pltpu.SEMAPHORE)
_EFFECT = pltpu.SideEffectType.DATAFLOW_SIDE_EFFECTING


def _exchange_copies(kinds, srcs, lands, send, recv):
    me = _index(_my_pos())
    out = []
    for a, kind in enumerate(kinds):
        for k in range(1, N_DEV):
            peer = _peer(k)
            if kind == "gather":
                r = srcs[a].shape[1]
                src, dst = srcs[a], lands[a].at[:, pl.ds(me * r, r), :]
            else:
                r = srcs[a].shape[1] // N_DEV
                src, dst = srcs[a].at[:, pl.ds(_index(peer) * r, r), :], lands[a].at[me]
            sem = a * (N_DEV - 1) + k - 1
            out.append(pltpu.make_async_remote_copy(src_ref=src, dst_ref=dst, send_sem=send.at[sem], recv_sem=recv.at[sem],
                                                    device_id=peer, device_id_type=MESH))
    return out


def _land_with_own(kind, src):
    me = _index(_my_pos())
    if kind == "gather":
        _, r, C = src.shape
        return lax.dynamic_update_slice(lax.empty((1, N_DEV * r, C), src.dtype), src, (0, me * r, 0))
    _, r8, C = src.shape
    r = r8 // N_DEV
    own = lax.dynamic_slice(src, (0, me * r, 0), (1, r, C))
    return lax.dynamic_update_slice(lax.empty((N_DEV, 1, r, C), src.dtype), own[None], (me, 0, 0, 0))


def exchange_start(kinds, srcs, name):
    n = len(srcs)
    lands = [_land_with_own(k, s) for k, s in zip(kinds, srcs)]

    def body(*refs):
        s, l = refs[:n], refs[n:2 * n]
        send, recv = refs[2 * n], refs[2 * n + 1]
        for cp in _exchange_copies(kinds, s, l, send, recv):
            cp.start()
        refs[-1][...] = jnp.zeros((8, 128), F32)

    hbm = [pltpu.HBM(a.shape, a.dtype) for a in srcs + lands]
    outs = pl.pallas_call(
        body, name=name,
        out_shape=(pltpu.SemaphoreType.DMA((n * (N_DEV - 1),)), pltpu.SemaphoreType.DMA((n * (N_DEV - 1),)), *hbm,
                   _sds((8, 128), F32)),
        in_specs=[_HBM] * (2 * n), out_specs=(_SEM, _SEM, *([_HBM] * (2 * n)), pl.BlockSpec(memory_space=pltpu.VMEM)),
        input_output_aliases={i: 2 + i for i in range(2 * n)},
        compiler_params=pltpu.CompilerParams(has_side_effects=_EFFECT),
    )(*[pltpu.with_memory_space_constraint(a, pltpu.HBM) for a in srcs + lands])
    return outs[0], outs[1], list(outs[2:2 + n]), list(outs[2 + n:2 + 2 * n]), outs[-1]


def exchange_wait(kinds, started, after, name):
    send, recv, srcs, lands, _ = started
    n = len(srcs)

    def body(*refs):
        s, l = refs[:n], refs[n:2 * n]
        for cp in _exchange_copies(kinds, s, l, refs[2 * n], refs[2 * n + 1]):
            cp.wait_send()
            cp.wait_recv()
        refs[-1][...] = jnp.zeros((8, 128), F32)

    outs = pl.pallas_call(
        body, name=name, out_shape=[pltpu.HBM(a.shape, a.dtype) for a in srcs + lands] + [_sds((8, 128), F32)],
        in_specs=[_HBM] * (2 * n) + [_SEM, _SEM, pl.BlockSpec(memory_space=pl.ANY)],
        out_specs=[_HBM] * (2 * n) + [pl.BlockSpec(memory_space=pltpu.VMEM)],
        input_output_aliases={i: i for i in range(2 * n)},
        compiler_params=pltpu.CompilerParams(has_side_effects=_EFFECT),
    )(*srcs, *lands, send, recv, after)
    return list(outs[n:2 * n]), outs[-1]


def sum_slots(lands, name):
    L = len(lands)
    _, _, r, C = lands[0].shape
    tr = _tile(r, 256, 16)

    def body(*refs):
        o_ref = refs[L]
        for l in range(L):
            @pl.when(pl.program_id(0) == l)
            def _(x_ref=refs[l]):
                acc = x_ref[0].astype(F32)
                for s in range(1, N_DEV):
                    acc = acc + x_ref[s].astype(F32)
                o_ref[...] = acc

    specs = [pl.BlockSpec((N_DEV, None, tr, C), lambda g, i, l=l: (0, 0, jnp.where(g == l, i, 0), 0)) for l in range(L)]
    return pl.pallas_call(
        body, grid=(L, r // tr), in_specs=specs,
        out_specs=pl.BlockSpec((None, tr, C), lambda g, i: (g, i, 0)), out_shape=_sds((L, r, C), F32),
        compiler_params=_cp(("arbitrary", "arbitrary")), name=name)(*lands)


def pack_flat(arrs):
    flat = jnp.concatenate([a.reshape(-1).astype(F32) for a in arrs])
    n = flat.shape[0]
    total = -(-n // 32768) * 32768
    return jnp.pad(flat, (0, total - n)).reshape(total // 128, 128)


def unpack_flat(flat, shapes):
    v = flat.reshape(-1)
    out, off = [], 0
    for s in shapes:
        n = int(np.prod(s))
        out.append(v[off:off + n].reshape(s))
        off += n
    return out


def adamw(w, g, m, v, name):
    shape = w.shape
    C = shape[-1]
    R = int(np.prod(shape)) // C
    tr = _tile(R, 128, 8)
    c1 = 1.0 - ADAM_B1 ** ADAM_STEP
    c2 = 1.0 - ADAM_B2 ** ADAM_STEP

    def body(w_ref, g_ref, m_ref, v_ref, d_ref, nm_ref, nv_ref):
        g = g_ref[...]
        m2 = ADAM_B1 * m_ref[...] + (1.0 - ADAM_B1) * g
        v2 = ADAM_B2 * v_ref[...] + (1.0 - ADAM_B2) * (g * g)
        nm_ref[...] = m2
        nv_ref[...] = v2
        d_ref[...] = -ADAM_LR * ((m2 / c1) / (jnp.sqrt(v2 / c2) + ADAM_EPS) + ADAM_WD * w_ref[...])

    spec = pl.BlockSpec((tr, C), lambda i: (i, 0))
    outs = pl.pallas_call(body, grid=(R // tr,), in_specs=[spec] * 4, out_specs=[spec] * 3,
                          out_shape=[_sds((R, C), F32)] * 3, compiler_params=_cp(("parallel",)), name=name)(
        w.reshape(R, C), g.reshape(R, C), m.reshape(R, C), v.reshape(R, C))
    return [o.reshape(shape) for o in outs]


def _bexp(sg_b):
    return jnp.repeat(sg_b.T, HEAD_DIM, axis=1)


def _block_diag(pool_w):
    out = jnp.zeros((GW, GW), F32)
    for i in range(4):
        out = out.at[i * 64:(i + 1) * 64, i * 64:(i + 1) * 64].set(pool_w[i])
    return out


def _cmp_rows(t):
    B, S, _ = t.shape
    t2 = t.reshape(B, S // CMP_STRIDE, CMP_STRIDE * HEAD_DIM)
    nxt = jnp.concatenate([t2[:, 1:], jnp.zeros_like(t2[:, :1])], axis=1)
    return jnp.concatenate([t2, nxt], axis=-1)


def _tile4(t):
    return jnp.tile(t, (1, 1, 4)).astype(BF16)


def layer_fwd(x, p, late, B, S):
    T = B * S
    sv = {"x0": x}
    h1, h1t = rms_fwd(x, p["g_pre_mix"], "rms_pre_mix")
    z = mm(h1, p["w_in"], name="mm_in")
    z3 = z.reshape(B, S, ZW)
    ya = mixa_fwd(z, p["sg_ln_g"], p["sg_w"], p["bexp"])
    yb, hc = mixb_fwd(z3, p["cv_w"], p["cv_b"], p["cv_ln_g"], p["cv_ln_b"], p["cv_pw"], p["cv_pw_b"])
    kv = z3[:, :, COL_KV:COL_KV + 384]
    ks = [kv[:, :, i * 64:(i + 1) * 64] for i in range(6)]
    tbk, tbv = _cmp_rows(ks[0]), _cmp_rows(ks[1])
    kc, vc = cmp_kv_fwd(tbk, tbv, p["cmp_pos_k"], p["cmp_pos_v"], p["cmp_w1_k"], p["cmp_w2_k"], p["cmp_w1_v"], p["cmp_w2_v"])
    kc4, vc4 = _tile4(kc), _tile4(vc)
    ks4, vs4, kw4, vw4 = _tile4(ks[2]), _tile4(ks[3]), _tile4(ks[4]), _tile4(ks[5])
    o_cmp, selT = cmp_attn_fwd(z3, kc4, vc4)
    o_slc, lse_slc = attn_fwd("slc", z3, ks4, vs4, selT)
    o_win, lse_win = attn_fwd("win", z3, kw4, vw4, selT)
    yc = combine_fwd(z3, o_cmp, o_slc, o_win)
    yd = mixd_fwd(z3, p["pool_bd"], p["pool_scale"])
    ycat = jnp.concatenate([ya, yb.reshape(T, GW), yc.reshape(T, GW), yd.reshape(T, GW)], axis=-1)
    p.update(late(ycat))
    mix = mm(ycat, p["w_out"], name="mm_out")
    x1 = rms_post_fwd(x, mix, p["g_post_mix"], "rms_post_mix")
    h2, h2t = rms_fwd(x1, p["g_pre_ffn"], "rms_pre_ffn")
    gu4, a3 = ffn_up_fwd(h2, p["w_gu"])
    f = mm(a3, p["w_down"], blk="k", name="mm_down")
    x2 = rms_post_fwd(x1, f, p["g_post_ffn"], "rms_post_ffn")
    sv.update(h1t=h1t, z=z, hc=hc, tbk=tbk, tbv=tbv, kc4=kc4, vc4=vc4, ks4=ks4, vs4=vs4, kw4=kw4, vw4=vw4, o_cmp=o_cmp, selT=selT,
              o_slc=o_slc, lse_slc=lse_slc, o_win=o_win, lse_win=lse_win, ycat=ycat, mix=mix, x1=x1, h2t=h2t, gu4=gu4, a3=a3, f=f)
    return x2, sv


def _unfold(t4):
    return t4[:, :, :HEAD_DIM]


def layer_bwd_ffn(dx2, p, sv, B, S):
    T = B * S
    gb, gs = {}, {}
    df, gs["g_post_ffn"] = rms_bwd(sv["f"], p["g_post_ffn"], dx2, None, BF16, "rms_post_ffn_bwd")
    dgu = ffn_down_dx(df, p["w_down"], sv["gu4"]).reshape(N_DEV, T, FFN_BLK)
    gb["w_down"] = mm(sv["a3"], df, ta=True, blk="m", out_dtype=BF16, name="mm_down_dw")
    dh2 = mm(dgu, p["w_gu"], tb=True, blk="k", name="mm_gu_dx")
    gb["w_gu"] = mm(sv["h2t"], dgu, blk="n", out_dtype=BF16, name="mm_gu_dw")
    dx1, gs["g_pre_ffn"] = rms_bwd(sv["x1"], p["g_pre_ffn"], dh2, dx2, F32, "rms_pre_ffn_bwd")
    gb["w_gu"] = gb["w_gu"].reshape(1, N_DEV * D_MODEL, FFN_BLK)
    gb["w_down"] = gb["w_down"].reshape(1, FFN_HIDDEN, D_MODEL)
    return dx1, gb, gs


def layer_bwd_mix(dx1, p, sv, B, S):
    T = B * S
    gb, gs = {}, {}
    dmix, gs["g_post_mix"] = rms_bwd(sv["mix"], p["g_post_mix"], dx1, None, BF16, "rms_post_mix_bwd")
    dycat = mm(dmix, p["w_out"], tb=True, name="mm_out_dx")
    gb["w_out"] = mm(sv["ycat"], dmix, ta=True, out_dtype=BF16, name="mm_out_dw")
    dycat3 = dycat.reshape(B, S, D_MODEL)
    z = sv["z"]
    z3 = z.reshape(B, S, ZW)
    du, dv, gs["sg_w"], db, gs["sg_ln_g"] = mixa_bwd(z, dycat, p["sg_ln_g"], p["sg_w"], p["bexp"])
    gs["sg_b"] = db[:, :4].T
    (da, dgt, gs["cv_w"], gs["cv_b"], gs["cv_ln_g"], gs["cv_ln_b"], gpw, gs["cv_pw_b"]) = mixb_bwd(
        z3, sv["hc"], dycat3, p["cv_w"], p["cv_ln_g"], p["cv_ln_b"], p["cv_pw"])
    gb["cv_pw"] = gpw.astype(BF16)
    dd, dwbd, gs["pool_scale"] = mixd_bwd(z3, dycat3, p["pool_bd"], p["pool_scale"])
    gs["pool_w"] = jnp.stack([dwbd[i * 64:(i + 1) * 64, i * 64:(i + 1) * 64] for i in range(4)])
    do_c, do_s, do_w, dgl = combine_bwd(z3, sv["o_cmp"], sv["o_slc"], sv["o_win"], dycat3)
    dq_s, dks4, dvs4 = attn_bwd("slc", z3, sv["ks4"], sv["vs4"], sv["selT"], sv["o_slc"], sv["lse_slc"], do_s)
    dq_w, dkw4, dvw4 = attn_bwd("win", z3, sv["kw4"], sv["vw4"], sv["selT"], sv["o_win"], sv["lse_win"], do_w)
    dq_c, dkc4, dvc4 = cmp_attn_bwd(z3, sv["kc4"], sv["vc4"], do_c)
    (dk2, dv2, gs["cmp_pos_k"], gs["cmp_pos_v"], gw1k, gs["cmp_w2_k"], gw1v, gs["cmp_w2_v"]) = cmp_kv_bwd(
        sv["tbk"], sv["tbv"], p["cmp_pos_k"], p["cmp_pos_v"], p["cmp_w1_k"], p["cmp_w2_k"], p["cmp_w1_v"], p["cmp_w2_v"],
        _unfold(dkc4), _unfold(dvc4))
    gb["cmp_w1_k"], gb["cmp_w1_v"] = gw1k.astype(BF16), gw1v.astype(BF16)
    dkv = jnp.concatenate([dk2.reshape(B, S, HEAD_DIM), dv2.reshape(B, S, HEAD_DIM), _unfold(dks4), _unfold(dvs4),
                           _unfold(dkw4), _unfold(dvw4)], axis=-1).reshape(T, 384)
    dz = assemble_dz(du, dv, da.reshape(T, GW), dgt.reshape(T, GW), dq_c.reshape(T, GW), dq_s.reshape(T, GW),
                     dq_w.reshape(T, GW), dd.reshape(T, GW), dkv, dgl.reshape(T, 128))
    dh1 = mm(dz, p["w_in"], tb=True, name="mm_in_dx")
    gb["w_in"] = mm(sv["h1t"], dz, out_dtype=BF16, name="mm_in_dw")
    dx0, gs["g_pre_mix"] = rms_bwd(sv["x0"], p["g_pre_mix"], dh1, dx1, F32, "rms_pre_mix_bwd")
    return dx0, gb, gs


SMALL = ["g_pre_mix", "g_post_mix", "g_pre_ffn", "g_post_ffn", "sg_ln_g", "sg_w", "sg_b", "cv_w", "cv_b", "cv_ln_g", "cv_ln_b",
         "cv_pw_b", "cmp_pos_k", "cmp_pos_v", "cmp_w2_k", "cmp_w2_v", "pool_w", "pool_scale"]
BIG = ["w_in", "w_out", "w_gu", "w_down", "cmp_w1_k", "cmp_w1_v", "cv_pw"]
NAMES = ["g_pre_mix", "g_post_mix", "g_pre_ffn", "g_post_ffn", "w_in", "sg_ln_g", "sg_w", "sg_b", "cv_w", "cv_b", "cv_ln_g",
         "cv_ln_b", "cv_pw", "cv_pw_b", "cmp_pos_k", "cmp_pos_v", "cmp_w1_k", "cmp_w2_k", "cmp_w1_v", "cmp_w2_v", "pool_w",
         "pool_scale", "w_out", "ffn_w_gu", "ffn_w_down"]


def kernel(x, g_pre_mix, g_post_mix, g_pre_ffn, g_post_ffn, w_in, sg_ln_g, sg_w, sg_b, cv_w, cv_b, cv_ln_g, cv_ln_b, cv_pw, cv_pw_b, cmp_pos_k, cmp_pos_v, cmp_w1_k, cmp_w2_k, cmp_w1_v, cmp_w2_v, pool_w, pool_scale, w_out, ffn_w_gu, ffn_w_down, loss_target, m_g_pre_mix, m_g_post_mix, m_g_pre_ffn, m_g_post_ffn, m_w_in, m_sg_ln_g, m_sg_w, m_sg_b, m_cv_w, m_cv_b, m_cv_ln_g, m_cv_ln_b, m_cv_pw, m_cv_pw_b, m_cmp_pos_k, m_cmp_pos_v, m_cmp_w1_k, m_cmp_w2_k, m_cmp_w1_v, m_cmp_w2_v, m_pool_w, m_pool_scale, m_w_out, m_ffn_w_gu, m_ffn_w_down, v_g_pre_mix, v_g_post_mix, v_g_pre_ffn, v_g_post_ffn, v_w_in, v_sg_ln_g, v_sg_w, v_sg_b, v_cv_w, v_cv_b, v_cv_ln_g, v_cv_ln_b, v_cv_pw, v_cv_pw_b, v_cmp_pos_k, v_cmp_pos_v, v_cmp_w1_k, v_cmp_w2_k, v_cmp_w1_v, v_cmp_w2_v, v_pool_w, v_pool_scale, v_w_out, v_ffn_w_gu, v_ffn_w_down):
    args = dict(locals())
    W = {n: args[n] for n in NAMES}
    M = {n: args["m_" + n] for n in NAMES}
    V = {n: args["v_" + n] for n in NAMES}
    B, S, _ = x.shape
    T = B * S
    L = w_in.shape[0]
    me = _index(_my_pos())
    cpd = GW // N_DEV

    shards = {"w_in": lambda l: pack_cols(w_in[l]).astype(BF16), "w_out": lambda l: w_out[l].astype(BF16),
              "w_gu": lambda l: ffn_w_gu[l].astype(BF16), "w_down": lambda l: ffn_w_down[l].astype(BF16),
              "cmp_w1_k": lambda l: cmp_w1_k[l].astype(BF16), "cmp_w1_v": lambda l: cmp_w1_v[l].astype(BF16),
              "cv_pw": lambda l: cv_pw[l].astype(BF16), "cv_w": lambda l: cv_w[l].T}
    early, later = ["w_in", "cmp_w1_k", "cmp_w1_v", "cv_pw", "cv_w"], ["w_out", "w_gu", "w_down"]

    def start_gather(names, l, tag, behind=None):
        srcs = [shards[n](l)[None] for n in names]
        if behind is not None:
            srcs[0] = srcs[0] + behind[0, 0].astype(srcs[0].dtype)
        return exchange_start(["gather"] * len(names), srcs, "gather_%s_start_%d" % (tag, l))

    def wait_gather(names, started, after, l, tag):
        arrived, done = exchange_wait(["gather"] * len(names), started, after, "gather_%s_wait_%d" % (tag, l))
        full = {n: a[0] for n, a in zip(names, arrived)}
        full["done"] = done
        if "w_gu" in full:
            full["w_gu"] = full["w_gu"].reshape(N_DEV, D_MODEL, FFN_BLK)
            full["w_down"] = full["w_down"].reshape(4, FFN_BLK, D_MODEL)
        if "cv_w" in full:
            full["cv_w"] = full["cv_w"].T
        return full

    def layer_params(l, full):
        p = dict(full)
        for n in ("g_pre_mix", "g_post_mix", "g_pre_ffn", "g_post_ffn", "sg_ln_g", "cv_b", "cv_ln_g", "cv_ln_b", "cv_pw_b",
                  "pool_scale"):
            p[n] = W[n][l][None, :]
        p["sg_w"] = sg_w[l]
        p["bexp"] = _bexp(sg_b[l])
        p["cmp_pos_k"] = cmp_pos_k[l].reshape(1, 2048)
        p["cmp_pos_v"] = cmp_pos_v[l].reshape(1, 2048)
        p["cmp_w2_k"], p["cmp_w2_v"] = cmp_w2_k[l], cmp_w2_v[l]
        p["pool_bd"] = _block_diag(pool_w[l])
        return p

    xs = x.reshape(T, D_MODEL)
    params, saved = [], []
    early_st = start_gather(early, 0, "early")
    ahead = {}
    for l in range(L):
        full = wait_gather(early, early_st, early_st[4], l, "early")
        later_st = start_gather(later, l, "later", behind=full["done"])
        p = layer_params(l, full)
        p["g_pre_mix"] = p["g_pre_mix"] + later_st[4][0, 0]

        def late(after, l=l, st=later_st, p=p):
            got = wait_gather(later, st, after, l, "later")
            if l + 1 < L:
                ahead["early"] = start_gather(early, l + 1, "early", behind=got["done"])
                got["g_post_mix"] = p["g_post_mix"] + ahead["early"][4][0, 0]
            return got

        xs, sv = layer_fwd(xs, p, late, B, S)
        params.append(p)
        saved.append(sv)
        early_st = ahead.get("early")
    dy, lpart = loss_fwd_bwd(xs, loss_target.reshape(T, D_MODEL))
    loss = lax.psum(lpart[0, 0], ("x", "y", "c"))

    ffn_big, mix_big = ["w_gu", "w_down"], ["w_in", "w_out", "cmp_w1_k", "cmp_w1_v", "cv_pw"]
    ffn_kinds, mix_kinds = ["scatter"] * len(ffn_big), ["scatter"] * len(mix_big) + ["gather"]
    pending, token = [], None
    for l in reversed(range(L)):
        p = dict(params[l])
        if token is not None:
            p["g_post_ffn"] = p["g_post_ffn"] + token[0, 0]
        dy, gb_ffn, gs = layer_bwd_ffn(dy, p, saved[l], B, S)
        st_ffn = exchange_start(ffn_kinds, [gb_ffn[n] for n in ffn_big], "scatter_ffn_start_%d" % l)
        p["g_post_mix"] = p["g_post_mix"] + st_ffn[4][0, 0]
        dy, gb_mix, gs_mix = layer_bwd_mix(dy, p, saved[l], B, S)
        gs.update(gs_mix)
        small_shapes = [tuple(gs[n].shape) for n in SMALL]
        st_mix = exchange_start(mix_kinds, [gb_mix[n][None] for n in mix_big] + [pack_flat([gs[n] for n in SMALL])[None]],
                                "scatter_mix_start_%d" % l)
        token = st_mix[4]
        pending.append((l, st_ffn, st_mix))
    grad_x = dy.reshape(B, S, D_MODEL)

    delta, new_m, new_v = {}, {}, {}
    lands = {l: {} for l in range(L)}
    for l, st_ffn, _ in pending:
        lands[l].update(zip(ffn_big, exchange_wait(ffn_kinds, st_ffn, token, "scatter_ffn_wait_%d" % l)[0]))
    grads = {}
    for n, name in zip(ffn_big, ("ffn_w_gu", "ffn_w_down")):
        grads[name] = sum_slots([lands[l][n] for l in range(L)], "sum_" + n)
        delta[name], new_m[name], new_v[name] = adamw(W[name], grads[name], M[name], V[name], "adamw_" + name)
    for l, _, st_mix in pending:
        lands[l].update(zip(mix_big + ["small"],
                            exchange_wait(mix_kinds, st_mix, delta["ffn_w_down"], "scatter_mix_wait_%d" % l)[0]))
    for n in mix_big:
        grads[n] = sum_slots([lands[l][n] for l in range(L)], "sum_" + n)
    grads["w_in"] = unpack_cols(grads["w_in"])
    rows = lands[0]["small"].shape[1] // N_DEV
    reduced = sum_slots([lands[l]["small"].reshape(N_DEV, 1, rows, 128) for l in range(L)], "sum_small")
    per_layer = [unpack_flat(reduced[l], small_shapes) for l in range(L)]
    for i, n in enumerate(SMALL):
        g = jnp.stack([per_layer[l][i] for l in range(L)])
        grads[n] = g.reshape(W[n].shape) if n != "cv_w" else g
    grads["cv_w"] = lax.dynamic_slice(grads["cv_w"], (0, 0, me * cpd), (L, CONV_WIDTH, cpd))

    for n in mix_big:
        delta[n], new_m[n], new_v[n] = adamw(W[n], grads[n], M[n], V[n], "adamw_" + n)
    shapes = [W[n].shape for n in SMALL]
    packed = adamw(pack_flat([W[n] for n in SMALL]), pack_flat([grads[n] for n in SMALL]),
                   pack_flat([M[n] for n in SMALL]), pack_flat([V[n] for n in SMALL]), "adamw_small")
    for out, flat in zip((delta, new_m, new_v), packed):
        for n, a in zip(SMALL, unpack_flat(flat, shapes)):
            out[n] = a

    return (loss, grad_x, *[grads[n] for n in NAMES], *[delta[n] for n in NAMES], *[new_m[n] for n in NAMES],
            *[new_v[n] for n in NAMES])
```

```python
import numpy as np
import jax
import jax.numpy as jnp
from jax import lax
from jax.experimental import pallas as pl
from jax.experimental.pallas import tpu as pltpu

F32 = jnp.float32
BF16 = jnp.bfloat16
HI = lax.Precision.HIGHEST

D_MODEL = 1024
GW = 256
HEAD_DIM = 64
ZW = 2048
SG_CHUNK = 128
CONV_WIDTH = 31
CONV_PAD = 32
CMP_STRIDE = 16
N_CMP = 128
SLC_BLOCK_SHIFT = 6
N_SLC = 32
SLC_TOPK = 8
WIN = 512
NEG = -1e30
FORCE_BONUS = 1e4
RMS_EPS = 1e-6
LN_EPS = 1e-5
FFN_HIDDEN = 2816
N_DEV = 8
FFN_BLK = 2 * FFN_HIDDEN // N_DEV
TQ = 256
ROW_TILE = 512
CONV_TILE = 256
VMEM_LIMIT = 56 * 1024 * 1024
MESH = pl.DeviceIdType.MESH

ADAM_LR, ADAM_B1, ADAM_B2, ADAM_EPS, ADAM_WD, ADAM_STEP = 0.001, 0.9, 0.999, 1e-08, 0.01, 10

COL_U, COL_V, COL_A, COL_G, COL_Q, COL_D, COL_KV, COL_GL = 0, 256, 512, 768, 1024, 1280, 1536, 1920


def _sds(shape, dtype):
    return jax.ShapeDtypeStruct(shape, dtype)


def _cp(sem=None):
    return pltpu.CompilerParams(dimension_semantics=sem, vmem_limit_bytes=VMEM_LIMIT)


def _tile(n, target, q=128):
    best = None
    for t in range(q, min(n, target) + 1, q):
        if n % t == 0:
            best = t
    return best or n


def _full(shape):
    nd = len(shape)
    return pl.BlockSpec(shape, lambda *_: (0,) * nd)


def _sigmoid(x):
    return jax.nn.sigmoid(x)


def _dot(a, b):
    return jnp.dot(a, b, preferred_element_type=F32)


def _dot_nt(a, b):
    return lax.dot_general(a, b, (((1,), (1,)), ((), ())), preferred_element_type=F32)


def _dot_tn(a, b):
    return lax.dot_general(a, b, (((0,), (0,)), ((), ())), preferred_element_type=F32)


def _lane_head(width=GW):
    return lax.shift_right_logical(lax.broadcasted_iota(jnp.int32, (1, width), 1), 6)


def _fold_heads(x):
    return x + pltpu.roll(x, 64, 1) + pltpu.roll(x, 128, 1) + pltpu.roll(x, 192, 1)


def pack_cols(w):
    pad = jnp.zeros(w.shape[:-1] + (ZW - 1932,), w.dtype)
    return jnp.concatenate([w[..., :1280], w[..., 1676:1932], w[..., 1280:1664], w[..., 1664:1676], pad], axis=-1)


def unpack_cols(wp):
    return jnp.concatenate([wp[..., :1280], wp[..., 1536:1920], wp[..., 1920:1932], wp[..., 1280:1536]], axis=-1)


def mm(a, b, *, ta=False, tb=False, blk=None, out_dtype=F32, name, tm=1024, tn=1024, tk=1024):
    a_dims = ("k", "m") if ta else ("m", "k")
    b_dims = ("n", "k") if tb else ("k", "n")
    a3, b3, o3 = blk in a_dims and blk is not None, blk in b_dims and blk is not None, blk in ("m", "n")
    size = {}
    size[a_dims[0]], size[a_dims[1]] = a.shape[-2:]
    size[b_dims[0]], size[b_dims[1]] = b.shape[-2:]
    nb = a.shape[0] if a3 else (b.shape[0] if b3 else 1)
    tile = {"m": _tile(size["m"], tm), "n": _tile(size["n"], tn), "k": _tile(size["k"], tk)}
    grid = {d: size[d] // tile[d] for d in "mnk"}
    if blk is not None:
        tile[blk] = size[blk]
        grid[blk] = nb
    nk = grid["k"]

    def spec(dims, is3):
        def im(i, j, k):
            g = {"m": i, "n": j, "k": k}
            idx = tuple(0 if d == blk else g[d] for d in dims)
            return ((g[blk],) + idx) if is3 else idx
        shape = (tile[dims[0]], tile[dims[1]])
        return pl.BlockSpec(((None,) + shape) if is3 else shape, im)

    dn = (((0 if ta else 1,), (1 if tb else 0,)), ((), ()))

    def partial(a_ref, b_ref):
        return lax.dot_general(a_ref[...].astype(BF16), b_ref[...].astype(BF16), dn, preferred_element_type=F32)

    def body_single(a_ref, b_ref, o_ref):
        o_ref[...] = partial(a_ref, b_ref).astype(o_ref.dtype)

    def body_acc(a_ref, b_ref, o_ref, acc):
        k = pl.program_id(2)

        @pl.when(k == 0)
        def _():
            acc[...] = partial(a_ref, b_ref)

        @pl.when((k > 0) & (k < nk - 1))
        def _():
            acc[...] += partial(a_ref, b_ref)

        @pl.when(k == nk - 1)
        def _():
            o_ref[...] = (acc[...] + partial(a_ref, b_ref)).astype(o_ref.dtype)

    oshape = ((nb,) if o3 else ()) + (size["m"], size["n"])
    return pl.pallas_call(
        body_single if nk == 1 else body_acc, grid=(grid["m"], grid["n"], nk),
        in_specs=[spec(a_dims, a3), spec(b_dims, b3)], out_specs=spec(("m", "n"), o3),
        out_shape=_sds(oshape, out_dtype),
        scratch_shapes=[] if nk == 1 else [pltpu.VMEM((tile["m"], tile["n"]), F32)],
        compiler_params=_cp(("parallel", "parallel", "arbitrary")), name=name)(a, b)


def _rows(tm, width):
    return pl.BlockSpec((tm, width), lambda i: (i, 0))


def rms_fwd(x, g, name):
    T = x.shape[0]

    def body(x_ref, g_ref, h_ref, ht_ref):
        x = x_ref[...]
        r = lax.rsqrt(jnp.mean(x * x, axis=-1, keepdims=True) + RMS_EPS)
        h = (x * r) * g_ref[...]
        h_ref[...] = h.astype(h_ref.dtype)
        ht_ref[...] = h.T.astype(ht_ref.dtype)

    return pl.pallas_call(body, grid=(T // ROW_TILE,), in_specs=[_rows(ROW_TILE, D_MODEL), _full((1, D_MODEL))],
                          out_specs=[_rows(ROW_TILE, D_MODEL), pl.BlockSpec((D_MODEL, ROW_TILE), lambda i: (0, i))],
                          out_shape=[_sds((T, D_MODEL), BF16), _sds((D_MODEL, T), BF16)],
                          compiler_params=_cp(("parallel",)), name=name)(x, g)


def rms_post_fwd(xres, m, g, name):
    T = m.shape[0]

    def body(x_ref, m_ref, g_ref, o_ref):
        m = m_ref[...]
        r = lax.rsqrt(jnp.mean(m * m, axis=-1, keepdims=True) + RMS_EPS)
        o_ref[...] = x_ref[...] + (m * r) * g_ref[...]

    return pl.pallas_call(body, grid=(T // ROW_TILE,),
                          in_specs=[_rows(ROW_TILE, D_MODEL), _rows(ROW_TILE, D_MODEL), _full((1, D_MODEL))],
                          out_specs=_rows(ROW_TILE, D_MODEL), out_shape=_sds((T, D_MODEL), F32),
                          compiler_params=_cp(("parallel",)), name=name)(xres, m, g)


def rms_bwd(m, g, dy, dres, out_dtype, name):
    T = m.shape[0]
    has_res = dres is not None

    def body(*refs):
        if has_res:
            m_ref, g_ref, dy_ref, dres_ref, dm_ref, dg_ref = refs
        else:
            m_ref, g_ref, dy_ref, dm_ref, dg_ref = refs
        m = m_ref[...]
        dy = dy_ref[...].astype(F32)
        r = lax.rsqrt(jnp.mean(m * m, axis=-1, keepdims=True) + RMS_EPS)
        n = m * r
        dn = dy * g_ref[...]
        dm = r * (dn - n * jnp.mean(dn * n, axis=-1, keepdims=True))
        if has_res:
            dm = dm + dres_ref[...]
        dm_ref[...] = dm.astype(dm_ref.dtype)

        @pl.when(pl.program_id(0) == 0)
        def _():
            dg_ref[...] = jnp.zeros_like(dg_ref)

        dg_ref[...] += jnp.sum(dy * n, axis=0, keepdims=True)

    ins = [m, g, dy] + ([dres] if has_res else [])
    specs = [_rows(ROW_TILE, D_MODEL), _full((1, D_MODEL)), _rows(ROW_TILE, D_MODEL)] + ([_rows(ROW_TILE, D_MODEL)] if has_res else [])
    return pl.pallas_call(body, grid=(T // ROW_TILE,), in_specs=specs,
                          out_specs=[_rows(ROW_TILE, D_MODEL), _full((1, D_MODEL))],
                          out_shape=[_sds((T, D_MODEL), out_dtype), _sds((1, D_MODEL), F32)],
                          compiler_params=_cp(("arbitrary",)), name=name)(*ins)


def loss_fwd_bwd(y, tgt):
    T = y.shape[0]

    def body(y_ref, t_ref, dy_ref, l_ref):
        e = y_ref[...] - t_ref[...]
        dy_ref[...] = e * (1.0 / D_MODEL)

        @pl.when(pl.program_id(0) == 0)
        def _():
            l_ref[...] = jnp.zeros_like(l_ref)

        l_ref[...] += jnp.full(l_ref.shape, 0.5 * jnp.sum(jnp.mean(e * e, axis=-1, keepdims=True)), F32)

    return pl.pallas_call(body, grid=(T // ROW_TILE,), in_specs=[_rows(ROW_TILE, D_MODEL)] * 2,
                          out_specs=[_rows(ROW_TILE, D_MODEL), _full((8, 128))],
                          out_shape=[_sds((T, D_MODEL), F32), _sds((8, 128), F32)],
                          compiler_params=_cp(("arbitrary",)), name="loss")(y, tgt)


FFN_TILE = 1024


def _gu_spec():
    return pl.BlockSpec((2, None, FFN_TILE, FFN_BLK), lambda i, j: (0, j, i, 0))


def ffn_up_fwd(h, w_gu):
    T = h.shape[0]

    def body(h_ref, wg_ref, wu_ref, gu_ref, a_ref):
        h = h_ref[...]
        gate = _dot(h, wg_ref[...])
        up = _dot(h, wu_ref[...])
        gu_ref[0] = gate.astype(gu_ref.dtype)
        gu_ref[1] = up.astype(gu_ref.dtype)
        a_ref[...] = (gate * _sigmoid(gate) * up).astype(a_ref.dtype)

    return pl.pallas_call(
        body, grid=(T // FFN_TILE, 4),
        in_specs=[pl.BlockSpec((FFN_TILE, D_MODEL), lambda i, j: (i, 0)),
                  pl.BlockSpec((None, D_MODEL, FFN_BLK), lambda i, j: (j, 0, 0)),
                  pl.BlockSpec((None, D_MODEL, FFN_BLK), lambda i, j: (j + 4, 0, 0))],
        out_specs=[_gu_spec(), pl.BlockSpec((None, FFN_TILE, FFN_BLK), lambda i, j: (j, i, 0))],
        out_shape=[_sds((2, 4, T, FFN_BLK), BF16), _sds((4, T, FFN_BLK), BF16)],
        compiler_params=_cp(("parallel", "parallel")), name="ffn_up_fwd")(h, w_gu, w_gu)


def ffn_down_dx(df, w_down, gu4):
    T = df.shape[0]

    def body(df_ref, w_ref, gu_ref, d_ref):
        da = _dot_nt(df_ref[...], w_ref[...])
        gate, up = gu_ref[0].astype(F32), gu_ref[1].astype(F32)
        sg = _sigmoid(gate)
        d_ref[0] = (da * up * (sg * (1.0 + gate * (1.0 - sg)))).astype(d_ref.dtype)
        d_ref[1] = (da * (gate * sg)).astype(d_ref.dtype)

    return pl.pallas_call(
        body, grid=(T // FFN_TILE, 4),
        in_specs=[pl.BlockSpec((FFN_TILE, D_MODEL), lambda i, j: (i, 0)),
                  pl.BlockSpec((None, FFN_BLK, D_MODEL), lambda i, j: (j, 0, 0)), _gu_spec()],
        out_specs=_gu_spec(), out_shape=_sds((2, 4, T, FFN_BLK), BF16),
        compiler_params=_cp(("parallel", "parallel")), name="ffn_down_dx")(df, w_down, gu4)


def _zcol(tm, col):
    return pl.BlockSpec((tm, GW), lambda i: (i, col // GW))


def _sg_common(v, g):
    mu = jnp.mean(v, axis=-1, keepdims=True)
    xc = v - mu
    rstd = lax.rsqrt(jnp.mean(xc * xc, axis=-1, keepdims=True) + LN_EPS)
    vhat = xc * rstd
    return vhat, rstd, vhat * g


def _tril_weights(w_ref):
    tri = lax.broadcasted_iota(jnp.int32, (SG_CHUNK, SG_CHUNK), 0) >= lax.broadcasted_iota(jnp.int32, (SG_CHUNK, SG_CHUNK), 1)
    return tri, [jnp.where(tri, w_ref[h], 0.0).astype(BF16) for h in range(4)]


def mixa_fwd(z, ln_g, w, bexp):
    T = z.shape[0]
    nch = ROW_TILE // SG_CHUNK

    def body(u_ref, v_ref, g_ref, w_ref, be_ref, y_ref):
        _, _, vln = _sg_common(v_ref[...], g_ref[...])
        vb = vln.astype(BF16)
        head = _lane_head()
        _, wh = _tril_weights(w_ref)
        for c in range(nch):
            rows = slice(c * SG_CHUNK, (c + 1) * SG_CHUNK)
            sv = be_ref[...]
            for h in range(4):
                sv = sv + jnp.where(head == h, _dot(wh[h], vb[rows]), 0.0)
            y_ref[rows, :] = (u_ref[rows, :] * sv).astype(y_ref.dtype)

    return pl.pallas_call(body, grid=(T // ROW_TILE,),
                          in_specs=[_zcol(ROW_TILE, COL_U), _zcol(ROW_TILE, COL_V), _full((1, GW)), _full((4, SG_CHUNK, SG_CHUNK)),
                                    _full((SG_CHUNK, GW))],
                          out_specs=_rows(ROW_TILE, GW), out_shape=_sds((T, GW), BF16),
                          compiler_params=_cp(("parallel",)), name="mixa_fwd")(z, z, ln_g, w, bexp)


def mixa_bwd(z, dycat, ln_g, w, bexp):
    T = z.shape[0]
    nch = ROW_TILE // SG_CHUNK
    nsteps = T // ROW_TILE

    def body(u_ref, v_ref, dy_ref, g_ref, w_ref, be_ref, du_ref, dv_ref, dw_ref, db_ref, dg_ref, dbe_acc):
        step = pl.program_id(0)

        @pl.when(step == 0)
        def _():
            dw_ref[...] = jnp.zeros_like(dw_ref)
            dg_ref[...] = jnp.zeros_like(dg_ref)
            dbe_acc[...] = jnp.zeros_like(dbe_acc)

        g = g_ref[...]
        vhat, rstd, vln = _sg_common(v_ref[...], g)
        vb = vln.astype(BF16)
        head = _lane_head()
        tri, wh = _tril_weights(w_ref)
        dgsum = jnp.zeros((1, GW), F32)
        for c in range(nch):
            rows = slice(c * SG_CHUNK, (c + 1) * SG_CHUNK)
            sv = be_ref[...]
            for h in range(4):
                sv = sv + jnp.where(head == h, _dot(wh[h], vb[rows]), 0.0)
            dy = dy_ref[rows, :]
            du_ref[rows, :] = (dy * sv).astype(du_ref.dtype)
            dsv = dy * u_ref[rows, :]
            dbe_acc[...] += dsv
            dvln = jnp.zeros((SG_CHUNK, GW), F32)
            for h in range(4):
                dsvm = jnp.where(head == h, dsv, 0.0).astype(BF16)
                dw_ref[h] += _dot_nt(dsvm, vb[rows])
                dvln = dvln + _dot_tn(wh[h], dsvm)
            vh = vhat[rows]
            dgsum = dgsum + jnp.sum(dvln * vh, axis=0, keepdims=True)
            dvhat = dvln * g
            dv = rstd[rows] * (dvhat - jnp.mean(dvhat, axis=-1, keepdims=True) - vh * jnp.mean(dvhat * vh, axis=-1, keepdims=True))
            dv_ref[rows, :] = dv.astype(dv_ref.dtype)
        dg_ref[...] += dgsum

        @pl.when(step == nsteps - 1)
        def _():
            for h in range(4):
                dw_ref[h] = jnp.where(tri, dw_ref[h], 0.0)
            fold = (lax.shift_right_logical(lax.broadcasted_iota(jnp.int32, (GW, 128), 0), 6)
                    == lax.broadcasted_iota(jnp.int32, (GW, 128), 1)).astype(F32)
            db_ref[...] = jnp.dot(dbe_acc[...], fold, precision=HI, preferred_element_type=F32)

    return pl.pallas_call(
        body, grid=(nsteps,),
        in_specs=[_zcol(ROW_TILE, COL_U), _zcol(ROW_TILE, COL_V), pl.BlockSpec((ROW_TILE, GW), lambda i: (i, 0)),
                  _full((1, GW)), _full((4, SG_CHUNK, SG_CHUNK)), _full((SG_CHUNK, GW))],
        out_specs=[_rows(ROW_TILE, GW), _rows(ROW_TILE, GW), _full((4, SG_CHUNK, SG_CHUNK)), _full((SG_CHUNK, 128)), _full((1, GW))],
        out_shape=[_sds((T, GW), BF16), _sds((T, GW), BF16), _sds((4, SG_CHUNK, SG_CHUNK), F32), _sds((SG_CHUNK, 128), F32),
                   _sds((1, GW), F32)],
        scratch_shapes=[pltpu.VMEM((SG_CHUNK, GW), F32)],
        compiler_params=_cp(("arbitrary",)), name="mixa_bwd")(z, z, dycat, ln_g, w, bexp)


def _seq(S, col):
    return pl.BlockSpec((None, S, GW), lambda b: (b, 0, col // GW))


def _taps(buf, r0, offsets):
    by_phase = {}
    for k, off in enumerate(offsets):
        by_phase.setdefault(off % 8, []).append((k, off))
    for phase, items in sorted(by_phase.items()):
        span = max(off for _, off in items) - phase
        win = buf[pl.ds(r0 + phase, CONV_TILE + span), :]
        for k, off in items:
            yield k, win[off - phase:off - phase + CONV_TILE]


_CONV_FWD_OFFSETS = [CONV_PAD - (CONV_WIDTH - 1) + k for k in range(CONV_WIDTH)]
_CONV_BWD_OFFSETS = [(CONV_WIDTH - 1) - k for k in range(CONV_WIDTH)]


def _conv(pad, r0, cw_ref, cb):
    acc = jnp.zeros((CONV_TILE, GW), F32) + cb
    for k, rows in _taps(pad, r0, _CONV_FWD_OFFSETS):
        acc = acc + cw_ref[k:k + 1, :] * rows
    return acc


def _conv_ln(acc, lg, lb):
    mu = jnp.mean(acc, axis=-1, keepdims=True)
    xc = acc - mu
    rstd = lax.rsqrt(jnp.mean(xc * xc, axis=-1, keepdims=True) + LN_EPS)
    hhat = xc * rstd
    return hhat, rstd, hhat * lg + lb


def mixb_fwd(z3, cw, cb, lg, lb, pw, pwb):
    B, S, _ = z3.shape

    def body(a_ref, gt_ref, cw_ref, cb_ref, lg_ref, lb_ref, pw_ref, pwb_ref, y_ref, hc_ref, pad):
        pad[0:CONV_PAD, :] = jnp.zeros((CONV_PAD, GW), F32)
        pad[CONV_PAD:CONV_PAD + S, :] = a_ref[...] * _sigmoid(gt_ref[...])
        pwv = pw_ref[...].astype(BF16)
        for r0 in range(0, S, CONV_TILE):
            hc = _conv(pad, r0, cw_ref, cb_ref[...])
            hc_ref[r0:r0 + CONV_TILE, :] = hc
            _, _, ln = _conv_ln(hc, lg_ref[...], lb_ref[...])
            s = ln * _sigmoid(ln)
            y_ref[r0:r0 + CONV_TILE, :] = (_dot(s.astype(BF16), pwv) + pwb_ref[...]).astype(y_ref.dtype)

    seq_out = pl.BlockSpec((None, S, GW), lambda b: (b, 0, 0))
    return pl.pallas_call(
        body, grid=(B,),
        in_specs=[_seq(S, COL_A), _seq(S, COL_G), _full((CONV_WIDTH, GW)), _full((1, GW)), _full((1, GW)), _full((1, GW)),
                  _full((GW, GW)), _full((1, GW))],
        out_specs=[seq_out, seq_out], out_shape=[_sds((B, S, GW), BF16), _sds((B, S, GW), F32)],
        scratch_shapes=[pltpu.VMEM((S + CONV_PAD, GW), F32)],
        compiler_params=_cp(("parallel",)), name="mixb_fwd")(z3, z3, cw, cb, lg, lb, pw, pwb)


def mixb_bwd(z3, hc3, dycat3, cw, lg, lb, pw):
    B, S, _ = z3.shape

    def body(a_ref, gt_ref, hc_ref, dy_ref, cw_ref, lg_ref, lb_ref, pw_ref,
             da_ref, dgt_ref, dcw_ref, dcb_ref, dlg_ref, dlb_ref, dpw_ref, dpwb_ref, pad, dpad, dcw_acc):
        @pl.when(pl.program_id(0) == 0)
        def _():
            for r in (dcb_ref, dlg_ref, dlb_ref, dpw_ref, dpwb_ref, dcw_acc):
                r[...] = jnp.zeros_like(r)

        pad[0:CONV_PAD, :] = jnp.zeros((CONV_PAD, GW), F32)
        pad[CONV_PAD:CONV_PAD + S, :] = a_ref[...] * _sigmoid(gt_ref[...])
        dpad[S:S + CONV_PAD, :] = jnp.zeros((CONV_PAD, GW), F32)
        pwv = pw_ref[...].astype(BF16)
        lg = lg_ref[...]
        for r0 in range(0, S, CONV_TILE):
            hhat, rstd, ln = _conv_ln(hc_ref[r0:r0 + CONV_TILE, :], lg, lb_ref[...])
            sg = _sigmoid(ln)
            s = ln * sg
            dy = dy_ref[r0:r0 + CONV_TILE, :]
            dyb = dy.astype(BF16)
            dpw_ref[...] += _dot_tn(s.astype(BF16), dyb)
            dpwb_ref[...] += jnp.sum(dy, axis=0, keepdims=True)
            dln = _dot_nt(dyb, pwv) * (sg * (1.0 + ln * (1.0 - sg)))
            dlg_ref[...] += jnp.sum(dln * hhat, axis=0, keepdims=True)
            dlb_ref[...] += jnp.sum(dln, axis=0, keepdims=True)
            dhh = dln * lg
            dhc = rstd * (dhh - jnp.mean(dhh, axis=-1, keepdims=True) - hhat * jnp.mean(dhh * hhat, axis=-1, keepdims=True))
            dpad[r0:r0 + CONV_TILE, :] = dhc
            dcb_ref[...] += jnp.sum(dhc, axis=0, keepdims=True)
            for k, rows in _taps(pad, r0, _CONV_FWD_OFFSETS):
                dcw_acc[k] += (dhc * rows).reshape(CONV_TILE // 8, 8, GW).sum(axis=0)
        for r0 in range(0, S, CONV_TILE):
            dhg = jnp.zeros((CONV_TILE, GW), F32)
            for k, rows in _taps(dpad, r0, _CONV_BWD_OFFSETS):
                dhg = dhg + cw_ref[k:k + 1, :] * rows
            a = a_ref[r0:r0 + CONV_TILE, :]
            sg = _sigmoid(gt_ref[r0:r0 + CONV_TILE, :])
            da_ref[r0:r0 + CONV_TILE, :] = (dhg * sg).astype(da_ref.dtype)
            dgt_ref[r0:r0 + CONV_TILE, :] = (dhg * a * sg * (1.0 - sg)).astype(dgt_ref.dtype)

        @pl.when(pl.program_id(0) == B - 1)
        def _():
            for k in range(CONV_WIDTH):
                dcw_ref[k:k + 1, :] = jnp.sum(dcw_acc[k], axis=0, keepdims=True)

    seq_out = pl.BlockSpec((None, S, GW), lambda b: (b, 0, 0))
    return pl.pallas_call(
        body, grid=(B,),
        in_specs=[_seq(S, COL_A), _seq(S, COL_G), seq_out, pl.BlockSpec((None, S, GW), lambda b: (b, 0, 1)),
                  _full((CONV_WIDTH, GW)), _full((1, GW)), _full((1, GW)), _full((GW, GW))],
        out_specs=[seq_out, seq_out, _full((CONV_WIDTH, GW)), _full((1, GW)), _full((1, GW)), _full((1, GW)), _full((GW, GW)),
                   _full((1, GW))],
        out_shape=[_sds((B, S, GW), BF16), _sds((B, S, GW), BF16), _sds((CONV_WIDTH, GW), F32), _sds((1, GW), F32),
                   _sds((1, GW), F32), _sds((1, GW), F32), _sds((GW, GW), F32), _sds((1, GW), F32)],
        scratch_shapes=[pltpu.VMEM((S + CONV_PAD, GW), F32), pltpu.VMEM((S + CONV_PAD, GW), F32),
                        pltpu.VMEM((CONV_WIDTH, 8, GW), F32)],
        compiler_params=_cp(("arbitrary",)), name="mixb_bwd")(z3, z3, hc3, dycat3, cw, lg, lb, pw)


POOL_PAD = 16


def _pool_window():
    lane = lax.broadcasted_iota(jnp.int32, (1, GW), 1)
    return jnp.where(lane < 64, 2, jnp.where(lane < 128, 4, jnp.where(lane < 192, 8, 16)))


def _pool_sums(pad, r0, base, sign):
    win = _pool_window()
    acc = pad[pl.ds(r0 + base, CONV_TILE), :]
    out = None
    for i in range(1, 16):
        acc = acc + pad[pl.ds(r0 + base + sign * i, CONV_TILE), :]
        if i + 1 in (2, 4, 8, 16):
            out = acc if out is None else jnp.where(win == i + 1, acc, out)
    return out


def _pool_cnt(r0):
    t1 = r0 + 1 + lax.broadcasted_iota(jnp.int32, (CONV_TILE, 1), 0)
    return jnp.minimum(t1, _pool_window()).astype(F32)


def mixd_fwd(z3, wbd, scale):
    B, S, _ = z3.shape

    def body(x_ref, w_ref, sc_ref, y_ref, pad):
        pad[0:POOL_PAD, :] = jnp.zeros((POOL_PAD, GW), F32)
        pad[POOL_PAD:POOL_PAD + S, :] = x_ref[...]
        wv = w_ref[...].astype(BF16)
        for r0 in range(0, S, CONV_TILE):
            mean = _pool_sums(pad, r0, POOL_PAD, -1) / _pool_cnt(r0)
            p = (mean - x_ref[r0:r0 + CONV_TILE, :]).astype(BF16)
            y_ref[r0:r0 + CONV_TILE, :] = (_dot(p, wv) * sc_ref[...]).astype(y_ref.dtype)

    return pl.pallas_call(
        body, grid=(B,), in_specs=[_seq(S, COL_D), _full((GW, GW)), _full((1, GW))],
        out_specs=pl.BlockSpec((None, S, GW), lambda b: (b, 0, 0)), out_shape=_sds((B, S, GW), BF16),
        scratch_shapes=[pltpu.VMEM((S + POOL_PAD, GW), F32)],
        compiler_params=_cp(("parallel",)), name="mixd_fwd")(z3, wbd, scale)


def mixd_bwd(z3, dycat3, wbd, scale):
    B, S, _ = z3.shape

    def body(x_ref, dy_ref, w_ref, sc_ref, dx_ref, dw_ref, dsc_ref, pad, qpad):
        @pl.when(pl.program_id(0) == 0)
        def _():
            dw_ref[...] = jnp.zeros_like(dw_ref)
            dsc_ref[...] = jnp.zeros_like(dsc_ref)

        pad[0:POOL_PAD, :] = jnp.zeros((POOL_PAD, GW), F32)
        pad[POOL_PAD:POOL_PAD + S, :] = x_ref[...]
        qpad[S:S + POOL_PAD, :] = jnp.zeros((POOL_PAD, GW), F32)
        wv = w_ref[...].astype(BF16)
        for r0 in range(0, S, CONV_TILE):
            cnt = _pool_cnt(r0)
            mean = _pool_sums(pad, r0, POOL_PAD, -1) / cnt
            p = (mean - x_ref[r0:r0 + CONV_TILE, :]).astype(BF16)
            dy = dy_ref[r0:r0 + CONV_TILE, :]
            dsc_ref[...] += jnp.sum(dy * _dot(p, wv), axis=0, keepdims=True)
            dyp = (dy * sc_ref[...]).astype(BF16)
            dw_ref[...] += _dot_tn(p, dyp)
            dp = _dot_nt(dyp, wv)
            dx_ref[r0:r0 + CONV_TILE, :] = (-dp).astype(dx_ref.dtype)
            qpad[r0:r0 + CONV_TILE, :] = dp / cnt
        for r0 in range(0, S, CONV_TILE):
            back = _pool_sums(qpad, r0, 0, 1)
            dx_ref[r0:r0 + CONV_TILE, :] = (dx_ref[r0:r0 + CONV_TILE, :].astype(F32) + back).astype(dx_ref.dtype)

    return pl.pallas_call(
        body, grid=(B,),
        in_specs=[_seq(S, COL_D), pl.BlockSpec((None, S, GW), lambda b: (b, 0, 3)), _full((GW, GW)), _full((1, GW))],
        out_specs=[pl.BlockSpec((None, S, GW), lambda b: (b, 0, 0)), _full((GW, GW)), _full((1, GW))],
        out_shape=[_sds((B, S, GW), F32), _sds((GW, GW), F32), _sds((1, GW), F32)],
        scratch_shapes=[pltpu.VMEM((S + POOL_PAD, GW), F32), pltpu.VMEM((S + POOL_PAD, GW), F32)],
        compiler_params=_cp(("arbitrary",)), name="mixd_bwd")(z3, dycat3, wbd, scale)


def cmp_kv_fwd(tbk, tbv, pek, pev, w1k, w2k, w1v, w2v):
    B = tbk.shape[0]

    def body(tbk_ref, tbv_ref, pek_ref, pev_ref, w1k_ref, w2k_ref, w1v_ref, w2v_ref, kc_ref, vc_ref):
        for tb_ref, pe_ref, w1_ref, w2_ref, o_ref in ((tbk_ref, pek_ref, w1k_ref, w2k_ref, kc_ref),
                                                      (tbv_ref, pev_ref, w1v_ref, w2v_ref, vc_ref)):
            pre = _dot((tb_ref[...] + pe_ref[...]).astype(BF16), w1_ref[...].astype(BF16))
            hm = pre * _sigmoid(pre)
            o_ref[...] = _dot(hm.astype(BF16), w2_ref[...].astype(BF16))

    tb_spec = pl.BlockSpec((None, N_CMP, 2048), lambda b: (b, 0, 0))
    o_spec = pl.BlockSpec((None, N_CMP, HEAD_DIM), lambda b: (b, 0, 0))
    return pl.pallas_call(
        body, grid=(B,),
        in_specs=[tb_spec, tb_spec, _full((1, 2048)), _full((1, 2048)), _full((2048, HEAD_DIM)), _full((HEAD_DIM, HEAD_DIM)),
                  _full((2048, HEAD_DIM)), _full((HEAD_DIM, HEAD_DIM))],
        out_specs=[o_spec, o_spec], out_shape=[_sds((B, N_CMP, HEAD_DIM), F32)] * 2,
        compiler_params=_cp(("parallel",)), name="cmp_kv_fwd")(tbk, tbv, pek, pev, w1k, w2k, w1v, w2v)


def cmp_kv_bwd(tbk, tbv, pek, pev, w1k, w2k, w1v, w2v, dkc, dvc):
    B = tbk.shape[0]

    def body(tbk_ref, tbv_ref, pek_ref, pev_ref, w1k_ref, w2k_ref, w1v_ref, w2v_ref, dkc_ref, dvc_ref,
             dk2_ref, dv2_ref, dpek_ref, dpev_ref, dw1k_ref, dw2k_ref, dw1v_ref, dw2v_ref):
        @pl.when(pl.program_id(0) == 0)
        def _():
            for r in (dpek_ref, dpev_ref, dw1k_ref, dw2k_ref, dw1v_ref, dw2v_ref):
                r[...] = jnp.zeros_like(r)

        row0 = lax.broadcasted_iota(jnp.int32, (N_CMP, 1), 0) == 0
        for tb_ref, pe_ref, w1_ref, w2_ref, do_ref, d2_ref, dpe_ref, dw1_ref, dw2_ref in (
                (tbk_ref, pek_ref, w1k_ref, w2k_ref, dkc_ref, dk2_ref, dpek_ref, dw1k_ref, dw2k_ref),
                (tbv_ref, pev_ref, w1v_ref, w2v_ref, dvc_ref, dv2_ref, dpev_ref, dw1v_ref, dw2v_ref)):
            tb = (tb_ref[...] + pe_ref[...]).astype(BF16)
            w1 = w1_ref[...].astype(BF16)
            pre = _dot(tb, w1)
            sg = _sigmoid(pre)
            hm = (pre * sg).astype(BF16)
            do = do_ref[...].astype(BF16)
            dw2_ref[...] += _dot_tn(hm, do)
            dpre = (_dot_nt(do, w2_ref[...].astype(BF16)) * (sg * (1.0 + pre * (1.0 - sg)))).astype(BF16)
            dw1_ref[...] += _dot_tn(tb, dpre)
            dtb = _dot_nt(dpre, w1)
            dpe_ref[...] += jnp.sum(dtb, axis=0, keepdims=True)
            down = jnp.where(row0, 0.0, pltpu.roll(dtb[:, 1024:], 1, 0))
            d2_ref[...] = dtb[:, :1024] + down

    tb_spec = pl.BlockSpec((None, N_CMP, 2048), lambda b: (b, 0, 0))
    c_spec = pl.BlockSpec((None, N_CMP, HEAD_DIM), lambda b: (b, 0, 0))
    d2_spec = pl.BlockSpec((None, N_CMP, 1024), lambda b: (b, 0, 0))
    return pl.pallas_call(
        body, grid=(B,),
        in_specs=[tb_spec, tb_spec, _full((1, 2048)), _full((1, 2048)), _full((2048, HEAD_DIM)), _full((HEAD_DIM, HEAD_DIM)),
                  _full((2048, HEAD_DIM)), _full((HEAD_DIM, HEAD_DIM)), c_spec, c_spec],
        out_specs=[d2_spec, d2_spec, _full((1, 2048)), _full((1, 2048)), _full((2048, HEAD_DIM)), _full((HEAD_DIM, HEAD_DIM)),
                   _full((2048, HEAD_DIM)), _full((HEAD_DIM, HEAD_DIM))],
        out_shape=[_sds((B, N_CMP, 1024), F32)] * 2 + [_sds((1, 2048), F32)] * 2
        + [_sds((2048, HEAD_DIM), F32), _sds((HEAD_DIM, HEAD_DIM), F32)] * 2,
        compiler_params=_cp(("arbitrary",)), name="cmp_kv_bwd")(tbk, tbv, pek, pev, w1k, w2k, w1v, w2v, dkc, dvc)


def _qtile(col):
    return pl.BlockSpec((None, TQ, GW), lambda b, i: (b, i, col // GW))


def _qtile0():
    return pl.BlockSpec((None, TQ, GW), lambda b, i: (b, i, 0))


def _cmp_probs(q, kc, qpos):
    head = _lane_head()
    cend = lax.broadcasted_iota(jnp.int32, (1, N_CMP), 1) * CMP_STRIDE + 31
    cmask = cend <= qpos
    has = qpos >= 31
    out = []
    for h in range(4):
        qm = jnp.where(head == h, q, 0.0).astype(BF16)
        s = jnp.where(cmask, _dot_nt(qm, kc), NEG)
        e = jnp.exp(s - jnp.max(s, axis=-1, keepdims=True))
        p = jnp.where(has, e / jnp.sum(e, axis=-1, keepdims=True), 0.0)
        out.append((qm, p))
    return out


def cmp_attn_fwd(z3, kc4, vc4):
    B, S, _ = z3.shape

    def body(q_ref, kc_ref, vc_ref, o_ref, sel_ref):
        t0 = pl.program_id(1) * TQ
        qpos = t0 + lax.broadcasted_iota(jnp.int32, (TQ, 1), 0)
        head = _lane_head()
        kc, vc = kc_ref[...], vc_ref[...]
        o = jnp.zeros((TQ, GW), F32)
        psum = jnp.zeros((TQ, N_CMP), F32)
        for h, (_, p) in enumerate(_cmp_probs(q_ref[...] * 0.125, kc, qpos)):
            o = o + jnp.where(head == h, _dot(p.astype(BF16), vc), 0.0)
            psum = psum + p
        o_ref[...] = o
        cst = lax.broadcasted_iota(jnp.int32, (N_SLC, N_CMP), 1) * CMP_STRIDE
        jst = lax.broadcasted_iota(jnp.int32, (N_SLC, N_CMP), 0) * 64
        overlap = ((cst <= jst + 63) & (cst + 31 >= jst)).astype(BF16)
        imp = _dot_nt(overlap, psum.astype(BF16))
        qp = t0 + lax.broadcasted_iota(jnp.int32, (1, TQ), 1)
        jj = lax.broadcasted_iota(jnp.int32, (N_SLC, 1), 0)
        cur = lax.shift_right_logical(qp, SLC_BLOCK_SHIFT)
        forced = (jj == 0) | (jj == cur) | (jj == cur - 1)
        score = jnp.where(jj * 64 <= qp, imp + jnp.where(forced, FORCE_BONUS, 0.0), NEG)
        rank = jnp.zeros((N_SLC, TQ), F32)
        for j2 in range(N_SLC):
            sj = score[j2:j2 + 1, :]
            rank = rank + jnp.where((sj > score) | ((sj == score) & (j2 < jj)), 1.0, 0.0)
        sel_ref[...] = jnp.where((rank < SLC_TOPK) & (score > NEG / 2), 1.0, 0.0)

    c_spec = pl.BlockSpec((None, N_CMP, GW), lambda b, i: (b, 0, 0))
    return pl.pallas_call(
        body, grid=(B, S // TQ), in_specs=[_qtile(COL_Q), c_spec, c_spec],
        out_specs=[_qtile0(), pl.BlockSpec((None, N_SLC, TQ), lambda b, i: (b, 0, i))],
        out_shape=[_sds((B, S, GW), F32), _sds((B, N_SLC, S), F32)],
        compiler_params=_cp(("parallel", "parallel")), name="cmp_attn_fwd")(z3, kc4, vc4)


def cmp_attn_bwd(z3, kc4, vc4, do):
    B, S, _ = z3.shape
    nq = S // TQ

    def body(q_ref, kc_ref, vc_ref, do_ref, dq_ref, dkc_ref, dvc_ref):
        qi = pl.program_id(1)

        @pl.when(qi == 0)
        def _():
            dkc_ref[...] = jnp.zeros_like(dkc_ref)
            dvc_ref[...] = jnp.zeros_like(dvc_ref)

        qpos = qi * TQ + lax.broadcasted_iota(jnp.int32, (TQ, 1), 0)
        head = _lane_head()
        kc, vc, do = kc_ref[...], vc_ref[...], do_ref[...]
        dq = jnp.zeros((TQ, GW), F32)
        for h, (qm, p) in enumerate(_cmp_probs(q_ref[...] * 0.125, kc, qpos)):
            dom = jnp.where(head == h, do, 0.0).astype(BF16)
            dp = _dot_nt(dom, vc)
            ds = (p * (dp - jnp.sum(p * dp, axis=-1, keepdims=True))).astype(BF16)
            dq = dq + jnp.where(head == h, _dot(ds, kc), 0.0)
            dkc_ref[...] += _dot_tn(ds, qm)
            dvc_ref[...] += _dot_tn(p.astype(BF16), dom)
        dq_ref[...] = dq * 0.125

        @pl.when(qi == nq - 1)
        def _():
            dkc_ref[...] = _fold_heads(dkc_ref[...])
            dvc_ref[...] = _fold_heads(dvc_ref[...])

    c_spec = pl.BlockSpec((None, N_CMP, GW), lambda b, i: (b, 0, 0))
    return pl.pallas_call(
        body, grid=(B, nq), in_specs=[_qtile(COL_Q), c_spec, c_spec, _qtile0()],
        out_specs=[_qtile0(), c_spec, c_spec],
        out_shape=[_sds((B, S, GW), F32), _sds((B, N_CMP, GW), F32), _sds((B, N_CMP, GW), F32)],
        compiler_params=_cp(("parallel", "arbitrary")), name="cmp_attn_bwd")(z3, kc4, vc4, do)


def _attn_mask(mode, qpos, k0, sel_b):
    kpos = k0 + lax.broadcasted_iota(jnp.int32, (1, TQ), 1)
    mask = kpos <= qpos
    if mode == "win":
        return mask & (kpos > qpos - WIN)
    blk = lax.shift_right_logical(k0 + lax.broadcasted_iota(jnp.int32, (N_SLC, TQ), 1), SLC_BLOCK_SHIFT)
    expand = (blk == lax.broadcasted_iota(jnp.int32, (N_SLC, TQ), 0)).astype(BF16)
    return mask & (_dot_tn(sel_b, expand) > 0.5)


def _attn_lo(mode, qi):
    return jnp.maximum(qi - WIN // TQ, 0) if mode == "win" else 0


def attn_fwd(mode, z3, k4, v4, selT):
    B, S, _ = z3.shape

    def body(q_ref, k_ref, v_ref, sel_ref, o_ref, lse_ref, s_all, m_acc, l_acc, o_acc):
        qi = pl.program_id(1)
        qpos = qi * TQ + lax.broadcasted_iota(jnp.int32, (TQ, 1), 0)
        head = _lane_head()
        q = q_ref[...] * 0.125
        qm = [jnp.where(head == h, q, 0.0).astype(BF16) for h in range(4)]
        sel_b = sel_ref[...].astype(BF16)
        lo, hi = _attn_lo(mode, qi), qi + 1
        m_acc[...] = jnp.full(m_acc.shape, NEG, F32)

        def scores(kb, carry):
            k0 = pl.multiple_of(kb * TQ, TQ)
            kblk = k_ref[pl.ds(k0, TQ), :]
            mask = _attn_mask(mode, qpos, k0, sel_b)
            for h in range(4):
                s = jnp.where(mask, _dot_nt(qm[h], kblk), NEG)
                s_all[h, kb] = s
                m_acc[h] = jnp.maximum(m_acc[h], s)
            return carry

        lax.fori_loop(lo, hi, scores, 0)
        for h in range(4):
            m_acc[h] = jnp.broadcast_to(jnp.max(m_acc[h], axis=-1, keepdims=True), (TQ, TQ))
        l_acc[...] = jnp.zeros_like(l_acc)
        o_acc[...] = jnp.zeros_like(o_acc)

        def weights(kb, carry):
            vblk = v_ref[pl.ds(pl.multiple_of(kb * TQ, TQ), TQ), :]
            for h in range(4):
                p = jnp.exp(s_all[h, kb] - m_acc[h])
                l_acc[h] += p
                o_acc[h] += _dot(p.astype(BF16), vblk)
            return carry

        lax.fori_loop(lo, hi, weights, 0)
        o = jnp.zeros((TQ, GW), F32)
        lse = jnp.zeros((TQ, 128), F32)
        lane = lax.broadcasted_iota(jnp.int32, (1, 128), 1)
        for h in range(4):
            l = jnp.sum(l_acc[h], axis=-1, keepdims=True)
            o = o + jnp.where(head == h, o_acc[h] / l, 0.0)
            lse = jnp.where(lane == h, jnp.max(m_acc[h], axis=-1, keepdims=True) + jnp.log(l), lse)
        o_ref[...] = o
        lse_ref[...] = lse

    kv_spec = pl.BlockSpec((None, S, GW), lambda b, i: (b, 0, 0))
    return pl.pallas_call(
        body, grid=(B, S // TQ),
        in_specs=[_qtile(COL_Q), kv_spec, kv_spec, pl.BlockSpec((None, N_SLC, TQ), lambda b, i: (b, 0, i))],
        out_specs=[_qtile0(), pl.BlockSpec((None, TQ, 128), lambda b, i: (b, i, 0))],
        out_shape=[_sds((B, S, GW), F32), _sds((B, S, 128), F32)],
        scratch_shapes=[pltpu.VMEM((4, S // TQ, TQ, TQ), F32), pltpu.VMEM((4, TQ, TQ), F32), pltpu.VMEM((4, TQ, TQ), F32),
                        pltpu.VMEM((4, TQ, GW), F32)],
        compiler_params=_cp(("parallel", "parallel")), name=mode + "_attn_fwd")(z3, k4, v4, selT)


def attn_bwd(mode, z3, k4, v4, selT, o, lse, do):
    B, S, _ = z3.shape
    nq = S // TQ

    def body(q_ref, k_ref, v_ref, sel_ref, o_ref, lse_ref, do_ref, dq_ref, dk_ref, dv_ref, dq_s):
        qi = pl.program_id(1)

        @pl.when(qi == 0)
        def _():
            dk_ref[...] = jnp.zeros_like(dk_ref)
            dv_ref[...] = jnp.zeros_like(dv_ref)

        qpos = qi * TQ + lax.broadcasted_iota(jnp.int32, (TQ, 1), 0)
        head = _lane_head()
        lane = lax.broadcasted_iota(jnp.int32, (1, 128), 1)
        q = q_ref[...] * 0.125
        do = do_ref[...]
        doo = do * o_ref[...]
        lse = lse_ref[...]
        qm = [jnp.where(head == h, q, 0.0).astype(BF16) for h in range(4)]
        dom = [jnp.where(head == h, do, 0.0).astype(BF16) for h in range(4)]
        delta = [jnp.sum(jnp.where(head == h, doo, 0.0), axis=-1, keepdims=True) for h in range(4)]
        lse_h = [jnp.max(jnp.where(lane == h, lse, NEG), axis=-1, keepdims=True) for h in range(4)]
        sel_b = sel_ref[...].astype(BF16)
        dq_s[...] = jnp.zeros_like(dq_s)

        def step(kb, carry):
            k0 = pl.multiple_of(kb * TQ, TQ)
            kblk = k_ref[pl.ds(k0, TQ), :]
            vblk = v_ref[pl.ds(k0, TQ), :]
            mask = _attn_mask(mode, qpos, k0, sel_b)
            for h in range(4):
                s = _dot_nt(qm[h], kblk)
                p = jnp.where(mask, jnp.exp(s - lse_h[h]), 0.0)
                dp = _dot_nt(dom[h], vblk)
                ds = (p * (dp - delta[h])).astype(BF16)
                dq_s[...] += jnp.where(head == h, _dot(ds, kblk), 0.0)
                dk_ref[pl.ds(k0, TQ), :] += _dot_tn(ds, qm[h])
                dv_ref[pl.ds(k0, TQ), :] += _dot_tn(p.astype(BF16), dom[h])
            return carry

        lax.fori_loop(_attn_lo(mode, qi), qi + 1, step, 0)
        dq_ref[...] = dq_s[...] * 0.125

        @pl.when(qi == nq - 1)
        def _():
            for r0 in range(0, S, TQ):
                dk_ref[r0:r0 + TQ, :] = _fold_heads(dk_ref[r0:r0 + TQ, :])
                dv_ref[r0:r0 + TQ, :] = _fold_heads(dv_ref[r0:r0 + TQ, :])

    kv_spec = pl.BlockSpec((None, S, GW), lambda b, i: (b, 0, 0))
    return pl.pallas_call(
        body, grid=(B, nq),
        in_specs=[_qtile(COL_Q), kv_spec, kv_spec, pl.BlockSpec((None, N_SLC, TQ), lambda b, i: (b, 0, i)), _qtile0(),
                  pl.BlockSpec((None, TQ, 128), lambda b, i: (b, i, 0)), _qtile0()],
        out_specs=[_qtile0(), kv_spec, kv_spec],
        out_shape=[_sds((B, S, GW), F32)] * 3,
        scratch_shapes=[pltpu.VMEM((TQ, GW), F32)],
        compiler_params=_cp(("parallel", "arbitrary")), name=mode + "_attn_bwd")(z3, k4, v4, selT, o, lse, do)


def _gate_expand(b):
    r = lax.broadcasted_iota(jnp.int32, (128, GW), 0)
    hl = lax.shift_right_logical(lax.broadcasted_iota(jnp.int32, (128, GW), 1), 6)
    return (r == 3 * hl + b).astype(F32)


def combine_fwd(z3, o_cmp, o_slc, o_win):
    B, S, _ = z3.shape

    def body(gl_ref, oc_ref, os_ref, ow_ref, y_ref):
        g = _sigmoid(gl_ref[...])
        y = jnp.zeros((TQ, GW), F32)
        for b, o_ref in enumerate((oc_ref, os_ref, ow_ref)):
            y = y + jnp.dot(g, _gate_expand(b), precision=HI, preferred_element_type=F32) * o_ref[...]
        y_ref[...] = y.astype(y_ref.dtype)

    return pl.pallas_call(
        body, grid=(B, S // TQ),
        in_specs=[pl.BlockSpec((None, TQ, 128), lambda b, i: (b, i, COL_GL // 128)), _qtile0(), _qtile0(), _qtile0()],
        out_specs=_qtile0(), out_shape=_sds((B, S, GW), BF16),
        compiler_params=_cp(("parallel", "parallel")), name="combine_fwd")(z3, o_cmp, o_slc, o_win)


def combine_bwd(z3, o_cmp, o_slc, o_win, dycat3):
    B, S, _ = z3.shape

    def body(gl_ref, oc_ref, os_ref, ow_ref, dy_ref, dc_ref, ds_ref, dw_ref, dgl_ref):
        g = _sigmoid(gl_ref[...])
        dy = dy_ref[...]
        dg = jnp.zeros((TQ, 128), F32)
        for b, (o_ref, d_ref) in enumerate(((oc_ref, dc_ref), (os_ref, ds_ref), (ow_ref, dw_ref))):
            ex = _gate_expand(b)
            d_ref[...] = jnp.dot(g, ex, precision=HI, preferred_element_type=F32) * dy
            dg = dg + lax.dot_general(dy * o_ref[...], ex, (((1,), (1,)), ((), ())), precision=HI, preferred_element_type=F32)
        dgl_ref[...] = dg * g * (1.0 - g)

    gl_spec = pl.BlockSpec((None, TQ, 128), lambda b, i: (b, i, COL_GL // 128))
    return pl.pallas_call(
        body, grid=(B, S // TQ),
        in_specs=[gl_spec, _qtile0(), _qtile0(), _qtile0(), pl.BlockSpec((None, TQ, GW), lambda b, i: (b, i, 2))],
        out_specs=[_qtile0(), _qtile0(), _qtile0(), pl.BlockSpec((None, TQ, 128), lambda b, i: (b, i, 0))],
        out_shape=[_sds((B, S, GW), F32)] * 3 + [_sds((B, S, 128), F32)],
        compiler_params=_cp(("parallel", "parallel")), name="combine_bwd")(z3, o_cmp, o_slc, o_win, dycat3)


def assemble_dz(du, dv, da, dgt, dq_c, dq_s, dq_w, dd, dkv, dgl):
    T = du.shape[0]

    def body(du_ref, dv_ref, da_ref, dgt_ref, dqc_ref, dqs_ref, dqw_ref, dd_ref, dkv_ref, dgl_ref, o_ref):
        o_ref[:, COL_U:COL_U + GW] = du_ref[...]
        o_ref[:, COL_V:COL_V + GW] = dv_ref[...]
        o_ref[:, COL_A:COL_A + GW] = da_ref[...]
        o_ref[:, COL_G:COL_G + GW] = dgt_ref[...]
        o_ref[:, COL_Q:COL_Q + GW] = (dqc_ref[...] + dqs_ref[...] + dqw_ref[...]).astype(BF16)
        o_ref[:, COL_D:COL_D + GW] = dd_ref[...].astype(BF16)
        o_ref[:, COL_KV:COL_KV + 384] = dkv_ref[...].astype(BF16)
        o_ref[:, COL_GL:COL_GL + 128] = dgl_ref[...].astype(BF16)

    specs = [_rows(ROW_TILE, GW)] * 8 + [_rows(ROW_TILE, 384), _rows(ROW_TILE, 128)]
    return pl.pallas_call(body, grid=(T // ROW_TILE,), in_specs=specs, out_specs=_rows(ROW_TILE, ZW),
                          out_shape=_sds((T, ZW), BF16), compiler_params=_cp(("parallel",)),
                          name="assemble_dz")(du, dv, da, dgt, dq_c, dq_s, dq_w, dd, dkv, dgl)


def _my_pos():
    return lax.axis_index("x"), lax.axis_index("y"), lax.axis_index("c")


def _peer(k):
    x, y, c = _my_pos()
    return ((1 - x) if k & 4 else x, (1 - y) if k & 2 else y, (1 - c) if k & 1 else c)


def _index(pos):
    return 4 * pos[0] + 2 * pos[1] + pos[2]


_HBM = pl.BlockSpec(memory_space=pltpu.HBM)


_SEM = pl.BlockSpec(memory_space=pltpu.SEMAPHORE)
_EFFECT = pltpu.SideEffectType.DATAFLOW_SIDE_EFFECTING


def _exchange_copies(kinds, srcs, lands, send, recv):
    me = _index(_my_pos())
    out = []
    for a, kind in enumerate(kinds):
        for k in range(1, N_DEV):
            peer = _peer(k)
            if kind == "gather":
                r = srcs[a].shape[1]
                src, dst = srcs[a], lands[a].at[:, pl.ds(me * r, r), :]
            else:
                r = srcs[a].shape[1] // N_DEV
                src, dst = srcs[a].at[:, pl.ds(_index(peer) * r, r), :], lands[a].at[me]
            sem = a * (N_DEV - 1) + k - 1
            out.append(pltpu.make_async_remote_copy(src_ref=src, dst_ref=dst, send_sem=send.at[sem], recv_sem=recv.at[sem],
                                                    device_id=peer, device_id_type=MESH))
    return out


def _land_with_own(kind, src):
    me = _index(_my_pos())
    if kind == "gather":
        _, r, C = src.shape
        return lax.dynamic_update_slice(lax.empty((1, N_DEV * r, C), src.dtype), src, (0, me * r, 0))
    _, r8, C = src.shape
    r = r8 // N_DEV
    own = lax.dynamic_slice(src, (0, me * r, 0), (1, r, C))
    return lax.dynamic_update_slice(lax.empty((N_DEV, 1, r, C), src.dtype), own[None], (me, 0, 0, 0))


def exchange_start(kinds, srcs, name):
    n = len(srcs)
    lands = [_land_with_own(k, s) for k, s in zip(kinds, srcs)]

    def body(*refs):
        s, l = refs[:n], refs[n:2 * n]
        send, recv = refs[2 * n], refs[2 * n + 1]
        for cp in _exchange_copies(kinds, s, l, send, recv):
            cp.start()
        refs[-1][...] = jnp.zeros((8, 128), F32)

    hbm = [pltpu.HBM(a.shape, a.dtype) for a in srcs + lands]
    outs = pl.pallas_call(
        body, name=name,
        out_shape=(pltpu.SemaphoreType.DMA((n * (N_DEV - 1),)), pltpu.SemaphoreType.DMA((n * (N_DEV - 1),)), *hbm,
                   _sds((8, 128), F32)),
        in_specs=[_HBM] * (2 * n), out_specs=(_SEM, _SEM, *([_HBM] * (2 * n)), pl.BlockSpec(memory_space=pltpu.VMEM)),
        input_output_aliases={i: 2 + i for i in range(2 * n)},
        compiler_params=pltpu.CompilerParams(has_side_effects=_EFFECT),
    )(*[pltpu.with_memory_space_constraint(a, pltpu.HBM) for a in srcs + lands])
    return outs[0], outs[1], list(outs[2:2 + n]), list(outs[2 + n:2 + 2 * n]), outs[-1]


def exchange_wait(kinds, started, after, name):
    send, recv, srcs, lands, _ = started
    n = len(srcs)
    after = list(after) if isinstance(after, (list, tuple)) else [after]

    def body(*refs):
        s, l = refs[:n], refs[n:2 * n]
        for cp in _exchange_copies(kinds, s, l, refs[2 * n], refs[2 * n + 1]):
            cp.wait_send()
            cp.wait_recv()
        refs[-1][...] = jnp.zeros((8, 128), F32)

    outs = pl.pallas_call(
        body, name=name, out_shape=[pltpu.HBM(a.shape, a.dtype) for a in srcs + lands] + [_sds((8, 128), F32)],
        in_specs=[_HBM] * (2 * n) + [_SEM, _SEM] + [pl.BlockSpec(memory_space=pl.ANY)] * len(after),
        out_specs=[_HBM] * (2 * n) + [pl.BlockSpec(memory_space=pltpu.VMEM)],
        input_output_aliases={i: i for i in range(2 * n)},
        compiler_params=pltpu.CompilerParams(has_side_effects=_EFFECT),
    )(*srcs, *lands, send, recv, *after)
    return list(outs[n:2 * n]), outs[-1]


def sum_slots(lands, name):
    L = len(lands)
    _, _, r, C = lands[0].shape
    tr = _tile(r, 256, 16)

    def body(*refs):
        o_ref = refs[L]
        for l in range(L):
            @pl.when(pl.program_id(0) == l)
            def _(x_ref=refs[l]):
                acc = x_ref[0].astype(F32)
                for s in range(1, N_DEV):
                    acc = acc + x_ref[s].astype(F32)
                o_ref[...] = acc

    specs = [pl.BlockSpec((N_DEV, None, tr, C), lambda g, i, l=l: (0, 0, jnp.where(g == l, i, 0), 0)) for l in range(L)]
    return pl.pallas_call(
        body, grid=(L, r // tr), in_specs=specs,
        out_specs=pl.BlockSpec((None, tr, C), lambda g, i: (g, i, 0)), out_shape=_sds((L, r, C), F32),
        compiler_params=_cp(("arbitrary", "arbitrary")), name=name)(*lands)


def pack_flat(arrs):
    flat = jnp.concatenate([a.reshape(-1).astype(F32) for a in arrs])
    n = flat.shape[0]
    total = -(-n // 32768) * 32768
    return jnp.pad(flat, (0, total - n)).reshape(total // 128, 128)


def unpack_flat(flat, shapes):
    v = flat.reshape(-1)
    out, off = [], 0
    for s in shapes:
        n = int(np.prod(s))
        out.append(v[off:off + n].reshape(s))
        off += n
    return out


def adamw(w, g, m, v, name):
    shape = w.shape
    C = shape[-1]
    R = int(np.prod(shape)) // C
    tr = _tile(R, 128, 8)
    c1 = 1.0 - ADAM_B1 ** ADAM_STEP
    c2 = 1.0 - ADAM_B2 ** ADAM_STEP

    def body(w_ref, g_ref, m_ref, v_ref, d_ref, nm_ref, nv_ref):
        g = g_ref[...]
        m2 = ADAM_B1 * m_ref[...] + (1.0 - ADAM_B1) * g
        v2 = ADAM_B2 * v_ref[...] + (1.0 - ADAM_B2) * (g * g)
        nm_ref[...] = m2
        nv_ref[...] = v2
        d_ref[...] = -ADAM_LR * ((m2 / c1) / (jnp.sqrt(v2 / c2) + ADAM_EPS) + ADAM_WD * w_ref[...])

    spec = pl.BlockSpec((tr, C), lambda i: (i, 0))
    outs = pl.pallas_call(body, grid=(R // tr,), in_specs=[spec] * 4, out_specs=[spec] * 3,
                          out_shape=[_sds((R, C), F32)] * 3, compiler_params=_cp(("parallel",)), name=name)(
        w.reshape(R, C), g.reshape(R, C), m.reshape(R, C), v.reshape(R, C))
    return [o.reshape(shape) for o in outs]


def _bexp(sg_b):
    return jnp.repeat(sg_b.T, HEAD_DIM, axis=1)


def _block_diag(pool_w):
    out = jnp.zeros((GW, GW), F32)
    for i in range(4):
        out = out.at[i * 64:(i + 1) * 64, i * 64:(i + 1) * 64].set(pool_w[i])
    return out


def _cmp_rows(t):
    B, S, _ = t.shape
    t2 = t.reshape(B, S // CMP_STRIDE, CMP_STRIDE * HEAD_DIM)
    nxt = jnp.concatenate([t2[:, 1:], jnp.zeros_like(t2[:, :1])], axis=1)
    return jnp.concatenate([t2, nxt], axis=-1)


def _tile4(t):
    return jnp.tile(t, (1, 1, 4)).astype(BF16)


def kv_tiles(z):
    T = z.shape[0]

    def body(x_ref, cv_ref, ks_ref, vs_ref, kw_ref, vw_ref):
        x = x_ref[...]
        cv_ref[...] = x[:, :128]
        xb = x.astype(BF16)
        src = lax.broadcasted_iota(jnp.int32, (384, GW), 0)
        lane = lax.broadcasted_iota(jnp.int32, (384, GW), 1) & 63
        for i, o_ref in enumerate((ks_ref, vs_ref, kw_ref, vw_ref)):
            expand = (src == lane + 64 * (i + 2)).astype(BF16)
            o_ref[...] = _dot(xb, expand).astype(o_ref.dtype)

    return pl.pallas_call(
        body, grid=(T // ROW_TILE,), in_specs=[pl.BlockSpec((ROW_TILE, 384), lambda i: (i, COL_KV // 384))],
        out_specs=[_rows(ROW_TILE, 128)] + [_rows(ROW_TILE, GW)] * 4,
        out_shape=[_sds((T, 128), F32)] + [_sds((T, GW), BF16)] * 4,
        compiler_params=_cp(("parallel",)), name="kv_tiles")(z)


def layer_fwd(x, p, late, B, S):
    T = B * S
    sv = {"x0": x}
    h1, h1t = rms_fwd(x, p["g_pre_mix"], "rms_pre_mix")
    z = mm(h1, p["w_in"], name="mm_in")
    z3 = z.reshape(B, S, ZW)
    ya = mixa_fwd(z, p["sg_ln_g"], p["sg_w"], p["bexp"])
    yb, hc = mixb_fwd(z3, p["cv_w"], p["cv_b"], p["cv_ln_g"], p["cv_ln_b"], p["cv_pw"], p["cv_pw_b"])
    kcv, ks4, vs4, kw4, vw4 = [a.reshape(B, S, -1) for a in kv_tiles(z)]
    tbk, tbv = _cmp_rows(kcv[:, :, :HEAD_DIM]), _cmp_rows(kcv[:, :, HEAD_DIM:])
    kc, vc = cmp_kv_fwd(tbk, tbv, p["cmp_pos_k"], p["cmp_pos_v"], p["cmp_w1_k"], p["cmp_w2_k"], p["cmp_w1_v"], p["cmp_w2_v"])
    kc4, vc4 = _tile4(kc), _tile4(vc)
    o_cmp, selT = cmp_attn_fwd(z3, kc4, vc4)
    o_slc, lse_slc = attn_fwd("slc", z3, ks4, vs4, selT)
    o_win, lse_win = attn_fwd("win", z3, kw4, vw4, selT)
    yc = combine_fwd(z3, o_cmp, o_slc, o_win)
    yd = mixd_fwd(z3, p["pool_bd"], p["pool_scale"])
    ycat = jnp.concatenate([ya, yb.reshape(T, GW), yc.reshape(T, GW), yd.reshape(T, GW)], axis=-1)
    p.update(late(ycat))
    mix = mm(ycat, p["w_out"], name="mm_out")
    x1 = rms_post_fwd(x, mix, p["g_post_mix"], "rms_post_mix")
    h2, h2t = rms_fwd(x1, p["g_pre_ffn"], "rms_pre_ffn")
    gu4, a3 = ffn_up_fwd(h2, p["w_gu"])
    f = mm(a3, p["w_down"], blk="k", name="mm_down")
    x2 = rms_post_fwd(x1, f, p["g_post_ffn"], "rms_post_ffn")
    sv.update(h1t=h1t, z=z, hc=hc, tbk=tbk, tbv=tbv, kc4=kc4, vc4=vc4, ks4=ks4, vs4=vs4, kw4=kw4, vw4=vw4, o_cmp=o_cmp, selT=selT,
              o_slc=o_slc, lse_slc=lse_slc, o_win=o_win, lse_win=lse_win, ycat=ycat, mix=mix, x1=x1, h2t=h2t, gu4=gu4, a3=a3, f=f)
    return x2, sv


def _unfold(t4):
    return t4[:, :, :HEAD_DIM]


def layer_bwd_ffn(dx2, p, sv, B, S):
    T = B * S
    gb, gs = {}, {}
    df, gs["g_post_ffn"] = rms_bwd(sv["f"], p["g_post_ffn"], dx2, None, BF16, "rms_post_ffn_bwd")
    dgu = ffn_down_dx(df, p["w_down"], sv["gu4"]).reshape(N_DEV, T, FFN_BLK)
    gb["w_down"] = mm(sv["a3"], df, ta=True, blk="m", out_dtype=BF16, name="mm_down_dw")
    dh2 = mm(dgu, p["w_gu"], tb=True, blk="k", name="mm_gu_dx")
    gb["w_gu"] = mm(sv["h2t"], dgu, blk="n", out_dtype=BF16, name="mm_gu_dw")
    dx1, gs["g_pre_ffn"] = rms_bwd(sv["x1"], p["g_pre_ffn"], dh2, dx2, F32, "rms_pre_ffn_bwd")
    gb["w_gu"] = gb["w_gu"].reshape(1, N_DEV * D_MODEL, FFN_BLK)
    gb["w_down"] = gb["w_down"].reshape(1, FFN_HIDDEN, D_MODEL)
    return dx1, gb, gs


def layer_bwd_mix(dx1, p, sv, B, S):
    T = B * S
    gb, gs = {}, {}
    dmix, gs["g_post_mix"] = rms_bwd(sv["mix"], p["g_post_mix"], dx1, None, BF16, "rms_post_mix_bwd")
    dycat = mm(dmix, p["w_out"], tb=True, name="mm_out_dx")
    gb["w_out"] = mm(sv["ycat"], dmix, ta=True, out_dtype=BF16, name="mm_out_dw")
    dycat3 = dycat.reshape(B, S, D_MODEL)
    z = sv["z"]
    z3 = z.reshape(B, S, ZW)
    du, dv, gs["sg_w"], db, gs["sg_ln_g"] = mixa_bwd(z, dycat, p["sg_ln_g"], p["sg_w"], p["bexp"])
    gs["sg_b"] = db[:, :4].T
    (da, dgt, gs["cv_w"], gs["cv_b"], gs["cv_ln_g"], gs["cv_ln_b"], gpw, gs["cv_pw_b"]) = mixb_bwd(
        z3, sv["hc"], dycat3, p["cv_w"], p["cv_ln_g"], p["cv_ln_b"], p["cv_pw"])
    gb["cv_pw"] = gpw.astype(BF16)
    dd, dwbd, gs["pool_scale"] = mixd_bwd(z3, dycat3, p["pool_bd"], p["pool_scale"])
    gs["pool_w"] = jnp.stack([dwbd[i * 64:(i + 1) * 64, i * 64:(i + 1) * 64] for i in range(4)])
    do_c, do_s, do_w, dgl = combine_bwd(z3, sv["o_cmp"], sv["o_slc"], sv["o_win"], dycat3)
    dq_s, dks4, dvs4 = attn_bwd("slc", z3, sv["ks4"], sv["vs4"], sv["selT"], sv["o_slc"], sv["lse_slc"], do_s)
    dq_w, dkw4, dvw4 = attn_bwd("win", z3, sv["kw4"], sv["vw4"], sv["selT"], sv["o_win"], sv["lse_win"], do_w)
    dq_c, dkc4, dvc4 = cmp_attn_bwd(z3, sv["kc4"], sv["vc4"], do_c)
    (dk2, dv2, gs["cmp_pos_k"], gs["cmp_pos_v"], gw1k, gs["cmp_w2_k"], gw1v, gs["cmp_w2_v"]) = cmp_kv_bwd(
        sv["tbk"], sv["tbv"], p["cmp_pos_k"], p["cmp_pos_v"], p["cmp_w1_k"], p["cmp_w2_k"], p["cmp_w1_v"], p["cmp_w2_v"],
        _unfold(dkc4), _unfold(dvc4))
    gb["cmp_w1_k"], gb["cmp_w1_v"] = gw1k.astype(BF16), gw1v.astype(BF16)
    dkv = jnp.concatenate([dk2.reshape(B, S, HEAD_DIM), dv2.reshape(B, S, HEAD_DIM), _unfold(dks4), _unfold(dvs4),
                           _unfold(dkw4), _unfold(dvw4)], axis=-1).reshape(T, 384)
    dz = assemble_dz(du, dv, da.reshape(T, GW), dgt.reshape(T, GW), dq_c.reshape(T, GW), dq_s.reshape(T, GW),
                     dq_w.reshape(T, GW), dd.reshape(T, GW), dkv, dgl.reshape(T, 128))
    dh1 = mm(dz, p["w_in"], tb=True, name="mm_in_dx")
    gb["w_in"] = mm(sv["h1t"], dz, out_dtype=BF16, name="mm_in_dw")
    dx0, gs["g_pre_mix"] = rms_bwd(sv["x0"], p["g_pre_mix"], dh1, dx1, F32, "rms_pre_mix_bwd")
    return dx0, gb, gs


SMALL = ["g_pre_mix", "g_post_mix", "g_pre_ffn", "g_post_ffn", "sg_ln_g", "sg_w", "sg_b", "cv_w", "cv_b", "cv_ln_g", "cv_ln_b",
         "cv_pw_b", "cmp_pos_k", "cmp_pos_v", "cmp_w2_k", "cmp_w2_v", "pool_w", "pool_scale"]
BIG = ["w_in", "w_out", "w_gu", "w_down", "cmp_w1_k", "cmp_w1_v", "cv_pw"]
NAMES = ["g_pre_mix", "g_post_mix", "g_pre_ffn", "g_post_ffn", "w_in", "sg_ln_g", "sg_w", "sg_b", "cv_w", "cv_b", "cv_ln_g",
         "cv_ln_b", "cv_pw", "cv_pw_b", "cmp_pos_k", "cmp_pos_v", "cmp_w1_k", "cmp_w2_k", "cmp_w1_v", "cmp_w2_v", "pool_w",
         "pool_scale", "w_out", "ffn_w_gu", "ffn_w_down"]


def kernel(x, g_pre_mix, g_post_mix, g_pre_ffn, g_post_ffn, w_in, sg_ln_g, sg_w, sg_b, cv_w, cv_b, cv_ln_g, cv_ln_b, cv_pw, cv_pw_b, cmp_pos_k, cmp_pos_v, cmp_w1_k, cmp_w2_k, cmp_w1_v, cmp_w2_v, pool_w, pool_scale, w_out, ffn_w_gu, ffn_w_down, loss_target, m_g_pre_mix, m_g_post_mix, m_g_pre_ffn, m_g_post_ffn, m_w_in, m_sg_ln_g, m_sg_w, m_sg_b, m_cv_w, m_cv_b, m_cv_ln_g, m_cv_ln_b, m_cv_pw, m_cv_pw_b, m_cmp_pos_k, m_cmp_pos_v, m_cmp_w1_k, m_cmp_w2_k, m_cmp_w1_v, m_cmp_w2_v, m_pool_w, m_pool_scale, m_w_out, m_ffn_w_gu, m_ffn_w_down, v_g_pre_mix, v_g_post_mix, v_g_pre_ffn, v_g_post_ffn, v_w_in, v_sg_ln_g, v_sg_w, v_sg_b, v_cv_w, v_cv_b, v_cv_ln_g, v_cv_ln_b, v_cv_pw, v_cv_pw_b, v_cmp_pos_k, v_cmp_pos_v, v_cmp_w1_k, v_cmp_w2_k, v_cmp_w1_v, v_cmp_w2_v, v_pool_w, v_pool_scale, v_w_out, v_ffn_w_gu, v_ffn_w_down):
    args = dict(locals())
    W = {n: args[n] for n in NAMES}
    M = {n: args["m_" + n] for n in NAMES}
    V = {n: args["v_" + n] for n in NAMES}
    B, S, _ = x.shape
    T = B * S
    L = w_in.shape[0]
    me = _index(_my_pos())
    cpd = GW // N_DEV

    shards = {"w_in": lambda l: pack_cols(w_in[l]).astype(BF16), "w_out": lambda l: w_out[l].astype(BF16),
              "w_gu": lambda l: ffn_w_gu[l].astype(BF16), "w_down": lambda l: ffn_w_down[l].astype(BF16),
              "cmp_w1_k": lambda l: cmp_w1_k[l].astype(BF16), "cmp_w1_v": lambda l: cmp_w1_v[l].astype(BF16),
              "cv_pw": lambda l: cv_pw[l].astype(BF16), "cv_w": lambda l: cv_w[l].T}
    early, later = ["w_in", "cmp_w1_k", "cmp_w1_v", "cv_pw", "cv_w"], ["w_out", "w_gu", "w_down"]

    def start_gather(names, l, tag, srcs=None, behind=None):
        srcs = list(srcs) if srcs is not None else [shards[n](l)[None] for n in names]
        if behind is not None:
            srcs[0] = srcs[0] + behind[0, 0].astype(srcs[0].dtype)
        return exchange_start(["gather"] * len(names), srcs, "gather_%s_start_%d" % (tag, l))

    def wait_gather(names, started, after, l, tag):
        arrived, done = exchange_wait(["gather"] * len(names), started, after, "gather_%s_wait_%d" % (tag, l))
        full = {n: a[0] for n, a in zip(names, arrived)}
        full["done"] = done
        if "w_gu" in full:
            full["w_gu"] = full["w_gu"].reshape(N_DEV, D_MODEL, FFN_BLK)
            full["w_down"] = full["w_down"].reshape(4, FFN_BLK, D_MODEL)
        if "cv_w" in full:
            full["cv_w"] = full["cv_w"].T
        return full

    def layer_params(l, full):
        p = dict(full)
        for n in ("g_pre_mix", "g_post_mix", "g_pre_ffn", "g_post_ffn", "sg_ln_g", "cv_b", "cv_ln_g", "cv_ln_b", "cv_pw_b",
                  "pool_scale"):
            p[n] = W[n][l][None, :]
        p["sg_w"] = sg_w[l]
        p["bexp"] = _bexp(sg_b[l])
        p["cmp_pos_k"] = cmp_pos_k[l].reshape(1, 2048)
        p["cmp_pos_v"] = cmp_pos_v[l].reshape(1, 2048)
        p["cmp_w2_k"], p["cmp_w2_v"] = cmp_w2_k[l], cmp_w2_v[l]
        p["pool_bd"] = _block_diag(pool_w[l])
        return p

    xs = x.reshape(T, D_MODEL)
    params, saved = [], []
    early_st = start_gather(early, 0, "early")
    ahead = {}
    for l in range(L):
        later_srcs = [shards[n](l)[None] for n in later]
        full = wait_gather(early, early_st, later_srcs + ([xs] if l > 0 else []), l, "early")
        later_st = start_gather(later, l, "later", srcs=later_srcs, behind=full["done"])
        p = layer_params(l, full)
        p["g_pre_mix"] = p["g_pre_mix"] + later_st[4][0, 0]

        def late(after, l=l, st=later_st, p=p):
            got = wait_gather(later, st, after, l, "later")
            if l + 1 < L:
                ahead["early"] = start_gather(early, l + 1, "early", behind=got["done"])
                got["g_post_mix"] = p["g_post_mix"] + ahead["early"][4][0, 0]
            return got

        xs, sv = layer_fwd(xs, p, late, B, S)
        params.append(p)
        saved.append(sv)
        early_st = ahead.get("early")
    dy, lpart = loss_fwd_bwd(xs, loss_target.reshape(T, D_MODEL))
    loss = lax.psum(lpart[0, 0], ("x", "y", "c"))

    ffn_big, mix_big = ["w_gu", "w_down"], ["w_in", "w_out", "cmp_w1_k", "cmp_w1_v", "cv_pw"]
    ffn_kinds, mix_kinds = ["scatter"] * len(ffn_big), ["scatter"] * len(mix_big) + ["gather"]
    pending, token = [], None
    for l in reversed(range(L)):
        p = dict(params[l])
        if token is not None:
            p["g_post_ffn"] = p["g_post_ffn"] + token[0, 0]
        dy, gb_ffn, gs = layer_bwd_ffn(dy, p, saved[l], B, S)
        st_ffn = exchange_start(ffn_kinds, [gb_ffn[n] for n in ffn_big], "scatter_ffn_start_%d" % l)
        p["g_post_mix"] = p["g_post_mix"] + st_ffn[4][0, 0]
        dy, gb_mix, gs_mix = layer_bwd_mix(dy, p, saved[l], B, S)
        gs.update(gs_mix)
        small_shapes = [tuple(gs[n].shape) for n in SMALL]
        st_mix = exchange_start(mix_kinds, [gb_mix[n][None] for n in mix_big] + [pack_flat([gs[n] for n in SMALL])[None]],
                                "scatter_mix_start_%d" % l)
        token = st_mix[4]
        pending.append((l, st_ffn, st_mix))
    grad_x = dy.reshape(B, S, D_MODEL)

    delta, new_m, new_v = {}, {}, {}
    lands = {l: {} for l in range(L)}
    for l, st_ffn, _ in pending:
        lands[l].update(zip(ffn_big, exchange_wait(ffn_kinds, st_ffn, token, "scatter_ffn_wait_%d" % l)[0]))
    grads = {}
    for n, name in zip(ffn_big, ("ffn_w_gu", "ffn_w_down")):
        grads[name] = sum_slots([lands[l][n] for l in range(L)], "sum_" + n)
        delta[name], new_m[name], new_v[name] = adamw(W[name], grads[name], M[name], V[name], "adamw_" + name)
    for l, _, st_mix in pending:
        lands[l].update(zip(mix_big + ["small"],
                            exchange_wait(mix_kinds, st_mix, delta["ffn_w_down"], "scatter_mix_wait_%d" % l)[0]))
    for n in mix_big:
        grads[n] = sum_slots([lands[l][n] for l in range(L)], "sum_" + n)
    grads["w_in"] = unpack_cols(grads["w_in"])
    rows = lands[0]["small"].shape[1] // N_DEV
    reduced = sum_slots([lands[l]["small"].reshape(N_DEV, 1, rows, 128) for l in range(L)], "sum_small")
    per_layer = [unpack_flat(reduced[l], small_shapes) for l in range(L)]
    for i, n in enumerate(SMALL):
        g = jnp.stack([per_layer[l][i] for l in range(L)])
        grads[n] = g.reshape(W[n].shape) if n != "cv_w" else g
    grads["cv_w"] = lax.dynamic_slice(grads["cv_w"], (0, 0, me * cpd), (L, CONV_WIDTH, cpd))

    for n in mix_big:
        delta[n], new_m[n], new_v[n] = adamw(W[n], grads[n], M[n], V[n], "adamw_" + n)
    shapes = [W[n].shape for n in SMALL]
    packed = adamw(pack_flat([W[n] for n in SMALL]), pack_flat([grads[n] for n in SMALL]),
                   pack_flat([M[n] for n in SMALL]), pack_flat([V[n] for n in SMALL]), "adamw_small")
    for out, flat in zip((delta, new_m, new_v), packed):
        for n, a in zip(SMALL, unpack_flat(flat, shapes)):
            out[n] = a

    return (loss, grad_x, *[grads[n] for n in NAMES], *[delta[n] for n in NAMES], *[new_m[n] for n in NAMES],
            *[new_v[n] for n in NAMES])
```

```python
import numpy as np
import jax
import jax.numpy as jnp
from jax import lax
from jax.experimental import pallas as pl
from jax.experimental.pallas import tpu as pltpu

F32 = jnp.float32
BF16 = jnp.bfloat16
HI = lax.Precision.HIGHEST

D_MODEL = 1024
GW = 256
HEAD_DIM = 64
ZW = 2048
SG_CHUNK = 128
CONV_WIDTH = 31
CONV_PAD = 32
CMP_STRIDE = 16
N_CMP = 128
SLC_BLOCK_SHIFT = 6
N_SLC = 32
SLC_TOPK = 8
WIN = 512
NEG = -1e30
FORCE_BONUS = 1e4
RMS_EPS = 1e-6
LN_EPS = 1e-5
FFN_HIDDEN = 2816
N_DEV = 8
FFN_BLK = 2 * FFN_HIDDEN // N_DEV
TQ = 256
ROW_TILE = 512
CONV_TILE = 256
VMEM_LIMIT = 56 * 1024 * 1024
MESH = pl.DeviceIdType.MESH

ADAM_LR, ADAM_B1, ADAM_B2, ADAM_EPS, ADAM_WD, ADAM_STEP = 0.001, 0.9, 0.999, 1e-08, 0.01, 10

COL_U, COL_V, COL_A, COL_G, COL_Q, COL_D, COL_KV, COL_GL = 0, 256, 512, 768, 1024, 1280, 1536, 1920


def _sds(shape, dtype):
    return jax.ShapeDtypeStruct(shape, dtype)


def _cp(sem=None):
    return pltpu.CompilerParams(dimension_semantics=sem, vmem_limit_bytes=VMEM_LIMIT)


def _tile(n, target, q=128):
    best = None
    for t in range(q, min(n, target) + 1, q):
        if n % t == 0:
            best = t
    return best or n


def _full(shape):
    nd = len(shape)
    return pl.BlockSpec(shape, lambda *_: (0,) * nd)


def _sigmoid(x):
    return jax.nn.sigmoid(x)


def _dot(a, b):
    return jnp.dot(a, b, preferred_element_type=F32)


def _dot_nt(a, b):
    return lax.dot_general(a, b, (((1,), (1,)), ((), ())), preferred_element_type=F32)


def _dot_tn(a, b):
    return lax.dot_general(a, b, (((0,), (0,)), ((), ())), preferred_element_type=F32)


def _lane_head(width=GW):
    return lax.shift_right_logical(lax.broadcasted_iota(jnp.int32, (1, width), 1), 6)


def _fold_heads(x):
    return x + pltpu.roll(x, 64, 1) + pltpu.roll(x, 128, 1) + pltpu.roll(x, 192, 1)


def pack_cols(w):
    pad = jnp.zeros(w.shape[:-1] + (ZW - 1932,), w.dtype)
    return jnp.concatenate([w[..., :1280], w[..., 1676:1932], w[..., 1280:1664], w[..., 1664:1676], pad], axis=-1)


def unpack_cols(wp):
    return jnp.concatenate([wp[..., :1280], wp[..., 1536:1920], wp[..., 1920:1932], wp[..., 1280:1536]], axis=-1)


def mm(a, b, *, ta=False, tb=False, blk=None, out_dtype=F32, name, tm=1024, tn=1024, tk=1024):
    a_dims = ("k", "m") if ta else ("m", "k")
    b_dims = ("n", "k") if tb else ("k", "n")
    a3, b3, o3 = blk in a_dims and blk is not None, blk in b_dims and blk is not None, blk in ("m", "n")
    size = {}
    size[a_dims[0]], size[a_dims[1]] = a.shape[-2:]
    size[b_dims[0]], size[b_dims[1]] = b.shape[-2:]
    nb = a.shape[0] if a3 else (b.shape[0] if b3 else 1)
    tile = {"m": _tile(size["m"], tm), "n": _tile(size["n"], tn), "k": _tile(size["k"], tk)}
    grid = {d: size[d] // tile[d] for d in "mnk"}
    if blk is not None:
        tile[blk] = size[blk]
        grid[blk] = nb
    nk = grid["k"]

    def spec(dims, is3):
        def im(i, j, k):
            g = {"m": i, "n": j, "k": k}
            idx = tuple(0 if d == blk else g[d] for d in dims)
            return ((g[blk],) + idx) if is3 else idx
        shape = (tile[dims[0]], tile[dims[1]])
        return pl.BlockSpec(((None,) + shape) if is3 else shape, im)

    dn = (((0 if ta else 1,), (1 if tb else 0,)), ((), ()))

    def partial(a_ref, b_ref):
        return lax.dot_general(a_ref[...].astype(BF16), b_ref[...].astype(BF16), dn, preferred_element_type=F32)

    def body_single(a_ref, b_ref, o_ref):
        o_ref[...] = partial(a_ref, b_ref).astype(o_ref.dtype)

    def body_acc(a_ref, b_ref, o_ref, acc):
        k = pl.program_id(2)

        @pl.when(k == 0)
        def _():
            acc[...] = partial(a_ref, b_ref)

        @pl.when((k > 0) & (k < nk - 1))
        def _():
            acc[...] += partial(a_ref, b_ref)

        @pl.when(k == nk - 1)
        def _():
            o_ref[...] = (acc[...] + partial(a_ref, b_ref)).astype(o_ref.dtype)

    oshape = ((nb,) if o3 else ()) + (size["m"], size["n"])
    return pl.pallas_call(
        body_single if nk == 1 else body_acc, grid=(grid["m"], grid["n"], nk),
        in_specs=[spec(a_dims, a3), spec(b_dims, b3)], out_specs=spec(("m", "n"), o3),
        out_shape=_sds(oshape, out_dtype),
        scratch_shapes=[] if nk == 1 else [pltpu.VMEM((tile["m"], tile["n"]), F32)],
        compiler_params=_cp(("parallel", "parallel", "arbitrary")), name=name)(a, b)


def _rows(tm, width):
    return pl.BlockSpec((tm, width), lambda i: (i, 0))


def rms_fwd(x, g, name):
    T = x.shape[0]

    def body(x_ref, g_ref, h_ref, ht_ref):
        x = x_ref[...]
        r = lax.rsqrt(jnp.mean(x * x, axis=-1, keepdims=True) + RMS_EPS)
        h = (x * r) * g_ref[...]
        h_ref[...] = h.astype(h_ref.dtype)
        ht_ref[...] = h.T.astype(ht_ref.dtype)

    return pl.pallas_call(body, grid=(T // ROW_TILE,), in_specs=[_rows(ROW_TILE, D_MODEL), _full((1, D_MODEL))],
                          out_specs=[_rows(ROW_TILE, D_MODEL), pl.BlockSpec((D_MODEL, ROW_TILE), lambda i: (0, i))],
                          out_shape=[_sds((T, D_MODEL), BF16), _sds((D_MODEL, T), BF16)],
                          compiler_params=_cp(("parallel",)), name=name)(x, g)


def rms_post_fwd(xres, m, g, name):
    T = m.shape[0]

    def body(x_ref, m_ref, g_ref, o_ref):
        m = m_ref[...]
        r = lax.rsqrt(jnp.mean(m * m, axis=-1, keepdims=True) + RMS_EPS)
        o_ref[...] = x_ref[...] + (m * r) * g_ref[...]

    return pl.pallas_call(body, grid=(T // ROW_TILE,),
                          in_specs=[_rows(ROW_TILE, D_MODEL), _rows(ROW_TILE, D_MODEL), _full((1, D_MODEL))],
                          out_specs=_rows(ROW_TILE, D_MODEL), out_shape=_sds((T, D_MODEL), F32),
                          compiler_params=_cp(("parallel",)), name=name)(xres, m, g)


def rms_bwd(m, g, dy, dres, out_dtype, name):
    T = m.shape[0]
    has_res = dres is not None

    def body(*refs):
        if has_res:
            m_ref, g_ref, dy_ref, dres_ref, dm_ref, dg_ref = refs
        else:
            m_ref, g_ref, dy_ref, dm_ref, dg_ref = refs
        m = m_ref[...]
        dy = dy_ref[...].astype(F32)
        r = lax.rsqrt(jnp.mean(m * m, axis=-1, keepdims=True) + RMS_EPS)
        n = m * r
        dn = dy * g_ref[...]
        dm = r * (dn - n * jnp.mean(dn * n, axis=-1, keepdims=True))
        if has_res:
            dm = dm + dres_ref[...]
        dm_ref[...] = dm.astype(dm_ref.dtype)

        @pl.when(pl.program_id(0) == 0)
        def _():
            dg_ref[...] = jnp.zeros_like(dg_ref)

        dg_ref[...] += jnp.sum(dy * n, axis=0, keepdims=True)

    ins = [m, g, dy] + ([dres] if has_res else [])
    specs = [_rows(ROW_TILE, D_MODEL), _full((1, D_MODEL)), _rows(ROW_TILE, D_MODEL)] + ([_rows(ROW_TILE, D_MODEL)] if has_res else [])
    return pl.pallas_call(body, grid=(T // ROW_TILE,), in_specs=specs,
                          out_specs=[_rows(ROW_TILE, D_MODEL), _full((1, D_MODEL))],
                          out_shape=[_sds((T, D_MODEL), out_dtype), _sds((1, D_MODEL), F32)],
                          compiler_params=_cp(("arbitrary",)), name=name)(*ins)


def loss_fwd_bwd(y, tgt):
    T = y.shape[0]

    def body(y_ref, t_ref, dy_ref, l_ref):
        e = y_ref[...] - t_ref[...]
        dy_ref[...] = e * (1.0 / D_MODEL)

        @pl.when(pl.program_id(0) == 0)
        def _():
            l_ref[...] = jnp.zeros_like(l_ref)

        l_ref[...] += jnp.full(l_ref.shape, 0.5 * jnp.sum(jnp.mean(e * e, axis=-1, keepdims=True)), F32)

    return pl.pallas_call(body, grid=(T // ROW_TILE,), in_specs=[_rows(ROW_TILE, D_MODEL)] * 2,
                          out_specs=[_rows(ROW_TILE, D_MODEL), _full((8, 128))],
                          out_shape=[_sds((T, D_MODEL), F32), _sds((8, 128), F32)],
                          compiler_params=_cp(("arbitrary",)), name="loss")(y, tgt)


FFN_TILE = 2048


def _gu_spec():
    return pl.BlockSpec((2, None, FFN_TILE, FFN_BLK), lambda i, j: (0, j, i, 0))


def ffn_up_fwd(h, w_gu):
    T = h.shape[0]

    def body(h_ref, wg_ref, wu_ref, gu_ref, a_ref):
        h = h_ref[...]
        gate = _dot(h, wg_ref[...])
        up = _dot(h, wu_ref[...])
        gu_ref[0] = gate.astype(gu_ref.dtype)
        gu_ref[1] = up.astype(gu_ref.dtype)
        a_ref[...] = (gate * _sigmoid(gate) * up).astype(a_ref.dtype)

    return pl.pallas_call(
        body, grid=(T // FFN_TILE, 4),
        in_specs=[pl.BlockSpec((FFN_TILE, D_MODEL), lambda i, j: (i, 0)),
                  pl.BlockSpec((None, D_MODEL, FFN_BLK), lambda i, j: (j, 0, 0)),
                  pl.BlockSpec((None, D_MODEL, FFN_BLK), lambda i, j: (j + 4, 0, 0))],
        out_specs=[_gu_spec(), pl.BlockSpec((None, FFN_TILE, FFN_BLK), lambda i, j: (j, i, 0))],
        out_shape=[_sds((2, 4, T, FFN_BLK), BF16), _sds((4, T, FFN_BLK), BF16)],
        compiler_params=_cp(("parallel", "parallel")), name="ffn_up_fwd")(h, w_gu, w_gu)


def ffn_down_dx(df, w_down, gu4):
    T = df.shape[0]

    def body(df_ref, w_ref, gu_ref, d_ref):
        da = _dot_nt(df_ref[...], w_ref[...])
        gate, up = gu_ref[0].astype(F32), gu_ref[1].astype(F32)
        sg = _sigmoid(gate)
        d_ref[0] = (da * up * (sg * (1.0 + gate * (1.0 - sg)))).astype(d_ref.dtype)
        d_ref[1] = (da * (gate * sg)).astype(d_ref.dtype)

    return pl.pallas_call(
        body, grid=(T // FFN_TILE, 4),
        in_specs=[pl.BlockSpec((FFN_TILE, D_MODEL), lambda i, j: (i, 0)),
                  pl.BlockSpec((None, FFN_BLK, D_MODEL), lambda i, j: (j, 0, 0)), _gu_spec()],
        out_specs=_gu_spec(), out_shape=_sds((2, 4, T, FFN_BLK), BF16),
        compiler_params=_cp(("parallel", "parallel")), name="ffn_down_dx")(df, w_down, gu4)


def _zcol(tm, col):
    return pl.BlockSpec((tm, GW), lambda i: (i, col // GW))


def _sg_common(v, g):
    mu = jnp.mean(v, axis=-1, keepdims=True)
    xc = v - mu
    rstd = lax.rsqrt(jnp.mean(xc * xc, axis=-1, keepdims=True) + LN_EPS)
    vhat = xc * rstd
    return vhat, rstd, vhat * g


def _tril_weights(w_ref):
    tri = lax.broadcasted_iota(jnp.int32, (SG_CHUNK, SG_CHUNK), 0) >= lax.broadcasted_iota(jnp.int32, (SG_CHUNK, SG_CHUNK), 1)
    return tri, [jnp.where(tri, w_ref[h], 0.0).astype(BF16) for h in range(4)]


def mixa_fwd(z, ln_g, w, bexp):
    T = z.shape[0]
    nch = ROW_TILE // SG_CHUNK

    def body(u_ref, v_ref, g_ref, w_ref, be_ref, y_ref):
        _, _, vln = _sg_common(v_ref[...], g_ref[...])
        vb = vln.astype(BF16)
        head = _lane_head()
        _, wh = _tril_weights(w_ref)
        for c in range(nch):
            rows = slice(c * SG_CHUNK, (c + 1) * SG_CHUNK)
            sv = be_ref[...]
            for h in range(4):
                sv = sv + jnp.where(head == h, _dot(wh[h], vb[rows]), 0.0)
            y_ref[rows, :] = (u_ref[rows, :] * sv).astype(y_ref.dtype)

    return pl.pallas_call(body, grid=(T // ROW_TILE,),
                          in_specs=[_zcol(ROW_TILE, COL_U), _zcol(ROW_TILE, COL_V), _full((1, GW)), _full((4, SG_CHUNK, SG_CHUNK)),
                                    _full((SG_CHUNK, GW))],
                          out_specs=_rows(ROW_TILE, GW), out_shape=_sds((T, GW), BF16),
                          compiler_params=_cp(("parallel",)), name="mixa_fwd")(z, z, ln_g, w, bexp)


def mixa_bwd(z, dycat, ln_g, w, bexp):
    T = z.shape[0]
    nch = ROW_TILE // SG_CHUNK
    nsteps = T // ROW_TILE

    def body(u_ref, v_ref, dy_ref, g_ref, w_ref, be_ref, du_ref, dv_ref, dw_ref, db_ref, dg_ref, dbe_acc):
        step = pl.program_id(0)

        @pl.when(step == 0)
        def _():
            dw_ref[...] = jnp.zeros_like(dw_ref)
            dg_ref[...] = jnp.zeros_like(dg_ref)
            dbe_acc[...] = jnp.zeros_like(dbe_acc)

        g = g_ref[...]
        vhat, rstd, vln = _sg_common(v_ref[...], g)
        vb = vln.astype(BF16)
        head = _lane_head()
        tri, wh = _tril_weights(w_ref)
        dgsum = jnp.zeros((1, GW), F32)
        for c in range(nch):
            rows = slice(c * SG_CHUNK, (c + 1) * SG_CHUNK)
            sv = be_ref[...]
            for h in range(4):
                sv = sv + jnp.where(head == h, _dot(wh[h], vb[rows]), 0.0)
            dy = dy_ref[rows, :]
            du_ref[rows, :] = (dy * sv).astype(du_ref.dtype)
            dsv = dy * u_ref[rows, :]
            dbe_acc[...] += dsv
            dvln = jnp.zeros((SG_CHUNK, GW), F32)
            for h in range(4):
                dsvm = jnp.where(head == h, dsv, 0.0).astype(BF16)
                dw_ref[h] += _dot_nt(dsvm, vb[rows])
                dvln = dvln + _dot_tn(wh[h], dsvm)
            vh = vhat[rows]
            dgsum = dgsum + jnp.sum(dvln * vh, axis=0, keepdims=True)
            dvhat = dvln * g
            dv = rstd[rows] * (dvhat - jnp.mean(dvhat, axis=-1, keepdims=True) - vh * jnp.mean(dvhat * vh, axis=-1, keepdims=True))
            dv_ref[rows, :] = dv.astype(dv_ref.dtype)
        dg_ref[...] += dgsum

        @pl.when(step == nsteps - 1)
        def _():
            for h in range(4):
                dw_ref[h] = jnp.where(tri, dw_ref[h], 0.0)
            fold = (lax.shift_right_logical(lax.broadcasted_iota(jnp.int32, (GW, 128), 0), 6)
                    == lax.broadcasted_iota(jnp.int32, (GW, 128), 1)).astype(F32)
            db_ref[...] = jnp.dot(dbe_acc[...], fold, precision=HI, preferred_element_type=F32)

    return pl.pallas_call(
        body, grid=(nsteps,),
        in_specs=[_zcol(ROW_TILE, COL_U), _zcol(ROW_TILE, COL_V), pl.BlockSpec((ROW_TILE, GW), lambda i: (i, 0)),
                  _full((1, GW)), _full((4, SG_CHUNK, SG_CHUNK)), _full((SG_CHUNK, GW))],
        out_specs=[_rows(ROW_TILE, GW), _rows(ROW_TILE, GW), _full((4, SG_CHUNK, SG_CHUNK)), _full((SG_CHUNK, 128)), _full((1, GW))],
        out_shape=[_sds((T, GW), BF16), _sds((T, GW), BF16), _sds((4, SG_CHUNK, SG_CHUNK), F32), _sds((SG_CHUNK, 128), F32),
                   _sds((1, GW), F32)],
        scratch_shapes=[pltpu.VMEM((SG_CHUNK, GW), F32)],
        compiler_params=_cp(("arbitrary",)), name="mixa_bwd")(z, z, dycat, ln_g, w, bexp)


def _seq(S, col):
    return pl.BlockSpec((None, S, GW), lambda b: (b, 0, col // GW))


def _taps(buf, r0, offsets):
    by_phase = {}
    for k, off in enumerate(offsets):
        by_phase.setdefault(off % 8, []).append((k, off))
    for phase, items in sorted(by_phase.items()):
        span = max(off for _, off in items) - phase
        win = buf[pl.ds(r0 + phase, CONV_TILE + span), :]
        for k, off in items:
            yield k, win[off - phase:off - phase + CONV_TILE]


_CONV_FWD_OFFSETS = [CONV_PAD - (CONV_WIDTH - 1) + k for k in range(CONV_WIDTH)]
_CONV_BWD_OFFSETS = [(CONV_WIDTH - 1) - k for k in range(CONV_WIDTH)]


def _conv(pad, r0, cw_ref, cb):
    acc = jnp.zeros((CONV_TILE, GW), F32) + cb
    for k, rows in _taps(pad, r0, _CONV_FWD_OFFSETS):
        acc = acc + cw_ref[k:k + 1, :] * rows
    return acc


def _conv_ln(acc, lg, lb):
    mu = jnp.mean(acc, axis=-1, keepdims=True)
    xc = acc - mu
    rstd = lax.rsqrt(jnp.mean(xc * xc, axis=-1, keepdims=True) + LN_EPS)
    hhat = xc * rstd
    return hhat, rstd, hhat * lg + lb


def mixb_fwd(z3, cw, cb, lg, lb, pw, pwb):
    B, S, _ = z3.shape

    def body(a_ref, gt_ref, cw_ref, cb_ref, lg_ref, lb_ref, pw_ref, pwb_ref, y_ref, hc_ref, pad):
        pad[0:CONV_PAD, :] = jnp.zeros((CONV_PAD, GW), F32)
        pad[CONV_PAD:CONV_PAD + S, :] = a_ref[...] * _sigmoid(gt_ref[...])
        pwv = pw_ref[...].astype(BF16)
        for r0 in range(0, S, CONV_TILE):
            hc = _conv(pad, r0, cw_ref, cb_ref[...])
            hc_ref[r0:r0 + CONV_TILE, :] = hc
            _, _, ln = _conv_ln(hc, lg_ref[...], lb_ref[...])
            s = ln * _sigmoid(ln)
            y_ref[r0:r0 + CONV_TILE, :] = (_dot(s.astype(BF16), pwv) + pwb_ref[...]).astype(y_ref.dtype)

    seq_out = pl.BlockSpec((None, S, GW), lambda b: (b, 0, 0))
    return pl.pallas_call(
        body, grid=(B,),
        in_specs=[_seq(S, COL_A), _seq(S, COL_G), _full((CONV_WIDTH, GW)), _full((1, GW)), _full((1, GW)), _full((1, GW)),
                  _full((GW, GW)), _full((1, GW))],
        out_specs=[seq_out, seq_out], out_shape=[_sds((B, S, GW), BF16), _sds((B, S, GW), F32)],
        scratch_shapes=[pltpu.VMEM((S + CONV_PAD, GW), F32)],
        compiler_params=_cp(("parallel",)), name="mixb_fwd")(z3, z3, cw, cb, lg, lb, pw, pwb)


def mixb_bwd(z3, hc3, dycat3, cw, lg, lb, pw):
    B, S, _ = z3.shape

    def body(a_ref, gt_ref, hc_ref, dy_ref, cw_ref, lg_ref, lb_ref, pw_ref,
             da_ref, dgt_ref, dcw_ref, dcb_ref, dlg_ref, dlb_ref, dpw_ref, dpwb_ref, pad, dpad, dcw_acc):
        @pl.when(pl.program_id(0) == 0)
        def _():
            for r in (dcb_ref, dlg_ref, dlb_ref, dpw_ref, dpwb_ref, dcw_acc):
                r[...] = jnp.zeros_like(r)

        pad[0:CONV_PAD, :] = jnp.zeros((CONV_PAD, GW), F32)
        pad[CONV_PAD:CONV_PAD + S, :] = a_ref[...] * _sigmoid(gt_ref[...])
        dpad[S:S + CONV_PAD, :] = jnp.zeros((CONV_PAD, GW), F32)
        pwv = pw_ref[...].astype(BF16)
        lg = lg_ref[...]
        for r0 in range(0, S, CONV_TILE):
            hhat, rstd, ln = _conv_ln(hc_ref[r0:r0 + CONV_TILE, :], lg, lb_ref[...])
            sg = _sigmoid(ln)
            s = ln * sg
            dy = dy_ref[r0:r0 + CONV_TILE, :]
            dyb = dy.astype(BF16)
            dpw_ref[...] += _dot_tn(s.astype(BF16), dyb)
            dpwb_ref[...] += jnp.sum(dy, axis=0, keepdims=True)
            dln = _dot_nt(dyb, pwv) * (sg * (1.0 + ln * (1.0 - sg)))
            dlg_ref[...] += jnp.sum(dln * hhat, axis=0, keepdims=True)
            dlb_ref[...] += jnp.sum(dln, axis=0, keepdims=True)
            dhh = dln * lg
            dhc = rstd * (dhh - jnp.mean(dhh, axis=-1, keepdims=True) - hhat * jnp.mean(dhh * hhat, axis=-1, keepdims=True))
            dpad[r0:r0 + CONV_TILE, :] = dhc
            dcb_ref[...] += jnp.sum(dhc, axis=0, keepdims=True)
            for k, rows in _taps(pad, r0, _CONV_FWD_OFFSETS):
                dcw_acc[k] += (dhc * rows).reshape(CONV_TILE // 8, 8, GW).sum(axis=0)
        for r0 in range(0, S, CONV_TILE):
            dhg = jnp.zeros((CONV_TILE, GW), F32)
            for k, rows in _taps(dpad, r0, _CONV_BWD_OFFSETS):
                dhg = dhg + cw_ref[k:k + 1, :] * rows
            a = a_ref[r0:r0 + CONV_TILE, :]
            sg = _sigmoid(gt_ref[r0:r0 + CONV_TILE, :])
            da_ref[r0:r0 + CONV_TILE, :] = (dhg * sg).astype(da_ref.dtype)
            dgt_ref[r0:r0 + CONV_TILE, :] = (dhg * a * sg * (1.0 - sg)).astype(dgt_ref.dtype)

        @pl.when(pl.program_id(0) == B - 1)
        def _():
            for k in range(CONV_WIDTH):
                dcw_ref[k:k + 1, :] = jnp.sum(dcw_acc[k], axis=0, keepdims=True)

    seq_out = pl.BlockSpec((None, S, GW), lambda b: (b, 0, 0))
    return pl.pallas_call(
        body, grid=(B,),
        in_specs=[_seq(S, COL_A), _seq(S, COL_G), seq_out, pl.BlockSpec((None, S, GW), lambda b: (b, 0, 1)),
                  _full((CONV_WIDTH, GW)), _full((1, GW)), _full((1, GW)), _full((GW, GW))],
        out_specs=[seq_out, seq_out, _full((CONV_WIDTH, GW)), _full((1, GW)), _full((1, GW)), _full((1, GW)), _full((GW, GW)),
                   _full((1, GW))],
        out_shape=[_sds((B, S, GW), BF16), _sds((B, S, GW), BF16), _sds((CONV_WIDTH, GW), F32), _sds((1, GW), F32),
                   _sds((1, GW), F32), _sds((1, GW), F32), _sds((GW, GW), F32), _sds((1, GW), F32)],
        scratch_shapes=[pltpu.VMEM((S + CONV_PAD, GW), F32), pltpu.VMEM((S + CONV_PAD, GW), F32),
                        pltpu.VMEM((CONV_WIDTH, 8, GW), F32)],
        compiler_params=_cp(("arbitrary",)), name="mixb_bwd")(z3, z3, hc3, dycat3, cw, lg, lb, pw)


POOL_PAD = 16


def _pool_window():
    lane = lax.broadcasted_iota(jnp.int32, (1, GW), 1)
    return jnp.where(lane < 64, 2, jnp.where(lane < 128, 4, jnp.where(lane < 192, 8, 16)))


def _pool_sums(pad, r0, base, sign):
    win = _pool_window()
    acc = pad[pl.ds(r0 + base, CONV_TILE), :]
    out = None
    for i in range(1, 16):
        acc = acc + pad[pl.ds(r0 + base + sign * i, CONV_TILE), :]
        if i + 1 in (2, 4, 8, 16):
            out = acc if out is None else jnp.where(win == i + 1, acc, out)
    return out


def _pool_cnt(r0):
    t1 = r0 + 1 + lax.broadcasted_iota(jnp.int32, (CONV_TILE, 1), 0)
    return jnp.minimum(t1, _pool_window()).astype(F32)


def mixd_fwd(z3, wbd, scale):
    B, S, _ = z3.shape

    def body(x_ref, w_ref, sc_ref, y_ref, pad):
        pad[0:POOL_PAD, :] = jnp.zeros((POOL_PAD, GW), F32)
        pad[POOL_PAD:POOL_PAD + S, :] = x_ref[...]
        wv = w_ref[...].astype(BF16)
        for r0 in range(0, S, CONV_TILE):
            mean = _pool_sums(pad, r0, POOL_PAD, -1) / _pool_cnt(r0)
            p = (mean - x_ref[r0:r0 + CONV_TILE, :]).astype(BF16)
            y_ref[r0:r0 + CONV_TILE, :] = (_dot(p, wv) * sc_ref[...]).astype(y_ref.dtype)

    return pl.pallas_call(
        body, grid=(B,), in_specs=[_seq(S, COL_D), _full((GW, GW)), _full((1, GW))],
        out_specs=pl.BlockSpec((None, S, GW), lambda b: (b, 0, 0)), out_shape=_sds((B, S, GW), BF16),
        scratch_shapes=[pltpu.VMEM((S + POOL_PAD, GW), F32)],
        compiler_params=_cp(("parallel",)), name="mixd_fwd")(z3, wbd, scale)


def mixd_bwd(z3, dycat3, wbd, scale):
    B, S, _ = z3.shape

    def body(x_ref, dy_ref, w_ref, sc_ref, dx_ref, dw_ref, dsc_ref, pad, qpad):
        @pl.when(pl.program_id(0) == 0)
        def _():
            dw_ref[...] = jnp.zeros_like(dw_ref)
            dsc_ref[...] = jnp.zeros_like(dsc_ref)

        pad[0:POOL_PAD, :] = jnp.zeros((POOL_PAD, GW), F32)
        pad[POOL_PAD:POOL_PAD + S, :] = x_ref[...]
        qpad[S:S + POOL_PAD, :] = jnp.zeros((POOL_PAD, GW), F32)
        wv = w_ref[...].astype(BF16)
        for r0 in range(0, S, CONV_TILE):
            cnt = _pool_cnt(r0)
            mean = _pool_sums(pad, r0, POOL_PAD, -1) / cnt
            p = (mean - x_ref[r0:r0 + CONV_TILE, :]).astype(BF16)
            dy = dy_ref[r0:r0 + CONV_TILE, :]
            dsc_ref[...] += jnp.sum(dy * _dot(p, wv), axis=0, keepdims=True)
            dyp = (dy * sc_ref[...]).astype(BF16)
            dw_ref[...] += _dot_tn(p, dyp)
            dp = _dot_nt(dyp, wv)
            dx_ref[r0:r0 + CONV_TILE, :] = (-dp).astype(dx_ref.dtype)
            qpad[r0:r0 + CONV_TILE, :] = dp / cnt
        for r0 in range(0, S, CONV_TILE):
            back = _pool_sums(qpad, r0, 0, 1)
            dx_ref[r0:r0 + CONV_TILE, :] = (dx_ref[r0:r0 + CONV_TILE, :].astype(F32) + back).astype(dx_ref.dtype)

    return pl.pallas_call(
        body, grid=(B,),
        in_specs=[_seq(S, COL_D), pl.BlockSpec((None, S, GW), lambda b: (b, 0, 3)), _full((GW, GW)), _full((1, GW))],
        out_specs=[pl.BlockSpec((None, S, GW), lambda b: (b, 0, 0)), _full((GW, GW)), _full((1, GW))],
        out_shape=[_sds((B, S, GW), F32), _sds((GW, GW), F32), _sds((1, GW), F32)],
        scratch_shapes=[pltpu.VMEM((S + POOL_PAD, GW), F32), pltpu.VMEM((S + POOL_PAD, GW), F32)],
        compiler_params=_cp(("arbitrary",)), name="mixd_bwd")(z3, dycat3, wbd, scale)


def cmp_kv_fwd(tbk, tbv, pek, pev, w1k, w2k, w1v, w2v):
    B = tbk.shape[0]

    def body(tbk_ref, tbv_ref, pek_ref, pev_ref, w1k_ref, w2k_ref, w1v_ref, w2v_ref, kc_ref, vc_ref):
        for tb_ref, pe_ref, w1_ref, w2_ref, o_ref in ((tbk_ref, pek_ref, w1k_ref, w2k_ref, kc_ref),
                                                      (tbv_ref, pev_ref, w1v_ref, w2v_ref, vc_ref)):
            pre = _dot((tb_ref[...] + pe_ref[...]).astype(BF16), w1_ref[...].astype(BF16))
            hm = pre * _sigmoid(pre)
            o_ref[...] = _dot(hm.astype(BF16), w2_ref[...].astype(BF16))

    tb_spec = pl.BlockSpec((None, N_CMP, 2048), lambda b: (b, 0, 0))
    o_spec = pl.BlockSpec((None, N_CMP, HEAD_DIM), lambda b: (b, 0, 0))
    return pl.pallas_call(
        body, grid=(B,),
        in_specs=[tb_spec, tb_spec, _full((1, 2048)), _full((1, 2048)), _full((2048, HEAD_DIM)), _full((HEAD_DIM, HEAD_DIM)),
                  _full((2048, HEAD_DIM)), _full((HEAD_DIM, HEAD_DIM))],
        out_specs=[o_spec, o_spec], out_shape=[_sds((B, N_CMP, HEAD_DIM), F32)] * 2,
        compiler_params=_cp(("parallel",)), name="cmp_kv_fwd")(tbk, tbv, pek, pev, w1k, w2k, w1v, w2v)


def cmp_kv_bwd(tbk, tbv, pek, pev, w1k, w2k, w1v, w2v, dkc, dvc):
    B = tbk.shape[0]

    def body(tbk_ref, tbv_ref, pek_ref, pev_ref, w1k_ref, w2k_ref, w1v_ref, w2v_ref, dkc_ref, dvc_ref,
             dk2_ref, dv2_ref, dpek_ref, dpev_ref, dw1k_ref, dw2k_ref, dw1v_ref, dw2v_ref):
        @pl.when(pl.program_id(0) == 0)
        def _():
            for r in (dpek_ref, dpev_ref, dw1k_ref, dw2k_ref, dw1v_ref, dw2v_ref):
                r[...] = jnp.zeros_like(r)

        row0 = lax.broadcasted_iota(jnp.int32, (N_CMP, 1), 0) == 0
        for tb_ref, pe_ref, w1_ref, w2_ref, do_ref, d2_ref, dpe_ref, dw1_ref, dw2_ref in (
                (tbk_ref, pek_ref, w1k_ref, w2k_ref, dkc_ref, dk2_ref, dpek_ref, dw1k_ref, dw2k_ref),
                (tbv_ref, pev_ref, w1v_ref, w2v_ref, dvc_ref, dv2_ref, dpev_ref, dw1v_ref, dw2v_ref)):
            tb = (tb_ref[...] + pe_ref[...]).astype(BF16)
            w1 = w1_ref[...].astype(BF16)
            pre = _dot(tb, w1)
            sg = _sigmoid(pre)
            hm = (pre * sg).astype(BF16)
            do = do_ref[...].astype(BF16)
            dw2_ref[...] += _dot_tn(hm, do)
            dpre = (_dot_nt(do, w2_ref[...].astype(BF16)) * (sg * (1.0 + pre * (1.0 - sg)))).astype(BF16)
            dw1_ref[...] += _dot_tn(tb, dpre)
            dtb = _dot_nt(dpre, w1)
            dpe_ref[...] += jnp.sum(dtb, axis=0, keepdims=True)
            down = jnp.where(row0, 0.0, pltpu.roll(dtb[:, 1024:], 1, 0))
            d2_ref[...] = dtb[:, :1024] + down

    tb_spec = pl.BlockSpec((None, N_CMP, 2048), lambda b: (b, 0, 0))
    c_spec = pl.BlockSpec((None, N_CMP, HEAD_DIM), lambda b: (b, 0, 0))
    d2_spec = pl.BlockSpec((None, N_CMP, 1024), lambda b: (b, 0, 0))
    return pl.pallas_call(
        body, grid=(B,),
        in_specs=[tb_spec, tb_spec, _full((1, 2048)), _full((1, 2048)), _full((2048, HEAD_DIM)), _full((HEAD_DIM, HEAD_DIM)),
                  _full((2048, HEAD_DIM)), _full((HEAD_DIM, HEAD_DIM)), c_spec, c_spec],
        out_specs=[d2_spec, d2_spec, _full((1, 2048)), _full((1, 2048)), _full((2048, HEAD_DIM)), _full((HEAD_DIM, HEAD_DIM)),
                   _full((2048, HEAD_DIM)), _full((HEAD_DIM, HEAD_DIM))],
        out_shape=[_sds((B, N_CMP, 1024), F32)] * 2 + [_sds((1, 2048), F32)] * 2
        + [_sds((2048, HEAD_DIM), F32), _sds((HEAD_DIM, HEAD_DIM), F32)] * 2,
        compiler_params=_cp(("arbitrary",)), name="cmp_kv_bwd")(tbk, tbv, pek, pev, w1k, w2k, w1v, w2v, dkc, dvc)


def _qtile(col):
    return pl.BlockSpec((None, TQ, GW), lambda b, i: (b, i, col // GW))


def _qtile0():
    return pl.BlockSpec((None, TQ, GW), lambda b, i: (b, i, 0))


def _cmp_probs(q, kc, qpos):
    head = _lane_head()
    cend = lax.broadcasted_iota(jnp.int32, (1, N_CMP), 1) * CMP_STRIDE + 31
    cmask = cend <= qpos
    has = qpos >= 31
    out = []
    for h in range(4):
        qm = jnp.where(head == h, q, 0.0).astype(BF16)
        s = jnp.where(cmask, _dot_nt(qm, kc), NEG)
        e = jnp.exp(s - jnp.max(s, axis=-1, keepdims=True))
        p = jnp.where(has, e / jnp.sum(e, axis=-1, keepdims=True), 0.0)
        out.append((qm, p))
    return out


def cmp_attn_fwd(z3, kc4, vc4):
    B, S, _ = z3.shape

    def body(q_ref, kc_ref, vc_ref, o_ref, sel_ref):
        t0 = pl.program_id(1) * TQ
        qpos = t0 + lax.broadcasted_iota(jnp.int32, (TQ, 1), 0)
        head = _lane_head()
        kc, vc = kc_ref[...], vc_ref[...]
        o = jnp.zeros((TQ, GW), F32)
        psum = jnp.zeros((TQ, N_CMP), F32)
        for h, (_, p) in enumerate(_cmp_probs(q_ref[...] * 0.125, kc, qpos)):
            o = o + jnp.where(head == h, _dot(p.astype(BF16), vc), 0.0)
            psum = psum + p
        o_ref[...] = o
        cst = lax.broadcasted_iota(jnp.int32, (N_SLC, N_CMP), 1) * CMP_STRIDE
        jst = lax.broadcasted_iota(jnp.int32, (N_SLC, N_CMP), 0) * 64
        overlap = ((cst <= jst + 63) & (cst + 31 >= jst)).astype(BF16)
        imp = _dot_nt(overlap, psum.astype(BF16))
        qp = t0 + lax.broadcasted_iota(jnp.int32, (1, TQ), 1)
        jj = lax.broadcasted_iota(jnp.int32, (N_SLC, 1), 0)
        cur = lax.shift_right_logical(qp, SLC_BLOCK_SHIFT)
        forced = (jj == 0) | (jj == cur) | (jj == cur - 1)
        score = jnp.where(jj * 64 <= qp, imp + jnp.where(forced, FORCE_BONUS, 0.0), NEG)
        rank = jnp.zeros((N_SLC, TQ), F32)
        for j2 in range(N_SLC):
            sj = score[j2:j2 + 1, :]
            rank = rank + jnp.where((sj > score) | ((sj == score) & (j2 < jj)), 1.0, 0.0)
        sel_ref[...] = jnp.where((rank < SLC_TOPK) & (score > NEG / 2), 1.0, 0.0)

    c_spec = pl.BlockSpec((None, N_CMP, GW), lambda b, i: (b, 0, 0))
    return pl.pallas_call(
        body, grid=(B, S // TQ), in_specs=[_qtile(COL_Q), c_spec, c_spec],
        out_specs=[_qtile0(), pl.BlockSpec((None, N_SLC, TQ), lambda b, i: (b, 0, i))],
        out_shape=[_sds((B, S, GW), F32), _sds((B, N_SLC, S), F32)],
        compiler_params=_cp(("parallel", "parallel")), name="cmp_attn_fwd")(z3, kc4, vc4)


def cmp_attn_bwd(z3, kc4, vc4, do):
    B, S, _ = z3.shape
    nq = S // TQ

    def body(q_ref, kc_ref, vc_ref, do_ref, dq_ref, dkc_out, dvc_out, dkc_ref, dvc_ref):
        qi = pl.program_id(1)

        @pl.when(qi == 0)
        def _():
            dkc_ref[...] = jnp.zeros_like(dkc_ref)
            dvc_ref[...] = jnp.zeros_like(dvc_ref)

        qpos = qi * TQ + lax.broadcasted_iota(jnp.int32, (TQ, 1), 0)
        head = _lane_head()
        kc, vc, do = kc_ref[...], vc_ref[...], do_ref[...]
        dq = jnp.zeros((TQ, GW), F32)
        for h, (qm, p) in enumerate(_cmp_probs(q_ref[...] * 0.125, kc, qpos)):
            dom = jnp.where(head == h, do, 0.0).astype(BF16)
            dp = _dot_nt(dom, vc)
            ds = (p * (dp - jnp.sum(p * dp, axis=-1, keepdims=True))).astype(BF16)
            dq = dq + jnp.where(head == h, _dot(ds, kc), 0.0)
            dkc_ref[...] += _dot_tn(ds, qm)
            dvc_ref[...] += _dot_tn(p.astype(BF16), dom)
        dq_ref[...] = dq * 0.125

        @pl.when(qi == nq - 1)
        def _():
            dkc_out[...] = _fold_heads(dkc_ref[...])[:, :HEAD_DIM]
            dvc_out[...] = _fold_heads(dvc_ref[...])[:, :HEAD_DIM]

    c_spec = pl.BlockSpec((None, N_CMP, GW), lambda b, i: (b, 0, 0))
    d_spec = pl.BlockSpec((None, N_CMP, HEAD_DIM), lambda b, i: (b, 0, 0))
    return pl.pallas_call(
        body, grid=(B, nq), in_specs=[_qtile(COL_Q), c_spec, c_spec, _qtile0()],
        out_specs=[_qtile0(), d_spec, d_spec],
        out_shape=[_sds((B, S, GW), F32), _sds((B, N_CMP, HEAD_DIM), F32), _sds((B, N_CMP, HEAD_DIM), F32)],
        scratch_shapes=[pltpu.VMEM((N_CMP, GW), F32), pltpu.VMEM((N_CMP, GW), F32)],
        compiler_params=_cp(("parallel", "arbitrary")), name="cmp_attn_bwd")(z3, kc4, vc4, do)


def _attn_mask(mode, qpos, k0, sel_b):
    kpos = k0 + lax.broadcasted_iota(jnp.int32, (1, TQ), 1)
    mask = kpos <= qpos
    if mode == "win":
        return mask & (kpos > qpos - WIN)
    blk = lax.shift_right_logical(k0 + lax.broadcasted_iota(jnp.int32, (N_SLC, TQ), 1), SLC_BLOCK_SHIFT)
    expand = (blk == lax.broadcasted_iota(jnp.int32, (N_SLC, TQ), 0)).astype(BF16)
    return mask & (_dot_tn(sel_b, expand) > 0.5)


def _attn_lo(mode, qi):
    return jnp.maximum(qi - WIN // TQ, 0) if mode == "win" else 0


def attn_fwd(mode, z3, k4, v4, selT):
    B, S, _ = z3.shape

    def body(q_ref, k_ref, v_ref, sel_ref, o_ref, lse_ref, s_all, m_acc, l_acc, o_acc):
        qi = pl.program_id(1)
        qpos = qi * TQ + lax.broadcasted_iota(jnp.int32, (TQ, 1), 0)
        head = _lane_head()
        q = q_ref[...] * 0.125
        qm = [jnp.where(head == h, q, 0.0).astype(BF16) for h in range(4)]
        sel_b = sel_ref[...].astype(BF16)
        lo, hi = _attn_lo(mode, qi), qi + 1
        m_acc[...] = jnp.full(m_acc.shape, NEG, F32)

        def scores(kb, carry):
            k0 = pl.multiple_of(kb * TQ, TQ)
            kblk = k_ref[pl.ds(k0, TQ), :]
            mask = _attn_mask(mode, qpos, k0, sel_b)
            for h in range(4):
                s = jnp.where(mask, _dot_nt(qm[h], kblk), NEG)
                s_all[h, kb] = s
                m_acc[h] = jnp.maximum(m_acc[h], s)
            return carry

        lax.fori_loop(lo, hi, scores, 0)
        for h in range(4):
            m_acc[h] = jnp.broadcast_to(jnp.max(m_acc[h], axis=-1, keepdims=True), (TQ, TQ))
        l_acc[...] = jnp.zeros_like(l_acc)
        o_acc[...] = jnp.zeros_like(o_acc)

        def weights(kb, carry):
            vblk = v_ref[pl.ds(pl.multiple_of(kb * TQ, TQ), TQ), :]
            for h in range(4):
                p = jnp.exp(s_all[h, kb] - m_acc[h])
                l_acc[h] += p
                o_acc[h] += _dot(p.astype(BF16), vblk)
            return carry

        lax.fori_loop(lo, hi, weights, 0)
        o = jnp.zeros((TQ, GW), F32)
        lse = jnp.zeros((TQ, 128), F32)
        lane = lax.broadcasted_iota(jnp.int32, (1, 128), 1)
        for h in range(4):
            l = jnp.sum(l_acc[h], axis=-1, keepdims=True)
            o = o + jnp.where(head == h, o_acc[h] / l, 0.0)
            lse = jnp.where(lane == h, jnp.max(m_acc[h], axis=-1, keepdims=True) + jnp.log(l), lse)
        o_ref[...] = o
        lse_ref[...] = lse

    kv_spec = pl.BlockSpec((None, S, GW), lambda b, i: (b, 0, 0))
    return pl.pallas_call(
        body, grid=(B, S // TQ),
        in_specs=[_qtile(COL_Q), kv_spec, kv_spec, pl.BlockSpec((None, N_SLC, TQ), lambda b, i: (b, 0, i))],
        out_specs=[_qtile0(), pl.BlockSpec((None, TQ, 128), lambda b, i: (b, i, 0))],
        out_shape=[_sds((B, S, GW), F32), _sds((B, S, 128), F32)],
        scratch_shapes=[pltpu.VMEM((4, S // TQ, TQ, TQ), F32), pltpu.VMEM((4, TQ, TQ), F32), pltpu.VMEM((4, TQ, TQ), F32),
                        pltpu.VMEM((4, TQ, GW), F32)],
        compiler_params=_cp(("parallel", "parallel")), name=mode + "_attn_fwd")(z3, k4, v4, selT)


def attn_bwd(mode, z3, k4, v4, selT, o, lse, do):
    B, S, _ = z3.shape
    nq = S // TQ

    def body(q_ref, k_ref, v_ref, sel_ref, o_ref, lse_ref, do_ref, dq_ref, dk_out, dv_out, dq_s, dk_ref, dv_ref):
        qi = pl.program_id(1)

        @pl.when(qi == 0)
        def _():
            dk_ref[...] = jnp.zeros_like(dk_ref)
            dv_ref[...] = jnp.zeros_like(dv_ref)

        qpos = qi * TQ + lax.broadcasted_iota(jnp.int32, (TQ, 1), 0)
        head = _lane_head()
        lane = lax.broadcasted_iota(jnp.int32, (1, 128), 1)
        q = q_ref[...] * 0.125
        do = do_ref[...]
        doo = do * o_ref[...]
        lse = lse_ref[...]
        qm = [jnp.where(head == h, q, 0.0).astype(BF16) for h in range(4)]
        dom = [jnp.where(head == h, do, 0.0).astype(BF16) for h in range(4)]
        delta = [jnp.sum(jnp.where(head == h, doo, 0.0), axis=-1, keepdims=True) for h in range(4)]
        lse_h = [jnp.max(jnp.where(lane == h, lse, NEG), axis=-1, keepdims=True) for h in range(4)]
        sel_b = sel_ref[...].astype(BF16)
        dq_s[...] = jnp.zeros_like(dq_s)

        def step(kb, carry):
            k0 = pl.multiple_of(kb * TQ, TQ)
            kblk = k_ref[pl.ds(k0, TQ), :]
            vblk = v_ref[pl.ds(k0, TQ), :]
            mask = _attn_mask(mode, qpos, k0, sel_b)
            for h in range(4):
                s = _dot_nt(qm[h], kblk)
                p = jnp.where(mask, jnp.exp(s - lse_h[h]), 0.0)
                dp = _dot_nt(dom[h], vblk)
                ds = (p * (dp - delta[h])).astype(BF16)
                dq_s[...] += jnp.where(head == h, _dot(ds, kblk), 0.0)
                dk_ref[pl.ds(k0, TQ), :] += _dot_tn(ds, qm[h])
                dv_ref[pl.ds(k0, TQ), :] += _dot_tn(p.astype(BF16), dom[h])
            return carry

        lax.fori_loop(_attn_lo(mode, qi), qi + 1, step, 0)
        dq_ref[...] = dq_s[...] * 0.125

        @pl.when(qi == nq - 1)
        def _():
            for r0 in range(0, S, TQ):
                dk_out[r0:r0 + TQ, :] = _fold_heads(dk_ref[r0:r0 + TQ, :])[:, :HEAD_DIM]
                dv_out[r0:r0 + TQ, :] = _fold_heads(dv_ref[r0:r0 + TQ, :])[:, :HEAD_DIM]

    kv_spec = pl.BlockSpec((None, S, GW), lambda b, i: (b, 0, 0))
    return pl.pallas_call(
        body, grid=(B, nq),
        in_specs=[_qtile(COL_Q), kv_spec, kv_spec, pl.BlockSpec((None, N_SLC, TQ), lambda b, i: (b, 0, i)), _qtile0(),
                  pl.BlockSpec((None, TQ, 128), lambda b, i: (b, i, 0)), _qtile0()],
        out_specs=[_qtile0(), pl.BlockSpec((None, S, HEAD_DIM), lambda b, i: (b, 0, 0)),
                   pl.BlockSpec((None, S, HEAD_DIM), lambda b, i: (b, 0, 0))],
        out_shape=[_sds((B, S, GW), F32), _sds((B, S, HEAD_DIM), F32), _sds((B, S, HEAD_DIM), F32)],
        scratch_shapes=[pltpu.VMEM((TQ, GW), F32), pltpu.VMEM((S, GW), F32), pltpu.VMEM((S, GW), F32)],
        compiler_params=_cp(("parallel", "arbitrary")), name=mode + "_attn_bwd")(z3, k4, v4, selT, o, lse, do)


def _gate_expand(g, b):
    head = _lane_head()
    out = jnp.zeros((TQ, GW), F32)
    for h in range(4):
        out = jnp.where(head == h, g[:, 3 * h + b:3 * h + b + 1], out)
    return out


def combine_fwd(z3, o_cmp, o_slc, o_win):
    B, S, _ = z3.shape

    def body(gl_ref, oc_ref, os_ref, ow_ref, y_ref):
        g = _sigmoid(gl_ref[...])
        y = jnp.zeros((TQ, GW), F32)
        for b, o_ref in enumerate((oc_ref, os_ref, ow_ref)):
            y = y + _gate_expand(g, b) * o_ref[...]
        y_ref[...] = y.astype(y_ref.dtype)

    return pl.pallas_call(
        body, grid=(B, S // TQ),
        in_specs=[pl.BlockSpec((None, TQ, 128), lambda b, i: (b, i, COL_GL // 128)), _qtile0(), _qtile0(), _qtile0()],
        out_specs=_qtile0(), out_shape=_sds((B, S, GW), BF16),
        compiler_params=_cp(("parallel", "parallel")), name="combine_fwd")(z3, o_cmp, o_slc, o_win)


def combine_bwd(z3, o_cmp, o_slc, o_win, dycat3):
    B, S, _ = z3.shape

    def body(gl_ref, oc_ref, os_ref, ow_ref, dy_ref, dc_ref, ds_ref, dw_ref, dgl_ref):
        g = _sigmoid(gl_ref[...])
        dy = dy_ref[...]
        head = _lane_head()
        lane = lax.broadcasted_iota(jnp.int32, (1, 128), 1)
        dg = jnp.zeros((TQ, 128), F32)
        for b, (o_ref, d_ref) in enumerate(((oc_ref, dc_ref), (os_ref, ds_ref), (ow_ref, dw_ref))):
            d_ref[...] = _gate_expand(g, b) * dy
            t = dy * o_ref[...]
            for h in range(4):
                dg = jnp.where(lane == 3 * h + b, jnp.sum(jnp.where(head == h, t, 0.0), axis=-1, keepdims=True), dg)
        dgl_ref[...] = dg * g * (1.0 - g)

    gl_spec = pl.BlockSpec((None, TQ, 128), lambda b, i: (b, i, COL_GL // 128))
    return pl.pallas_call(
        body, grid=(B, S // TQ),
        in_specs=[gl_spec, _qtile0(), _qtile0(), _qtile0(), pl.BlockSpec((None, TQ, GW), lambda b, i: (b, i, 2))],
        out_specs=[_qtile0(), _qtile0(), _qtile0(), pl.BlockSpec((None, TQ, 128), lambda b, i: (b, i, 0))],
        out_shape=[_sds((B, S, GW), F32)] * 3 + [_sds((B, S, 128), F32)],
        compiler_params=_cp(("parallel", "parallel")), name="combine_bwd")(z3, o_cmp, o_slc, o_win, dycat3)


def assemble_dz(du, dv, da, dgt, dq_c, dq_s, dq_w, dd, dkvs, dgl):
    T = du.shape[0]

    def body(du_ref, dv_ref, da_ref, dgt_ref, dqc_ref, dqs_ref, dqw_ref, dd_ref, dgl_ref, *rest):
        kv_refs, o_ref = rest[:6], rest[6]
        o_ref[:, COL_U:COL_U + GW] = du_ref[...]
        o_ref[:, COL_V:COL_V + GW] = dv_ref[...]
        o_ref[:, COL_A:COL_A + GW] = da_ref[...]
        o_ref[:, COL_G:COL_G + GW] = dgt_ref[...]
        o_ref[:, COL_Q:COL_Q + GW] = (dqc_ref[...] + dqs_ref[...] + dqw_ref[...]).astype(BF16)
        o_ref[:, COL_D:COL_D + GW] = dd_ref[...].astype(BF16)
        for i, kv_ref in enumerate(kv_refs):
            o_ref[:, COL_KV + i * HEAD_DIM:COL_KV + (i + 1) * HEAD_DIM] = kv_ref[...].astype(BF16)
        o_ref[:, COL_GL:COL_GL + 128] = dgl_ref[...].astype(BF16)

    specs = [_rows(ROW_TILE, GW)] * 8 + [_rows(ROW_TILE, 128)] + [_rows(ROW_TILE, HEAD_DIM)] * 6
    return pl.pallas_call(body, grid=(T // ROW_TILE,), in_specs=specs, out_specs=_rows(ROW_TILE, ZW),
                          out_shape=_sds((T, ZW), BF16), compiler_params=_cp(("parallel",)),
                          name="assemble_dz")(du, dv, da, dgt, dq_c, dq_s, dq_w, dd, dgl, *dkvs)


def _my_pos():
    return lax.axis_index("x"), lax.axis_index("y"), lax.axis_index("c")


def _peer(k):
    x, y, c = _my_pos()
    return ((1 - x) if k & 4 else x, (1 - y) if k & 2 else y, (1 - c) if k & 1 else c)


def _index(pos):
    return 4 * pos[0] + 2 * pos[1] + pos[2]


_HBM = pl.BlockSpec(memory_space=pltpu.HBM)


_SEM = pl.BlockSpec(memory_space=pltpu.SEMAPHORE)
_EFFECT = pltpu.SideEffectType.DATAFLOW_SIDE_EFFECTING


def _exchange_copies(kinds, srcs, lands, send, recv):
    me = _index(_my_pos())
    out = []
    for a, kind in enumerate(kinds):
        for k in range(1, N_DEV):
            peer = _peer(k)
            if kind == "gather":
                r = srcs[a].shape[1]
                src, dst = srcs[a], lands[a].at[:, pl.ds(me * r, r), :]
            else:
                r = srcs[a].shape[1] // N_DEV
                src, dst = srcs[a].at[:, pl.ds(_index(peer) * r, r), :], lands[a].at[me]
            sem = a * (N_DEV - 1) + k - 1
            out.append(pltpu.make_async_remote_copy(src_ref=src, dst_ref=dst, send_sem=send.at[sem], recv_sem=recv.at[sem],
                                                    device_id=peer, device_id_type=MESH))
    return out


def _land_with_own(kind, src):
    me = _index(_my_pos())
    if kind == "gather":
        _, r, C = src.shape
        return lax.dynamic_update_slice(lax.empty((1, N_DEV * r, C), src.dtype), src, (0, me * r, 0))
    _, r8, C = src.shape
    r = r8 // N_DEV
    own = lax.dynamic_slice(src, (0, me * r, 0), (1, r, C))
    return lax.dynamic_update_slice(lax.empty((N_DEV, 1, r, C), src.dtype), own[None], (me, 0, 0, 0))


def exchange_start(kinds, srcs, name):
    n = len(srcs)
    lands = [_land_with_own(k, s) for k, s in zip(kinds, srcs)]

    def body(*refs):
        s, l = refs[:n], refs[n:2 * n]
        send, recv = refs[2 * n], refs[2 * n + 1]
        for cp in _exchange_copies(kinds, s, l, send, recv):
            cp.start()
        refs[-1][...] = jnp.zeros((8, 128), F32)

    hbm = [pltpu.HBM(a.shape, a.dtype) for a in srcs + lands]
    outs = pl.pallas_call(
        body, name=name,
        out_shape=(pltpu.SemaphoreType.DMA((n * (N_DEV - 1),)), pltpu.SemaphoreType.DMA((n * (N_DEV - 1),)), *hbm,
                   _sds((8, 128), F32)),
        in_specs=[_HBM] * (2 * n), out_specs=(_SEM, _SEM, *([_HBM] * (2 * n)), pl.BlockSpec(memory_space=pltpu.VMEM)),
        input_output_aliases={i: 2 + i for i in range(2 * n)},
        compiler_params=pltpu.CompilerParams(has_side_effects=_EFFECT),
    )(*[pltpu.with_memory_space_constraint(a, pltpu.HBM) for a in srcs + lands])
    return outs[0], outs[1], list(outs[2:2 + n]), list(outs[2 + n:2 + 2 * n]), outs[-1]


def exchange_wait(kinds, started, after, name):
    send, recv, srcs, lands, _ = started
    n = len(srcs)
    after = list(after) if isinstance(after, (list, tuple)) else [after]

    def body(*refs):
        s, l = refs[:n], refs[n:2 * n]
        for cp in _exchange_copies(kinds, s, l, refs[2 * n], refs[2 * n + 1]):
            cp.wait_send()
            cp.wait_recv()
        refs[-1][...] = jnp.zeros((8, 128), F32)

    outs = pl.pallas_call(
        body, name=name, out_shape=[pltpu.HBM(a.shape, a.dtype) for a in srcs + lands] + [_sds((8, 128), F32)],
        in_specs=[_HBM] * (2 * n) + [_SEM, _SEM] + [pl.BlockSpec(memory_space=pl.ANY)] * len(after),
        out_specs=[_HBM] * (2 * n) + [pl.BlockSpec(memory_space=pltpu.VMEM)],
        input_output_aliases={i: i for i in range(2 * n)},
        compiler_params=pltpu.CompilerParams(has_side_effects=_EFFECT),
    )(*srcs, *lands, send, recv, *after)
    return list(outs[n:2 * n]), outs[-1]


def sum_slots(lands, name):
    L = len(lands)
    _, _, r, C = lands[0].shape
    tr = _tile(r, 256, 16)

    def body(*refs):
        o_ref = refs[L]
        for l in range(L):
            @pl.when(pl.program_id(0) == l)
            def _(x_ref=refs[l]):
                acc = x_ref[0].astype(F32)
                for s in range(1, N_DEV):
                    acc = acc + x_ref[s].astype(F32)
                o_ref[...] = acc

    specs = [pl.BlockSpec((N_DEV, None, tr, C), lambda g, i, l=l: (0, 0, jnp.where(g == l, i, 0), 0)) for l in range(L)]
    return pl.pallas_call(
        body, grid=(L, r // tr), in_specs=specs,
        out_specs=pl.BlockSpec((None, tr, C), lambda g, i: (g, i, 0)), out_shape=_sds((L, r, C), F32),
        compiler_params=_cp(("arbitrary", "arbitrary")), name=name)(*lands)


def pack_flat(arrs):
    flat = jnp.concatenate([a.reshape(-1).astype(F32) for a in arrs])
    n = flat.shape[0]
    total = -(-n // 32768) * 32768
    return jnp.pad(flat, (0, total - n)).reshape(total // 128, 128)


def unpack_flat(flat, shapes):
    v = flat.reshape(-1)
    out, off = [], 0
    for s in shapes:
        n = int(np.prod(s))
        out.append(v[off:off + n].reshape(s))
        off += n
    return out


def adamw(w, g, m, v, name):
    shape = w.shape
    C = shape[-1]
    R = int(np.prod(shape)) // C
    tr = _tile(R, 128, 8)
    c1 = 1.0 - ADAM_B1 ** ADAM_STEP
    c2 = 1.0 - ADAM_B2 ** ADAM_STEP

    def body(w_ref, g_ref, m_ref, v_ref, d_ref, nm_ref, nv_ref):
        g = g_ref[...]
        m2 = ADAM_B1 * m_ref[...] + (1.0 - ADAM_B1) * g
        v2 = ADAM_B2 * v_ref[...] + (1.0 - ADAM_B2) * (g * g)
        nm_ref[...] = m2
        nv_ref[...] = v2
        d_ref[...] = -ADAM_LR * ((m2 / c1) / (jnp.sqrt(v2 / c2) + ADAM_EPS) + ADAM_WD * w_ref[...])

    spec = pl.BlockSpec((tr, C), lambda i: (i, 0))
    outs = pl.pallas_call(body, grid=(R // tr,), in_specs=[spec] * 4, out_specs=[spec] * 3,
                          out_shape=[_sds((R, C), F32)] * 3, compiler_params=_cp(("parallel",)), name=name)(
        w.reshape(R, C), g.reshape(R, C), m.reshape(R, C), v.reshape(R, C))
    return [o.reshape(shape) for o in outs]


def _bexp(sg_b):
    return jnp.repeat(sg_b.T, HEAD_DIM, axis=1)


def _block_diag(pool_w):
    out = jnp.zeros((GW, GW), F32)
    for i in range(4):
        out = out.at[i * 64:(i + 1) * 64, i * 64:(i + 1) * 64].set(pool_w[i])
    return out


def _cmp_rows(t):
    B, S, _ = t.shape
    t2 = t.reshape(B, S // CMP_STRIDE, CMP_STRIDE * HEAD_DIM)
    nxt = jnp.concatenate([t2[:, 1:], jnp.zeros_like(t2[:, :1])], axis=1)
    return jnp.concatenate([t2, nxt], axis=-1)


def _tile4(t):
    return jnp.tile(t, (1, 1, 4)).astype(BF16)


def kv_tiles(z):
    T = z.shape[0]

    def body(x_ref, cv_ref, ks_ref, vs_ref, kw_ref, vw_ref):
        x = x_ref[...]
        cv_ref[...] = x[:, :128]
        xb = x.astype(BF16)
        src = lax.broadcasted_iota(jnp.int32, (384, GW), 0)
        lane = lax.broadcasted_iota(jnp.int32, (384, GW), 1) & 63
        for i, o_ref in enumerate((ks_ref, vs_ref, kw_ref, vw_ref)):
            expand = (src == lane + 64 * (i + 2)).astype(BF16)
            o_ref[...] = _dot(xb, expand).astype(o_ref.dtype)

    return pl.pallas_call(
        body, grid=(T // ROW_TILE,), in_specs=[pl.BlockSpec((ROW_TILE, 384), lambda i: (i, COL_KV // 384))],
        out_specs=[_rows(ROW_TILE, 128)] + [_rows(ROW_TILE, GW)] * 4,
        out_shape=[_sds((T, 128), F32)] + [_sds((T, GW), BF16)] * 4,
        compiler_params=_cp(("parallel",)), name="kv_tiles")(z)


def layer_fwd(x, p, late, B, S):
    T = B * S
    sv = {"x0": x}
    h1, h1t = rms_fwd(x, p["g_pre_mix"], "rms_pre_mix")
    z = mm(h1, p["w_in"], name="mm_in")
    z3 = z.reshape(B, S, ZW)
    ya = mixa_fwd(z, p["sg_ln_g"], p["sg_w"], p["bexp"])
    yb, hc = mixb_fwd(z3, p["cv_w"], p["cv_b"], p["cv_ln_g"], p["cv_ln_b"], p["cv_pw"], p["cv_pw_b"])
    kcv, ks4, vs4, kw4, vw4 = [a.reshape(B, S, -1) for a in kv_tiles(z)]
    tbk, tbv = _cmp_rows(kcv[:, :, :HEAD_DIM]), _cmp_rows(kcv[:, :, HEAD_DIM:])
    kc, vc = cmp_kv_fwd(tbk, tbv, p["cmp_pos_k"], p["cmp_pos_v"], p["cmp_w1_k"], p["cmp_w2_k"], p["cmp_w1_v"], p["cmp_w2_v"])
    kc4, vc4 = _tile4(kc), _tile4(vc)
    o_cmp, selT = cmp_attn_fwd(z3, kc4, vc4)
    o_slc, lse_slc = attn_fwd("slc", z3, ks4, vs4, selT)
    o_win, lse_win = attn_fwd("win", z3, kw4, vw4, selT)
    yc = combine_fwd(z3, o_cmp, o_slc, o_win)
    yd = mixd_fwd(z3, p["pool_bd"], p["pool_scale"])
    ycat = jnp.concatenate([ya, yb.reshape(T, GW), yc.reshape(T, GW), yd.reshape(T, GW)], axis=-1)
    p.update(late(ycat))
    mix = mm(ycat, p["w_out"], name="mm_out")
    x1 = rms_post_fwd(x, mix, p["g_post_mix"], "rms_post_mix")
    h2, h2t = rms_fwd(x1, p["g_pre_ffn"], "rms_pre_ffn")
    gu4, a3 = ffn_up_fwd(h2, p["w_gu"])
    f = mm(a3, p["w_down"], blk="k", name="mm_down")
    x2 = rms_post_fwd(x1, f, p["g_post_ffn"], "rms_post_ffn")
    sv.update(h1t=h1t, z=z, hc=hc, tbk=tbk, tbv=tbv, kc4=kc4, vc4=vc4, ks4=ks4, vs4=vs4, kw4=kw4, vw4=vw4, o_cmp=o_cmp, selT=selT,
              o_slc=o_slc, lse_slc=lse_slc, o_win=o_win, lse_win=lse_win, ycat=ycat, mix=mix, x1=x1, h2t=h2t, gu4=gu4, a3=a3, f=f)
    return x2, sv


def layer_bwd_ffn(dx2, p, sv, B, S):
    T = B * S
    gb, gs = {}, {}
    df, gs["g_post_ffn"] = rms_bwd(sv["f"], p["g_post_ffn"], dx2, None, BF16, "rms_post_ffn_bwd")
    dgu = ffn_down_dx(df, p["w_down"], sv["gu4"]).reshape(N_DEV, T, FFN_BLK)
    gb["w_down"] = mm(sv["a3"], df, ta=True, blk="m", out_dtype=BF16, name="mm_down_dw")
    dh2 = mm(dgu, p["w_gu"], tb=True, blk="k", name="mm_gu_dx", tm=2048)
    gb["w_gu"] = mm(sv["h2t"], dgu, blk="n", out_dtype=BF16, name="mm_gu_dw")
    dx1, gs["g_pre_ffn"] = rms_bwd(sv["x1"], p["g_pre_ffn"], dh2, dx2, F32, "rms_pre_ffn_bwd")
    gb["w_gu"] = gb["w_gu"].reshape(1, N_DEV * D_MODEL, FFN_BLK)
    gb["w_down"] = gb["w_down"].reshape(1, FFN_HIDDEN, D_MODEL)
    return dx1, gb, gs


def layer_bwd_mix(dx1, p, sv, B, S):
    T = B * S
    gb, gs = {}, {}
    dmix, gs["g_post_mix"] = rms_bwd(sv["mix"], p["g_post_mix"], dx1, None, BF16, "rms_post_mix_bwd")
    dycat = mm(dmix, p["w_out"], tb=True, name="mm_out_dx")
    gb["w_out"] = mm(sv["ycat"], dmix, ta=True, out_dtype=BF16, name="mm_out_dw")
    dycat3 = dycat.reshape(B, S, D_MODEL)
    z = sv["z"]
    z3 = z.reshape(B, S, ZW)
    du, dv, gs["sg_w"], db, gs["sg_ln_g"] = mixa_bwd(z, dycat, p["sg_ln_g"], p["sg_w"], p["bexp"])
    gs["sg_b"] = db[:, :4].T
    (da, dgt, gs["cv_w"], gs["cv_b"], gs["cv_ln_g"], gs["cv_ln_b"], gpw, gs["cv_pw_b"]) = mixb_bwd(
        z3, sv["hc"], dycat3, p["cv_w"], p["cv_ln_g"], p["cv_ln_b"], p["cv_pw"])
    gb["cv_pw"] = gpw.astype(BF16)
    dd, dwbd, gs["pool_scale"] = mixd_bwd(z3, dycat3, p["pool_bd"], p["pool_scale"])
    gs["pool_w"] = jnp.stack([dwbd[i * 64:(i + 1) * 64, i * 64:(i + 1) * 64] for i in range(4)])
    do_c, do_s, do_w, dgl = combine_bwd(z3, sv["o_cmp"], sv["o_slc"], sv["o_win"], dycat3)
    dq_s, dks, dvs = attn_bwd("slc", z3, sv["ks4"], sv["vs4"], sv["selT"], sv["o_slc"], sv["lse_slc"], do_s)
    dq_w, dkw, dvw = attn_bwd("win", z3, sv["kw4"], sv["vw4"], sv["selT"], sv["o_win"], sv["lse_win"], do_w)
    dq_c, dkc, dvc = cmp_attn_bwd(z3, sv["kc4"], sv["vc4"], do_c)
    (dk2, dv2, gs["cmp_pos_k"], gs["cmp_pos_v"], gw1k, gs["cmp_w2_k"], gw1v, gs["cmp_w2_v"]) = cmp_kv_bwd(
        sv["tbk"], sv["tbv"], p["cmp_pos_k"], p["cmp_pos_v"], p["cmp_w1_k"], p["cmp_w2_k"], p["cmp_w1_v"], p["cmp_w2_v"],
        dkc, dvc)
    gb["cmp_w1_k"], gb["cmp_w1_v"] = gw1k.astype(BF16), gw1v.astype(BF16)
    dkvs = [t.reshape(T, HEAD_DIM) for t in (dk2, dv2, dks, dvs, dkw, dvw)]
    dz = assemble_dz(du, dv, da.reshape(T, GW), dgt.reshape(T, GW), dq_c.reshape(T, GW), dq_s.reshape(T, GW),
                     dq_w.reshape(T, GW), dd.reshape(T, GW), dkvs, dgl.reshape(T, 128))
    dh1 = mm(dz, p["w_in"], tb=True, name="mm_in_dx")
    gb["w_in"] = mm(sv["h1t"], dz, out_dtype=BF16, name="mm_in_dw")
    dx0, gs["g_pre_mix"] = rms_bwd(sv["x0"], p["g_pre_mix"], dh1, dx1, F32, "rms_pre_mix_bwd")
    return dx0, gb, gs


SMALL = ["g_pre_mix", "g_post_mix", "g_pre_ffn", "g_post_ffn", "sg_ln_g", "sg_w", "sg_b", "cv_w", "cv_b", "cv_ln_g", "cv_ln_b",
         "cv_pw_b", "cmp_pos_k", "cmp_pos_v", "cmp_w2_k", "cmp_w2_v", "pool_w", "pool_scale"]
BIG = ["w_in", "w_out", "w_gu", "w_down", "cmp_w1_k", "cmp_w1_v", "cv_pw"]
NAMES = ["g_pre_mix", "g_post_mix", "g_pre_ffn", "g_post_ffn", "w_in", "sg_ln_g", "sg_w", "sg_b", "cv_w", "cv_b", "cv_ln_g",
         "cv_ln_b", "cv_pw", "cv_pw_b", "cmp_pos_k", "cmp_pos_v", "cmp_w1_k", "cmp_w2_k", "cmp_w1_v", "cmp_w2_v", "pool_w",
         "pool_scale", "w_out", "ffn_w_gu", "ffn_w_down"]


def kernel(x, g_pre_mix, g_post_mix, g_pre_ffn, g_post_ffn, w_in, sg_ln_g, sg_w, sg_b, cv_w, cv_b, cv_ln_g, cv_ln_b, cv_pw, cv_pw_b, cmp_pos_k, cmp_pos_v, cmp_w1_k, cmp_w2_k, cmp_w1_v, cmp_w2_v, pool_w, pool_scale, w_out, ffn_w_gu, ffn_w_down, loss_target, m_g_pre_mix, m_g_post_mix, m_g_pre_ffn, m_g_post_ffn, m_w_in, m_sg_ln_g, m_sg_w, m_sg_b, m_cv_w, m_cv_b, m_cv_ln_g, m_cv_ln_b, m_cv_pw, m_cv_pw_b, m_cmp_pos_k, m_cmp_pos_v, m_cmp_w1_k, m_cmp_w2_k, m_cmp_w1_v, m_cmp_w2_v, m_pool_w, m_pool_scale, m_w_out, m_ffn_w_gu, m_ffn_w_down, v_g_pre_mix, v_g_post_mix, v_g_pre_ffn, v_g_post_ffn, v_w_in, v_sg_ln_g, v_sg_w, v_sg_b, v_cv_w, v_cv_b, v_cv_ln_g, v_cv_ln_b, v_cv_pw, v_cv_pw_b, v_cmp_pos_k, v_cmp_pos_v, v_cmp_w1_k, v_cmp_w2_k, v_cmp_w1_v, v_cmp_w2_v, v_pool_w, v_pool_scale, v_w_out, v_ffn_w_gu, v_ffn_w_down):
    args = dict(locals())
    W = {n: args[n] for n in NAMES}
    M = {n: args["m_" + n] for n in NAMES}
    V = {n: args["v_" + n] for n in NAMES}
    B, S, _ = x.shape
    T = B * S
    L = w_in.shape[0]
    me = _index(_my_pos())
    cpd = GW // N_DEV

    shards = {"w_in": lambda l: pack_cols(w_in[l]).astype(BF16), "w_out": lambda l: w_out[l].astype(BF16),
              "w_gu": lambda l: ffn_w_gu[l].astype(BF16), "w_down": lambda l: ffn_w_down[l].astype(BF16),
              "cmp_w1_k": lambda l: cmp_w1_k[l].astype(BF16), "cmp_w1_v": lambda l: cmp_w1_v[l].astype(BF16),
              "cv_pw": lambda l: cv_pw[l].astype(BF16), "cv_w": lambda l: cv_w[l].T}
    early, later = ["w_in", "cmp_w1_k", "cmp_w1_v", "cv_pw", "cv_w"], ["w_out", "w_gu", "w_down"]

    def start_gather(names, l, tag, srcs=None, behind=None):
        srcs = list(srcs) if srcs is not None else [shards[n](l)[None] for n in names]
        if behind is not None:
            srcs[0] = srcs[0] + behind[0, 0].astype(srcs[0].dtype)
        return exchange_start(["gather"] * len(names), srcs, "gather_%s_start_%d" % (tag, l))

    def wait_gather(names, started, after, l, tag):
        arrived, done = exchange_wait(["gather"] * len(names), started, after, "gather_%s_wait_%d" % (tag, l))
        full = {n: a[0] for n, a in zip(names, arrived)}
        full["done"] = done
        if "w_gu" in full:
            full["w_gu"] = full["w_gu"].reshape(N_DEV, D_MODEL, FFN_BLK)
            full["w_down"] = full["w_down"].reshape(4, FFN_BLK, D_MODEL)
        if "cv_w" in full:
            full["cv_w"] = full["cv_w"].T
        return full

    def layer_params(l, full):
        p = dict(full)
        for n in ("g_pre_mix", "g_post_mix", "g_pre_ffn", "g_post_ffn", "sg_ln_g", "cv_b", "cv_ln_g", "cv_ln_b", "cv_pw_b",
                  "pool_scale"):
            p[n] = W[n][l][None, :]
        p["sg_w"] = sg_w[l]
        p["bexp"] = _bexp(sg_b[l])
        p["cmp_pos_k"] = cmp_pos_k[l].reshape(1, 2048)
        p["cmp_pos_v"] = cmp_pos_v[l].reshape(1, 2048)
        p["cmp_w2_k"], p["cmp_w2_v"] = cmp_w2_k[l], cmp_w2_v[l]
        p["pool_bd"] = _block_diag(pool_w[l])
        return p

    xs = x.reshape(T, D_MODEL)
    params, saved = [], []
    early_st = start_gather(early, 0, "early")
    ahead = {}
    for l in range(L):
        later_srcs = [shards[n](l)[None] for n in later]
        full = wait_gather(early, early_st, later_srcs + ([xs] if l > 0 else []), l, "early")
        later_st = start_gather(later, l, "later", srcs=later_srcs, behind=full["done"])
        p = layer_params(l, full)
        p["g_pre_mix"] = p["g_pre_mix"] + later_st[4][0, 0]

        def late(after, l=l, st=later_st, p=p):
            got = wait_gather(later, st, after, l, "later")
            if l + 1 < L:
                ahead["early"] = start_gather(early, l + 1, "early", behind=got["done"])
                got["g_post_mix"] = p["g_post_mix"] + ahead["early"][4][0, 0]
            return got

        xs, sv = layer_fwd(xs, p, late, B, S)
        params.append(p)
        saved.append(sv)
        early_st = ahead.get("early")
    dy, lpart = loss_fwd_bwd(xs, loss_target.reshape(T, D_MODEL))
    loss = lax.psum(lpart[0, 0], ("x", "y", "c"))

    ffn_big, mix_big = ["w_gu", "w_down"], ["w_in", "w_out", "cmp_w1_k", "cmp_w1_v", "cv_pw"]
    ffn_kinds, mix_kinds = ["scatter"] * len(ffn_big), ["scatter"] * len(mix_big) + ["gather"]
    pending, token = [], None
    for l in reversed(range(L)):
        p = dict(params[l])
        if token is not None:
            p["g_post_ffn"] = p["g_post_ffn"] + token[0, 0]
        dy, gb_ffn, gs = layer_bwd_ffn(dy, p, saved[l], B, S)
        st_ffn = exchange_start(ffn_kinds, [gb_ffn[n] for n in ffn_big], "scatter_ffn_start_%d" % l)
        p["g_post_mix"] = p["g_post_mix"] + st_ffn[4][0, 0]
        dy, gb_mix, gs_mix = layer_bwd_mix(dy, p, saved[l], B, S)
        gs.update(gs_mix)
        small_shapes = [tuple(gs[n].shape) for n in SMALL]
        st_mix = exchange_start(mix_kinds, [gb_mix[n][None] for n in mix_big] + [pack_flat([gs[n] for n in SMALL])[None]],
                                "scatter_mix_start_%d" % l)
        token = st_mix[4]
        pending.append((l, st_ffn, st_mix))
    grad_x = dy.reshape(B, S, D_MODEL)

    delta, new_m, new_v = {}, {}, {}
    lands = {l: {} for l in range(L)}
    for l, st_ffn, _ in pending:
        lands[l].update(zip(ffn_big, exchange_wait(ffn_kinds, st_ffn, token, "scatter_ffn_wait_%d" % l)[0]))
    grads = {}
    for n, name in zip(ffn_big, ("ffn_w_gu", "ffn_w_down")):
        grads[name] = sum_slots([lands[l][n] for l in range(L)], "sum_" + n)
        delta[name], new_m[name], new_v[name] = adamw(W[name], grads[name], M[name], V[name], "adamw_" + name)
    for l, _, st_mix in pending:
        lands[l].update(zip(mix_big + ["small"],
                            exchange_wait(mix_kinds, st_mix, delta["ffn_w_down"], "scatter_mix_wait_%d" % l)[0]))
    for n in mix_big:
        grads[n] = sum_slots([lands[l][n] for l in range(L)], "sum_" + n)
    grads["w_in"] = unpack_cols(grads["w_in"])
    rows = lands[0]["small"].shape[1] // N_DEV
    reduced = sum_slots([lands[l]["small"].reshape(N_DEV, 1, rows, 128) for l in range(L)], "sum_small")
    per_layer = [unpack_flat(reduced[l], small_shapes) for l in range(L)]
    for i, n in enumerate(SMALL):
        g = jnp.stack([per_layer[l][i] for l in range(L)])
        grads[n] = g.reshape(W[n].shape) if n != "cv_w" else g
    grads["cv_w"] = lax.dynamic_slice(grads["cv_w"], (0, 0, me * cpd), (L, CONV_WIDTH, cpd))

    for n in mix_big:
        delta[n], new_m[n], new_v[n] = adamw(W[n], grads[n], M[n], V[n], "adamw_" + n)
    shapes = [W[n].shape for n in SMALL]
    packed = adamw(pack_flat([W[n] for n in SMALL]), pack_flat([grads[n] for n in SMALL]),
                   pack_flat([M[n] for n in SMALL]), pack_flat([V[n] for n in SMALL]), "adamw_small")
    for out, flat in zip((delta, new_m, new_v), packed):
        for n, a in zip(SMALL, unpack_flat(flat, shapes)):
            out[n] = a

    return (loss, grad_x, *[grads[n] for n in NAMES], *[delta[n] for n in NAMES], *[new_m[n] for n in NAMES],
            *[new_v[n] for n in NAMES])
```

```python
import numpy as np
import jax
import jax.numpy as jnp
from jax import lax
from jax.experimental import pallas as pl
from jax.experimental.pallas import tpu as pltpu

F32 = jnp.float32
BF16 = jnp.bfloat16
HI = lax.Precision.HIGHEST

D_MODEL = 1024
GW = 256
HEAD_DIM = 64
ZW = 2048
SG_CHUNK = 128
CONV_WIDTH = 31
CONV_PAD = 32
CMP_STRIDE = 16
N_CMP = 128
SLC_BLOCK_SHIFT = 6
N_SLC = 32
SLC_TOPK = 8
WIN = 512
NEG = -1e30
FORCE_BONUS = 1e4
RMS_EPS = 1e-6
LN_EPS = 1e-5
FFN_HIDDEN = 2816
N_DEV = 8
FFN_BLK = 2 * FFN_HIDDEN // N_DEV
TQ = 256
ROW_TILE = 512
CONV_TILE = 256
VMEM_LIMIT = 56 * 1024 * 1024
MESH = pl.DeviceIdType.MESH

ADAM_LR, ADAM_B1, ADAM_B2, ADAM_EPS, ADAM_WD, ADAM_STEP = 0.001, 0.9, 0.999, 1e-08, 0.01, 10

COL_U, COL_V, COL_A, COL_G, COL_Q, COL_D, COL_KV, COL_GL = 0, 256, 512, 768, 1024, 1280, 1536, 1920


def _sds(shape, dtype):
    return jax.ShapeDtypeStruct(shape, dtype)


def _cp(sem=None):
    return pltpu.CompilerParams(dimension_semantics=sem, vmem_limit_bytes=VMEM_LIMIT)


def _tile(n, target, q=128):
    best = None
    for t in range(q, min(n, target) + 1, q):
        if n % t == 0:
            best = t
    return best or n


def _full(shape):
    nd = len(shape)
    return pl.BlockSpec(shape, lambda *_: (0,) * nd)


def _sigmoid(x):
    return jax.nn.sigmoid(x)


def _dot(a, b):
    return jnp.dot(a, b, preferred_element_type=F32)


def _dot_nt(a, b):
    return lax.dot_general(a, b, (((1,), (1,)), ((), ())), preferred_element_type=F32)


def _dot_tn(a, b):
    return lax.dot_general(a, b, (((0,), (0,)), ((), ())), preferred_element_type=F32)


def _lane_head(width=GW):
    return lax.shift_right_logical(lax.broadcasted_iota(jnp.int32, (1, width), 1), 6)


def _fold_heads(x):
    return x + pltpu.roll(x, 64, 1) + pltpu.roll(x, 128, 1) + pltpu.roll(x, 192, 1)


def pack_cols(w):
    pad = jnp.zeros(w.shape[:-1] + (ZW - 1932,), w.dtype)
    return jnp.concatenate([w[..., :1280], w[..., 1676:1932], w[..., 1280:1664], w[..., 1664:1676], pad], axis=-1)


def unpack_cols(wp):
    return jnp.concatenate([wp[..., :1280], wp[..., 1536:1920], wp[..., 1920:1932], wp[..., 1280:1536]], axis=-1)


def mm(a, b, *, ta=False, tb=False, blk=None, out_dtype=F32, name, tm=1024, tn=1024, tk=1024):
    a_dims = ("k", "m") if ta else ("m", "k")
    b_dims = ("n", "k") if tb else ("k", "n")
    a3, b3, o3 = blk in a_dims and blk is not None, blk in b_dims and blk is not None, blk in ("m", "n")
    size = {}
    size[a_dims[0]], size[a_dims[1]] = a.shape[-2:]
    size[b_dims[0]], size[b_dims[1]] = b.shape[-2:]
    nb = a.shape[0] if a3 else (b.shape[0] if b3 else 1)
    tile = {"m": _tile(size["m"], tm), "n": _tile(size["n"], tn), "k": _tile(size["k"], tk)}
    grid = {d: size[d] // tile[d] for d in "mnk"}
    if blk is not None:
        tile[blk] = size[blk]
        grid[blk] = nb
    nk = grid["k"]

    def spec(dims, is3):
        def im(i, j, k):
            g = {"m": i, "n": j, "k": k}
            idx = tuple(0 if d == blk else g[d] for d in dims)
            return ((g[blk],) + idx) if is3 else idx
        shape = (tile[dims[0]], tile[dims[1]])
        return pl.BlockSpec(((None,) + shape) if is3 else shape, im)

    dn = (((0 if ta else 1,), (1 if tb else 0,)), ((), ()))

    def partial(a_ref, b_ref):
        return lax.dot_general(a_ref[...].astype(BF16), b_ref[...].astype(BF16), dn, preferred_element_type=F32)

    def body_single(a_ref, b_ref, o_ref):
        o_ref[...] = partial(a_ref, b_ref).astype(o_ref.dtype)

    def body_acc(a_ref, b_ref, o_ref, acc):
        k = pl.program_id(2)

        @pl.when(k == 0)
        def _():
            acc[...] = partial(a_ref, b_ref)

        @pl.when((k > 0) & (k < nk - 1))
        def _():
            acc[...] += partial(a_ref, b_ref)

        @pl.when(k == nk - 1)
        def _():
            o_ref[...] = (acc[...] + partial(a_ref, b_ref)).astype(o_ref.dtype)

    oshape = ((nb,) if o3 else ()) + (size["m"], size["n"])
    return pl.pallas_call(
        body_single if nk == 1 else body_acc, grid=(grid["m"], grid["n"], nk),
        in_specs=[spec(a_dims, a3), spec(b_dims, b3)], out_specs=spec(("m", "n"), o3),
        out_shape=_sds(oshape, out_dtype),
        scratch_shapes=[] if nk == 1 else [pltpu.VMEM((tile["m"], tile["n"]), F32)],
        compiler_params=_cp(("parallel", "parallel", "arbitrary")), name=name)(a, b)


def mm_kblocks(a, b, *, tb, name, tm):
    nb, M, kb = a.shape
    N = b.shape[1] if tb else b.shape[2]
    dn = (((1,), (1 if tb else 0,)), ((), ()))

    def body(a_ref, b_ref, o_ref):
        acc = lax.dot_general(a_ref[0], b_ref[0], dn, preferred_element_type=F32)
        for j in range(1, nb):
            acc = acc + lax.dot_general(a_ref[j], b_ref[j], dn, preferred_element_type=F32)
        o_ref[...] = acc

    return pl.pallas_call(
        body, grid=(M // tm,), in_specs=[pl.BlockSpec((nb, tm, kb), lambda i: (0, i, 0)), _full(b.shape)],
        out_specs=_rows(tm, N), out_shape=_sds((M, N), F32),
        compiler_params=_cp(("parallel",)), name=name)(a, b)


def _rows(tm, width):
    return pl.BlockSpec((tm, width), lambda i: (i, 0))


def rms_fwd(x, g, name):
    T = x.shape[0]

    def body(x_ref, g_ref, h_ref, ht_ref):
        x = x_ref[...]
        r = lax.rsqrt(jnp.mean(x * x, axis=-1, keepdims=True) + RMS_EPS)
        h = (x * r) * g_ref[...]
        h_ref[...] = h.astype(h_ref.dtype)
        ht_ref[...] = h.T.astype(ht_ref.dtype)

    return pl.pallas_call(body, grid=(T // ROW_TILE,), in_specs=[_rows(ROW_TILE, D_MODEL), _full((1, D_MODEL))],
                          out_specs=[_rows(ROW_TILE, D_MODEL), pl.BlockSpec((D_MODEL, ROW_TILE), lambda i: (0, i))],
                          out_shape=[_sds((T, D_MODEL), BF16), _sds((D_MODEL, T), BF16)],
                          compiler_params=_cp(("parallel",)), name=name)(x, g)


def rms_post_fwd(xres, m, g, name):
    T = m.shape[0]

    def body(x_ref, m_ref, g_ref, o_ref):
        m = m_ref[...]
        r = lax.rsqrt(jnp.mean(m * m, axis=-1, keepdims=True) + RMS_EPS)
        o_ref[...] = x_ref[...] + (m * r) * g_ref[...]

    return pl.pallas_call(body, grid=(T // ROW_TILE,),
                          in_specs=[_rows(ROW_TILE, D_MODEL), _rows(ROW_TILE, D_MODEL), _full((1, D_MODEL))],
                          out_specs=_rows(ROW_TILE, D_MODEL), out_shape=_sds((T, D_MODEL), F32),
                          compiler_params=_cp(("parallel",)), name=name)(xres, m, g)


def rms_bwd(m, g, dy, dres, out_dtype, name):
    T = m.shape[0]
    has_res = dres is not None

    def body(*refs):
        if has_res:
            m_ref, g_ref, dy_ref, dres_ref, dm_ref, dg_ref = refs
        else:
            m_ref, g_ref, dy_ref, dm_ref, dg_ref = refs
        m = m_ref[...]
        dy = dy_ref[...].astype(F32)
        r = lax.rsqrt(jnp.mean(m * m, axis=-1, keepdims=True) + RMS_EPS)
        n = m * r
        dn = dy * g_ref[...]
        dm = r * (dn - n * jnp.mean(dn * n, axis=-1, keepdims=True))
        if has_res:
            dm = dm + dres_ref[...]
        dm_ref[...] = dm.astype(dm_ref.dtype)

        @pl.when(pl.program_id(0) == 0)
        def _():
            dg_ref[...] = jnp.zeros_like(dg_ref)

        dg_ref[...] += jnp.sum(dy * n, axis=0, keepdims=True)

    ins = [m, g, dy] + ([dres] if has_res else [])
    specs = [_rows(ROW_TILE, D_MODEL), _full((1, D_MODEL)), _rows(ROW_TILE, D_MODEL)] + ([_rows(ROW_TILE, D_MODEL)] if has_res else [])
    return pl.pallas_call(body, grid=(T // ROW_TILE,), in_specs=specs,
                          out_specs=[_rows(ROW_TILE, D_MODEL), _full((1, D_MODEL))],
                          out_shape=[_sds((T, D_MODEL), out_dtype), _sds((1, D_MODEL), F32)],
                          compiler_params=_cp(("arbitrary",)), name=name)(*ins)


def loss_fwd_bwd(y, tgt):
    T = y.shape[0]

    def body(y_ref, t_ref, dy_ref, l_ref):
        e = y_ref[...] - t_ref[...]
        dy_ref[...] = e * (1.0 / D_MODEL)

        @pl.when(pl.program_id(0) == 0)
        def _():
            l_ref[...] = jnp.zeros_like(l_ref)

        l_ref[...] += jnp.full(l_ref.shape, 0.5 * jnp.sum(jnp.mean(e * e, axis=-1, keepdims=True)), F32)

    return pl.pallas_call(body, grid=(T // ROW_TILE,), in_specs=[_rows(ROW_TILE, D_MODEL)] * 2,
                          out_specs=[_rows(ROW_TILE, D_MODEL), _full((8, 128))],
                          out_shape=[_sds((T, D_MODEL), F32), _sds((8, 128), F32)],
                          compiler_params=_cp(("arbitrary",)), name="loss")(y, tgt)


FFN_TILE = 2048


def _gu_spec():
    return pl.BlockSpec((2, None, FFN_TILE, FFN_BLK), lambda i, j: (0, j, i, 0))


def ffn_up_fwd(h, w_gu):
    T = h.shape[0]

    def body(h_ref, wg_ref, wu_ref, gu_ref, a_ref):
        h = h_ref[...]
        gate = _dot(h, wg_ref[...])
        up = _dot(h, wu_ref[...])
        gu_ref[0] = gate.astype(gu_ref.dtype)
        gu_ref[1] = up.astype(gu_ref.dtype)
        a_ref[...] = (gate * _sigmoid(gate) * up).astype(a_ref.dtype)

    return pl.pallas_call(
        body, grid=(T // FFN_TILE, 4),
        in_specs=[pl.BlockSpec((FFN_TILE, D_MODEL), lambda i, j: (i, 0)),
                  pl.BlockSpec((None, D_MODEL, FFN_BLK), lambda i, j: (j, 0, 0)),
                  pl.BlockSpec((None, D_MODEL, FFN_BLK), lambda i, j: (j + 4, 0, 0))],
        out_specs=[_gu_spec(), pl.BlockSpec((None, FFN_TILE, FFN_BLK), lambda i, j: (j, i, 0))],
        out_shape=[_sds((2, 4, T, FFN_BLK), BF16), _sds((4, T, FFN_BLK), BF16)],
        compiler_params=_cp(("parallel", "parallel")), name="ffn_up_fwd")(h, w_gu, w_gu)


def ffn_down_dx(df, w_down, gu4):
    T = df.shape[0]

    def body(df_ref, w_ref, gu_ref, d_ref):
        da = _dot_nt(df_ref[...], w_ref[...])
        gate, up = gu_ref[0].astype(F32), gu_ref[1].astype(F32)
        sg = _sigmoid(gate)
        d_ref[0] = (da * up * (sg * (1.0 + gate * (1.0 - sg)))).astype(d_ref.dtype)
        d_ref[1] = (da * (gate * sg)).astype(d_ref.dtype)

    return pl.pallas_call(
        body, grid=(T // FFN_TILE, 4),
        in_specs=[pl.BlockSpec((FFN_TILE, D_MODEL), lambda i, j: (i, 0)),
                  pl.BlockSpec((None, FFN_BLK, D_MODEL), lambda i, j: (j, 0, 0)), _gu_spec()],
        out_specs=_gu_spec(), out_shape=_sds((2, 4, T, FFN_BLK), BF16),
        compiler_params=_cp(("parallel", "parallel")), name="ffn_down_dx")(df, w_down, gu4)


def _zcol(tm, col):
    return pl.BlockSpec((tm, GW), lambda i: (i, col // GW))


def _sg_common(v, g):
    mu = jnp.mean(v, axis=-1, keepdims=True)
    xc = v - mu
    rstd = lax.rsqrt(jnp.mean(xc * xc, axis=-1, keepdims=True) + LN_EPS)
    vhat = xc * rstd
    return vhat, rstd, vhat * g


def _tril_weights(w_ref):
    tri = lax.broadcasted_iota(jnp.int32, (SG_CHUNK, SG_CHUNK), 0) >= lax.broadcasted_iota(jnp.int32, (SG_CHUNK, SG_CHUNK), 1)
    return tri, [jnp.where(tri, w_ref[h], 0.0).astype(BF16) for h in range(4)]


def mixa_fwd(z, ln_g, w, bexp):
    T = z.shape[0]
    nch = ROW_TILE // SG_CHUNK

    def body(u_ref, v_ref, g_ref, w_ref, be_ref, y_ref):
        _, _, vln = _sg_common(v_ref[...], g_ref[...])
        vb = vln.astype(BF16)
        head = _lane_head()
        _, wh = _tril_weights(w_ref)
        for c in range(nch):
            rows = slice(c * SG_CHUNK, (c + 1) * SG_CHUNK)
            sv = be_ref[...]
            for h in range(4):
                sv = sv + jnp.where(head == h, _dot(wh[h], vb[rows]), 0.0)
            y_ref[rows, :] = (u_ref[rows, :] * sv).astype(y_ref.dtype)

    return pl.pallas_call(body, grid=(T // ROW_TILE,),
                          in_specs=[_zcol(ROW_TILE, COL_U), _zcol(ROW_TILE, COL_V), _full((1, GW)), _full((4, SG_CHUNK, SG_CHUNK)),
                                    _full((SG_CHUNK, GW))],
                          out_specs=_rows(ROW_TILE, GW), out_shape=_sds((T, GW), BF16),
                          compiler_params=_cp(("parallel",)), name="mixa_fwd")(z, z, ln_g, w, bexp)


def mixa_bwd(z, dycat, ln_g, w, bexp):
    T = z.shape[0]
    nch = ROW_TILE // SG_CHUNK
    nsteps = T // ROW_TILE

    def body(u_ref, v_ref, dy_ref, g_ref, w_ref, be_ref, du_ref, dv_ref, dw_ref, db_ref, dg_ref, dbe_acc):
        step = pl.program_id(0)

        @pl.when(step == 0)
        def _():
            dw_ref[...] = jnp.zeros_like(dw_ref)
            dg_ref[...] = jnp.zeros_like(dg_ref)
            dbe_acc[...] = jnp.zeros_like(dbe_acc)

        g = g_ref[...]
        vhat, rstd, vln = _sg_common(v_ref[...], g)
        vb = vln.astype(BF16)
        head = _lane_head()
        tri, wh = _tril_weights(w_ref)
        dgsum = jnp.zeros((1, GW), F32)
        for c in range(nch):
            rows = slice(c * SG_CHUNK, (c + 1) * SG_CHUNK)
            sv = be_ref[...]
            for h in range(4):
                sv = sv + jnp.where(head == h, _dot(wh[h], vb[rows]), 0.0)
            dy = dy_ref[rows, :]
            du_ref[rows, :] = (dy * sv).astype(du_ref.dtype)
            dsv = dy * u_ref[rows, :]
            dbe_acc[...] += dsv
            dvln = jnp.zeros((SG_CHUNK, GW), F32)
            for h in range(4):
                dsvm = jnp.where(head == h, dsv, 0.0).astype(BF16)
                dw_ref[h] += _dot_nt(dsvm, vb[rows])
                dvln = dvln + _dot_tn(wh[h], dsvm)
            vh = vhat[rows]
            dgsum = dgsum + jnp.sum(dvln * vh, axis=0, keepdims=True)
            dvhat = dvln * g
            dv = rstd[rows] * (dvhat - jnp.mean(dvhat, axis=-1, keepdims=True) - vh * jnp.mean(dvhat * vh, axis=-1, keepdims=True))
            dv_ref[rows, :] = dv.astype(dv_ref.dtype)
        dg_ref[...] += dgsum

        @pl.when(step == nsteps - 1)
        def _():
            for h in range(4):
                dw_ref[h] = jnp.where(tri, dw_ref[h], 0.0)
            fold = (lax.shift_right_logical(lax.broadcasted_iota(jnp.int32, (GW, 128), 0), 6)
                    == lax.broadcasted_iota(jnp.int32, (GW, 128), 1)).astype(F32)
            db_ref[...] = jnp.dot(dbe_acc[...], fold, precision=HI, preferred_element_type=F32)

    return pl.pallas_call(
        body, grid=(nsteps,),
        in_specs=[_zcol(ROW_TILE, COL_U), _zcol(ROW_TILE, COL_V), pl.BlockSpec((ROW_TILE, GW), lambda i: (i, 0)),
                  _full((1, GW)), _full((4, SG_CHUNK, SG_CHUNK)), _full((SG_CHUNK, GW))],
        out_specs=[_rows(ROW_TILE, GW), _rows(ROW_TILE, GW), _full((4, SG_CHUNK, SG_CHUNK)), _full((SG_CHUNK, 128)), _full((1, GW))],
        out_shape=[_sds((T, GW), BF16), _sds((T, GW), BF16), _sds((4, SG_CHUNK, SG_CHUNK), F32), _sds((SG_CHUNK, 128), F32),
                   _sds((1, GW), F32)],
        scratch_shapes=[pltpu.VMEM((SG_CHUNK, GW), F32)],
        compiler_params=_cp(("arbitrary",)), name="mixa_bwd")(z, z, dycat, ln_g, w, bexp)


def _seq(S, col):
    return pl.BlockSpec((None, S, GW), lambda b: (b, 0, col // GW))


def _taps(buf, r0, offsets):
    by_phase = {}
    for k, off in enumerate(offsets):
        by_phase.setdefault(off % 8, []).append((k, off))
    for phase, items in sorted(by_phase.items()):
        span = max(off for _, off in items) - phase
        win = buf[pl.ds(r0 + phase, CONV_TILE + span), :]
        for k, off in items:
            yield k, win[off - phase:off - phase + CONV_TILE]


_CONV_FWD_OFFSETS = [CONV_PAD - (CONV_WIDTH - 1) + k for k in range(CONV_WIDTH)]
_CONV_BWD_OFFSETS = [(CONV_WIDTH - 1) - k for k in range(CONV_WIDTH)]


def _conv(pad, r0, cw_ref, cb):
    acc = jnp.zeros((CONV_TILE, GW), F32) + cb
    for k, rows in _taps(pad, r0, _CONV_FWD_OFFSETS):
        acc = acc + cw_ref[k:k + 1, :] * rows
    return acc


def _conv_ln(acc, lg, lb):
    mu = jnp.mean(acc, axis=-1, keepdims=True)
    xc = acc - mu
    rstd = lax.rsqrt(jnp.mean(xc * xc, axis=-1, keepdims=True) + LN_EPS)
    hhat = xc * rstd
    return hhat, rstd, hhat * lg + lb


def mixb_fwd(z3, cw, cb, lg, lb, pw, pwb):
    B, S, _ = z3.shape

    def body(a_ref, gt_ref, cw_ref, cb_ref, lg_ref, lb_ref, pw_ref, pwb_ref, y_ref, hc_ref, pad):
        pad[0:CONV_PAD, :] = jnp.zeros((CONV_PAD, GW), F32)
        pad[CONV_PAD:CONV_PAD + S, :] = a_ref[...] * _sigmoid(gt_ref[...])
        pwv = pw_ref[...].astype(BF16)
        for r0 in range(0, S, CONV_TILE):
            hc = _conv(pad, r0, cw_ref, cb_ref[...])
            hc_ref[r0:r0 + CONV_TILE, :] = hc
            _, _, ln = _conv_ln(hc, lg_ref[...], lb_ref[...])
            s = ln * _sigmoid(ln)
            y_ref[r0:r0 + CONV_TILE, :] = (_dot(s.astype(BF16), pwv) + pwb_ref[...]).astype(y_ref.dtype)

    seq_out = pl.BlockSpec((None, S, GW), lambda b: (b, 0, 0))
    return pl.pallas_call(
        body, grid=(B,),
        in_specs=[_seq(S, COL_A), _seq(S, COL_G), _full((CONV_WIDTH, GW)), _full((1, GW)), _full((1, GW)), _full((1, GW)),
                  _full((GW, GW)), _full((1, GW))],
        out_specs=[seq_out, seq_out], out_shape=[_sds((B, S, GW), BF16), _sds((B, S, GW), F32)],
        scratch_shapes=[pltpu.VMEM((S + CONV_PAD, GW), F32)],
        compiler_params=_cp(("parallel",)), name="mixb_fwd")(z3, z3, cw, cb, lg, lb, pw, pwb)


def mixb_bwd(z3, hc3, dycat3, cw, lg, lb, pw):
    B, S, _ = z3.shape

    def body(a_ref, gt_ref, hc_ref, dy_ref, cw_ref, lg_ref, lb_ref, pw_ref,
             da_ref, dgt_ref, dcw_ref, dcb_ref, dlg_ref, dlb_ref, dpw_ref, dpwb_ref, pad, dpad, dcw_acc):
        @pl.when(pl.program_id(0) == 0)
        def _():
            for r in (dcb_ref, dlg_ref, dlb_ref, dpw_ref, dpwb_ref, dcw_acc):
                r[...] = jnp.zeros_like(r)

        pad[0:CONV_PAD, :] = jnp.zeros((CONV_PAD, GW), F32)
        pad[CONV_PAD:CONV_PAD + S, :] = a_ref[...] * _sigmoid(gt_ref[...])
        dpad[S:S + CONV_PAD, :] = jnp.zeros((CONV_PAD, GW), F32)
        pwv = pw_ref[...].astype(BF16)
        lg = lg_ref[...]
        for r0 in range(0, S, CONV_TILE):
            hhat, rstd, ln = _conv_ln(hc_ref[r0:r0 + CONV_TILE, :], lg, lb_ref[...])
            sg = _sigmoid(ln)
            s = ln * sg
            dy = dy_ref[r0:r0 + CONV_TILE, :]
            dyb = dy.astype(BF16)
            dpw_ref[...] += _dot_tn(s.astype(BF16), dyb)
            dpwb_ref[...] += jnp.sum(dy, axis=0, keepdims=True)
            dln = _dot_nt(dyb, pwv) * (sg * (1.0 + ln * (1.0 - sg)))
            dlg_ref[...] += jnp.sum(dln * hhat, axis=0, keepdims=True)
            dlb_ref[...] += jnp.sum(dln, axis=0, keepdims=True)
            dhh = dln * lg
            dhc = rstd * (dhh - jnp.mean(dhh, axis=-1, keepdims=True) - hhat * jnp.mean(dhh * hhat, axis=-1, keepdims=True))
            dpad[r0:r0 + CONV_TILE, :] = dhc
            dcb_ref[...] += jnp.sum(dhc, axis=0, keepdims=True)
            for k, rows in _taps(pad, r0, _CONV_FWD_OFFSETS):
                dcw_acc[k] += (dhc * rows).reshape(CONV_TILE // 8, 8, GW).sum(axis=0)
        for r0 in range(0, S, CONV_TILE):
            dhg = jnp.zeros((CONV_TILE, GW), F32)
            for k, rows in _taps(dpad, r0, _CONV_BWD_OFFSETS):
                dhg = dhg + cw_ref[k:k + 1, :] * rows
            a = a_ref[r0:r0 + CONV_TILE, :]
            sg = _sigmoid(gt_ref[r0:r0 + CONV_TILE, :])
            da_ref[r0:r0 + CONV_TILE, :] = (dhg * sg).astype(da_ref.dtype)
            dgt_ref[r0:r0 + CONV_TILE, :] = (dhg * a * sg * (1.0 - sg)).astype(dgt_ref.dtype)

        @pl.when(pl.program_id(0) == B - 1)
        def _():
            for k in range(CONV_WIDTH):
                dcw_ref[k:k + 1, :] = jnp.sum(dcw_acc[k], axis=0, keepdims=True)

    seq_out = pl.BlockSpec((None, S, GW), lambda b: (b, 0, 0))
    return pl.pallas_call(
        body, grid=(B,),
        in_specs=[_seq(S, COL_A), _seq(S, COL_G), seq_out, pl.BlockSpec((None, S, GW), lambda b: (b, 0, 1)),
                  _full((CONV_WIDTH, GW)), _full((1, GW)), _full((1, GW)), _full((GW, GW))],
        out_specs=[seq_out, seq_out, _full((CONV_WIDTH, GW)), _full((1, GW)), _full((1, GW)), _full((1, GW)), _full((GW, GW)),
                   _full((1, GW))],
        out_shape=[_sds((B, S, GW), BF16), _sds((B, S, GW), BF16), _sds((CONV_WIDTH, GW), F32), _sds((1, GW), F32),
                   _sds((1, GW), F32), _sds((1, GW), F32), _sds((GW, GW), F32), _sds((1, GW), F32)],
        scratch_shapes=[pltpu.VMEM((S + CONV_PAD, GW), F32), pltpu.VMEM((S + CONV_PAD, GW), F32),
                        pltpu.VMEM((CONV_WIDTH, 8, GW), F32)],
        compiler_params=_cp(("arbitrary",)), name="mixb_bwd")(z3, z3, hc3, dycat3, cw, lg, lb, pw)


POOL_PAD = 16


def _pool_window():
    lane = lax.broadcasted_iota(jnp.int32, (1, GW), 1)
    return jnp.where(lane < 64, 2, jnp.where(lane < 128, 4, jnp.where(lane < 192, 8, 16)))


def _pool_sums(pad, r0, base, sign):
    win = _pool_window()
    acc = pad[pl.ds(r0 + base, CONV_TILE), :]
    out = None
    for i in range(1, 16):
        acc = acc + pad[pl.ds(r0 + base + sign * i, CONV_TILE), :]
        if i + 1 in (2, 4, 8, 16):
            out = acc if out is None else jnp.where(win == i + 1, acc, out)
    return out


def _pool_cnt(r0):
    t1 = r0 + 1 + lax.broadcasted_iota(jnp.int32, (CONV_TILE, 1), 0)
    return jnp.minimum(t1, _pool_window()).astype(F32)


def mixd_fwd(z3, wbd, scale):
    B, S, _ = z3.shape

    def body(x_ref, w_ref, sc_ref, y_ref, pad):
        pad[0:POOL_PAD, :] = jnp.zeros((POOL_PAD, GW), F32)
        pad[POOL_PAD:POOL_PAD + S, :] = x_ref[...]
        wv = w_ref[...].astype(BF16)
        for r0 in range(0, S, CONV_TILE):
            mean = _pool_sums(pad, r0, POOL_PAD, -1) / _pool_cnt(r0)
            p = (mean - x_ref[r0:r0 + CONV_TILE, :]).astype(BF16)
            y_ref[r0:r0 + CONV_TILE, :] = (_dot(p, wv) * sc_ref[...]).astype(y_ref.dtype)

    return pl.pallas_call(
        body, grid=(B,), in_specs=[_seq(S, COL_D), _full((GW, GW)), _full((1, GW))],
        out_specs=pl.BlockSpec((None, S, GW), lambda b: (b, 0, 0)), out_shape=_sds((B, S, GW), BF16),
        scratch_shapes=[pltpu.VMEM((S + POOL_PAD, GW), F32)],
        compiler_params=_cp(("parallel",)), name="mixd_fwd")(z3, wbd, scale)


def mixd_bwd(z3, dycat3, wbd, scale):
    B, S, _ = z3.shape

    def body(x_ref, dy_ref, w_ref, sc_ref, dx_ref, dw_ref, dsc_ref, pad, qpad):
        @pl.when(pl.program_id(0) == 0)
        def _():
            dw_ref[...] = jnp.zeros_like(dw_ref)
            dsc_ref[...] = jnp.zeros_like(dsc_ref)

        pad[0:POOL_PAD, :] = jnp.zeros((POOL_PAD, GW), F32)
        pad[POOL_PAD:POOL_PAD + S, :] = x_ref[...]
        qpad[S:S + POOL_PAD, :] = jnp.zeros((POOL_PAD, GW), F32)
        wv = w_ref[...].astype(BF16)
        for r0 in range(0, S, CONV_TILE):
            cnt = _pool_cnt(r0)
            mean = _pool_sums(pad, r0, POOL_PAD, -1) / cnt
            p = (mean - x_ref[r0:r0 + CONV_TILE, :]).astype(BF16)
            dy = dy_ref[r0:r0 + CONV_TILE, :]
            dsc_ref[...] += jnp.sum(dy * _dot(p, wv), axis=0, keepdims=True)
            dyp = (dy * sc_ref[...]).astype(BF16)
            dw_ref[...] += _dot_tn(p, dyp)
            dp = _dot_nt(dyp, wv)
            dx_ref[r0:r0 + CONV_TILE, :] = (-dp).astype(dx_ref.dtype)
            qpad[r0:r0 + CONV_TILE, :] = dp / cnt
        for r0 in range(0, S, CONV_TILE):
            back = _pool_sums(qpad, r0, 0, 1)
            dx_ref[r0:r0 + CONV_TILE, :] = (dx_ref[r0:r0 + CONV_TILE, :].astype(F32) + back).astype(dx_ref.dtype)

    return pl.pallas_call(
        body, grid=(B,),
        in_specs=[_seq(S, COL_D), pl.BlockSpec((None, S, GW), lambda b: (b, 0, 3)), _full((GW, GW)), _full((1, GW))],
        out_specs=[pl.BlockSpec((None, S, GW), lambda b: (b, 0, 0)), _full((GW, GW)), _full((1, GW))],
        out_shape=[_sds((B, S, GW), F32), _sds((GW, GW), F32), _sds((1, GW), F32)],
        scratch_shapes=[pltpu.VMEM((S + POOL_PAD, GW), F32), pltpu.VMEM((S + POOL_PAD, GW), F32)],
        compiler_params=_cp(("arbitrary",)), name="mixd_bwd")(z3, dycat3, wbd, scale)


def cmp_kv_fwd(tbk, tbv, pek, pev, w1k, w2k, w1v, w2v):
    B = tbk.shape[0]

    def body(tbk_ref, tbv_ref, pek_ref, pev_ref, w1k_ref, w2k_ref, w1v_ref, w2v_ref, kc_ref, vc_ref):
        for tb_ref, pe_ref, w1_ref, w2_ref, o_ref in ((tbk_ref, pek_ref, w1k_ref, w2k_ref, kc_ref),
                                                      (tbv_ref, pev_ref, w1v_ref, w2v_ref, vc_ref)):
            pre = _dot((tb_ref[...] + pe_ref[...]).astype(BF16), w1_ref[...].astype(BF16))
            hm = pre * _sigmoid(pre)
            o_ref[...] = _dot(hm.astype(BF16), w2_ref[...].astype(BF16))

    tb_spec = pl.BlockSpec((None, N_CMP, 2048), lambda b: (b, 0, 0))
    o_spec = pl.BlockSpec((None, N_CMP, HEAD_DIM), lambda b: (b, 0, 0))
    return pl.pallas_call(
        body, grid=(B,),
        in_specs=[tb_spec, tb_spec, _full((1, 2048)), _full((1, 2048)), _full((2048, HEAD_DIM)), _full((HEAD_DIM, HEAD_DIM)),
                  _full((2048, HEAD_DIM)), _full((HEAD_DIM, HEAD_DIM))],
        out_specs=[o_spec, o_spec], out_shape=[_sds((B, N_CMP, HEAD_DIM), F32)] * 2,
        compiler_params=_cp(("parallel",)), name="cmp_kv_fwd")(tbk, tbv, pek, pev, w1k, w2k, w1v, w2v)


def cmp_kv_bwd(tbk, tbv, pek, pev, w1k, w2k, w1v, w2v, dkc, dvc):
    B = tbk.shape[0]

    def body(tbk_ref, tbv_ref, pek_ref, pev_ref, w1k_ref, w2k_ref, w1v_ref, w2v_ref, dkc_ref, dvc_ref,
             dk2_ref, dv2_ref, dpek_ref, dpev_ref, dw1k_ref, dw2k_ref, dw1v_ref, dw2v_ref):
        @pl.when(pl.program_id(0) == 0)
        def _():
            for r in (dpek_ref, dpev_ref, dw1k_ref, dw2k_ref, dw1v_ref, dw2v_ref):
                r[...] = jnp.zeros_like(r)

        row0 = lax.broadcasted_iota(jnp.int32, (N_CMP, 1), 0) == 0
        for tb_ref, pe_ref, w1_ref, w2_ref, do_ref, d2_ref, dpe_ref, dw1_ref, dw2_ref in (
                (tbk_ref, pek_ref, w1k_ref, w2k_ref, dkc_ref, dk2_ref, dpek_ref, dw1k_ref, dw2k_ref),
                (tbv_ref, pev_ref, w1v_ref, w2v_ref, dvc_ref, dv2_ref, dpev_ref, dw1v_ref, dw2v_ref)):
            tb = (tb_ref[...] + pe_ref[...]).astype(BF16)
            w1 = w1_ref[...].astype(BF16)
            pre = _dot(tb, w1)
            sg = _sigmoid(pre)
            hm = (pre * sg).astype(BF16)
            do = do_ref[...].astype(BF16)
            dw2_ref[...] += _dot_tn(hm, do)
            dpre = (_dot_nt(do, w2_ref[...].astype(BF16)) * (sg * (1.0 + pre * (1.0 - sg)))).astype(BF16)
            dw1_ref[...] += _dot_tn(tb, dpre)
            dtb = _dot_nt(dpre, w1)
            dpe_ref[...] += jnp.sum(dtb, axis=0, keepdims=True)
            down = jnp.where(row0, 0.0, pltpu.roll(dtb[:, 1024:], 1, 0))
            d2_ref[...] = dtb[:, :1024] + down

    tb_spec = pl.BlockSpec((None, N_CMP, 2048), lambda b: (b, 0, 0))
    c_spec = pl.BlockSpec((None, N_CMP, HEAD_DIM), lambda b: (b, 0, 0))
    d2_spec = pl.BlockSpec((None, N_CMP, 1024), lambda b: (b, 0, 0))
    return pl.pallas_call(
        body, grid=(B,),
        in_specs=[tb_spec, tb_spec, _full((1, 2048)), _full((1, 2048)), _full((2048, HEAD_DIM)), _full((HEAD_DIM, HEAD_DIM)),
                  _full((2048, HEAD_DIM)), _full((HEAD_DIM, HEAD_DIM)), c_spec, c_spec],
        out_specs=[d2_spec, d2_spec, _full((1, 2048)), _full((1, 2048)), _full((2048, HEAD_DIM)), _full((HEAD_DIM, HEAD_DIM)),
                   _full((2048, HEAD_DIM)), _full((HEAD_DIM, HEAD_DIM))],
        out_shape=[_sds((B, N_CMP, 1024), F32)] * 2 + [_sds((1, 2048), F32)] * 2
        + [_sds((2048, HEAD_DIM), F32), _sds((HEAD_DIM, HEAD_DIM), F32)] * 2,
        compiler_params=_cp(("arbitrary",)), name="cmp_kv_bwd")(tbk, tbv, pek, pev, w1k, w2k, w1v, w2v, dkc, dvc)


def _qtile(col):
    return pl.BlockSpec((None, TQ, GW), lambda b, i: (b, i, col // GW))


def _qtile0():
    return pl.BlockSpec((None, TQ, GW), lambda b, i: (b, i, 0))


def _cmp_probs(q, kc, qpos):
    head = _lane_head()
    cend = lax.broadcasted_iota(jnp.int32, (1, N_CMP), 1) * CMP_STRIDE + 31
    cmask = cend <= qpos
    has = qpos >= 31
    out = []
    for h in range(4):
        qm = jnp.where(head == h, q, 0.0).astype(BF16)
        s = jnp.where(cmask, _dot_nt(qm, kc), NEG)
        e = jnp.exp(s - jnp.max(s, axis=-1, keepdims=True))
        p = jnp.where(has, e / jnp.sum(e, axis=-1, keepdims=True), 0.0)
        out.append((qm, p))
    return out


def cmp_attn_fwd(z3, kc4, vc4):
    B, S, _ = z3.shape

    def body(q_ref, kc_ref, vc_ref, o_ref, sel_ref):
        t0 = pl.program_id(1) * TQ
        qpos = t0 + lax.broadcasted_iota(jnp.int32, (TQ, 1), 0)
        head = _lane_head()
        kc, vc = kc_ref[...], vc_ref[...]
        o = jnp.zeros((TQ, GW), F32)
        psum = jnp.zeros((TQ, N_CMP), F32)
        for h, (_, p) in enumerate(_cmp_probs(q_ref[...] * 0.125, kc, qpos)):
            o = o + jnp.where(head == h, _dot(p.astype(BF16), vc), 0.0)
            psum = psum + p
        o_ref[...] = o
        cst = lax.broadcasted_iota(jnp.int32, (N_SLC, N_CMP), 1) * CMP_STRIDE
        jst = lax.broadcasted_iota(jnp.int32, (N_SLC, N_CMP), 0) * 64
        overlap = ((cst <= jst + 63) & (cst + 31 >= jst)).astype(BF16)
        imp = _dot_nt(overlap, psum.astype(BF16))
        qp = t0 + lax.broadcasted_iota(jnp.int32, (1, TQ), 1)
        jj = lax.broadcasted_iota(jnp.int32, (N_SLC, 1), 0)
        cur = lax.shift_right_logical(qp, SLC_BLOCK_SHIFT)
        forced = (jj == 0) | (jj == cur) | (jj == cur - 1)
        score = jnp.where(jj * 64 <= qp, imp + jnp.where(forced, FORCE_BONUS, 0.0), NEG)
        rank = jnp.zeros((N_SLC, TQ), F32)
        for j2 in range(N_SLC):
            sj = score[j2:j2 + 1, :]
            rank = rank + jnp.where((sj > score) | ((sj == score) & (j2 < jj)), 1.0, 0.0)
        sel_ref[...] = jnp.where((rank < SLC_TOPK) & (score > NEG / 2), 1.0, 0.0)

    c_spec = pl.BlockSpec((None, N_CMP, GW), lambda b, i: (b, 0, 0))
    return pl.pallas_call(
        body, grid=(B, S // TQ), in_specs=[_qtile(COL_Q), c_spec, c_spec],
        out_specs=[_qtile0(), pl.BlockSpec((None, N_SLC, TQ), lambda b, i: (b, 0, i))],
        out_shape=[_sds((B, S, GW), F32), _sds((B, N_SLC, S), F32)],
        compiler_params=_cp(("parallel", "parallel")), name="cmp_attn_fwd")(z3, kc4, vc4)


def cmp_attn_bwd(z3, kc4, vc4, do):
    B, S, _ = z3.shape
    nq = S // TQ

    def body(q_ref, kc_ref, vc_ref, do_ref, dq_ref, dkc_out, dvc_out, dkc_ref, dvc_ref):
        qi = pl.program_id(1)

        @pl.when(qi == 0)
        def _():
            dkc_ref[...] = jnp.zeros_like(dkc_ref)
            dvc_ref[...] = jnp.zeros_like(dvc_ref)

        qpos = qi * TQ + lax.broadcasted_iota(jnp.int32, (TQ, 1), 0)
        head = _lane_head()
        kc, vc, do = kc_ref[...], vc_ref[...], do_ref[...]
        dq = jnp.zeros((TQ, GW), F32)
        for h, (qm, p) in enumerate(_cmp_probs(q_ref[...] * 0.125, kc, qpos)):
            dom = jnp.where(head == h, do, 0.0).astype(BF16)
            dp = _dot_nt(dom, vc)
            ds = (p * (dp - jnp.sum(p * dp, axis=-1, keepdims=True))).astype(BF16)
            dq = dq + jnp.where(head == h, _dot(ds, kc), 0.0)
            dkc_ref[...] += _dot_tn(ds, qm)
            dvc_ref[...] += _dot_tn(p.astype(BF16), dom)
        dq_ref[...] = dq * 0.125

        @pl.when(qi == nq - 1)
        def _():
            dkc_out[...] = _fold_heads(dkc_ref[...])[:, :HEAD_DIM]
            dvc_out[...] = _fold_heads(dvc_ref[...])[:, :HEAD_DIM]

    c_spec = pl.BlockSpec((None, N_CMP, GW), lambda b, i: (b, 0, 0))
    d_spec = pl.BlockSpec((None, N_CMP, HEAD_DIM), lambda b, i: (b, 0, 0))
    return pl.pallas_call(
        body, grid=(B, nq), in_specs=[_qtile(COL_Q), c_spec, c_spec, _qtile0()],
        out_specs=[_qtile0(), d_spec, d_spec],
        out_shape=[_sds((B, S, GW), F32), _sds((B, N_CMP, HEAD_DIM), F32), _sds((B, N_CMP, HEAD_DIM), F32)],
        scratch_shapes=[pltpu.VMEM((N_CMP, GW), F32), pltpu.VMEM((N_CMP, GW), F32)],
        compiler_params=_cp(("parallel", "arbitrary")), name="cmp_attn_bwd")(z3, kc4, vc4, do)


def _attn_mask(mode, qpos, k0, sel_b):
    kpos = k0 + lax.broadcasted_iota(jnp.int32, (1, TQ), 1)
    mask = kpos <= qpos
    if mode == "win":
        return mask & (kpos > qpos - WIN)
    blk = lax.shift_right_logical(k0 + lax.broadcasted_iota(jnp.int32, (N_SLC, TQ), 1), SLC_BLOCK_SHIFT)
    expand = (blk == lax.broadcasted_iota(jnp.int32, (N_SLC, TQ), 0)).astype(BF16)
    return mask & (_dot_tn(sel_b, expand) > 0.5)


def _attn_lo(mode, qi):
    return jnp.maximum(qi - WIN // TQ, 0) if mode == "win" else 0


def attn_fwd(mode, z3, k4, v4, selT):
    B, S, _ = z3.shape

    def body(q_ref, k_ref, v_ref, sel_ref, o_ref, lse_ref, s_all, m_acc, l_acc, o_acc):
        qi = pl.program_id(1)
        qpos = qi * TQ + lax.broadcasted_iota(jnp.int32, (TQ, 1), 0)
        head = _lane_head()
        q = q_ref[...] * 0.125
        qm = [jnp.where(head == h, q, 0.0).astype(BF16) for h in range(4)]
        sel_b = sel_ref[...].astype(BF16)
        lo, hi = _attn_lo(mode, qi), qi + 1
        m_acc[...] = jnp.full(m_acc.shape, NEG, F32)

        def scores(kb, carry):
            k0 = pl.multiple_of(kb * TQ, TQ)
            kblk = k_ref[pl.ds(k0, TQ), :]
            mask = _attn_mask(mode, qpos, k0, sel_b)
            for h in range(4):
                s = jnp.where(mask, _dot_nt(qm[h], kblk), NEG)
                s_all[h, kb] = s
                m_acc[h] = jnp.maximum(m_acc[h], s)
            return carry

        lax.fori_loop(lo, hi, scores, 0)
        for h in range(4):
            m_acc[h] = jnp.broadcast_to(jnp.max(m_acc[h], axis=-1, keepdims=True), (TQ, TQ))
        l_acc[...] = jnp.zeros_like(l_acc)
        o_acc[...] = jnp.zeros_like(o_acc)

        def weights(kb, carry):
            vblk = v_ref[pl.ds(pl.multiple_of(kb * TQ, TQ), TQ), :]
            for h in range(4):
                p = jnp.exp(s_all[h, kb] - m_acc[h])
                l_acc[h] += p
                o_acc[h] += _dot(p.astype(BF16), vblk)
            return carry

        lax.fori_loop(lo, hi, weights, 0)
        o = jnp.zeros((TQ, GW), F32)
        lse = jnp.zeros((TQ, 128), F32)
        lane = lax.broadcasted_iota(jnp.int32, (1, 128), 1)
        for h in range(4):
            l = jnp.sum(l_acc[h], axis=-1, keepdims=True)
            o = o + jnp.where(head == h, o_acc[h] / l, 0.0)
            lse = jnp.where(lane == h, jnp.max(m_acc[h], axis=-1, keepdims=True) + jnp.log(l), lse)
        o_ref[...] = o
        lse_ref[...] = lse

    kv_spec = pl.BlockSpec((None, S, GW), lambda b, i: (b, 0, 0))
    return pl.pallas_call(
        body, grid=(B, S // TQ),
        in_specs=[_qtile(COL_Q), kv_spec, kv_spec, pl.BlockSpec((None, N_SLC, TQ), lambda b, i: (b, 0, i))],
        out_specs=[_qtile0(), pl.BlockSpec((None, TQ, 128), lambda b, i: (b, i, 0))],
        out_shape=[_sds((B, S, GW), F32), _sds((B, S, 128), F32)],
        scratch_shapes=[pltpu.VMEM((4, S // TQ, TQ, TQ), F32), pltpu.VMEM((4, TQ, TQ), F32), pltpu.VMEM((4, TQ, TQ), F32),
                        pltpu.VMEM((4, TQ, GW), F32)],
        compiler_params=_cp(("parallel", "parallel")), name=mode + "_attn_fwd")(z3, k4, v4, selT)


def attn_bwd(mode, z3, k4, v4, selT, o, lse, do):
    B, S, _ = z3.shape
    nq = S // TQ

    def body(q_ref, k_ref, v_ref, sel_ref, o_ref, lse_ref, do_ref, dq_ref, dk_out, dv_out, dq_s, dk_ref, dv_ref):
        qi = pl.program_id(1)

        @pl.when(qi == 0)
        def _():
            dk_ref[...] = jnp.zeros_like(dk_ref)
            dv_ref[...] = jnp.zeros_like(dv_ref)

        qpos = qi * TQ + lax.broadcasted_iota(jnp.int32, (TQ, 1), 0)
        head = _lane_head()
        lane = lax.broadcasted_iota(jnp.int32, (1, 128), 1)
        q = q_ref[...] * 0.125
        do = do_ref[...]
        doo = do * o_ref[...]
        lse = lse_ref[...]
        qm = [jnp.where(head == h, q, 0.0).astype(BF16) for h in range(4)]
        dom = [jnp.where(head == h, do, 0.0).astype(BF16) for h in range(4)]
        delta = [jnp.sum(jnp.where(head == h, doo, 0.0), axis=-1, keepdims=True) for h in range(4)]
        lse_h = [jnp.max(jnp.where(lane == h, lse, NEG), axis=-1, keepdims=True) for h in range(4)]
        sel_b = sel_ref[...].astype(BF16)
        dq_s[...] = jnp.zeros_like(dq_s)

        def step(kb, carry):
            k0 = pl.multiple_of(kb * TQ, TQ)
            kblk = k_ref[pl.ds(k0, TQ), :]
            vblk = v_ref[pl.ds(k0, TQ), :]
            mask = _attn_mask(mode, qpos, k0, sel_b)
            for h in range(4):
                s = _dot_nt(qm[h], kblk)
                p = jnp.where(mask, jnp.exp(s - lse_h[h]), 0.0)
                dp = _dot_nt(dom[h], vblk)
                ds = (p * (dp - delta[h])).astype(BF16)
                dq_s[...] += jnp.where(head == h, _dot(ds, kblk), 0.0)
                dk_ref[pl.ds(k0, TQ), :] += _dot_tn(ds, qm[h])
                dv_ref[pl.ds(k0, TQ), :] += _dot_tn(p.astype(BF16), dom[h])
            return carry

        lax.fori_loop(_attn_lo(mode, qi), qi + 1, step, 0)
        dq_ref[...] = dq_s[...] * 0.125

        @pl.when(qi == nq - 1)
        def _():
            for r0 in range(0, S, TQ):
                dk_out[r0:r0 + TQ, :] = _fold_heads(dk_ref[r0:r0 + TQ, :])[:, :HEAD_DIM]
                dv_out[r0:r0 + TQ, :] = _fold_heads(dv_ref[r0:r0 + TQ, :])[:, :HEAD_DIM]

    kv_spec = pl.BlockSpec((None, S, GW), lambda b, i: (b, 0, 0))
    return pl.pallas_call(
        body, grid=(B, nq),
        in_specs=[_qtile(COL_Q), kv_spec, kv_spec, pl.BlockSpec((None, N_SLC, TQ), lambda b, i: (b, 0, i)), _qtile0(),
                  pl.BlockSpec((None, TQ, 128), lambda b, i: (b, i, 0)), _qtile0()],
        out_specs=[_qtile0(), pl.BlockSpec((None, S, HEAD_DIM), lambda b, i: (b, 0, 0)),
                   pl.BlockSpec((None, S, HEAD_DIM), lambda b, i: (b, 0, 0))],
        out_shape=[_sds((B, S, GW), F32), _sds((B, S, HEAD_DIM), F32), _sds((B, S, HEAD_DIM), F32)],
        scratch_shapes=[pltpu.VMEM((TQ, GW), F32), pltpu.VMEM((S, GW), F32), pltpu.VMEM((S, GW), F32)],
        compiler_params=_cp(("parallel", "arbitrary")), name=mode + "_attn_bwd")(z3, k4, v4, selT, o, lse, do)


def _gate_expand(g, b):
    head = _lane_head()
    out = jnp.zeros((TQ, GW), F32)
    for h in range(4):
        out = jnp.where(head == h, g[:, 3 * h + b:3 * h + b + 1], out)
    return out


def combine_fwd(z3, o_cmp, o_slc, o_win):
    B, S, _ = z3.shape

    def body(gl_ref, oc_ref, os_ref, ow_ref, y_ref):
        g = _sigmoid(gl_ref[...])
        y = jnp.zeros((TQ, GW), F32)
        for b, o_ref in enumerate((oc_ref, os_ref, ow_ref)):
            y = y + _gate_expand(g, b) * o_ref[...]
        y_ref[...] = y.astype(y_ref.dtype)

    return pl.pallas_call(
        body, grid=(B, S // TQ),
        in_specs=[pl.BlockSpec((None, TQ, 128), lambda b, i: (b, i, COL_GL // 128)), _qtile0(), _qtile0(), _qtile0()],
        out_specs=_qtile0(), out_shape=_sds((B, S, GW), BF16),
        compiler_params=_cp(("parallel", "parallel")), name="combine_fwd")(z3, o_cmp, o_slc, o_win)


def combine_bwd(z3, o_cmp, o_slc, o_win, dycat3):
    B, S, _ = z3.shape

    def body(gl_ref, oc_ref, os_ref, ow_ref, dy_ref, dc_ref, ds_ref, dw_ref, dgl_ref):
        g = _sigmoid(gl_ref[...])
        dy = dy_ref[...]
        head = _lane_head()
        lane = lax.broadcasted_iota(jnp.int32, (1, 128), 1)
        dg = jnp.zeros((TQ, 128), F32)
        for b, (o_ref, d_ref) in enumerate(((oc_ref, dc_ref), (os_ref, ds_ref), (ow_ref, dw_ref))):
            d_ref[...] = _gate_expand(g, b) * dy
            t = dy * o_ref[...]
            for h in range(4):
                dg = jnp.where(lane == 3 * h + b, jnp.sum(jnp.where(head == h, t, 0.0), axis=-1, keepdims=True), dg)
        dgl_ref[...] = dg * g * (1.0 - g)

    gl_spec = pl.BlockSpec((None, TQ, 128), lambda b, i: (b, i, COL_GL // 128))
    return pl.pallas_call(
        body, grid=(B, S // TQ),
        in_specs=[gl_spec, _qtile0(), _qtile0(), _qtile0(), pl.BlockSpec((None, TQ, GW), lambda b, i: (b, i, 2))],
        out_specs=[_qtile0(), _qtile0(), _qtile0(), pl.BlockSpec((None, TQ, 128), lambda b, i: (b, i, 0))],
        out_shape=[_sds((B, S, GW), F32)] * 3 + [_sds((B, S, 128), F32)],
        compiler_params=_cp(("parallel", "parallel")), name="combine_bwd")(z3, o_cmp, o_slc, o_win, dycat3)


def assemble_dz(du, dv, da, dgt, dq_c, dq_s, dq_w, dd, dkvs, dgl):
    T = du.shape[0]

    def body(du_ref, dv_ref, da_ref, dgt_ref, dqc_ref, dqs_ref, dqw_ref, dd_ref, dgl_ref, *rest):
        kv_refs, o_ref = rest[:6], rest[6]
        o_ref[:, COL_U:COL_U + GW] = du_ref[...]
        o_ref[:, COL_V:COL_V + GW] = dv_ref[...]
        o_ref[:, COL_A:COL_A + GW] = da_ref[...]
        o_ref[:, COL_G:COL_G + GW] = dgt_ref[...]
        o_ref[:, COL_Q:COL_Q + GW] = (dqc_ref[...] + dqs_ref[...] + dqw_ref[...]).astype(BF16)
        o_ref[:, COL_D:COL_D + GW] = dd_ref[...].astype(BF16)
        for i, kv_ref in enumerate(kv_refs):
            o_ref[:, COL_KV + i * HEAD_DIM:COL_KV + (i + 1) * HEAD_DIM] = kv_ref[...].astype(BF16)
        o_ref[:, COL_GL:COL_GL + 128] = dgl_ref[...].astype(BF16)

    specs = [_rows(ROW_TILE, GW)] * 8 + [_rows(ROW_TILE, 128)] + [_rows(ROW_TILE, HEAD_DIM)] * 6
    return pl.pallas_call(body, grid=(T // ROW_TILE,), in_specs=specs, out_specs=_rows(ROW_TILE, ZW),
                          out_shape=_sds((T, ZW), BF16), compiler_params=_cp(("parallel",)),
                          name="assemble_dz")(du, dv, da, dgt, dq_c, dq_s, dq_w, dd, dgl, *dkvs)


def _my_pos():
    return lax.axis_index("x"), lax.axis_index("y"), lax.axis_index("c")


def _peer(k):
    x, y, c = _my_pos()
    return ((1 - x) if k & 4 else x, (1 - y) if k & 2 else y, (1 - c) if k & 1 else c)


def _index(pos):
    return 4 * pos[0] + 2 * pos[1] + pos[2]


_HBM = pl.BlockSpec(memory_space=pltpu.HBM)


_SEM = pl.BlockSpec(memory_space=pltpu.SEMAPHORE)
_EFFECT = pltpu.SideEffectType.DATAFLOW_SIDE_EFFECTING


def _exchange_copies(kinds, srcs, lands, send, recv):
    me = _index(_my_pos())
    out = []
    for a, kind in enumerate(kinds):
        for k in range(1, N_DEV):
            peer = _peer(k)
            if kind == "gather":
                r = srcs[a].shape[1]
                src, dst = srcs[a], lands[a].at[:, pl.ds(me * r, r), :]
            else:
                r = srcs[a].shape[1] // N_DEV
                src, dst = srcs[a].at[:, pl.ds(_index(peer) * r, r), :], lands[a].at[me]
            sem = a * (N_DEV - 1) + k - 1
            out.append(pltpu.make_async_remote_copy(src_ref=src, dst_ref=dst, send_sem=send.at[sem], recv_sem=recv.at[sem],
                                                    device_id=peer, device_id_type=MESH))
    return out


def _land_with_own(kind, src):
    me = _index(_my_pos())
    if kind == "gather":
        _, r, C = src.shape
        return lax.dynamic_update_slice(lax.empty((1, N_DEV * r, C), src.dtype), src, (0, me * r, 0))
    _, r8, C = src.shape
    r = r8 // N_DEV
    own = lax.dynamic_slice(src, (0, me * r, 0), (1, r, C))
    return lax.dynamic_update_slice(lax.empty((N_DEV, 1, r, C), src.dtype), own[None], (me, 0, 0, 0))


def exchange_start(kinds, srcs, name):
    n = len(srcs)
    lands = [_land_with_own(k, s) for k, s in zip(kinds, srcs)]

    def body(*refs):
        s, l = refs[:n], refs[n:2 * n]
        send, recv = refs[2 * n], refs[2 * n + 1]
        for cp in _exchange_copies(kinds, s, l, send, recv):
            cp.start()
        refs[-1][...] = jnp.zeros((8, 128), F32)

    hbm = [pltpu.HBM(a.shape, a.dtype) for a in srcs + lands]
    outs = pl.pallas_call(
        body, name=name,
        out_shape=(pltpu.SemaphoreType.DMA((n * (N_DEV - 1),)), pltpu.SemaphoreType.DMA((n * (N_DEV - 1),)), *hbm,
                   _sds((8, 128), F32)),
        in_specs=[_HBM] * (2 * n), out_specs=(_SEM, _SEM, *([_HBM] * (2 * n)), pl.BlockSpec(memory_space=pltpu.VMEM)),
        input_output_aliases={i: 2 + i for i in range(2 * n)},
        compiler_params=pltpu.CompilerParams(has_side_effects=_EFFECT),
    )(*[pltpu.with_memory_space_constraint(a, pltpu.HBM) for a in srcs + lands])
    return outs[0], outs[1], list(outs[2:2 + n]), list(outs[2 + n:2 + 2 * n]), outs[-1]


def exchange_wait(kinds, started, after, name):
    send, recv, srcs, lands, _ = started
    n = len(srcs)
    after = list(after) if isinstance(after, (list, tuple)) else [after]

    def body(*refs):
        s, l = refs[:n], refs[n:2 * n]
        for cp in _exchange_copies(kinds, s, l, refs[2 * n], refs[2 * n + 1]):
            cp.wait_send()
            cp.wait_recv()
        refs[-1][...] = jnp.zeros((8, 128), F32)

    outs = pl.pallas_call(
        body, name=name, out_shape=[pltpu.HBM(a.shape, a.dtype) for a in srcs + lands] + [_sds((8, 128), F32)],
        in_specs=[_HBM] * (2 * n) + [_SEM, _SEM] + [pl.BlockSpec(memory_space=pl.ANY)] * len(after),
        out_specs=[_HBM] * (2 * n) + [pl.BlockSpec(memory_space=pltpu.VMEM)],
        input_output_aliases={i: i for i in range(2 * n)},
        compiler_params=pltpu.CompilerParams(has_side_effects=_EFFECT),
    )(*srcs, *lands, send, recv, *after)
    return list(outs[n:2 * n]), outs[-1]


def sum_slots(lands, name):
    L = len(lands)
    _, _, r, C = lands[0].shape
    tr = _tile(r, 256, 16)

    def body(*refs):
        o_ref = refs[L]
        for l in range(L):
            @pl.when(pl.program_id(0) == l)
            def _(x_ref=refs[l]):
                acc = x_ref[0].astype(F32)
                for s in range(1, N_DEV):
                    acc = acc + x_ref[s].astype(F32)
                o_ref[...] = acc

    specs = [pl.BlockSpec((N_DEV, None, tr, C), lambda g, i, l=l: (0, 0, jnp.where(g == l, i, 0), 0)) for l in range(L)]
    return pl.pallas_call(
        body, grid=(L, r // tr), in_specs=specs,
        out_specs=pl.BlockSpec((None, tr, C), lambda g, i: (g, i, 0)), out_shape=_sds((L, r, C), F32),
        compiler_params=_cp(("arbitrary", "arbitrary")), name=name)(*lands)


def pack_flat(arrs):
    flat = jnp.concatenate([a.reshape(-1).astype(F32) for a in arrs])
    n = flat.shape[0]
    total = -(-n // 32768) * 32768
    return jnp.pad(flat, (0, total - n)).reshape(total // 128, 128)


def unpack_flat(flat, shapes):
    v = flat.reshape(-1)
    out, off = [], 0
    for s in shapes:
        n = int(np.prod(s))
        out.append(v[off:off + n].reshape(s))
        off += n
    return out


def adamw(w, g, m, v, name):
    shape = w.shape
    C = shape[-1]
    R = int(np.prod(shape)) // C
    tr = _tile(R, 128, 8)
    c1 = 1.0 - ADAM_B1 ** ADAM_STEP
    c2 = 1.0 - ADAM_B2 ** ADAM_STEP

    def body(w_ref, g_ref, m_ref, v_ref, d_ref, nm_ref, nv_ref):
        g = g_ref[...]
        m2 = ADAM_B1 * m_ref[...] + (1.0 - ADAM_B1) * g
        v2 = ADAM_B2 * v_ref[...] + (1.0 - ADAM_B2) * (g * g)
        nm_ref[...] = m2
        nv_ref[...] = v2
        d_ref[...] = -ADAM_LR * ((m2 / c1) / (jnp.sqrt(v2 / c2) + ADAM_EPS) + ADAM_WD * w_ref[...])

    spec = pl.BlockSpec((tr, C), lambda i: (i, 0))
    outs = pl.pallas_call(body, grid=(R // tr,), in_specs=[spec] * 4, out_specs=[spec] * 3,
                          out_shape=[_sds((R, C), F32)] * 3, compiler_params=_cp(("parallel",)), name=name)(
        w.reshape(R, C), g.reshape(R, C), m.reshape(R, C), v.reshape(R, C))
    return [o.reshape(shape) for o in outs]


def _bexp(sg_b):
    return jnp.repeat(sg_b.T, HEAD_DIM, axis=1)


def _block_diag(pool_w):
    out = jnp.zeros((GW, GW), F32)
    for i in range(4):
        out = out.at[i * 64:(i + 1) * 64, i * 64:(i + 1) * 64].set(pool_w[i])
    return out


def _cmp_rows(t):
    B, S, _ = t.shape
    t2 = t.reshape(B, S // CMP_STRIDE, CMP_STRIDE * HEAD_DIM)
    nxt = jnp.concatenate([t2[:, 1:], jnp.zeros_like(t2[:, :1])], axis=1)
    return jnp.concatenate([t2, nxt], axis=-1)


def _tile4(t):
    return jnp.tile(t, (1, 1, 4)).astype(BF16)


def kv_tiles(z):
    T = z.shape[0]

    def body(x_ref, cv_ref, ks_ref, vs_ref, kw_ref, vw_ref):
        x = x_ref[...]
        cv_ref[...] = x[:, :128]
        xb = x.astype(BF16)
        src = lax.broadcasted_iota(jnp.int32, (384, GW), 0)
        lane = lax.broadcasted_iota(jnp.int32, (384, GW), 1) & 63
        for i, o_ref in enumerate((ks_ref, vs_ref, kw_ref, vw_ref)):
            expand = (src == lane + 64 * (i + 2)).astype(BF16)
            o_ref[...] = _dot(xb, expand).astype(o_ref.dtype)

    return pl.pallas_call(
        body, grid=(T // ROW_TILE,), in_specs=[pl.BlockSpec((ROW_TILE, 384), lambda i: (i, COL_KV // 384))],
        out_specs=[_rows(ROW_TILE, 128)] + [_rows(ROW_TILE, GW)] * 4,
        out_shape=[_sds((T, 128), F32)] + [_sds((T, GW), BF16)] * 4,
        compiler_params=_cp(("parallel",)), name="kv_tiles")(z)


def layer_fwd(x, p, late, B, S):
    T = B * S
    sv = {"x0": x}
    h1, h1t = rms_fwd(x, p["g_pre_mix"], "rms_pre_mix")
    z = mm(h1, p["w_in"], name="mm_in", tn=2048)
    z3 = z.reshape(B, S, ZW)
    ya = mixa_fwd(z, p["sg_ln_g"], p["sg_w"], p["bexp"])
    yb, hc = mixb_fwd(z3, p["cv_w"], p["cv_b"], p["cv_ln_g"], p["cv_ln_b"], p["cv_pw"], p["cv_pw_b"])
    kcv, ks4, vs4, kw4, vw4 = [a.reshape(B, S, -1) for a in kv_tiles(z)]
    tbk, tbv = _cmp_rows(kcv[:, :, :HEAD_DIM]), _cmp_rows(kcv[:, :, HEAD_DIM:])
    kc, vc = cmp_kv_fwd(tbk, tbv, p["cmp_pos_k"], p["cmp_pos_v"], p["cmp_w1_k"], p["cmp_w2_k"], p["cmp_w1_v"], p["cmp_w2_v"])
    kc4, vc4 = _tile4(kc), _tile4(vc)
    o_cmp, selT = cmp_attn_fwd(z3, kc4, vc4)
    o_slc, lse_slc = attn_fwd("slc", z3, ks4, vs4, selT)
    o_win, lse_win = attn_fwd("win", z3, kw4, vw4, selT)
    yc = combine_fwd(z3, o_cmp, o_slc, o_win)
    yd = mixd_fwd(z3, p["pool_bd"], p["pool_scale"])
    ycat = jnp.concatenate([ya, yb.reshape(T, GW), yc.reshape(T, GW), yd.reshape(T, GW)], axis=-1)
    p.update(late(ycat))
    mix = mm(ycat, p["w_out"], name="mm_out")
    x1 = rms_post_fwd(x, mix, p["g_post_mix"], "rms_post_mix")
    h2, h2t = rms_fwd(x1, p["g_pre_ffn"], "rms_pre_ffn")
    gu4, a3 = ffn_up_fwd(h2, p["w_gu"])
    f = mm_kblocks(a3, p["w_down"], tb=False, name="mm_down", tm=1024)
    x2 = rms_post_fwd(x1, f, p["g_post_ffn"], "rms_post_ffn")
    sv.update(h1t=h1t, z=z, hc=hc, tbk=tbk, tbv=tbv, kc4=kc4, vc4=vc4, ks4=ks4, vs4=vs4, kw4=kw4, vw4=vw4, o_cmp=o_cmp, selT=selT,
              o_slc=o_slc, lse_slc=lse_slc, o_win=o_win, lse_win=lse_win, ycat=ycat, mix=mix, x1=x1, h2t=h2t, gu4=gu4, a3=a3, f=f)
    return x2, sv


def layer_bwd_ffn(dx2, p, sv, B, S):
    T = B * S
    gb, gs = {}, {}
    df, gs["g_post_ffn"] = rms_bwd(sv["f"], p["g_post_ffn"], dx2, None, BF16, "rms_post_ffn_bwd")
    dgu = ffn_down_dx(df, p["w_down"], sv["gu4"]).reshape(N_DEV, T, FFN_BLK)
    gb["w_down"] = mm(sv["a3"], df, ta=True, blk="m", out_dtype=BF16, name="mm_down_dw", tk=4096)
    dh2 = mm_kblocks(dgu, p["w_gu"], tb=True, name="mm_gu_dx", tm=512)
    gb["w_gu"] = mm(sv["h2t"], dgu, blk="n", out_dtype=BF16, name="mm_gu_dw", tk=4096)
    dx1, gs["g_pre_ffn"] = rms_bwd(sv["x1"], p["g_pre_ffn"], dh2, dx2, F32, "rms_pre_ffn_bwd")
    gb["w_gu"] = gb["w_gu"].reshape(1, N_DEV * D_MODEL, FFN_BLK)
    gb["w_down"] = gb["w_down"].reshape(1, FFN_HIDDEN, D_MODEL)
    return dx1, gb, gs


def layer_bwd_mix(dx1, p, sv, B, S):
    T = B * S
    gb, gs = {}, {}
    dmix, gs["g_post_mix"] = rms_bwd(sv["mix"], p["g_post_mix"], dx1, None, BF16, "rms_post_mix_bwd")
    dycat = mm(dmix, p["w_out"], tb=True, name="mm_out_dx")
    gb["w_out"] = mm(sv["ycat"], dmix, ta=True, out_dtype=BF16, name="mm_out_dw")
    dycat3 = dycat.reshape(B, S, D_MODEL)
    z = sv["z"]
    z3 = z.reshape(B, S, ZW)
    du, dv, gs["sg_w"], db, gs["sg_ln_g"] = mixa_bwd(z, dycat, p["sg_ln_g"], p["sg_w"], p["bexp"])
    gs["sg_b"] = db[:, :4].T
    (da, dgt, gs["cv_w"], gs["cv_b"], gs["cv_ln_g"], gs["cv_ln_b"], gpw, gs["cv_pw_b"]) = mixb_bwd(
        z3, sv["hc"], dycat3, p["cv_w"], p["cv_ln_g"], p["cv_ln_b"], p["cv_pw"])
    gb["cv_pw"] = gpw.astype(BF16)
    dd, dwbd, gs["pool_scale"] = mixd_bwd(z3, dycat3, p["pool_bd"], p["pool_scale"])
    gs["pool_w"] = jnp.stack([dwbd[i * 64:(i + 1) * 64, i * 64:(i + 1) * 64] for i in range(4)])
    do_c, do_s, do_w, dgl = combine_bwd(z3, sv["o_cmp"], sv["o_slc"], sv["o_win"], dycat3)
    dq_s, dks, dvs = attn_bwd("slc", z3, sv["ks4"], sv["vs4"], sv["selT"], sv["o_slc"], sv["lse_slc"], do_s)
    dq_w, dkw, dvw = attn_bwd("win", z3, sv["kw4"], sv["vw4"], sv["selT"], sv["o_win"], sv["lse_win"], do_w)
    dq_c, dkc, dvc = cmp_attn_bwd(z3, sv["kc4"], sv["vc4"], do_c)
    (dk2, dv2, gs["cmp_pos_k"], gs["cmp_pos_v"], gw1k, gs["cmp_w2_k"], gw1v, gs["cmp_w2_v"]) = cmp_kv_bwd(
        sv["tbk"], sv["tbv"], p["cmp_pos_k"], p["cmp_pos_v"], p["cmp_w1_k"], p["cmp_w2_k"], p["cmp_w1_v"], p["cmp_w2_v"],
        dkc, dvc)
    gb["cmp_w1_k"], gb["cmp_w1_v"] = gw1k.astype(BF16), gw1v.astype(BF16)
    dkvs = [t.reshape(T, HEAD_DIM) for t in (dk2, dv2, dks, dvs, dkw, dvw)]
    dz = assemble_dz(du, dv, da.reshape(T, GW), dgt.reshape(T, GW), dq_c.reshape(T, GW), dq_s.reshape(T, GW),
                     dq_w.reshape(T, GW), dd.reshape(T, GW), dkvs, dgl.reshape(T, 128))
    dh1 = mm(dz, p["w_in"], tb=True, name="mm_in_dx", tk=2048)
    gb["w_in"] = mm(sv["h1t"], dz, out_dtype=BF16, name="mm_in_dw", tk=4096, tn=512)
    dx0, gs["g_pre_mix"] = rms_bwd(sv["x0"], p["g_pre_mix"], dh1, dx1, F32, "rms_pre_mix_bwd")
    return dx0, gb, gs


SMALL = ["g_pre_mix", "g_post_mix", "g_pre_ffn", "g_post_ffn", "sg_ln_g", "sg_w", "sg_b", "cv_w", "cv_b", "cv_ln_g", "cv_ln_b",
         "cv_pw_b", "cmp_pos_k", "cmp_pos_v", "cmp_w2_k", "cmp_w2_v", "pool_w", "pool_scale"]
BIG = ["w_in", "w_out", "w_gu", "w_down", "cmp_w1_k", "cmp_w1_v", "cv_pw"]
NAMES = ["g_pre_mix", "g_post_mix", "g_pre_ffn", "g_post_ffn", "w_in", "sg_ln_g", "sg_w", "sg_b", "cv_w", "cv_b", "cv_ln_g",
         "cv_ln_b", "cv_pw", "cv_pw_b", "cmp_pos_k", "cmp_pos_v", "cmp_w1_k", "cmp_w2_k", "cmp_w1_v", "cmp_w2_v", "pool_w",
         "pool_scale", "w_out", "ffn_w_gu", "ffn_w_down"]


def kernel(x, g_pre_mix, g_post_mix, g_pre_ffn, g_post_ffn, w_in, sg_ln_g, sg_w, sg_b, cv_w, cv_b, cv_ln_g, cv_ln_b, cv_pw, cv_pw_b, cmp_pos_k, cmp_pos_v, cmp_w1_k, cmp_w2_k, cmp_w1_v, cmp_w2_v, pool_w, pool_scale, w_out, ffn_w_gu, ffn_w_down, loss_target, m_g_pre_mix, m_g_post_mix, m_g_pre_ffn, m_g_post_ffn, m_w_in, m_sg_ln_g, m_sg_w, m_sg_b, m_cv_w, m_cv_b, m_cv_ln_g, m_cv_ln_b, m_cv_pw, m_cv_pw_b, m_cmp_pos_k, m_cmp_pos_v, m_cmp_w1_k, m_cmp_w2_k, m_cmp_w1_v, m_cmp_w2_v, m_pool_w, m_pool_scale, m_w_out, m_ffn_w_gu, m_ffn_w_down, v_g_pre_mix, v_g_post_mix, v_g_pre_ffn, v_g_post_ffn, v_w_in, v_sg_ln_g, v_sg_w, v_sg_b, v_cv_w, v_cv_b, v_cv_ln_g, v_cv_ln_b, v_cv_pw, v_cv_pw_b, v_cmp_pos_k, v_cmp_pos_v, v_cmp_w1_k, v_cmp_w2_k, v_cmp_w1_v, v_cmp_w2_v, v_pool_w, v_pool_scale, v_w_out, v_ffn_w_gu, v_ffn_w_down):
    args = dict(locals())
    W = {n: args[n] for n in NAMES}
    M = {n: args["m_" + n] for n in NAMES}
    V = {n: args["v_" + n] for n in NAMES}
    B, S, _ = x.shape
    T = B * S
    L = w_in.shape[0]
    me = _index(_my_pos())
    cpd = GW // N_DEV

    shards = {"w_in": lambda l: pack_cols(w_in[l]).astype(BF16), "w_out": lambda l: w_out[l].astype(BF16),
              "w_gu": lambda l: ffn_w_gu[l].astype(BF16), "w_down": lambda l: ffn_w_down[l].astype(BF16),
              "cmp_w1_k": lambda l: cmp_w1_k[l].astype(BF16), "cmp_w1_v": lambda l: cmp_w1_v[l].astype(BF16),
              "cv_pw": lambda l: cv_pw[l].astype(BF16), "cv_w": lambda l: cv_w[l].T}
    early, later = ["w_in", "cmp_w1_k", "cmp_w1_v", "cv_pw", "cv_w"], ["w_out", "w_gu", "w_down"]

    def start_gather(names, l, tag, srcs=None, behind=None):
        srcs = list(srcs) if srcs is not None else [shards[n](l)[None] for n in names]
        if behind is not None:
            srcs[0] = srcs[0] + behind[0, 0].astype(srcs[0].dtype)
        return exchange_start(["gather"] * len(names), srcs, "gather_%s_start_%d" % (tag, l))

    def wait_gather(names, started, after, l, tag):
        arrived, done = exchange_wait(["gather"] * len(names), started, after, "gather_%s_wait_%d" % (tag, l))
        full = {n: a[0] for n, a in zip(names, arrived)}
        full["done"] = done
        if "w_gu" in full:
            full["w_gu"] = full["w_gu"].reshape(N_DEV, D_MODEL, FFN_BLK)
            full["w_down"] = full["w_down"].reshape(4, FFN_BLK, D_MODEL)
        if "cv_w" in full:
            full["cv_w"] = full["cv_w"].T
        return full

    def layer_params(l, full):
        p = dict(full)
        for n in ("g_pre_mix", "g_post_mix", "g_pre_ffn", "g_post_ffn", "sg_ln_g", "cv_b", "cv_ln_g", "cv_ln_b", "cv_pw_b",
                  "pool_scale"):
            p[n] = W[n][l][None, :]
        p["sg_w"] = sg_w[l]
        p["bexp"] = _bexp(sg_b[l])
        p["cmp_pos_k"] = cmp_pos_k[l].reshape(1, 2048)
        p["cmp_pos_v"] = cmp_pos_v[l].reshape(1, 2048)
        p["cmp_w2_k"], p["cmp_w2_v"] = cmp_w2_k[l], cmp_w2_v[l]
        p["pool_bd"] = _block_diag(pool_w[l])
        return p

    xs = x.reshape(T, D_MODEL)
    params, saved = [], []
    early_st = start_gather(early, 0, "early")
    ahead = {}
    for l in range(L):
        later_srcs = [shards[n](l)[None] for n in later]
        full = wait_gather(early, early_st, later_srcs + ([xs] if l > 0 else []), l, "early")
        later_st = start_gather(later, l, "later", srcs=later_srcs, behind=full["done"])
        p = layer_params(l, full)
        p["g_pre_mix"] = p["g_pre_mix"] + later_st[4][0, 0]

        def late(after, l=l, st=later_st, p=p):
            got = wait_gather(later, st, after, l, "later")
            if l + 1 < L:
                ahead["early"] = start_gather(early, l + 1, "early", behind=got["done"])
                got["g_post_mix"] = p["g_post_mix"] + ahead["early"][4][0, 0]
            return got

        xs, sv = layer_fwd(xs, p, late, B, S)
        params.append(p)
        saved.append(sv)
        early_st = ahead.get("early")
    dy, lpart = loss_fwd_bwd(xs, loss_target.reshape(T, D_MODEL))
    loss = lax.psum(lpart[0, 0], ("x", "y", "c"))

    ffn_big, mix_big = ["w_gu", "w_down"], ["w_in", "w_out", "cmp_w1_k", "cmp_w1_v", "cv_pw"]
    ffn_kinds, mix_kinds = ["scatter"] * len(ffn_big), ["scatter"] * len(mix_big) + ["gather"]
    pending, token = [], None
    for l in reversed(range(L)):
        p = dict(params[l])
        if token is not None:
            p["g_post_ffn"] = p["g_post_ffn"] + token[0, 0]
        dy, gb_ffn, gs = layer_bwd_ffn(dy, p, saved[l], B, S)
        st_ffn = exchange_start(ffn_kinds, [gb_ffn[n] for n in ffn_big], "scatter_ffn_start_%d" % l)
        p["g_post_mix"] = p["g_post_mix"] + st_ffn[4][0, 0]
        dy, gb_mix, gs_mix = layer_bwd_mix(dy, p, saved[l], B, S)
        gs.update(gs_mix)
        small_shapes = [tuple(gs[n].shape) for n in SMALL]
        st_mix = exchange_start(mix_kinds, [gb_mix[n][None] for n in mix_big] + [pack_flat([gs[n] for n in SMALL])[None]],
                                "scatter_mix_start_%d" % l)
        token = st_mix[4]
        pending.append((l, st_ffn, st_mix))
    grad_x = dy.reshape(B, S, D_MODEL)

    delta, new_m, new_v = {}, {}, {}
    lands = {l: {} for l in range(L)}
    for l, st_ffn, _ in pending:
        lands[l].update(zip(ffn_big, exchange_wait(ffn_kinds, st_ffn, token, "scatter_ffn_wait_%d" % l)[0]))
    grads = {}
    for n, name in zip(ffn_big, ("ffn_w_gu", "ffn_w_down")):
        grads[name] = sum_slots([lands[l][n] for l in range(L)], "sum_" + n)
        delta[name], new_m[name], new_v[name] = adamw(W[name], grads[name], M[name], V[name], "adamw_" + name)
    for l, _, st_mix in pending:
        lands[l].update(zip(mix_big + ["small"],
                            exchange_wait(mix_kinds, st_mix, delta["ffn_w_down"], "scatter_mix_wait_%d" % l)[0]))
    for n in mix_big:
        grads[n] = sum_slots([lands[l][n] for l in range(L)], "sum_" + n)
    grads["w_in"] = unpack_cols(grads["w_in"])
    rows = lands[0]["small"].shape[1] // N_DEV
    reduced = sum_slots([lands[l]["small"].reshape(N_DEV, 1, rows, 128) for l in range(L)], "sum_small")
    per_layer = [unpack_flat(reduced[l], small_shapes) for l in range(L)]
    for i, n in enumerate(SMALL):
        g = jnp.stack([per_layer[l][i] for l in range(L)])
        grads[n] = g.reshape(W[n].shape) if n != "cv_w" else g
    grads["cv_w"] = lax.dynamic_slice(grads["cv_w"], (0, 0, me * cpd), (L, CONV_WIDTH, cpd))

    for n in mix_big:
        delta[n], new_m[n], new_v[n] = adamw(W[n], grads[n], M[n], V[n], "adamw_" + n)
    shapes = [W[n].shape for n in SMALL]
    packed = adamw(pack_flat([W[n] for n in SMALL]), pack_flat([grads[n] for n in SMALL]),
                   pack_flat([M[n] for n in SMALL]), pack_flat([V[n] for n in SMALL]), "adamw_small")
    for out, flat in zip((delta, new_m, new_v), packed):
        for n, a in zip(SMALL, unpack_flat(flat, shapes)):
            out[n] = a

    return (loss, grad_x, *[grads[n] for n in NAMES], *[delta[n] for n in NAMES], *[new_m[n] for n in NAMES],
            *[new_v[n] for n in NAMES])
```

```python
import numpy as np
import jax
import jax.numpy as jnp
from jax import lax
from jax.experimental import pallas as pl
from jax.experimental.pallas import tpu as pltpu

F32 = jnp.float32
BF16 = jnp.bfloat16
HI = lax.Precision.HIGHEST

D_MODEL = 1024
GW = 256
HEAD_DIM = 64
ZW = 2048
SG_CHUNK = 128
CONV_WIDTH = 31
CONV_PAD = 32
CMP_STRIDE = 16
N_CMP = 128
SLC_BLOCK_SHIFT = 6
N_SLC = 32
SLC_TOPK = 8
WIN = 512
NEG = -1e30
FORCE_BONUS = 1e4
RMS_EPS = 1e-6
LN_EPS = 1e-5
FFN_HIDDEN = 2816
N_DEV = 8
FFN_BLK = 2 * FFN_HIDDEN // N_DEV
TQ = 256
ROW_TILE = 512
CONV_TILE = 256
VMEM_LIMIT = 56 * 1024 * 1024
MESH = pl.DeviceIdType.MESH

ADAM_LR, ADAM_B1, ADAM_B2, ADAM_EPS, ADAM_WD, ADAM_STEP = 0.001, 0.9, 0.999, 1e-08, 0.01, 10

COL_U, COL_V, COL_A, COL_G, COL_Q, COL_D, COL_KV, COL_GL = 0, 256, 512, 768, 1024, 1280, 1536, 1920


def _sds(shape, dtype):
    return jax.ShapeDtypeStruct(shape, dtype)


def _cp(sem=None):
    return pltpu.CompilerParams(dimension_semantics=sem, vmem_limit_bytes=VMEM_LIMIT)


def _tile(n, target, q=128):
    best = None
    for t in range(q, min(n, target) + 1, q):
        if n % t == 0:
            best = t
    return best or n


def _full(shape):
    nd = len(shape)
    return pl.BlockSpec(shape, lambda *_: (0,) * nd)


def _sigmoid(x):
    return jax.nn.sigmoid(x)


def _dot(a, b):
    return jnp.dot(a, b, preferred_element_type=F32)


def _dot_nt(a, b):
    return lax.dot_general(a, b, (((1,), (1,)), ((), ())), preferred_element_type=F32)


def _dot_tn(a, b):
    return lax.dot_general(a, b, (((0,), (0,)), ((), ())), preferred_element_type=F32)


def _lane_head(width=GW):
    return lax.shift_right_logical(lax.broadcasted_iota(jnp.int32, (1, width), 1), 6)


def _fold_heads(x):
    return x + pltpu.roll(x, 64, 1) + pltpu.roll(x, 128, 1) + pltpu.roll(x, 192, 1)


def pack_cols(w):
    pad = jnp.zeros(w.shape[:-1] + (ZW - 1932,), w.dtype)
    return jnp.concatenate([w[..., :1280], w[..., 1676:1932], w[..., 1280:1664], w[..., 1664:1676], pad], axis=-1)


def unpack_cols(wp):
    return jnp.concatenate([wp[..., :1280], wp[..., 1536:1920], wp[..., 1920:1932], wp[..., 1280:1536]], axis=-1)


def mm(a, b, *, ta=False, tb=False, blk=None, out_dtype=F32, name, tm=1024, tn=1024, tk=1024):
    a_dims = ("k", "m") if ta else ("m", "k")
    b_dims = ("n", "k") if tb else ("k", "n")
    a3, b3, o3 = blk in a_dims and blk is not None, blk in b_dims and blk is not None, blk in ("m", "n")
    size = {}
    size[a_dims[0]], size[a_dims[1]] = a.shape[-2:]
    size[b_dims[0]], size[b_dims[1]] = b.shape[-2:]
    nb = a.shape[0] if a3 else (b.shape[0] if b3 else 1)
    tile = {"m": _tile(size["m"], tm), "n": _tile(size["n"], tn), "k": _tile(size["k"], tk)}
    grid = {d: size[d] // tile[d] for d in "mnk"}
    if blk is not None:
        tile[blk] = size[blk]
        grid[blk] = nb
    nk = grid["k"]

    def spec(dims, is3):
        def im(i, j, k):
            g = {"m": i, "n": j, "k": k}
            idx = tuple(0 if d == blk else g[d] for d in dims)
            return ((g[blk],) + idx) if is3 else idx
        shape = (tile[dims[0]], tile[dims[1]])
        return pl.BlockSpec(((None,) + shape) if is3 else shape, im)

    dn = (((0 if ta else 1,), (1 if tb else 0,)), ((), ()))

    def partial(a_ref, b_ref):
        return lax.dot_general(a_ref[...].astype(BF16), b_ref[...].astype(BF16), dn, preferred_element_type=F32)

    def body_single(a_ref, b_ref, o_ref):
        o_ref[...] = partial(a_ref, b_ref).astype(o_ref.dtype)

    def body_acc(a_ref, b_ref, o_ref, acc):
        k = pl.program_id(2)

        @pl.when(k == 0)
        def _():
            acc[...] = partial(a_ref, b_ref)

        @pl.when((k > 0) & (k < nk - 1))
        def _():
            acc[...] += partial(a_ref, b_ref)

        @pl.when(k == nk - 1)
        def _():
            o_ref[...] = (acc[...] + partial(a_ref, b_ref)).astype(o_ref.dtype)

    oshape = ((nb,) if o3 else ()) + (size["m"], size["n"])
    return pl.pallas_call(
        body_single if nk == 1 else body_acc, grid=(grid["m"], grid["n"], nk),
        in_specs=[spec(a_dims, a3), spec(b_dims, b3)], out_specs=spec(("m", "n"), o3),
        out_shape=_sds(oshape, out_dtype),
        scratch_shapes=[] if nk == 1 else [pltpu.VMEM((tile["m"], tile["n"]), F32)],
        compiler_params=_cp(("parallel", "parallel", "arbitrary")), name=name)(a, b)


def mm_kblocks(a, b, *, tb, name, tm):
    nb, M, kb = a.shape
    N = b.shape[1] if tb else b.shape[2]
    dn = (((1,), (1 if tb else 0,)), ((), ()))

    def body(a_ref, b_ref, o_ref):
        acc = lax.dot_general(a_ref[0], b_ref[0], dn, preferred_element_type=F32)
        for j in range(1, nb):
            acc = acc + lax.dot_general(a_ref[j], b_ref[j], dn, preferred_element_type=F32)
        o_ref[...] = acc

    return pl.pallas_call(
        body, grid=(M // tm,), in_specs=[pl.BlockSpec((nb, tm, kb), lambda i: (0, i, 0)), _full(b.shape)],
        out_specs=_rows(tm, N), out_shape=_sds((M, N), F32),
        compiler_params=_cp(("parallel",)), name=name)(a, b)


def _rows(tm, width):
    return pl.BlockSpec((tm, width), lambda i: (i, 0))


def rms_fwd(x, g, name):
    T = x.shape[0]

    def body(x_ref, g_ref, h_ref, ht_ref):
        x = x_ref[...]
        r = lax.rsqrt(jnp.mean(x * x, axis=-1, keepdims=True) + RMS_EPS)
        h = (x * r) * g_ref[...]
        h_ref[...] = h.astype(h_ref.dtype)
        ht_ref[...] = h.T.astype(ht_ref.dtype)

    return pl.pallas_call(body, grid=(T // ROW_TILE,), in_specs=[_rows(ROW_TILE, D_MODEL), _full((1, D_MODEL))],
                          out_specs=[_rows(ROW_TILE, D_MODEL), pl.BlockSpec((D_MODEL, ROW_TILE), lambda i: (0, i))],
                          out_shape=[_sds((T, D_MODEL), BF16), _sds((D_MODEL, T), BF16)],
                          compiler_params=_cp(("parallel",)), name=name)(x, g)


def rms_post_fwd(xres, m, g, name):
    T = m.shape[0]

    def body(x_ref, m_ref, g_ref, o_ref):
        m = m_ref[...]
        r = lax.rsqrt(jnp.mean(m * m, axis=-1, keepdims=True) + RMS_EPS)
        o_ref[...] = x_ref[...] + (m * r) * g_ref[...]

    return pl.pallas_call(body, grid=(T // ROW_TILE,),
                          in_specs=[_rows(ROW_TILE, D_MODEL), _rows(ROW_TILE, D_MODEL), _full((1, D_MODEL))],
                          out_specs=_rows(ROW_TILE, D_MODEL), out_shape=_sds((T, D_MODEL), F32),
                          compiler_params=_cp(("parallel",)), name=name)(xres, m, g)


def rms_bwd(m, g, dy, dres, out_dtype, name):
    T = m.shape[0]
    has_res = dres is not None

    def body(*refs):
        if has_res:
            m_ref, g_ref, dy_ref, dres_ref, dm_ref, dg_ref = refs
        else:
            m_ref, g_ref, dy_ref, dm_ref, dg_ref = refs
        m = m_ref[...]
        dy = dy_ref[...].astype(F32)
        r = lax.rsqrt(jnp.mean(m * m, axis=-1, keepdims=True) + RMS_EPS)
        n = m * r
        dn = dy * g_ref[...]
        dm = r * (dn - n * jnp.mean(dn * n, axis=-1, keepdims=True))
        if has_res:
            dm = dm + dres_ref[...]
        dm_ref[...] = dm.astype(dm_ref.dtype)

        @pl.when(pl.program_id(0) == 0)
        def _():
            dg_ref[...] = jnp.zeros_like(dg_ref)

        dg_ref[...] += jnp.sum(dy * n, axis=0, keepdims=True)

    ins = [m, g, dy] + ([dres] if has_res else [])
    specs = [_rows(ROW_TILE, D_MODEL), _full((1, D_MODEL)), _rows(ROW_TILE, D_MODEL)] + ([_rows(ROW_TILE, D_MODEL)] if has_res else [])
    return pl.pallas_call(body, grid=(T // ROW_TILE,), in_specs=specs,
                          out_specs=[_rows(ROW_TILE, D_MODEL), _full((1, D_MODEL))],
                          out_shape=[_sds((T, D_MODEL), out_dtype), _sds((1, D_MODEL), F32)],
                          compiler_params=_cp(("arbitrary",)), name=name)(*ins)


def loss_fwd_bwd(y, tgt):
    T = y.shape[0]

    def body(y_ref, t_ref, dy_ref, l_ref):
        e = y_ref[...] - t_ref[...]
        dy_ref[...] = e * (1.0 / D_MODEL)

        @pl.when(pl.program_id(0) == 0)
        def _():
            l_ref[...] = jnp.zeros_like(l_ref)

        l_ref[...] += jnp.full(l_ref.shape, 0.5 * jnp.sum(jnp.mean(e * e, axis=-1, keepdims=True)), F32)

    return pl.pallas_call(body, grid=(T // ROW_TILE,), in_specs=[_rows(ROW_TILE, D_MODEL)] * 2,
                          out_specs=[_rows(ROW_TILE, D_MODEL), _full((8, 128))],
                          out_shape=[_sds((T, D_MODEL), F32), _sds((8, 128), F32)],
                          compiler_params=_cp(("arbitrary",)), name="loss")(y, tgt)


FFN_TILE = 2048


def _gu_spec():
    return pl.BlockSpec((2, None, FFN_TILE, FFN_BLK), lambda i, j: (0, j, i, 0))


def ffn_up_fwd(h, w_gu):
    T = h.shape[0]

    def body(h_ref, wg_ref, wu_ref, gu_ref, a_ref):
        h = h_ref[...]
        gate = _dot(h, wg_ref[...])
        up = _dot(h, wu_ref[...])
        gu_ref[0] = gate.astype(gu_ref.dtype)
        gu_ref[1] = up.astype(gu_ref.dtype)
        a_ref[...] = (gate * _sigmoid(gate) * up).astype(a_ref.dtype)

    return pl.pallas_call(
        body, grid=(T // FFN_TILE, 4),
        in_specs=[pl.BlockSpec((FFN_TILE, D_MODEL), lambda i, j: (i, 0)),
                  pl.BlockSpec((None, D_MODEL, FFN_BLK), lambda i, j: (j, 0, 0)),
                  pl.BlockSpec((None, D_MODEL, FFN_BLK), lambda i, j: (j + 4, 0, 0))],
        out_specs=[_gu_spec(), pl.BlockSpec((None, FFN_TILE, FFN_BLK), lambda i, j: (j, i, 0))],
        out_shape=[_sds((2, 4, T, FFN_BLK), BF16), _sds((4, T, FFN_BLK), BF16)],
        compiler_params=_cp(("parallel", "parallel")), name="ffn_up_fwd")(h, w_gu, w_gu)


def ffn_down_dx(df, w_down, gu4):
    T = df.shape[0]

    def body(df_ref, w_ref, gu_ref, d_ref):
        da = _dot_nt(df_ref[...], w_ref[...])
        gate, up = gu_ref[0].astype(F32), gu_ref[1].astype(F32)
        sg = _sigmoid(gate)
        d_ref[0] = (da * up * (sg * (1.0 + gate * (1.0 - sg)))).astype(d_ref.dtype)
        d_ref[1] = (da * (gate * sg)).astype(d_ref.dtype)

    return pl.pallas_call(
        body, grid=(T // FFN_TILE, 4),
        in_specs=[pl.BlockSpec((FFN_TILE, D_MODEL), lambda i, j: (i, 0)),
                  pl.BlockSpec((None, FFN_BLK, D_MODEL), lambda i, j: (j, 0, 0)), _gu_spec()],
        out_specs=_gu_spec(), out_shape=_sds((2, 4, T, FFN_BLK), BF16),
        compiler_params=_cp(("parallel", "parallel")), name="ffn_down_dx")(df, w_down, gu4)


def _zcol(tm, col):
    return pl.BlockSpec((tm, GW), lambda i: (i, col // GW))


def _sg_common(v, g):
    mu = jnp.mean(v, axis=-1, keepdims=True)
    xc = v - mu
    rstd = lax.rsqrt(jnp.mean(xc * xc, axis=-1, keepdims=True) + LN_EPS)
    vhat = xc * rstd
    return vhat, rstd, vhat * g


def _tril_weights(w_ref):
    tri = lax.broadcasted_iota(jnp.int32, (SG_CHUNK, SG_CHUNK), 0) >= lax.broadcasted_iota(jnp.int32, (SG_CHUNK, SG_CHUNK), 1)
    return tri, [jnp.where(tri, w_ref[h], 0.0).astype(BF16) for h in range(4)]


def mixa_fwd(z, ln_g, w, bexp):
    T = z.shape[0]
    nch = ROW_TILE // SG_CHUNK

    def body(u_ref, v_ref, g_ref, w_ref, be_ref, y_ref):
        _, _, vln = _sg_common(v_ref[...], g_ref[...])
        vb = vln.astype(BF16)
        head = _lane_head()
        _, wh = _tril_weights(w_ref)
        for c in range(nch):
            rows = slice(c * SG_CHUNK, (c + 1) * SG_CHUNK)
            sv = be_ref[...]
            for h in range(4):
                sv = sv + jnp.where(head == h, _dot(wh[h], vb[rows]), 0.0)
            y_ref[rows, :] = (u_ref[rows, :] * sv).astype(y_ref.dtype)

    return pl.pallas_call(body, grid=(T // ROW_TILE,),
                          in_specs=[_zcol(ROW_TILE, COL_U), _zcol(ROW_TILE, COL_V), _full((1, GW)), _full((4, SG_CHUNK, SG_CHUNK)),
                                    _full((SG_CHUNK, GW))],
                          out_specs=_rows(ROW_TILE, GW), out_shape=_sds((T, GW), BF16),
                          compiler_params=_cp(("parallel",)), name="mixa_fwd")(z, z, ln_g, w, bexp)


def mixa_bwd(z, dycat, ln_g, w, bexp):
    T = z.shape[0]
    nch = ROW_TILE // SG_CHUNK
    nsteps = T // ROW_TILE

    def body(u_ref, v_ref, dy_ref, g_ref, w_ref, be_ref, du_ref, dv_ref, dw_ref, db_ref, dg_ref, dbe_acc):
        step = pl.program_id(0)

        @pl.when(step == 0)
        def _():
            dw_ref[...] = jnp.zeros_like(dw_ref)
            dg_ref[...] = jnp.zeros_like(dg_ref)
            dbe_acc[...] = jnp.zeros_like(dbe_acc)

        g = g_ref[...]
        vhat, rstd, vln = _sg_common(v_ref[...], g)
        vb = vln.astype(BF16)
        head = _lane_head()
        tri, wh = _tril_weights(w_ref)
        dgsum = jnp.zeros((1, GW), F32)
        for c in range(nch):
            rows = slice(c * SG_CHUNK, (c + 1) * SG_CHUNK)
            sv = be_ref[...]
            for h in range(4):
                sv = sv + jnp.where(head == h, _dot(wh[h], vb[rows]), 0.0)
            dy = dy_ref[rows, :]
            du_ref[rows, :] = (dy * sv).astype(du_ref.dtype)
            dsv = dy * u_ref[rows, :]
            dbe_acc[...] += dsv
            dvln = jnp.zeros((SG_CHUNK, GW), F32)
            for h in range(4):
                dsvm = jnp.where(head == h, dsv, 0.0).astype(BF16)
                dw_ref[h] += _dot_nt(dsvm, vb[rows])
                dvln = dvln + _dot_tn(wh[h], dsvm)
            vh = vhat[rows]
            dgsum = dgsum + jnp.sum(dvln * vh, axis=0, keepdims=True)
            dvhat = dvln * g
            dv = rstd[rows] * (dvhat - jnp.mean(dvhat, axis=-1, keepdims=True) - vh * jnp.mean(dvhat * vh, axis=-1, keepdims=True))
            dv_ref[rows, :] = dv.astype(dv_ref.dtype)
        dg_ref[...] += dgsum

        @pl.when(step == nsteps - 1)
        def _():
            for h in range(4):
                dw_ref[h] = jnp.where(tri, dw_ref[h], 0.0)
            fold = (lax.shift_right_logical(lax.broadcasted_iota(jnp.int32, (GW, 128), 0), 6)
                    == lax.broadcasted_iota(jnp.int32, (GW, 128), 1)).astype(F32)
            db_ref[...] = jnp.dot(dbe_acc[...], fold, precision=HI, preferred_element_type=F32)

    return pl.pallas_call(
        body, grid=(nsteps,),
        in_specs=[_zcol(ROW_TILE, COL_U), _zcol(ROW_TILE, COL_V), pl.BlockSpec((ROW_TILE, GW), lambda i: (i, 0)),
                  _full((1, GW)), _full((4, SG_CHUNK, SG_CHUNK)), _full((SG_CHUNK, GW))],
        out_specs=[_rows(ROW_TILE, GW), _rows(ROW_TILE, GW), _full((4, SG_CHUNK, SG_CHUNK)), _full((SG_CHUNK, 128)), _full((1, GW))],
        out_shape=[_sds((T, GW), BF16), _sds((T, GW), BF16), _sds((4, SG_CHUNK, SG_CHUNK), F32), _sds((SG_CHUNK, 128), F32),
                   _sds((1, GW), F32)],
        scratch_shapes=[pltpu.VMEM((SG_CHUNK, GW), F32)],
        compiler_params=_cp(("arbitrary",)), name="mixa_bwd")(z, z, dycat, ln_g, w, bexp)


def _seq(S, col):
    return pl.BlockSpec((None, S, GW), lambda b: (b, 0, col // GW))


def _taps(buf, r0, offsets):
    by_phase = {}
    for k, off in enumerate(offsets):
        by_phase.setdefault(off % 8, []).append((k, off))
    for phase, items in sorted(by_phase.items()):
        span = max(off for _, off in items) - phase
        win = buf[pl.ds(r0 + phase, CONV_TILE + span), :]
        for k, off in items:
            yield k, win[off - phase:off - phase + CONV_TILE]


_CONV_FWD_OFFSETS = [CONV_PAD - (CONV_WIDTH - 1) + k for k in range(CONV_WIDTH)]
_CONV_BWD_OFFSETS = [(CONV_WIDTH - 1) - k for k in range(CONV_WIDTH)]


def _conv(pad, r0, cw_ref, cb):
    acc = jnp.zeros((CONV_TILE, GW), F32) + cb
    for k, rows in _taps(pad, r0, _CONV_FWD_OFFSETS):
        acc = acc + cw_ref[k:k + 1, :] * rows
    return acc


def _conv_ln(acc, lg, lb):
    mu = jnp.mean(acc, axis=-1, keepdims=True)
    xc = acc - mu
    rstd = lax.rsqrt(jnp.mean(xc * xc, axis=-1, keepdims=True) + LN_EPS)
    hhat = xc * rstd
    return hhat, rstd, hhat * lg + lb


def mixb_fwd(z3, cw, cb, lg, lb, pw, pwb):
    B, S, _ = z3.shape

    def body(a_ref, gt_ref, cw_ref, cb_ref, lg_ref, lb_ref, pw_ref, pwb_ref, y_ref, hc_ref, pad):
        pad[0:CONV_PAD, :] = jnp.zeros((CONV_PAD, GW), F32)
        pad[CONV_PAD:CONV_PAD + S, :] = a_ref[...] * _sigmoid(gt_ref[...])
        pwv = pw_ref[...].astype(BF16)
        for r0 in range(0, S, CONV_TILE):
            hc = _conv(pad, r0, cw_ref, cb_ref[...])
            hc_ref[r0:r0 + CONV_TILE, :] = hc
            _, _, ln = _conv_ln(hc, lg_ref[...], lb_ref[...])
            s = ln * _sigmoid(ln)
            y_ref[r0:r0 + CONV_TILE, :] = (_dot(s.astype(BF16), pwv) + pwb_ref[...]).astype(y_ref.dtype)

    seq_out = pl.BlockSpec((None, S, GW), lambda b: (b, 0, 0))
    return pl.pallas_call(
        body, grid=(B,),
        in_specs=[_seq(S, COL_A), _seq(S, COL_G), _full((CONV_WIDTH, GW)), _full((1, GW)), _full((1, GW)), _full((1, GW)),
                  _full((GW, GW)), _full((1, GW))],
        out_specs=[seq_out, seq_out], out_shape=[_sds((B, S, GW), BF16), _sds((B, S, GW), F32)],
        scratch_shapes=[pltpu.VMEM((S + CONV_PAD, GW), F32)],
        compiler_params=_cp(("parallel",)), name="mixb_fwd")(z3, z3, cw, cb, lg, lb, pw, pwb)


def mixb_bwd(z3, hc3, dycat3, cw, lg, lb, pw):
    B, S, _ = z3.shape

    def body(a_ref, gt_ref, hc_ref, dy_ref, cw_ref, lg_ref, lb_ref, pw_ref,
             da_ref, dgt_ref, dcw_ref, dcb_ref, dlg_ref, dlb_ref, dpw_ref, dpwb_ref, pad, dpad, dcw_acc):
        @pl.when(pl.program_id(0) == 0)
        def _():
            for r in (dcb_ref, dlg_ref, dlb_ref, dpw_ref, dpwb_ref, dcw_acc):
                r[...] = jnp.zeros_like(r)

        pad[0:CONV_PAD, :] = jnp.zeros((CONV_PAD, GW), F32)
        pad[CONV_PAD:CONV_PAD + S, :] = a_ref[...] * _sigmoid(gt_ref[...])
        dpad[S:S + CONV_PAD, :] = jnp.zeros((CONV_PAD, GW), F32)
        pwv = pw_ref[...].astype(BF16)
        lg = lg_ref[...]
        for r0 in range(0, S, CONV_TILE):
            hhat, rstd, ln = _conv_ln(hc_ref[r0:r0 + CONV_TILE, :], lg, lb_ref[...])
            sg = _sigmoid(ln)
            s = ln * sg
            dy = dy_ref[r0:r0 + CONV_TILE, :]
            dyb = dy.astype(BF16)
            dpw_ref[...] += _dot_tn(s.astype(BF16), dyb)
            dpwb_ref[...] += jnp.sum(dy, axis=0, keepdims=True)
            dln = _dot_nt(dyb, pwv) * (sg * (1.0 + ln * (1.0 - sg)))
            dlg_ref[...] += jnp.sum(dln * hhat, axis=0, keepdims=True)
            dlb_ref[...] += jnp.sum(dln, axis=0, keepdims=True)
            dhh = dln * lg
            dhc = rstd * (dhh - jnp.mean(dhh, axis=-1, keepdims=True) - hhat * jnp.mean(dhh * hhat, axis=-1, keepdims=True))
            dpad[r0:r0 + CONV_TILE, :] = dhc
            dcb_ref[...] += jnp.sum(dhc, axis=0, keepdims=True)
            for k, rows in _taps(pad, r0, _CONV_FWD_OFFSETS):
                dcw_acc[k] += (dhc * rows).reshape(CONV_TILE // 8, 8, GW).sum(axis=0)
        for r0 in range(0, S, CONV_TILE):
            dhg = jnp.zeros((CONV_TILE, GW), F32)
            for k, rows in _taps(dpad, r0, _CONV_BWD_OFFSETS):
                dhg = dhg + cw_ref[k:k + 1, :] * rows
            a = a_ref[r0:r0 + CONV_TILE, :]
            sg = _sigmoid(gt_ref[r0:r0 + CONV_TILE, :])
            da_ref[r0:r0 + CONV_TILE, :] = (dhg * sg).astype(da_ref.dtype)
            dgt_ref[r0:r0 + CONV_TILE, :] = (dhg * a * sg * (1.0 - sg)).astype(dgt_ref.dtype)

        @pl.when(pl.program_id(0) == B - 1)
        def _():
            for k in range(CONV_WIDTH):
                dcw_ref[k:k + 1, :] = jnp.sum(dcw_acc[k], axis=0, keepdims=True)

    seq_out = pl.BlockSpec((None, S, GW), lambda b: (b, 0, 0))
    return pl.pallas_call(
        body, grid=(B,),
        in_specs=[_seq(S, COL_A), _seq(S, COL_G), seq_out, pl.BlockSpec((None, S, GW), lambda b: (b, 0, 1)),
                  _full((CONV_WIDTH, GW)), _full((1, GW)), _full((1, GW)), _full((GW, GW))],
        out_specs=[seq_out, seq_out, _full((CONV_WIDTH, GW)), _full((1, GW)), _full((1, GW)), _full((1, GW)), _full((GW, GW)),
                   _full((1, GW))],
        out_shape=[_sds((B, S, GW), BF16), _sds((B, S, GW), BF16), _sds((CONV_WIDTH, GW), F32), _sds((1, GW), F32),
                   _sds((1, GW), F32), _sds((1, GW), F32), _sds((GW, GW), F32), _sds((1, GW), F32)],
        scratch_shapes=[pltpu.VMEM((S + CONV_PAD, GW), F32), pltpu.VMEM((S + CONV_PAD, GW), F32),
                        pltpu.VMEM((CONV_WIDTH, 8, GW), F32)],
        compiler_params=_cp(("arbitrary",)), name="mixb_bwd")(z3, z3, hc3, dycat3, cw, lg, lb, pw)


POOL_PAD = 16


def _pool_window():
    lane = lax.broadcasted_iota(jnp.int32, (1, GW), 1)
    return jnp.where(lane < 64, 2, jnp.where(lane < 128, 4, jnp.where(lane < 192, 8, 16)))


def _pool_sums(pad, r0, base, sign):
    win = _pool_window()
    acc = pad[pl.ds(r0 + base, CONV_TILE), :]
    out = None
    for i in range(1, 16):
        acc = acc + pad[pl.ds(r0 + base + sign * i, CONV_TILE), :]
        if i + 1 in (2, 4, 8, 16):
            out = acc if out is None else jnp.where(win == i + 1, acc, out)
    return out


def _pool_cnt(r0):
    t1 = r0 + 1 + lax.broadcasted_iota(jnp.int32, (CONV_TILE, 1), 0)
    return jnp.minimum(t1, _pool_window()).astype(F32)


def mixd_fwd(z3, wbd, scale):
    B, S, _ = z3.shape

    def body(x_ref, w_ref, sc_ref, y_ref, pad):
        pad[0:POOL_PAD, :] = jnp.zeros((POOL_PAD, GW), F32)
        pad[POOL_PAD:POOL_PAD + S, :] = x_ref[...]
        wv = w_ref[...].astype(BF16)
        for r0 in range(0, S, CONV_TILE):
            mean = _pool_sums(pad, r0, POOL_PAD, -1) / _pool_cnt(r0)
            p = (mean - x_ref[r0:r0 + CONV_TILE, :]).astype(BF16)
            y_ref[r0:r0 + CONV_TILE, :] = (_dot(p, wv) * sc_ref[...]).astype(y_ref.dtype)

    return pl.pallas_call(
        body, grid=(B,), in_specs=[_seq(S, COL_D), _full((GW, GW)), _full((1, GW))],
        out_specs=pl.BlockSpec((None, S, GW), lambda b: (b, 0, 0)), out_shape=_sds((B, S, GW), BF16),
        scratch_shapes=[pltpu.VMEM((S + POOL_PAD, GW), F32)],
        compiler_params=_cp(("parallel",)), name="mixd_fwd")(z3, wbd, scale)


def mixd_bwd(z3, dycat3, wbd, scale):
    B, S, _ = z3.shape

    def body(x_ref, dy_ref, w_ref, sc_ref, dx_ref, dw_ref, dsc_ref, pad, qpad):
        @pl.when(pl.program_id(0) == 0)
        def _():
            dw_ref[...] = jnp.zeros_like(dw_ref)
            dsc_ref[...] = jnp.zeros_like(dsc_ref)

        pad[0:POOL_PAD, :] = jnp.zeros((POOL_PAD, GW), F32)
        pad[POOL_PAD:POOL_PAD + S, :] = x_ref[...]
        qpad[S:S + POOL_PAD, :] = jnp.zeros((POOL_PAD, GW), F32)
        wv = w_ref[...].astype(BF16)
        for r0 in range(0, S, CONV_TILE):
            cnt = _pool_cnt(r0)
            mean = _pool_sums(pad, r0, POOL_PAD, -1) / cnt
            p = (mean - x_ref[r0:r0 + CONV_TILE, :]).astype(BF16)
            dy = dy_ref[r0:r0 + CONV_TILE, :]
            dsc_ref[...] += jnp.sum(dy * _dot(p, wv), axis=0, keepdims=True)
            dyp = (dy * sc_ref[...]).astype(BF16)
            dw_ref[...] += _dot_tn(p, dyp)
            dp = _dot_nt(dyp, wv)
            dx_ref[r0:r0 + CONV_TILE, :] = (-dp).astype(dx_ref.dtype)
            qpad[r0:r0 + CONV_TILE, :] = dp / cnt
        for r0 in range(0, S, CONV_TILE):
            back = _pool_sums(qpad, r0, 0, 1)
            dx_ref[r0:r0 + CONV_TILE, :] = (dx_ref[r0:r0 + CONV_TILE, :].astype(F32) + back).astype(dx_ref.dtype)

    return pl.pallas_call(
        body, grid=(B,),
        in_specs=[_seq(S, COL_D), pl.BlockSpec((None, S, GW), lambda b: (b, 0, 3)), _full((GW, GW)), _full((1, GW))],
        out_specs=[pl.BlockSpec((None, S, GW), lambda b: (b, 0, 0)), _full((GW, GW)), _full((1, GW))],
        out_shape=[_sds((B, S, GW), F32), _sds((GW, GW), F32), _sds((1, GW), F32)],
        scratch_shapes=[pltpu.VMEM((S + POOL_PAD, GW), F32), pltpu.VMEM((S + POOL_PAD, GW), F32)],
        compiler_params=_cp(("arbitrary",)), name="mixd_bwd")(z3, dycat3, wbd, scale)


def cmp_kv_fwd(tbk, tbv, pek, pev, w1k, w2k, w1v, w2v):
    B = tbk.shape[0]

    def body(tbk_ref, tbv_ref, pek_ref, pev_ref, w1k_ref, w2k_ref, w1v_ref, w2v_ref, kc_ref, vc_ref):
        for tb_ref, pe_ref, w1_ref, w2_ref, o_ref in ((tbk_ref, pek_ref, w1k_ref, w2k_ref, kc_ref),
                                                      (tbv_ref, pev_ref, w1v_ref, w2v_ref, vc_ref)):
            pre = _dot((tb_ref[...] + pe_ref[...]).astype(BF16), w1_ref[...].astype(BF16))
            hm = pre * _sigmoid(pre)
            o_ref[...] = _dot(hm.astype(BF16), w2_ref[...].astype(BF16))

    tb_spec = pl.BlockSpec((None, N_CMP, 2048), lambda b: (b, 0, 0))
    o_spec = pl.BlockSpec((None, N_CMP, HEAD_DIM), lambda b: (b, 0, 0))
    return pl.pallas_call(
        body, grid=(B,),
        in_specs=[tb_spec, tb_spec, _full((1, 2048)), _full((1, 2048)), _full((2048, HEAD_DIM)), _full((HEAD_DIM, HEAD_DIM)),
                  _full((2048, HEAD_DIM)), _full((HEAD_DIM, HEAD_DIM))],
        out_specs=[o_spec, o_spec], out_shape=[_sds((B, N_CMP, HEAD_DIM), F32)] * 2,
        compiler_params=_cp(("parallel",)), name="cmp_kv_fwd")(tbk, tbv, pek, pev, w1k, w2k, w1v, w2v)


def cmp_kv_bwd(tbk, tbv, pek, pev, w1k, w2k, w1v, w2v, dkc, dvc):
    B = tbk.shape[0]

    def body(tbk_ref, tbv_ref, pek_ref, pev_ref, w1k_ref, w2k_ref, w1v_ref, w2v_ref, dkc_ref, dvc_ref,
             dk2_ref, dv2_ref, dpek_ref, dpev_ref, dw1k_ref, dw2k_ref, dw1v_ref, dw2v_ref):
        @pl.when(pl.program_id(0) == 0)
        def _():
            for r in (dpek_ref, dpev_ref, dw1k_ref, dw2k_ref, dw1v_ref, dw2v_ref):
                r[...] = jnp.zeros_like(r)

        row0 = lax.broadcasted_iota(jnp.int32, (N_CMP, 1), 0) == 0
        for tb_ref, pe_ref, w1_ref, w2_ref, do_ref, d2_ref, dpe_ref, dw1_ref, dw2_ref in (
                (tbk_ref, pek_ref, w1k_ref, w2k_ref, dkc_ref, dk2_ref, dpek_ref, dw1k_ref, dw2k_ref),
                (tbv_ref, pev_ref, w1v_ref, w2v_ref, dvc_ref, dv2_ref, dpev_ref, dw1v_ref, dw2v_ref)):
            tb = (tb_ref[...] + pe_ref[...]).astype(BF16)
            w1 = w1_ref[...].astype(BF16)
            pre = _dot(tb, w1)
            sg = _sigmoid(pre)
            hm = (pre * sg).astype(BF16)
            do = do_ref[...].astype(BF16)
            dw2_ref[...] += _dot_tn(hm, do)
            dpre = (_dot_nt(do, w2_ref[...].astype(BF16)) * (sg * (1.0 + pre * (1.0 - sg)))).astype(BF16)
            dw1_ref[...] += _dot_tn(tb, dpre)
            dtb = _dot_nt(dpre, w1)
            dpe_ref[...] += jnp.sum(dtb, axis=0, keepdims=True)
            down = jnp.where(row0, 0.0, pltpu.roll(dtb[:, 1024:], 1, 0))
            d2_ref[...] = dtb[:, :1024] + down

    tb_spec = pl.BlockSpec((None, N_CMP, 2048), lambda b: (b, 0, 0))
    c_spec = pl.BlockSpec((None, N_CMP, HEAD_DIM), lambda b: (b, 0, 0))
    d2_spec = pl.BlockSpec((None, N_CMP, 1024), lambda b: (b, 0, 0))
    return pl.pallas_call(
        body, grid=(B,),
        in_specs=[tb_spec, tb_spec, _full((1, 2048)), _full((1, 2048)), _full((2048, HEAD_DIM)), _full((HEAD_DIM, HEAD_DIM)),
                  _full((2048, HEAD_DIM)), _full((HEAD_DIM, HEAD_DIM)), c_spec, c_spec],
        out_specs=[d2_spec, d2_spec, _full((1, 2048)), _full((1, 2048)), _full((2048, HEAD_DIM)), _full((HEAD_DIM, HEAD_DIM)),
                   _full((2048, HEAD_DIM)), _full((HEAD_DIM, HEAD_DIM))],
        out_shape=[_sds((B, N_CMP, 1024), F32)] * 2 + [_sds((1, 2048), F32)] * 2
        + [_sds((2048, HEAD_DIM), F32), _sds((HEAD_DIM, HEAD_DIM), F32)] * 2,
        compiler_params=_cp(("arbitrary",)), name="cmp_kv_bwd")(tbk, tbv, pek, pev, w1k, w2k, w1v, w2v, dkc, dvc)


def _qtile(col):
    return pl.BlockSpec((None, TQ, GW), lambda b, i: (b, i, col // GW))


def _qtile0():
    return pl.BlockSpec((None, TQ, GW), lambda b, i: (b, i, 0))


def _cmp_probs(q, kc, qpos):
    head = _lane_head()
    cend = lax.broadcasted_iota(jnp.int32, (1, N_CMP), 1) * CMP_STRIDE + 31
    cmask = cend <= qpos
    has = qpos >= 31
    out = []
    for h in range(4):
        qm = jnp.where(head == h, q, 0.0).astype(BF16)
        s = jnp.where(cmask, _dot_nt(qm, kc), NEG)
        e = jnp.exp(s - jnp.max(s, axis=-1, keepdims=True))
        p = jnp.where(has, e / jnp.sum(e, axis=-1, keepdims=True), 0.0)
        out.append((qm, p))
    return out


def cmp_attn_fwd(z3, kc4, vc4):
    B, S, _ = z3.shape

    def body(q_ref, kc_ref, vc_ref, o_ref, sel_ref):
        t0 = pl.program_id(1) * TQ
        qpos = t0 + lax.broadcasted_iota(jnp.int32, (TQ, 1), 0)
        head = _lane_head()
        kc, vc = kc_ref[...], vc_ref[...]
        o = jnp.zeros((TQ, GW), F32)
        psum = jnp.zeros((TQ, N_CMP), F32)
        for h, (_, p) in enumerate(_cmp_probs(q_ref[...] * 0.125, kc, qpos)):
            o = o + jnp.where(head == h, _dot(p.astype(BF16), vc), 0.0)
            psum = psum + p
        o_ref[...] = o
        cst = lax.broadcasted_iota(jnp.int32, (N_SLC, N_CMP), 1) * CMP_STRIDE
        jst = lax.broadcasted_iota(jnp.int32, (N_SLC, N_CMP), 0) * 64
        overlap = ((cst <= jst + 63) & (cst + 31 >= jst)).astype(BF16)
        imp = _dot_nt(overlap, psum.astype(BF16))
        qp = t0 + lax.broadcasted_iota(jnp.int32, (1, TQ), 1)
        jj = lax.broadcasted_iota(jnp.int32, (N_SLC, 1), 0)
        cur = lax.shift_right_logical(qp, SLC_BLOCK_SHIFT)
        forced = (jj == 0) | (jj == cur) | (jj == cur - 1)
        score = jnp.where(jj * 64 <= qp, imp + jnp.where(forced, FORCE_BONUS, 0.0), NEG)
        rank = jnp.zeros((N_SLC, TQ), F32)
        for j2 in range(N_SLC):
            sj = score[j2:j2 + 1, :]
            rank = rank + jnp.where((sj > score) | ((sj == score) & (j2 < jj)), 1.0, 0.0)
        sel_ref[...] = jnp.where((rank < SLC_TOPK) & (score > NEG / 2), 1.0, 0.0)

    c_spec = pl.BlockSpec((None, N_CMP, GW), lambda b, i: (b, 0, 0))
    return pl.pallas_call(
        body, grid=(B, S // TQ), in_specs=[_qtile(COL_Q), c_spec, c_spec],
        out_specs=[_qtile0(), pl.BlockSpec((None, N_SLC, TQ), lambda b, i: (b, 0, i))],
        out_shape=[_sds((B, S, GW), F32), _sds((B, N_SLC, S), F32)],
        compiler_params=_cp(("parallel", "parallel")), name="cmp_attn_fwd")(z3, kc4, vc4)


def cmp_attn_bwd(z3, kc4, vc4, do):
    B, S, _ = z3.shape
    nq = S // TQ

    def body(q_ref, kc_ref, vc_ref, do_ref, dq_ref, dkc_out, dvc_out, dkc_ref, dvc_ref):
        qi = pl.program_id(1)

        @pl.when(qi == 0)
        def _():
            dkc_ref[...] = jnp.zeros_like(dkc_ref)
            dvc_ref[...] = jnp.zeros_like(dvc_ref)

        qpos = qi * TQ + lax.broadcasted_iota(jnp.int32, (TQ, 1), 0)
        head = _lane_head()
        kc, vc, do = kc_ref[...], vc_ref[...], do_ref[...]
        dq = jnp.zeros((TQ, GW), F32)
        for h, (qm, p) in enumerate(_cmp_probs(q_ref[...] * 0.125, kc, qpos)):
            dom = jnp.where(head == h, do, 0.0).astype(BF16)
            dp = _dot_nt(dom, vc)
            ds = (p * (dp - jnp.sum(p * dp, axis=-1, keepdims=True))).astype(BF16)
            dq = dq + jnp.where(head == h, _dot(ds, kc), 0.0)
            dkc_ref[...] += _dot_tn(ds, qm)
            dvc_ref[...] += _dot_tn(p.astype(BF16), dom)
        dq_ref[...] = dq * 0.125

        @pl.when(qi == nq - 1)
        def _():
            dkc_out[...] = _fold_heads(dkc_ref[...])[:, :HEAD_DIM]
            dvc_out[...] = _fold_heads(dvc_ref[...])[:, :HEAD_DIM]

    c_spec = pl.BlockSpec((None, N_CMP, GW), lambda b, i: (b, 0, 0))
    d_spec = pl.BlockSpec((None, N_CMP, HEAD_DIM), lambda b, i: (b, 0, 0))
    return pl.pallas_call(
        body, grid=(B, nq), in_specs=[_qtile(COL_Q), c_spec, c_spec, _qtile0()],
        out_specs=[_qtile0(), d_spec, d_spec],
        out_shape=[_sds((B, S, GW), F32), _sds((B, N_CMP, HEAD_DIM), F32), _sds((B, N_CMP, HEAD_DIM), F32)],
        scratch_shapes=[pltpu.VMEM((N_CMP, GW), F32), pltpu.VMEM((N_CMP, GW), F32)],
        compiler_params=_cp(("parallel", "arbitrary")), name="cmp_attn_bwd")(z3, kc4, vc4, do)


def _attn_mask(mode, qpos, k0, sel_b):
    kpos = k0 + lax.broadcasted_iota(jnp.int32, (1, TQ), 1)
    mask = kpos <= qpos
    if mode == "win":
        return mask & (kpos > qpos - WIN)
    blk = lax.shift_right_logical(k0 + lax.broadcasted_iota(jnp.int32, (N_SLC, TQ), 1), SLC_BLOCK_SHIFT)
    expand = (blk == lax.broadcasted_iota(jnp.int32, (N_SLC, TQ), 0)).astype(BF16)
    return mask & (_dot_tn(sel_b, expand) > 0.5)


def _attn_lo(mode, qi):
    return jnp.maximum(qi - WIN // TQ, 0) if mode == "win" else 0


def attn_fwd(mode, z3, k4, v4, selT):
    B, S, _ = z3.shape

    def body(q_ref, k_ref, v_ref, sel_ref, o_ref, lse_ref, s_all, m_acc, l_acc, o_acc):
        qi = pl.program_id(1)
        qpos = qi * TQ + lax.broadcasted_iota(jnp.int32, (TQ, 1), 0)
        head = _lane_head()
        q = q_ref[...] * 0.125
        qm = [jnp.where(head == h, q, 0.0).astype(BF16) for h in range(4)]
        sel_b = sel_ref[...].astype(BF16)
        lo, hi = _attn_lo(mode, qi), qi + 1
        m_acc[...] = jnp.full(m_acc.shape, NEG, F32)

        def scores(kb, carry):
            k0 = pl.multiple_of(kb * TQ, TQ)
            kblk = k_ref[pl.ds(k0, TQ), :]
            mask = _attn_mask(mode, qpos, k0, sel_b)
            for h in range(4):
                s = jnp.where(mask, _dot_nt(qm[h], kblk), NEG)
                s_all[h, kb] = s
                m_acc[h] = jnp.maximum(m_acc[h], s)
            return carry

        lax.fori_loop(lo, hi, scores, 0)
        for h in range(4):
            m_acc[h] = jnp.broadcast_to(jnp.max(m_acc[h], axis=-1, keepdims=True), (TQ, TQ))
        l_acc[...] = jnp.zeros_like(l_acc)
        o_acc[...] = jnp.zeros_like(o_acc)

        def weights(kb, carry):
            vblk = v_ref[pl.ds(pl.multiple_of(kb * TQ, TQ), TQ), :]
            for h in range(4):
                p = jnp.exp(s_all[h, kb] - m_acc[h])
                l_acc[h] += p
                o_acc[h] += _dot(p.astype(BF16), vblk)
            return carry

        lax.fori_loop(lo, hi, weights, 0)
        o = jnp.zeros((TQ, GW), F32)
        lse = jnp.zeros((TQ, 128), F32)
        lane = lax.broadcasted_iota(jnp.int32, (1, 128), 1)
        for h in range(4):
            l = jnp.sum(l_acc[h], axis=-1, keepdims=True)
            o = o + jnp.where(head == h, o_acc[h] / l, 0.0)
            lse = jnp.where(lane == h, jnp.max(m_acc[h], axis=-1, keepdims=True) + jnp.log(l), lse)
        o_ref[...] = o
        lse_ref[...] = lse

    kv_spec = pl.BlockSpec((None, S, GW), lambda b, i: (b, 0, 0))
    return pl.pallas_call(
        body, grid=(B, S // TQ),
        in_specs=[_qtile(COL_Q), kv_spec, kv_spec, pl.BlockSpec((None, N_SLC, TQ), lambda b, i: (b, 0, i))],
        out_specs=[_qtile0(), pl.BlockSpec((None, TQ, 128), lambda b, i: (b, i, 0))],
        out_shape=[_sds((B, S, GW), F32), _sds((B, S, 128), F32)],
        scratch_shapes=[pltpu.VMEM((4, S // TQ, TQ, TQ), F32), pltpu.VMEM((4, TQ, TQ), F32), pltpu.VMEM((4, TQ, TQ), F32),
                        pltpu.VMEM((4, TQ, GW), F32)],
        compiler_params=_cp(("parallel", "parallel")), name=mode + "_attn_fwd")(z3, k4, v4, selT)


def attn_bwd(mode, z3, k4, v4, selT, o, lse, do):
    B, S, _ = z3.shape
    nq = S // TQ

    def body(q_ref, k_ref, v_ref, sel_ref, o_ref, lse_ref, do_ref, dq_ref, dk_out, dv_out, dq_s, dk_ref, dv_ref):
        qi = pl.program_id(1)

        @pl.when(qi == 0)
        def _():
            dk_ref[...] = jnp.zeros_like(dk_ref)
            dv_ref[...] = jnp.zeros_like(dv_ref)

        qpos = qi * TQ + lax.broadcasted_iota(jnp.int32, (TQ, 1), 0)
        head = _lane_head()
        lane = lax.broadcasted_iota(jnp.int32, (1, 128), 1)
        q = q_ref[...] * 0.125
        do = do_ref[...]
        doo = do * o_ref[...]
        lse = lse_ref[...]
        qm = [jnp.where(head == h, q, 0.0).astype(BF16) for h in range(4)]
        dom = [jnp.where(head == h, do, 0.0).astype(BF16) for h in range(4)]
        delta = [jnp.sum(jnp.where(head == h, doo, 0.0), axis=-1, keepdims=True) for h in range(4)]
        lse_h = [jnp.max(jnp.where(lane == h, lse, NEG), axis=-1, keepdims=True) for h in range(4)]
        sel_b = sel_ref[...].astype(BF16)
        dq_s[...] = jnp.zeros_like(dq_s)

        def step(kb, carry):
            k0 = pl.multiple_of(kb * TQ, TQ)
            kblk = k_ref[pl.ds(k0, TQ), :]
            vblk = v_ref[pl.ds(k0, TQ), :]
            mask = _attn_mask(mode, qpos, k0, sel_b)
            for h in range(4):
                s = _dot_nt(qm[h], kblk)
                p = jnp.where(mask, jnp.exp(s - lse_h[h]), 0.0)
                dp = _dot_nt(dom[h], vblk)
                ds = (p * (dp - delta[h])).astype(BF16)
                dq_s[...] += jnp.where(head == h, _dot(ds, kblk), 0.0)
                dk_ref[pl.ds(k0, TQ), :] += _dot_tn(ds, qm[h])
                dv_ref[pl.ds(k0, TQ), :] += _dot_tn(p.astype(BF16), dom[h])
            return carry

        lax.fori_loop(_attn_lo(mode, qi), qi + 1, step, 0)
        dq_ref[...] = dq_s[...] * 0.125

        @pl.when(qi == nq - 1)
        def _():
            for r0 in range(0, S, TQ):
                dk_out[r0:r0 + TQ, :] = _fold_heads(dk_ref[r0:r0 + TQ, :])[:, :HEAD_DIM]
                dv_out[r0:r0 + TQ, :] = _fold_heads(dv_ref[r0:r0 + TQ, :])[:, :HEAD_DIM]

    kv_spec = pl.BlockSpec((None, S, GW), lambda b, i: (b, 0, 0))
    return pl.pallas_call(
        body, grid=(B, nq),
        in_specs=[_qtile(COL_Q), kv_spec, kv_spec, pl.BlockSpec((None, N_SLC, TQ), lambda b, i: (b, 0, i)), _qtile0(),
                  pl.BlockSpec((None, TQ, 128), lambda b, i: (b, i, 0)), _qtile0()],
        out_specs=[_qtile0(), pl.BlockSpec((None, S, HEAD_DIM), lambda b, i: (b, 0, 0)),
                   pl.BlockSpec((None, S, HEAD_DIM), lambda b, i: (b, 0, 0))],
        out_shape=[_sds((B, S, GW), F32), _sds((B, S, HEAD_DIM), F32), _sds((B, S, HEAD_DIM), F32)],
        scratch_shapes=[pltpu.VMEM((TQ, GW), F32), pltpu.VMEM((S, GW), F32), pltpu.VMEM((S, GW), F32)],
        compiler_params=_cp(("parallel", "arbitrary")), name=mode + "_attn_bwd")(z3, k4, v4, selT, o, lse, do)


def _gate_expand(g, b):
    head = _lane_head()
    out = jnp.zeros((TQ, GW), F32)
    for h in range(4):
        out = jnp.where(head == h, g[:, 3 * h + b:3 * h + b + 1], out)
    return out


def combine_fwd(z3, o_cmp, o_slc, o_win):
    B, S, _ = z3.shape

    def body(gl_ref, oc_ref, os_ref, ow_ref, y_ref):
        g = _sigmoid(gl_ref[...])
        y = jnp.zeros((TQ, GW), F32)
        for b, o_ref in enumerate((oc_ref, os_ref, ow_ref)):
            y = y + _gate_expand(g, b) * o_ref[...]
        y_ref[...] = y.astype(y_ref.dtype)

    return pl.pallas_call(
        body, grid=(B, S // TQ),
        in_specs=[pl.BlockSpec((None, TQ, 128), lambda b, i: (b, i, COL_GL // 128)), _qtile0(), _qtile0(), _qtile0()],
        out_specs=_qtile0(), out_shape=_sds((B, S, GW), BF16),
        compiler_params=_cp(("parallel", "parallel")), name="combine_fwd")(z3, o_cmp, o_slc, o_win)


def combine_bwd(z3, o_cmp, o_slc, o_win, dycat3):
    B, S, _ = z3.shape

    def body(gl_ref, oc_ref, os_ref, ow_ref, dy_ref, dc_ref, ds_ref, dw_ref, dgl_ref):
        g = _sigmoid(gl_ref[...])
        dy = dy_ref[...]
        head = _lane_head()
        lane = lax.broadcasted_iota(jnp.int32, (1, 128), 1)
        dg = jnp.zeros((TQ, 128), F32)
        for b, (o_ref, d_ref) in enumerate(((oc_ref, dc_ref), (os_ref, ds_ref), (ow_ref, dw_ref))):
            d_ref[...] = _gate_expand(g, b) * dy
            t = dy * o_ref[...]
            for h in range(4):
                dg = jnp.where(lane == 3 * h + b, jnp.sum(jnp.where(head == h, t, 0.0), axis=-1, keepdims=True), dg)
        dgl_ref[...] = dg * g * (1.0 - g)

    gl_spec = pl.BlockSpec((None, TQ, 128), lambda b, i: (b, i, COL_GL // 128))
    return pl.pallas_call(
        body, grid=(B, S // TQ),
        in_specs=[gl_spec, _qtile0(), _qtile0(), _qtile0(), pl.BlockSpec((None, TQ, GW), lambda b, i: (b, i, 2))],
        out_specs=[_qtile0(), _qtile0(), _qtile0(), pl.BlockSpec((None, TQ, 128), lambda b, i: (b, i, 0))],
        out_shape=[_sds((B, S, GW), F32)] * 3 + [_sds((B, S, 128), F32)],
        compiler_params=_cp(("parallel", "parallel")), name="combine_bwd")(z3, o_cmp, o_slc, o_win, dycat3)


def assemble_dz(du, dv, da, dgt, dq_c, dq_s, dq_w, dd, dkvs, dgl):
    T = du.shape[0]

    def body(du_ref, dv_ref, da_ref, dgt_ref, dqc_ref, dqs_ref, dqw_ref, dd_ref, dgl_ref, *rest):
        kv_refs, o_ref = rest[:6], rest[6]
        o_ref[:, COL_U:COL_U + GW] = du_ref[...]
        o_ref[:, COL_V:COL_V + GW] = dv_ref[...]
        o_ref[:, COL_A:COL_A + GW] = da_ref[...]
        o_ref[:, COL_G:COL_G + GW] = dgt_ref[...]
        o_ref[:, COL_Q:COL_Q + GW] = (dqc_ref[...] + dqs_ref[...] + dqw_ref[...]).astype(BF16)
        o_ref[:, COL_D:COL_D + GW] = dd_ref[...].astype(BF16)
        for i, kv_ref in enumerate(kv_refs):
            o_ref[:, COL_KV + i * HEAD_DIM:COL_KV + (i + 1) * HEAD_DIM] = kv_ref[...].astype(BF16)
        o_ref[:, COL_GL:COL_GL + 128] = dgl_ref[...].astype(BF16)

    specs = [_rows(ROW_TILE, GW)] * 8 + [_rows(ROW_TILE, 128)] + [_rows(ROW_TILE, HEAD_DIM)] * 6
    return pl.pallas_call(body, grid=(T // ROW_TILE,), in_specs=specs, out_specs=_rows(ROW_TILE, ZW),
                          out_shape=_sds((T, ZW), BF16), compiler_params=_cp(("parallel",)),
                          name="assemble_dz")(du, dv, da, dgt, dq_c, dq_s, dq_w, dd, dgl, *dkvs)


def _my_pos():
    return lax.axis_index("x"), lax.axis_index("y"), lax.axis_index("c")


def _peer(k):
    x, y, c = _my_pos()
    return ((1 - x) if k & 4 else x, (1 - y) if k & 2 else y, (1 - c) if k & 1 else c)


def _index(pos):
    return 4 * pos[0] + 2 * pos[1] + pos[2]


_HBM = pl.BlockSpec(memory_space=pltpu.HBM)


_SEM = pl.BlockSpec(memory_space=pltpu.SEMAPHORE)
_EFFECT = pltpu.SideEffectType.DATAFLOW_SIDE_EFFECTING


def _exchange_copies(kinds, srcs, lands, send, recv):
    me = _index(_my_pos())
    out = []
    for a, kind in enumerate(kinds):
        for k in range(N_DEV):
            peer = _peer(k)
            if kind == "gather":
                r = srcs[a].shape[1]
                src, dst = srcs[a], lands[a].at[:, pl.ds(me * r, r), :]
            else:
                r = srcs[a].shape[1] // N_DEV
                src, dst = srcs[a].at[:, pl.ds(_index(peer) * r, r), :], lands[a].at[me]
            sem = a * N_DEV + k
            out.append(pltpu.make_async_remote_copy(src_ref=src, dst_ref=dst, send_sem=send.at[sem], recv_sem=recv.at[sem],
                                                    device_id=peer, device_id_type=MESH))
    return out


def _landing_zone(kind, src):
    _, r, C = src.shape
    return lax.empty((1, N_DEV * r, C) if kind == "gather" else (N_DEV, 1, r // N_DEV, C), src.dtype)


def exchange_start(kinds, srcs, name):
    n = len(srcs)
    lands = [_landing_zone(k, s) for k, s in zip(kinds, srcs)]

    def body(*refs):
        s, l = refs[:n], refs[n:2 * n]
        send, recv = refs[2 * n], refs[2 * n + 1]
        for cp in _exchange_copies(kinds, s, l, send, recv):
            cp.start()
        refs[-1][...] = jnp.zeros((8, 128), F32)

    hbm = [pltpu.HBM(a.shape, a.dtype) for a in srcs + lands]
    outs = pl.pallas_call(
        body, name=name,
        out_shape=(pltpu.SemaphoreType.DMA((n * N_DEV,)), pltpu.SemaphoreType.DMA((n * N_DEV,)), *hbm,
                   _sds((8, 128), F32)),
        in_specs=[_HBM] * (2 * n), out_specs=(_SEM, _SEM, *([_HBM] * (2 * n)), pl.BlockSpec(memory_space=pltpu.VMEM)),
        input_output_aliases={i: 2 + i for i in range(2 * n)},
        compiler_params=pltpu.CompilerParams(has_side_effects=_EFFECT),
    )(*[pltpu.with_memory_space_constraint(a, pltpu.HBM) for a in srcs + lands])
    return outs[0], outs[1], list(outs[2:2 + n]), list(outs[2 + n:2 + 2 * n]), outs[-1]


def exchange_wait(kinds, started, after, name):
    send, recv, srcs, lands, _ = started
    n = len(srcs)
    after = list(after) if isinstance(after, (list, tuple)) else [after]

    def body(*refs):
        s, l = refs[:n], refs[n:2 * n]
        for cp in _exchange_copies(kinds, s, l, refs[2 * n], refs[2 * n + 1]):
            cp.wait_send()
            cp.wait_recv()
        refs[-1][...] = jnp.zeros((8, 128), F32)

    outs = pl.pallas_call(
        body, name=name, out_shape=[pltpu.HBM(a.shape, a.dtype) for a in srcs + lands] + [_sds((8, 128), F32)],
        in_specs=[_HBM] * (2 * n) + [_SEM, _SEM] + [pl.BlockSpec(memory_space=pl.ANY)] * len(after),
        out_specs=[_HBM] * (2 * n) + [pl.BlockSpec(memory_space=pltpu.VMEM)],
        input_output_aliases={i: i for i in range(2 * n)},
        compiler_params=pltpu.CompilerParams(has_side_effects=_EFFECT),
    )(*srcs, *lands, send, recv, *after)
    return list(outs[n:2 * n]), outs[-1]


def sum_slots(lands, name):
    L = len(lands)
    _, _, r, C = lands[0].shape
    tr = _tile(r, 256, 16)

    def body(*refs):
        o_ref = refs[L]
        for l in range(L):
            @pl.when(pl.program_id(0) == l)
            def _(x_ref=refs[l]):
                acc = x_ref[0].astype(F32)
                for s in range(1, N_DEV):
                    acc = acc + x_ref[s].astype(F32)
                o_ref[...] = acc

    specs = [pl.BlockSpec((N_DEV, None, tr, C), lambda g, i, l=l: (0, 0, jnp.where(g == l, i, 0), 0)) for l in range(L)]
    return pl.pallas_call(
        body, grid=(L, r // tr), in_specs=specs,
        out_specs=pl.BlockSpec((None, tr, C), lambda g, i: (g, i, 0)), out_shape=_sds((L, r, C), F32),
        compiler_params=_cp(("arbitrary", "arbitrary")), name=name)(*lands)


def pack_flat(arrs):
    flat = jnp.concatenate([a.reshape(-1).astype(F32) for a in arrs])
    n = flat.shape[0]
    total = -(-n // 32768) * 32768
    return jnp.pad(flat, (0, total - n)).reshape(total // 128, 128)


def unpack_flat(flat, shapes):
    v = flat.reshape(-1)
    out, off = [], 0
    for s in shapes:
        n = int(np.prod(s))
        out.append(v[off:off + n].reshape(s))
        off += n
    return out


def adamw(w, g, m, v, name):
    shape = w.shape
    C = shape[-1]
    R = int(np.prod(shape)) // C
    tr = _tile(R, 128, 8)
    c1 = 1.0 - ADAM_B1 ** ADAM_STEP
    c2 = 1.0 - ADAM_B2 ** ADAM_STEP

    def body(w_ref, g_ref, m_ref, v_ref, d_ref, nm_ref, nv_ref):
        g = g_ref[...]
        m2 = ADAM_B1 * m_ref[...] + (1.0 - ADAM_B1) * g
        v2 = ADAM_B2 * v_ref[...] + (1.0 - ADAM_B2) * (g * g)
        nm_ref[...] = m2
        nv_ref[...] = v2
        d_ref[...] = -ADAM_LR * ((m2 / c1) / (jnp.sqrt(v2 / c2) + ADAM_EPS) + ADAM_WD * w_ref[...])

    spec = pl.BlockSpec((tr, C), lambda i: (i, 0))
    outs = pl.pallas_call(body, grid=(R // tr,), in_specs=[spec] * 4, out_specs=[spec] * 3,
                          out_shape=[_sds((R, C), F32)] * 3, compiler_params=_cp(("parallel",)), name=name)(
        w.reshape(R, C), g.reshape(R, C), m.reshape(R, C), v.reshape(R, C))
    return [o.reshape(shape) for o in outs]


def _bexp(sg_b):
    return jnp.repeat(sg_b.T, HEAD_DIM, axis=1)


def _block_diag(pool_w):
    out = jnp.zeros((GW, GW), F32)
    for i in range(4):
        out = out.at[i * 64:(i + 1) * 64, i * 64:(i + 1) * 64].set(pool_w[i])
    return out


def _cmp_rows(t):
    B, S, _ = t.shape
    t2 = t.reshape(B, S // CMP_STRIDE, CMP_STRIDE * HEAD_DIM)
    nxt = jnp.concatenate([t2[:, 1:], jnp.zeros_like(t2[:, :1])], axis=1)
    return jnp.concatenate([t2, nxt], axis=-1)


def _tile4(t):
    return jnp.tile(t, (1, 1, 4)).astype(BF16)


def kv_tiles(z):
    T = z.shape[0]

    def body(x_ref, cv_ref, ks_ref, vs_ref, kw_ref, vw_ref):
        x = x_ref[...]
        cv_ref[...] = x[:, :128]
        xb = x.astype(BF16)
        src = lax.broadcasted_iota(jnp.int32, (384, GW), 0)
        lane = lax.broadcasted_iota(jnp.int32, (384, GW), 1) & 63
        for i, o_ref in enumerate((ks_ref, vs_ref, kw_ref, vw_ref)):
            expand = (src == lane + 64 * (i + 2)).astype(BF16)
            o_ref[...] = _dot(xb, expand).astype(o_ref.dtype)

    return pl.pallas_call(
        body, grid=(T // ROW_TILE,), in_specs=[pl.BlockSpec((ROW_TILE, 384), lambda i: (i, COL_KV // 384))],
        out_specs=[_rows(ROW_TILE, 128)] + [_rows(ROW_TILE, GW)] * 4,
        out_shape=[_sds((T, 128), F32)] + [_sds((T, GW), BF16)] * 4,
        compiler_params=_cp(("parallel",)), name="kv_tiles")(z)


def layer_fwd(x, p, late, B, S):
    T = B * S
    sv = {"x0": x}
    h1, h1t = rms_fwd(x, p["g_pre_mix"], "rms_pre_mix")
    z = mm(h1, p["w_in"], name="mm_in", tn=2048)
    z3 = z.reshape(B, S, ZW)
    ya = mixa_fwd(z, p["sg_ln_g"], p["sg_w"], p["bexp"])
    yb, hc = mixb_fwd(z3, p["cv_w"], p["cv_b"], p["cv_ln_g"], p["cv_ln_b"], p["cv_pw"], p["cv_pw_b"])
    kcv, ks4, vs4, kw4, vw4 = [a.reshape(B, S, -1) for a in kv_tiles(z)]
    tbk, tbv = _cmp_rows(kcv[:, :, :HEAD_DIM]), _cmp_rows(kcv[:, :, HEAD_DIM:])
    kc, vc = cmp_kv_fwd(tbk, tbv, p["cmp_pos_k"], p["cmp_pos_v"], p["cmp_w1_k"], p["cmp_w2_k"], p["cmp_w1_v"], p["cmp_w2_v"])
    kc4, vc4 = _tile4(kc), _tile4(vc)
    o_cmp, selT = cmp_attn_fwd(z3, kc4, vc4)
    o_slc, lse_slc = attn_fwd("slc", z3, ks4, vs4, selT)
    o_win, lse_win = attn_fwd("win", z3, kw4, vw4, selT)
    yc = combine_fwd(z3, o_cmp, o_slc, o_win)
    yd = mixd_fwd(z3, p["pool_bd"], p["pool_scale"])
    ycat = jnp.concatenate([ya, yb.reshape(T, GW), yc.reshape(T, GW), yd.reshape(T, GW)], axis=-1)
    p.update(late(ycat))
    mix = mm(ycat, p["w_out"], name="mm_out")
    x1 = rms_post_fwd(x, mix, p["g_post_mix"], "rms_post_mix")
    h2, h2t = rms_fwd(x1, p["g_pre_ffn"], "rms_pre_ffn")
    gu4, a3 = ffn_up_fwd(h2, p["w_gu"])
    f = mm_kblocks(a3, p["w_down"], tb=False, name="mm_down", tm=1024)
    x2 = rms_post_fwd(x1, f, p["g_post_ffn"], "rms_post_ffn")
    sv.update(h1t=h1t, z=z, hc=hc, tbk=tbk, tbv=tbv, kc4=kc4, vc4=vc4, ks4=ks4, vs4=vs4, kw4=kw4, vw4=vw4, o_cmp=o_cmp, selT=selT,
              o_slc=o_slc, lse_slc=lse_slc, o_win=o_win, lse_win=lse_win, ycat=ycat, mix=mix, x1=x1, h2t=h2t, gu4=gu4, a3=a3, f=f)
    return x2, sv


def layer_bwd_ffn(dx2, p, sv, B, S):
    T = B * S
    gb, gs = {}, {}
    df, gs["g_post_ffn"] = rms_bwd(sv["f"], p["g_post_ffn"], dx2, None, BF16, "rms_post_ffn_bwd")
    dgu = ffn_down_dx(df, p["w_down"], sv["gu4"]).reshape(N_DEV, T, FFN_BLK)
    gb["w_down"] = mm(sv["a3"], df, ta=True, blk="m", out_dtype=BF16, name="mm_down_dw", tk=4096)
    dh2 = mm_kblocks(dgu, p["w_gu"], tb=True, name="mm_gu_dx", tm=512)
    gb["w_gu"] = mm(sv["h2t"], dgu, blk="n", out_dtype=BF16, name="mm_gu_dw", tk=4096)
    dx1, gs["g_pre_ffn"] = rms_bwd(sv["x1"], p["g_pre_ffn"], dh2, dx2, F32, "rms_pre_ffn_bwd")
    gb["w_gu"] = gb["w_gu"].reshape(1, N_DEV * D_MODEL, FFN_BLK)
    gb["w_down"] = gb["w_down"].reshape(1, FFN_HIDDEN, D_MODEL)
    return dx1, gb, gs


def layer_bwd_mix(dx1, p, sv, B, S):
    T = B * S
    gb, gs = {}, {}
    dmix, gs["g_post_mix"] = rms_bwd(sv["mix"], p["g_post_mix"], dx1, None, BF16, "rms_post_mix_bwd")
    dycat = mm(dmix, p["w_out"], tb=True, name="mm_out_dx")
    gb["w_out"] = mm(sv["ycat"], dmix, ta=True, out_dtype=BF16, name="mm_out_dw")
    dycat3 = dycat.reshape(B, S, D_MODEL)
    z = sv["z"]
    z3 = z.reshape(B, S, ZW)
    du, dv, gs["sg_w"], db, gs["sg_ln_g"] = mixa_bwd(z, dycat, p["sg_ln_g"], p["sg_w"], p["bexp"])
    gs["sg_b"] = db[:, :4].T
    (da, dgt, gs["cv_w"], gs["cv_b"], gs["cv_ln_g"], gs["cv_ln_b"], gpw, gs["cv_pw_b"]) = mixb_bwd(
        z3, sv["hc"], dycat3, p["cv_w"], p["cv_ln_g"], p["cv_ln_b"], p["cv_pw"])
    gb["cv_pw"] = gpw.astype(BF16)
    dd, dwbd, gs["pool_scale"] = mixd_bwd(z3, dycat3, p["pool_bd"], p["pool_scale"])
    gs["pool_w"] = jnp.stack([dwbd[i * 64:(i + 1) * 64, i * 64:(i + 1) * 64] for i in range(4)])
    do_c, do_s, do_w, dgl = combine_bwd(z3, sv["o_cmp"], sv["o_slc"], sv["o_win"], dycat3)
    dq_s, dks, dvs = attn_bwd("slc", z3, sv["ks4"], sv["vs4"], sv["selT"], sv["o_slc"], sv["lse_slc"], do_s)
    dq_w, dkw, dvw = attn_bwd("win", z3, sv["kw4"], sv["vw4"], sv["selT"], sv["o_win"], sv["lse_win"], do_w)
    dq_c, dkc, dvc = cmp_attn_bwd(z3, sv["kc4"], sv["vc4"], do_c)
    (dk2, dv2, gs["cmp_pos_k"], gs["cmp_pos_v"], gw1k, gs["cmp_w2_k"], gw1v, gs["cmp_w2_v"]) = cmp_kv_bwd(
        sv["tbk"], sv["tbv"], p["cmp_pos_k"], p["cmp_pos_v"], p["cmp_w1_k"], p["cmp_w2_k"], p["cmp_w1_v"], p["cmp_w2_v"],
        dkc, dvc)
    gb["cmp_w1_k"], gb["cmp_w1_v"] = gw1k.astype(BF16), gw1v.astype(BF16)
    dkvs = [t.reshape(T, HEAD_DIM) for t in (dk2, dv2, dks, dvs, dkw, dvw)]
    dz = assemble_dz(du, dv, da.reshape(T, GW), dgt.reshape(T, GW), dq_c.reshape(T, GW), dq_s.reshape(T, GW),
                     dq_w.reshape(T, GW), dd.reshape(T, GW), dkvs, dgl.reshape(T, 128))
    dh1 = mm(dz, p["w_in"], tb=True, name="mm_in_dx", tk=2048)
    gb["w_in"] = mm(sv["h1t"], dz, out_dtype=BF16, name="mm_in_dw", tk=4096, tn=512)
    dx0, gs["g_pre_mix"] = rms_bwd(sv["x0"], p["g_pre_mix"], dh1, dx1, F32, "rms_pre_mix_bwd")
    return dx0, gb, gs


SMALL = ["g_pre_mix", "g_post_mix", "g_pre_ffn", "g_post_ffn", "sg_ln_g", "sg_w", "sg_b", "cv_w", "cv_b", "cv_ln_g", "cv_ln_b",
         "cv_pw_b", "cmp_pos_k", "cmp_pos_v", "cmp_w2_k", "cmp_w2_v", "pool_w", "pool_scale"]
BIG = ["w_in", "w_out", "w_gu", "w_down", "cmp_w1_k", "cmp_w1_v", "cv_pw"]
NAMES = ["g_pre_mix", "g_post_mix", "g_pre_ffn", "g_post_ffn", "w_in", "sg_ln_g", "sg_w", "sg_b", "cv_w", "cv_b", "cv_ln_g",
         "cv_ln_b", "cv_pw", "cv_pw_b", "cmp_pos_k", "cmp_pos_v", "cmp_w1_k", "cmp_w2_k", "cmp_w1_v", "cmp_w2_v", "pool_w",
         "pool_scale", "w_out", "ffn_w_gu", "ffn_w_down"]


def kernel(x, g_pre_mix, g_post_mix, g_pre_ffn, g_post_ffn, w_in, sg_ln_g, sg_w, sg_b, cv_w, cv_b, cv_ln_g, cv_ln_b, cv_pw, cv_pw_b, cmp_pos_k, cmp_pos_v, cmp_w1_k, cmp_w2_k, cmp_w1_v, cmp_w2_v, pool_w, pool_scale, w_out, ffn_w_gu, ffn_w_down, loss_target, m_g_pre_mix, m_g_post_mix, m_g_pre_ffn, m_g_post_ffn, m_w_in, m_sg_ln_g, m_sg_w, m_sg_b, m_cv_w, m_cv_b, m_cv_ln_g, m_cv_ln_b, m_cv_pw, m_cv_pw_b, m_cmp_pos_k, m_cmp_pos_v, m_cmp_w1_k, m_cmp_w2_k, m_cmp_w1_v, m_cmp_w2_v, m_pool_w, m_pool_scale, m_w_out, m_ffn_w_gu, m_ffn_w_down, v_g_pre_mix, v_g_post_mix, v_g_pre_ffn, v_g_post_ffn, v_w_in, v_sg_ln_g, v_sg_w, v_sg_b, v_cv_w, v_cv_b, v_cv_ln_g, v_cv_ln_b, v_cv_pw, v_cv_pw_b, v_cmp_pos_k, v_cmp_pos_v, v_cmp_w1_k, v_cmp_w2_k, v_cmp_w1_v, v_cmp_w2_v, v_pool_w, v_pool_scale, v_w_out, v_ffn_w_gu, v_ffn_w_down):
    args = dict(locals())
    W = {n: args[n] for n in NAMES}
    M = {n: args["m_" + n] for n in NAMES}
    V = {n: args["v_" + n] for n in NAMES}
    B, S, _ = x.shape
    T = B * S
    L = w_in.shape[0]
    me = _index(_my_pos())
    cpd = GW // N_DEV

    shards = {"w_in": lambda l: pack_cols(w_in[l]).astype(BF16), "w_out": lambda l: w_out[l].astype(BF16),
              "w_gu": lambda l: ffn_w_gu[l].astype(BF16), "w_down": lambda l: ffn_w_down[l].astype(BF16),
              "cmp_w1_k": lambda l: cmp_w1_k[l].astype(BF16), "cmp_w1_v": lambda l: cmp_w1_v[l].astype(BF16),
              "cv_pw": lambda l: cv_pw[l].astype(BF16), "cv_w": lambda l: cv_w[l].T}
    early, later = ["w_in", "cmp_w1_k", "cmp_w1_v", "cv_pw", "cv_w"], ["w_out", "w_gu", "w_down"]

    def start_gather(names, l, tag, srcs=None, behind=None):
        srcs = list(srcs) if srcs is not None else [shards[n](l)[None] for n in names]
        if behind is not None:
            srcs[0] = srcs[0] + behind[0, 0].astype(srcs[0].dtype)
        return exchange_start(["gather"] * len(names), srcs, "gather_%s_start_%d" % (tag, l))

    def wait_gather(names, started, after, l, tag):
        arrived, done = exchange_wait(["gather"] * len(names), started, after, "gather_%s_wait_%d" % (tag, l))
        full = {n: a[0] for n, a in zip(names, arrived)}
        full["done"] = done
        if "w_gu" in full:
            full["w_gu"] = full["w_gu"].reshape(N_DEV, D_MODEL, FFN_BLK)
            full["w_down"] = full["w_down"].reshape(4, FFN_BLK, D_MODEL)
        if "cv_w" in full:
            full["cv_w"] = full["cv_w"].T
        return full

    def layer_params(l, full):
        p = dict(full)
        for n in ("g_pre_mix", "g_post_mix", "g_pre_ffn", "g_post_ffn", "sg_ln_g", "cv_b", "cv_ln_g", "cv_ln_b", "cv_pw_b",
                  "pool_scale"):
            p[n] = W[n][l][None, :]
        p["sg_w"] = sg_w[l]
        p["bexp"] = _bexp(sg_b[l])
        p["cmp_pos_k"] = cmp_pos_k[l].reshape(1, 2048)
        p["cmp_pos_v"] = cmp_pos_v[l].reshape(1, 2048)
        p["cmp_w2_k"], p["cmp_w2_v"] = cmp_w2_k[l], cmp_w2_v[l]
        p["pool_bd"] = _block_diag(pool_w[l])
        return p

    xs = x.reshape(T, D_MODEL)
    params, saved = [], []
    early_st = start_gather(early, 0, "early")
    ahead = {}
    for l in range(L):
        later_srcs = [shards[n](l)[None] for n in later]
        if l == 0:
            later_srcs = [s + early_st[4][0, 0].astype(s.dtype) for s in later_srcs]
        full = wait_gather(early, early_st, later_srcs + ([xs] if l > 0 else []), l, "early")
        later_st = start_gather(later, l, "later", srcs=later_srcs, behind=full["done"])
        p = layer_params(l, full)
        p["g_pre_mix"] = p["g_pre_mix"] + later_st[4][0, 0]

        def late(after, l=l, st=later_st, p=p):
            got = wait_gather(later, st, after, l, "later")
            if l + 1 < L:
                ahead["early"] = start_gather(early, l + 1, "early", behind=got["done"])
                got["g_post_mix"] = p["g_post_mix"] + ahead["early"][4][0, 0]
            return got

        xs, sv = layer_fwd(xs, p, late, B, S)
        params.append(p)
        saved.append(sv)
        early_st = ahead.get("early")
    dy, lpart = loss_fwd_bwd(xs, loss_target.reshape(T, D_MODEL))
    loss = lax.psum(lpart[0, 0], ("x", "y", "c"))

    ffn_big, mix_big = ["w_gu", "w_down"], ["w_in", "w_out", "cmp_w1_k", "cmp_w1_v", "cv_pw"]
    ffn_kinds, mix_kinds = ["scatter"] * len(ffn_big), ["scatter"] * len(mix_big) + ["gather"]
    pending, token = [], None
    for l in reversed(range(L)):
        p = dict(params[l])
        if token is not None:
            p["g_post_ffn"] = p["g_post_ffn"] + token[0, 0]
        dy, gb_ffn, gs = layer_bwd_ffn(dy, p, saved[l], B, S)
        st_ffn = exchange_start(ffn_kinds, [gb_ffn[n] for n in ffn_big], "scatter_ffn_start_%d" % l)
        p["g_post_mix"] = p["g_post_mix"] + st_ffn[4][0, 0]
        dy, gb_mix, gs_mix = layer_bwd_mix(dy, p, saved[l], B, S)
        gs.update(gs_mix)
        small_shapes = [tuple(gs[n].shape) for n in SMALL]
        st_mix = exchange_start(mix_kinds, [gb_mix[n][None] for n in mix_big] + [pack_flat([gs[n] for n in SMALL])[None]],
                                "scatter_mix_start_%d" % l)
        token = st_mix[4]
        pending.append((l, st_ffn, st_mix))
    grad_x = dy.reshape(B, S, D_MODEL)

    delta, new_m, new_v = {}, {}, {}
    lands = {l: {} for l in range(L)}
    for l, st_ffn, _ in pending:
        lands[l].update(zip(ffn_big, exchange_wait(ffn_kinds, st_ffn, token, "scatter_ffn_wait_%d" % l)[0]))
    grads = {}
    for n, name in zip(ffn_big, ("ffn_w_gu", "ffn_w_down")):
        grads[name] = sum_slots([lands[l][n] for l in range(L)], "sum_" + n)
        delta[name], new_m[name], new_v[name] = adamw(W[name], grads[name], M[name], V[name], "adamw_" + name)
    for l, _, st_mix in pending:
        lands[l].update(zip(mix_big + ["small"],
                            exchange_wait(mix_kinds, st_mix, delta["ffn_w_down"], "scatter_mix_wait_%d" % l)[0]))
    for n in mix_big:
        grads[n] = sum_slots([lands[l][n] for l in range(L)], "sum_" + n)
    grads["w_in"] = unpack_cols(grads["w_in"])
    rows = lands[0]["small"].shape[1] // N_DEV
    reduced = sum_slots([lands[l]["small"].reshape(N_DEV, 1, rows, 128) for l in range(L)], "sum_small")
    per_layer = [unpack_flat(reduced[l], small_shapes) for l in range(L)]
    for i, n in enumerate(SMALL):
        g = jnp.stack([per_layer[l][i] for l in range(L)])
        grads[n] = g.reshape(W[n].shape) if n != "cv_w" else g
    grads["cv_w"] = lax.dynamic_slice(grads["cv_w"], (0, 0, me * cpd), (L, CONV_WIDTH, cpd))

    for n in mix_big:
        delta[n], new_m[n], new_v[n] = adamw(W[n], grads[n], M[n], V[n], "adamw_" + n)
    shapes = [W[n].shape for n in SMALL]
    packed = adamw(pack_flat([W[n] for n in SMALL]), pack_flat([grads[n] for n in SMALL]),
                   pack_flat([M[n] for n in SMALL]), pack_flat([V[n] for n in SMALL]), "adamw_small")
    for out, flat in zip((delta, new_m, new_v), packed):
        for n, a in zip(SMALL, unpack_flat(flat, shapes)):
            out[n] = a

    return (loss, grad_x, *[grads[n] for n in NAMES], *[delta[n] for n in NAMES], *[new_m[n] for n in NAMES],
            *[new_v[n] for n in NAMES])
```

```python
import numpy as np
import jax
import jax.numpy as jnp
from jax import lax
from jax.experimental import pallas as pl
from jax.experimental.pallas import tpu as pltpu

F32 = jnp.float32
BF16 = jnp.bfloat16
HI = lax.Precision.HIGHEST

D_MODEL = 1024
GW = 256
HEAD_DIM = 64
ZW = 2048
SG_CHUNK = 128
CONV_WIDTH = 31
CONV_PAD = 32
CMP_STRIDE = 16
N_CMP = 128
SLC_BLOCK_SHIFT = 6
N_SLC = 32
SLC_TOPK = 8
WIN = 512
NEG = -1e30
FORCE_BONUS = 1e4
RMS_EPS = 1e-6
LN_EPS = 1e-5
FFN_HIDDEN = 2816
N_DEV = 8
FFN_BLK = 2 * FFN_HIDDEN // N_DEV
TQ = 256
ROW_TILE = 512
CONV_TILE = 256
VMEM_LIMIT = 56 * 1024 * 1024
MESH = pl.DeviceIdType.MESH

ADAM_LR, ADAM_B1, ADAM_B2, ADAM_EPS, ADAM_WD, ADAM_STEP = 0.001, 0.9, 0.999, 1e-08, 0.01, 10

COL_U, COL_V, COL_A, COL_G, COL_Q, COL_D, COL_KV, COL_GL = 0, 256, 512, 768, 1024, 1280, 1536, 1920


def _sds(shape, dtype):
    return jax.ShapeDtypeStruct(shape, dtype)


def _cp(sem=None):
    return pltpu.CompilerParams(dimension_semantics=sem, vmem_limit_bytes=VMEM_LIMIT)


def _tile(n, target, q=128):
    best = None
    for t in range(q, min(n, target) + 1, q):
        if n % t == 0:
            best = t
    return best or n


def _full(shape):
    nd = len(shape)
    return pl.BlockSpec(shape, lambda *_: (0,) * nd)


def _sigmoid(x):
    return jax.nn.sigmoid(x)


def _dot(a, b):
    return jnp.dot(a, b, preferred_element_type=F32)


def _dot_nt(a, b):
    return lax.dot_general(a, b, (((1,), (1,)), ((), ())), preferred_element_type=F32)


def _dot_tn(a, b):
    return lax.dot_general(a, b, (((0,), (0,)), ((), ())), preferred_element_type=F32)


def _lane_head(width=GW):
    return lax.shift_right_logical(lax.broadcasted_iota(jnp.int32, (1, width), 1), 6)


def _fold_heads(x):
    return x + pltpu.roll(x, 64, 1) + pltpu.roll(x, 128, 1) + pltpu.roll(x, 192, 1)


def pack_cols(w):
    pad = jnp.zeros(w.shape[:-1] + (ZW - 1932,), w.dtype)
    return jnp.concatenate([w[..., :1280], w[..., 1676:1932], w[..., 1280:1664], w[..., 1664:1676], pad], axis=-1)


def unpack_cols(wp):
    return jnp.concatenate([wp[..., :1280], wp[..., 1536:1920], wp[..., 1920:1932], wp[..., 1280:1536]], axis=-1)


def mm(a, b, *, ta=False, tb=False, blk=None, out_dtype=F32, name, tm=1024, tn=1024, tk=1024):
    a_dims = ("k", "m") if ta else ("m", "k")
    b_dims = ("n", "k") if tb else ("k", "n")
    a3, b3, o3 = blk in a_dims and blk is not None, blk in b_dims and blk is not None, blk in ("m", "n")
    size = {}
    size[a_dims[0]], size[a_dims[1]] = a.shape[-2:]
    size[b_dims[0]], size[b_dims[1]] = b.shape[-2:]
    nb = a.shape[0] if a3 else (b.shape[0] if b3 else 1)
    tile = {"m": _tile(size["m"], tm), "n": _tile(size["n"], tn), "k": _tile(size["k"], tk)}
    grid = {d: size[d] // tile[d] for d in "mnk"}
    if blk is not None:
        tile[blk] = size[blk]
        grid[blk] = nb
    nk = grid["k"]

    def spec(dims, is3):
        def im(i, j, k):
            g = {"m": i, "n": j, "k": k}
            idx = tuple(0 if d == blk else g[d] for d in dims)
            return ((g[blk],) + idx) if is3 else idx
        shape = (tile[dims[0]], tile[dims[1]])
        return pl.BlockSpec(((None,) + shape) if is3 else shape, im)

    dn = (((0 if ta else 1,), (1 if tb else 0,)), ((), ()))

    def partial(a_ref, b_ref):
        return lax.dot_general(a_ref[...].astype(BF16), b_ref[...].astype(BF16), dn, preferred_element_type=F32)

    def body_single(a_ref, b_ref, o_ref):
        o_ref[...] = partial(a_ref, b_ref).astype(o_ref.dtype)

    def body_acc(a_ref, b_ref, o_ref, acc):
        k = pl.program_id(2)

        @pl.when(k == 0)
        def _():
            acc[...] = partial(a_ref, b_ref)

        @pl.when((k > 0) & (k < nk - 1))
        def _():
            acc[...] += partial(a_ref, b_ref)

        @pl.when(k == nk - 1)
        def _():
            o_ref[...] = (acc[...] + partial(a_ref, b_ref)).astype(o_ref.dtype)

    oshape = ((nb,) if o3 else ()) + (size["m"], size["n"])
    return pl.pallas_call(
        body_single if nk == 1 else body_acc, grid=(grid["m"], grid["n"], nk),
        in_specs=[spec(a_dims, a3), spec(b_dims, b3)], out_specs=spec(("m", "n"), o3),
        out_shape=_sds(oshape, out_dtype),
        scratch_shapes=[] if nk == 1 else [pltpu.VMEM((tile["m"], tile["n"]), F32)],
        compiler_params=_cp(("parallel", "parallel", "arbitrary")), name=name)(a, b)


def mm_kblocks(a, b, *, tb, name, tm):
    nb, M, kb = a.shape
    N = b.shape[1] if tb else b.shape[2]
    dn = (((1,), (1 if tb else 0,)), ((), ()))

    def body(a_ref, b_ref, o_ref):
        acc = lax.dot_general(a_ref[0], b_ref[0], dn, preferred_element_type=F32)
        for j in range(1, nb):
            acc = acc + lax.dot_general(a_ref[j], b_ref[j], dn, preferred_element_type=F32)
        o_ref[...] = acc

    return pl.pallas_call(
        body, grid=(M // tm,), in_specs=[pl.BlockSpec((nb, tm, kb), lambda i: (0, i, 0)), _full(b.shape)],
        out_specs=_rows(tm, N), out_shape=_sds((M, N), F32),
        compiler_params=_cp(("parallel",)), name=name)(a, b)


def _rows(tm, width):
    return pl.BlockSpec((tm, width), lambda i: (i, 0))


def rms_fwd(x, g, name):
    T = x.shape[0]

    def body(x_ref, g_ref, h_ref, ht_ref):
        x = x_ref[...]
        r = lax.rsqrt(jnp.mean(x * x, axis=-1, keepdims=True) + RMS_EPS)
        h = (x * r) * g_ref[...]
        h_ref[...] = h.astype(h_ref.dtype)
        ht_ref[...] = h.T.astype(ht_ref.dtype)

    return pl.pallas_call(body, grid=(T // ROW_TILE,), in_specs=[_rows(ROW_TILE, D_MODEL), _full((1, D_MODEL))],
                          out_specs=[_rows(ROW_TILE, D_MODEL), pl.BlockSpec((D_MODEL, ROW_TILE), lambda i: (0, i))],
                          out_shape=[_sds((T, D_MODEL), BF16), _sds((D_MODEL, T), BF16)],
                          compiler_params=_cp(("parallel",)), name=name)(x, g)


def rms_post_fwd(xres, m, g, name):
    T = m.shape[0]

    def body(x_ref, m_ref, g_ref, o_ref):
        m = m_ref[...]
        r = lax.rsqrt(jnp.mean(m * m, axis=-1, keepdims=True) + RMS_EPS)
        o_ref[...] = x_ref[...] + (m * r) * g_ref[...]

    return pl.pallas_call(body, grid=(T // ROW_TILE,),
                          in_specs=[_rows(ROW_TILE, D_MODEL), _rows(ROW_TILE, D_MODEL), _full((1, D_MODEL))],
                          out_specs=_rows(ROW_TILE, D_MODEL), out_shape=_sds((T, D_MODEL), F32),
                          compiler_params=_cp(("parallel",)), name=name)(xres, m, g)


def rms_bwd(m, g, dy, dres, out_dtype, name):
    T = m.shape[0]
    has_res = dres is not None

    def body(*refs):
        if has_res:
            m_ref, g_ref, dy_ref, dres_ref, dm_ref, dg_ref = refs
        else:
            m_ref, g_ref, dy_ref, dm_ref, dg_ref = refs
        m = m_ref[...]
        dy = dy_ref[...].astype(F32)
        r = lax.rsqrt(jnp.mean(m * m, axis=-1, keepdims=True) + RMS_EPS)
        n = m * r
        dn = dy * g_ref[...]
        dm = r * (dn - n * jnp.mean(dn * n, axis=-1, keepdims=True))
        if has_res:
            dm = dm + dres_ref[...]
        dm_ref[...] = dm.astype(dm_ref.dtype)

        @pl.when(pl.program_id(0) == 0)
        def _():
            dg_ref[...] = jnp.zeros_like(dg_ref)

        dg_ref[...] += jnp.sum(dy * n, axis=0, keepdims=True)

    ins = [m, g, dy] + ([dres] if has_res else [])
    specs = [_rows(ROW_TILE, D_MODEL), _full((1, D_MODEL)), _rows(ROW_TILE, D_MODEL)] + ([_rows(ROW_TILE, D_MODEL)] if has_res else [])
    return pl.pallas_call(body, grid=(T // ROW_TILE,), in_specs=specs,
                          out_specs=[_rows(ROW_TILE, D_MODEL), _full((1, D_MODEL))],
                          out_shape=[_sds((T, D_MODEL), out_dtype), _sds((1, D_MODEL), F32)],
                          compiler_params=_cp(("arbitrary",)), name=name)(*ins)


def loss_fwd_bwd(y, tgt):
    T = y.shape[0]

    def body(y_ref, t_ref, dy_ref, l_ref):
        e = y_ref[...] - t_ref[...]
        dy_ref[...] = e * (1.0 / D_MODEL)

        @pl.when(pl.program_id(0) == 0)
        def _():
            l_ref[...] = jnp.zeros_like(l_ref)

        l_ref[...] += jnp.full(l_ref.shape, 0.5 * jnp.sum(jnp.mean(e * e, axis=-1, keepdims=True)), F32)

    return pl.pallas_call(body, grid=(T // ROW_TILE,), in_specs=[_rows(ROW_TILE, D_MODEL)] * 2,
                          out_specs=[_rows(ROW_TILE, D_MODEL), _full((8, 128))],
                          out_shape=[_sds((T, D_MODEL), F32), _sds((8, 128), F32)],
                          compiler_params=_cp(("arbitrary",)), name="loss")(y, tgt)


FFN_TILE = 2048


def _gu_spec():
    return pl.BlockSpec((2, None, FFN_TILE, FFN_BLK), lambda i, j: (0, j, i, 0))


def ffn_up_fwd(h, w_gu):
    T = h.shape[0]

    def body(h_ref, wg_ref, wu_ref, gu_ref, a_ref):
        h = h_ref[...]
        gate = _dot(h, wg_ref[...])
        up = _dot(h, wu_ref[...])
        gu_ref[0] = gate.astype(gu_ref.dtype)
        gu_ref[1] = up.astype(gu_ref.dtype)
        a_ref[...] = (gate * _sigmoid(gate) * up).astype(a_ref.dtype)

    return pl.pallas_call(
        body, grid=(T // FFN_TILE, 4),
        in_specs=[pl.BlockSpec((FFN_TILE, D_MODEL), lambda i, j: (i, 0)),
                  pl.BlockSpec((None, D_MODEL, FFN_BLK), lambda i, j: (j, 0, 0)),
                  pl.BlockSpec((None, D_MODEL, FFN_BLK), lambda i, j: (j + 4, 0, 0))],
        out_specs=[_gu_spec(), pl.BlockSpec((None, FFN_TILE, FFN_BLK), lambda i, j: (j, i, 0))],
        out_shape=[_sds((2, 4, T, FFN_BLK), BF16), _sds((4, T, FFN_BLK), BF16)],
        compiler_params=_cp(("parallel", "parallel")), name="ffn_up_fwd")(h, w_gu, w_gu)


def ffn_down_dx(df, w_down, gu4):
    T = df.shape[0]
    nsteps = (T // FFN_TILE) * 4
    nbuf = 3

    def body(df_ref, w_ref, gu_hbm, d_ref, buf, sem):
        step = pl.program_id(0) * 4 + pl.program_id(1)

        def fetch(s):
            start = (s // 4) * FFN_TILE
            rows = pl.ds(start if isinstance(start, int) else pl.multiple_of(start, FFN_TILE), FFN_TILE)
            return pltpu.make_async_copy(gu_hbm.at[:, s % 4, rows, :], buf.at[s % nbuf], sem.at[s % nbuf])

        @pl.when(step == 0)
        def _():
            for s in range(min(nbuf - 1, nsteps)):
                fetch(s).start()

        @pl.when(step + nbuf - 1 < nsteps)
        def _():
            fetch(step + nbuf - 1).start()

        fetch(step).wait()
        slot = step % nbuf
        da = _dot_nt(df_ref[...], w_ref[...])
        gate, up = buf[slot, 0].astype(F32), buf[slot, 1].astype(F32)
        sg = _sigmoid(gate)
        d_ref[0] = (da * up * (sg * (1.0 + gate * (1.0 - sg)))).astype(d_ref.dtype)
        d_ref[1] = (da * (gate * sg)).astype(d_ref.dtype)

    return pl.pallas_call(
        body, grid=(T // FFN_TILE, 4),
        in_specs=[pl.BlockSpec((FFN_TILE, D_MODEL), lambda i, j: (i, 0)),
                  pl.BlockSpec((None, FFN_BLK, D_MODEL), lambda i, j: (j, 0, 0)), pl.BlockSpec(memory_space=pl.ANY)],
        out_specs=_gu_spec(), out_shape=_sds((2, 4, T, FFN_BLK), BF16),
        scratch_shapes=[pltpu.VMEM((nbuf, 2, FFN_TILE, FFN_BLK), gu4.dtype), pltpu.SemaphoreType.DMA((nbuf,))],
        compiler_params=_cp(("arbitrary", "arbitrary")), name="ffn_down_dx")(df, w_down, gu4)


def _zcol(tm, col):
    return pl.BlockSpec((tm, GW), lambda i: (i, col // GW))


def _sg_common(v, g):
    mu = jnp.mean(v, axis=-1, keepdims=True)
    xc = v - mu
    rstd = lax.rsqrt(jnp.mean(xc * xc, axis=-1, keepdims=True) + LN_EPS)
    vhat = xc * rstd
    return vhat, rstd, vhat * g


def _tril_weights(w_ref):
    tri = lax.broadcasted_iota(jnp.int32, (SG_CHUNK, SG_CHUNK), 0) >= lax.broadcasted_iota(jnp.int32, (SG_CHUNK, SG_CHUNK), 1)
    return tri, [jnp.where(tri, w_ref[h], 0.0).astype(BF16) for h in range(4)]


def mixa_fwd(z, ln_g, w, bexp):
    T = z.shape[0]
    nch = ROW_TILE // SG_CHUNK

    def body(u_ref, v_ref, g_ref, w_ref, be_ref, y_ref):
        _, _, vln = _sg_common(v_ref[...], g_ref[...])
        vb = vln.astype(BF16)
        head = _lane_head()
        _, wh = _tril_weights(w_ref)
        for c in range(nch):
            rows = slice(c * SG_CHUNK, (c + 1) * SG_CHUNK)
            sv = be_ref[...]
            for h in range(4):
                sv = sv + jnp.where(head == h, _dot(wh[h], vb[rows]), 0.0)
            y_ref[rows, :] = (u_ref[rows, :] * sv).astype(y_ref.dtype)

    return pl.pallas_call(body, grid=(T // ROW_TILE,),
                          in_specs=[_zcol(ROW_TILE, COL_U), _zcol(ROW_TILE, COL_V), _full((1, GW)), _full((4, SG_CHUNK, SG_CHUNK)),
                                    _full((SG_CHUNK, GW))],
                          out_specs=_rows(ROW_TILE, GW), out_shape=_sds((T, GW), BF16),
                          compiler_params=_cp(("parallel",)), name="mixa_fwd")(z, z, ln_g, w, bexp)


def mixa_bwd(z, dycat, ln_g, w, bexp):
    T = z.shape[0]
    nch = ROW_TILE // SG_CHUNK
    nsteps = T // ROW_TILE

    def body(u_ref, v_ref, dy_ref, g_ref, w_ref, be_ref, du_ref, dv_ref, dw_ref, db_ref, dg_ref, dbe_acc):
        step = pl.program_id(0)

        @pl.when(step == 0)
        def _():
            dw_ref[...] = jnp.zeros_like(dw_ref)
            dg_ref[...] = jnp.zeros_like(dg_ref)
            dbe_acc[...] = jnp.zeros_like(dbe_acc)

        g = g_ref[...]
        vhat, rstd, vln = _sg_common(v_ref[...], g)
        vb = vln.astype(BF16)
        head = _lane_head()
        tri, wh = _tril_weights(w_ref)
        dgsum = jnp.zeros((1, GW), F32)
        for c in range(nch):
            rows = slice(c * SG_CHUNK, (c + 1) * SG_CHUNK)
            sv = be_ref[...]
            for h in range(4):
                sv = sv + jnp.where(head == h, _dot(wh[h], vb[rows]), 0.0)
            dy = dy_ref[rows, :]
            du_ref[rows, :] = (dy * sv).astype(du_ref.dtype)
            dsv = dy * u_ref[rows, :]
            dbe_acc[...] += dsv
            dvln = jnp.zeros((SG_CHUNK, GW), F32)
            for h in range(4):
                dsvm = jnp.where(head == h, dsv, 0.0).astype(BF16)
                dw_ref[h] += _dot_nt(dsvm, vb[rows])
                dvln = dvln + _dot_tn(wh[h], dsvm)
            vh = vhat[rows]
            dgsum = dgsum + jnp.sum(dvln * vh, axis=0, keepdims=True)
            dvhat = dvln * g
            dv = rstd[rows] * (dvhat - jnp.mean(dvhat, axis=-1, keepdims=True) - vh * jnp.mean(dvhat * vh, axis=-1, keepdims=True))
            dv_ref[rows, :] = dv.astype(dv_ref.dtype)
        dg_ref[...] += dgsum

        @pl.when(step == nsteps - 1)
        def _():
            for h in range(4):
                dw_ref[h] = jnp.where(tri, dw_ref[h], 0.0)
            fold = (lax.shift_right_logical(lax.broadcasted_iota(jnp.int32, (GW, 128), 0), 6)
                    == lax.broadcasted_iota(jnp.int32, (GW, 128), 1)).astype(F32)
            db_ref[...] = jnp.dot(dbe_acc[...], fold, precision=HI, preferred_element_type=F32)

    return pl.pallas_call(
        body, grid=(nsteps,),
        in_specs=[_zcol(ROW_TILE, COL_U), _zcol(ROW_TILE, COL_V), pl.BlockSpec((ROW_TILE, GW), lambda i: (i, 0)),
                  _full((1, GW)), _full((4, SG_CHUNK, SG_CHUNK)), _full((SG_CHUNK, GW))],
        out_specs=[_rows(ROW_TILE, GW), _rows(ROW_TILE, GW), _full((4, SG_CHUNK, SG_CHUNK)), _full((SG_CHUNK, 128)), _full((1, GW))],
        out_shape=[_sds((T, GW), BF16), _sds((T, GW), BF16), _sds((4, SG_CHUNK, SG_CHUNK), F32), _sds((SG_CHUNK, 128), F32),
                   _sds((1, GW), F32)],
        scratch_shapes=[pltpu.VMEM((SG_CHUNK, GW), F32)],
        compiler_params=_cp(("arbitrary",)), name="mixa_bwd")(z, z, dycat, ln_g, w, bexp)


def _seq(S, col):
    return pl.BlockSpec((None, S, GW), lambda b: (b, 0, col // GW))


def _taps(buf, r0, offsets):
    by_phase = {}
    for k, off in enumerate(offsets):
        by_phase.setdefault(off % 8, []).append((k, off))
    for phase, items in sorted(by_phase.items()):
        span = max(off for _, off in items) - phase
        win = buf[pl.ds(r0 + phase, CONV_TILE + span), :]
        for k, off in items:
            yield k, win[off - phase:off - phase + CONV_TILE]


_CONV_FWD_OFFSETS = [CONV_PAD - (CONV_WIDTH - 1) + k for k in range(CONV_WIDTH)]
_CONV_BWD_OFFSETS = [(CONV_WIDTH - 1) - k for k in range(CONV_WIDTH)]


def _conv(pad, r0, cw_ref, cb):
    acc = jnp.zeros((CONV_TILE, GW), F32) + cb
    for k, rows in _taps(pad, r0, _CONV_FWD_OFFSETS):
        acc = acc + cw_ref[k:k + 1, :] * rows
    return acc


def _conv_ln(acc, lg, lb):
    mu = jnp.mean(acc, axis=-1, keepdims=True)
    xc = acc - mu
    rstd = lax.rsqrt(jnp.mean(xc * xc, axis=-1, keepdims=True) + LN_EPS)
    hhat = xc * rstd
    return hhat, rstd, hhat * lg + lb


def mixb_fwd(z3, cw, cb, lg, lb, pw, pwb):
    B, S, _ = z3.shape

    def body(a_ref, gt_ref, cw_ref, cb_ref, lg_ref, lb_ref, pw_ref, pwb_ref, y_ref, hc_ref, pad):
        pad[0:CONV_PAD, :] = jnp.zeros((CONV_PAD, GW), F32)
        pad[CONV_PAD:CONV_PAD + S, :] = a_ref[...] * _sigmoid(gt_ref[...])
        pwv = pw_ref[...].astype(BF16)
        for r0 in range(0, S, CONV_TILE):
            hc = _conv(pad, r0, cw_ref, cb_ref[...])
            hc_ref[r0:r0 + CONV_TILE, :] = hc
            _, _, ln = _conv_ln(hc, lg_ref[...], lb_ref[...])
            s = ln * _sigmoid(ln)
            y_ref[r0:r0 + CONV_TILE, :] = (_dot(s.astype(BF16), pwv) + pwb_ref[...]).astype(y_ref.dtype)

    seq_out = pl.BlockSpec((None, S, GW), lambda b: (b, 0, 0))
    return pl.pallas_call(
        body, grid=(B,),
        in_specs=[_seq(S, COL_A), _seq(S, COL_G), _full((CONV_WIDTH, GW)), _full((1, GW)), _full((1, GW)), _full((1, GW)),
                  _full((GW, GW)), _full((1, GW))],
        out_specs=[seq_out, seq_out], out_shape=[_sds((B, S, GW), BF16), _sds((B, S, GW), F32)],
        scratch_shapes=[pltpu.VMEM((S + CONV_PAD, GW), F32)],
        compiler_params=_cp(("parallel",)), name="mixb_fwd")(z3, z3, cw, cb, lg, lb, pw, pwb)


def mixb_bwd(z3, hc3, dycat3, cw, lg, lb, pw):
    B, S, _ = z3.shape

    def body(a_ref, gt_ref, hc_ref, dy_ref, cw_ref, lg_ref, lb_ref, pw_ref,
             da_ref, dgt_ref, dcw_ref, dcb_ref, dlg_ref, dlb_ref, dpw_ref, dpwb_ref, pad, dpad, dcw_acc):
        @pl.when(pl.program_id(0) == 0)
        def _():
            for r in (dcb_ref, dlg_ref, dlb_ref, dpw_ref, dpwb_ref, dcw_acc):
                r[...] = jnp.zeros_like(r)

        pad[0:CONV_PAD, :] = jnp.zeros((CONV_PAD, GW), F32)
        pad[CONV_PAD:CONV_PAD + S, :] = a_ref[...] * _sigmoid(gt_ref[...])
        dpad[S:S + CONV_PAD, :] = jnp.zeros((CONV_PAD, GW), F32)
        pwv = pw_ref[...].astype(BF16)
        lg = lg_ref[...]
        for r0 in range(0, S, CONV_TILE):
            hhat, rstd, ln = _conv_ln(hc_ref[r0:r0 + CONV_TILE, :], lg, lb_ref[...])
            sg = _sigmoid(ln)
            s = ln * sg
            dy = dy_ref[r0:r0 + CONV_TILE, :]
            dyb = dy.astype(BF16)
            dpw_ref[...] += _dot_tn(s.astype(BF16), dyb)
            dpwb_ref[...] += jnp.sum(dy, axis=0, keepdims=True)
            dln = _dot_nt(dyb, pwv) * (sg * (1.0 + ln * (1.0 - sg)))
            dlg_ref[...] += jnp.sum(dln * hhat, axis=0, keepdims=True)
            dlb_ref[...] += jnp.sum(dln, axis=0, keepdims=True)
            dhh = dln * lg
            dhc = rstd * (dhh - jnp.mean(dhh, axis=-1, keepdims=True) - hhat * jnp.mean(dhh * hhat, axis=-1, keepdims=True))
            dpad[r0:r0 + CONV_TILE, :] = dhc
            dcb_ref[...] += jnp.sum(dhc, axis=0, keepdims=True)
            for k, rows in _taps(pad, r0, _CONV_FWD_OFFSETS):
                dcw_acc[k] += (dhc * rows).reshape(CONV_TILE // 8, 8, GW).sum(axis=0)
        for r0 in range(0, S, CONV_TILE):
            dhg = jnp.zeros((CONV_TILE, GW), F32)
            for k, rows in _taps(dpad, r0, _CONV_BWD_OFFSETS):
                dhg = dhg + cw_ref[k:k + 1, :] * rows
            a = a_ref[r0:r0 + CONV_TILE, :]
            sg = _sigmoid(gt_ref[r0:r0 + CONV_TILE, :])
            da_ref[r0:r0 + CONV_TILE, :] = (dhg * sg).astype(da_ref.dtype)
            dgt_ref[r0:r0 + CONV_TILE, :] = (dhg * a * sg * (1.0 - sg)).astype(dgt_ref.dtype)

        @pl.when(pl.program_id(0) == B - 1)
        def _():
            for k in range(CONV_WIDTH):
                dcw_ref[k:k + 1, :] = jnp.sum(dcw_acc[k], axis=0, keepdims=True)

    seq_out = pl.BlockSpec((None, S, GW), lambda b: (b, 0, 0))
    return pl.pallas_call(
        body, grid=(B,),
        in_specs=[_seq(S, COL_A), _seq(S, COL_G), seq_out, pl.BlockSpec((None, S, GW), lambda b: (b, 0, 1)),
                  _full((CONV_WIDTH, GW)), _full((1, GW)), _full((1, GW)), _full((GW, GW))],
        out_specs=[seq_out, seq_out, _full((CONV_WIDTH, GW)), _full((1, GW)), _full((1, GW)), _full((1, GW)), _full((GW, GW)),
                   _full((1, GW))],
        out_shape=[_sds((B, S, GW), BF16), _sds((B, S, GW), BF16), _sds((CONV_WIDTH, GW), F32), _sds((1, GW), F32),
                   _sds((1, GW), F32), _sds((1, GW), F32), _sds((GW, GW), F32), _sds((1, GW), F32)],
        scratch_shapes=[pltpu.VMEM((S + CONV_PAD, GW), F32), pltpu.VMEM((S + CONV_PAD, GW), F32),
                        pltpu.VMEM((CONV_WIDTH, 8, GW), F32)],
        compiler_params=_cp(("arbitrary",)), name="mixb_bwd")(z3, z3, hc3, dycat3, cw, lg, lb, pw)


POOL_PAD = 16


def _pool_window():
    lane = lax.broadcasted_iota(jnp.int32, (1, GW), 1)
    return jnp.where(lane < 64, 2, jnp.where(lane < 128, 4, jnp.where(lane < 192, 8, 16)))


def _pool_sums(pad, r0, base, sign):
    win = _pool_window()
    acc = pad[pl.ds(r0 + base, CONV_TILE), :]
    out = None
    for i in range(1, 16):
        acc = acc + pad[pl.ds(r0 + base + sign * i, CONV_TILE), :]
        if i + 1 in (2, 4, 8, 16):
            out = acc if out is None else jnp.where(win == i + 1, acc, out)
    return out


def _pool_cnt(r0):
    t1 = r0 + 1 + lax.broadcasted_iota(jnp.int32, (CONV_TILE, 1), 0)
    return jnp.minimum(t1, _pool_window()).astype(F32)


def mixd_fwd(z3, wbd, scale):
    B, S, _ = z3.shape

    def body(x_ref, w_ref, sc_ref, y_ref, pad):
        pad[0:POOL_PAD, :] = jnp.zeros((POOL_PAD, GW), F32)
        pad[POOL_PAD:POOL_PAD + S, :] = x_ref[...]
        wv = w_ref[...].astype(BF16)
        for r0 in range(0, S, CONV_TILE):
            mean = _pool_sums(pad, r0, POOL_PAD, -1) / _pool_cnt(r0)
            p = (mean - x_ref[r0:r0 + CONV_TILE, :]).astype(BF16)
            y_ref[r0:r0 + CONV_TILE, :] = (_dot(p, wv) * sc_ref[...]).astype(y_ref.dtype)

    return pl.pallas_call(
        body, grid=(B,), in_specs=[_seq(S, COL_D), _full((GW, GW)), _full((1, GW))],
        out_specs=pl.BlockSpec((None, S, GW), lambda b: (b, 0, 0)), out_shape=_sds((B, S, GW), BF16),
        scratch_shapes=[pltpu.VMEM((S + POOL_PAD, GW), F32)],
        compiler_params=_cp(("parallel",)), name="mixd_fwd")(z3, wbd, scale)


def mixd_bwd(z3, dycat3, wbd, scale):
    B, S, _ = z3.shape

    def body(x_ref, dy_ref, w_ref, sc_ref, dx_ref, dw_ref, dsc_ref, pad, qpad):
        @pl.when(pl.program_id(0) == 0)
        def _():
            dw_ref[...] = jnp.zeros_like(dw_ref)
            dsc_ref[...] = jnp.zeros_like(dsc_ref)

        pad[0:POOL_PAD, :] = jnp.zeros((POOL_PAD, GW), F32)
        pad[POOL_PAD:POOL_PAD + S, :] = x_ref[...]
        qpad[S:S + POOL_PAD, :] = jnp.zeros((POOL_PAD, GW), F32)
        wv = w_ref[...].astype(BF16)
        for r0 in range(0, S, CONV_TILE):
            cnt = _pool_cnt(r0)
            mean = _pool_sums(pad, r0, POOL_PAD, -1) / cnt
            p = (mean - x_ref[r0:r0 + CONV_TILE, :]).astype(BF16)
            dy = dy_ref[r0:r0 + CONV_TILE, :]
            dsc_ref[...] += jnp.sum(dy * _dot(p, wv), axis=0, keepdims=True)
            dyp = (dy * sc_ref[...]).astype(BF16)
            dw_ref[...] += _dot_tn(p, dyp)
            dp = _dot_nt(dyp, wv)
            dx_ref[r0:r0 + CONV_TILE, :] = (-dp).astype(dx_ref.dtype)
            qpad[r0:r0 + CONV_TILE, :] = dp / cnt
        for r0 in range(0, S, CONV_TILE):
            back = _pool_sums(qpad, r0, 0, 1)
            dx_ref[r0:r0 + CONV_TILE, :] = (dx_ref[r0:r0 + CONV_TILE, :].astype(F32) + back).astype(dx_ref.dtype)

    return pl.pallas_call(
        body, grid=(B,),
        in_specs=[_seq(S, COL_D), pl.BlockSpec((None, S, GW), lambda b: (b, 0, 3)), _full((GW, GW)), _full((1, GW))],
        out_specs=[pl.BlockSpec((None, S, GW), lambda b: (b, 0, 0)), _full((GW, GW)), _full((1, GW))],
        out_shape=[_sds((B, S, GW), F32), _sds((GW, GW), F32), _sds((1, GW), F32)],
        scratch_shapes=[pltpu.VMEM((S + POOL_PAD, GW), F32), pltpu.VMEM((S + POOL_PAD, GW), F32)],
        compiler_params=_cp(("arbitrary",)), name="mixd_bwd")(z3, dycat3, wbd, scale)


def cmp_kv_fwd(tbk, tbv, pek, pev, w1k, w2k, w1v, w2v):
    B = tbk.shape[0]

    def body(tbk_ref, tbv_ref, pek_ref, pev_ref, w1k_ref, w2k_ref, w1v_ref, w2v_ref, kc_ref, vc_ref):
        for tb_ref, pe_ref, w1_ref, w2_ref, o_ref in ((tbk_ref, pek_ref, w1k_ref, w2k_ref, kc_ref),
                                                      (tbv_ref, pev_ref, w1v_ref, w2v_ref, vc_ref)):
            pre = _dot((tb_ref[...] + pe_ref[...]).astype(BF16), w1_ref[...].astype(BF16))
            hm = pre * _sigmoid(pre)
            o_ref[...] = _dot(hm.astype(BF16), w2_ref[...].astype(BF16))

    tb_spec = pl.BlockSpec((None, N_CMP, 2048), lambda b: (b, 0, 0))
    o_spec = pl.BlockSpec((None, N_CMP, HEAD_DIM), lambda b: (b, 0, 0))
    return pl.pallas_call(
        body, grid=(B,),
        in_specs=[tb_spec, tb_spec, _full((1, 2048)), _full((1, 2048)), _full((2048, HEAD_DIM)), _full((HEAD_DIM, HEAD_DIM)),
                  _full((2048, HEAD_DIM)), _full((HEAD_DIM, HEAD_DIM))],
        out_specs=[o_spec, o_spec], out_shape=[_sds((B, N_CMP, HEAD_DIM), F32)] * 2,
        compiler_params=_cp(("parallel",)), name="cmp_kv_fwd")(tbk, tbv, pek, pev, w1k, w2k, w1v, w2v)


def cmp_kv_bwd(tbk, tbv, pek, pev, w1k, w2k, w1v, w2v, dkc, dvc):
    B = tbk.shape[0]

    def body(tbk_ref, tbv_ref, pek_ref, pev_ref, w1k_ref, w2k_ref, w1v_ref, w2v_ref, dkc_ref, dvc_ref,
             dk2_ref, dv2_ref, dpek_ref, dpev_ref, dw1k_ref, dw2k_ref, dw1v_ref, dw2v_ref):
        @pl.when(pl.program_id(0) == 0)
        def _():
            for r in (dpek_ref, dpev_ref, dw1k_ref, dw2k_ref, dw1v_ref, dw2v_ref):
                r[...] = jnp.zeros_like(r)

        row0 = lax.broadcasted_iota(jnp.int32, (N_CMP, 1), 0) == 0
        for tb_ref, pe_ref, w1_ref, w2_ref, do_ref, d2_ref, dpe_ref, dw1_ref, dw2_ref in (
                (tbk_ref, pek_ref, w1k_ref, w2k_ref, dkc_ref, dk2_ref, dpek_ref, dw1k_ref, dw2k_ref),
                (tbv_ref, pev_ref, w1v_ref, w2v_ref, dvc_ref, dv2_ref, dpev_ref, dw1v_ref, dw2v_ref)):
            tb = (tb_ref[...] + pe_ref[...]).astype(BF16)
            w1 = w1_ref[...].astype(BF16)
            pre = _dot(tb, w1)
            sg = _sigmoid(pre)
            hm = (pre * sg).astype(BF16)
            do = do_ref[...].astype(BF16)
            dw2_ref[...] += _dot_tn(hm, do)
            dpre = (_dot_nt(do, w2_ref[...].astype(BF16)) * (sg * (1.0 + pre * (1.0 - sg)))).astype(BF16)
            dw1_ref[...] += _dot_tn(tb, dpre)
            dtb = _dot_nt(dpre, w1)
            dpe_ref[...] += jnp.sum(dtb, axis=0, keepdims=True)
            down = jnp.where(row0, 0.0, pltpu.roll(dtb[:, 1024:], 1, 0))
            d2_ref[...] = dtb[:, :1024] + down

    tb_spec = pl.BlockSpec((None, N_CMP, 2048), lambda b: (b, 0, 0))
    c_spec = pl.BlockSpec((None, N_CMP, HEAD_DIM), lambda b: (b, 0, 0))
    d2_spec = pl.BlockSpec((None, N_CMP, 1024), lambda b: (b, 0, 0))
    return pl.pallas_call(
        body, grid=(B,),
        in_specs=[tb_spec, tb_spec, _full((1, 2048)), _full((1, 2048)), _full((2048, HEAD_DIM)), _full((HEAD_DIM, HEAD_DIM)),
                  _full((2048, HEAD_DIM)), _full((HEAD_DIM, HEAD_DIM)), c_spec, c_spec],
        out_specs=[d2_spec, d2_spec, _full((1, 2048)), _full((1, 2048)), _full((2048, HEAD_DIM)), _full((HEAD_DIM, HEAD_DIM)),
                   _full((2048, HEAD_DIM)), _full((HEAD_DIM, HEAD_DIM))],
        out_shape=[_sds((B, N_CMP, 1024), F32)] * 2 + [_sds((1, 2048), F32)] * 2
        + [_sds((2048, HEAD_DIM), F32), _sds((HEAD_DIM, HEAD_DIM), F32)] * 2,
        compiler_params=_cp(("arbitrary",)), name="cmp_kv_bwd")(tbk, tbv, pek, pev, w1k, w2k, w1v, w2v, dkc, dvc)


def _qtile(col):
    return pl.BlockSpec((None, TQ, GW), lambda b, i: (b, i, col // GW))


def _qtile0():
    return pl.BlockSpec((None, TQ, GW), lambda b, i: (b, i, 0))


def _cmp_probs(q, kc, qpos):
    head = _lane_head()
    cend = lax.broadcasted_iota(jnp.int32, (1, N_CMP), 1) * CMP_STRIDE + 31
    cmask = cend <= qpos
    has = qpos >= 31
    out = []
    for h in range(4):
        qm = jnp.where(head == h, q, 0.0).astype(BF16)
        s = jnp.where(cmask, _dot_nt(qm, kc), NEG)
        e = jnp.exp(s - jnp.max(s, axis=-1, keepdims=True))
        p = jnp.where(has, e / jnp.sum(e, axis=-1, keepdims=True), 0.0)
        out.append((qm, p))
    return out


def cmp_attn_fwd(z3, kc4, vc4):
    B, S, _ = z3.shape

    def body(q_ref, kc_ref, vc_ref, o_ref, sel_ref):
        t0 = pl.program_id(1) * TQ
        qpos = t0 + lax.broadcasted_iota(jnp.int32, (TQ, 1), 0)
        head = _lane_head()
        kc, vc = kc_ref[...], vc_ref[...]
        o = jnp.zeros((TQ, GW), F32)
        psum = jnp.zeros((TQ, N_CMP), F32)
        for h, (_, p) in enumerate(_cmp_probs(q_ref[...] * 0.125, kc, qpos)):
            o = o + jnp.where(head == h, _dot(p.astype(BF16), vc), 0.0)
            psum = psum + p
        o_ref[...] = o
        cst = lax.broadcasted_iota(jnp.int32, (N_SLC, N_CMP), 1) * CMP_STRIDE
        jst = lax.broadcasted_iota(jnp.int32, (N_SLC, N_CMP), 0) * 64
        overlap = ((cst <= jst + 63) & (cst + 31 >= jst)).astype(BF16)
        imp = _dot_nt(overlap, psum.astype(BF16))
        qp = t0 + lax.broadcasted_iota(jnp.int32, (1, TQ), 1)
        jj = lax.broadcasted_iota(jnp.int32, (N_SLC, 1), 0)
        cur = lax.shift_right_logical(qp, SLC_BLOCK_SHIFT)
        forced = (jj == 0) | (jj == cur) | (jj == cur - 1)
        score = jnp.where(jj * 64 <= qp, imp + jnp.where(forced, FORCE_BONUS, 0.0), NEG)
        rank = jnp.zeros((N_SLC, TQ), F32)
        for j2 in range(N_SLC):
            sj = score[j2:j2 + 1, :]
            rank = rank + jnp.where((sj > score) | ((sj == score) & (j2 < jj)), 1.0, 0.0)
        sel_ref[...] = jnp.where((rank < SLC_TOPK) & (score > NEG / 2), 1.0, 0.0)

    c_spec = pl.BlockSpec((None, N_CMP, GW), lambda b, i: (b, 0, 0))
    return pl.pallas_call(
        body, grid=(B, S // TQ), in_specs=[_qtile(COL_Q), c_spec, c_spec],
        out_specs=[_qtile0(), pl.BlockSpec((None, N_SLC, TQ), lambda b, i: (b, 0, i))],
        out_shape=[_sds((B, S, GW), F32), _sds((B, N_SLC, S), F32)],
        compiler_params=_cp(("parallel", "parallel")), name="cmp_attn_fwd")(z3, kc4, vc4)


def cmp_attn_bwd(z3, kc4, vc4, do):
    B, S, _ = z3.shape
    nq = S // TQ

    def body(q_ref, kc_ref, vc_ref, do_ref, dq_ref, dkc_out, dvc_out, dkc_ref, dvc_ref):
        qi = pl.program_id(1)

        @pl.when(qi == 0)
        def _():
            dkc_ref[...] = jnp.zeros_like(dkc_ref)
            dvc_ref[...] = jnp.zeros_like(dvc_ref)

        qpos = qi * TQ + lax.broadcasted_iota(jnp.int32, (TQ, 1), 0)
        head = _lane_head()
        kc, vc, do = kc_ref[...], vc_ref[...], do_ref[...]
        dq = jnp.zeros((TQ, GW), F32)
        for h, (qm, p) in enumerate(_cmp_probs(q_ref[...] * 0.125, kc, qpos)):
            dom = jnp.where(head == h, do, 0.0).astype(BF16)
            dp = _dot_nt(dom, vc)
            ds = (p * (dp - jnp.sum(p * dp, axis=-1, keepdims=True))).astype(BF16)
            dq = dq + jnp.where(head == h, _dot(ds, kc), 0.0)
            dkc_ref[...] += _dot_tn(ds, qm)
            dvc_ref[...] += _dot_tn(p.astype(BF16), dom)
        dq_ref[...] = dq * 0.125

        @pl.when(qi == nq - 1)
        def _():
            dkc_out[...] = _fold_heads(dkc_ref[...])[:, :HEAD_DIM]
            dvc_out[...] = _fold_heads(dvc_ref[...])[:, :HEAD_DIM]

    c_spec = pl.BlockSpec((None, N_CMP, GW), lambda b, i: (b, 0, 0))
    d_spec = pl.BlockSpec((None, N_CMP, HEAD_DIM), lambda b, i: (b, 0, 0))
    return pl.pallas_call(
        body, grid=(B, nq), in_specs=[_qtile(COL_Q), c_spec, c_spec, _qtile0()],
        out_specs=[_qtile0(), d_spec, d_spec],
        out_shape=[_sds((B, S, GW), F32), _sds((B, N_CMP, HEAD_DIM), F32), _sds((B, N_CMP, HEAD_DIM), F32)],
        scratch_shapes=[pltpu.VMEM((N_CMP, GW), F32), pltpu.VMEM((N_CMP, GW), F32)],
        compiler_params=_cp(("parallel", "arbitrary")), name="cmp_attn_bwd")(z3, kc4, vc4, do)


def _attn_mask(mode, qpos, k0, sel_b):
    kpos = k0 + lax.broadcasted_iota(jnp.int32, (1, TQ), 1)
    mask = kpos <= qpos
    if mode == "win":
        return mask & (kpos > qpos - WIN)
    blk = lax.shift_right_logical(k0 + lax.broadcasted_iota(jnp.int32, (N_SLC, TQ), 1), SLC_BLOCK_SHIFT)
    expand = (blk == lax.broadcasted_iota(jnp.int32, (N_SLC, TQ), 0)).astype(BF16)
    return mask & (_dot_tn(sel_b, expand) > 0.5)


def _attn_lo(mode, qi):
    return jnp.maximum(qi - WIN // TQ, 0) if mode == "win" else 0


def attn_fwd(mode, z3, k4, v4, selT):
    B, S, _ = z3.shape

    def body(q_ref, k_ref, v_ref, sel_ref, o_ref, lse_ref, s_all, m_acc, l_acc, o_acc):
        qi = pl.program_id(1)
        qpos = qi * TQ + lax.broadcasted_iota(jnp.int32, (TQ, 1), 0)
        head = _lane_head()
        q = q_ref[...] * 0.125
        qm = [jnp.where(head == h, q, 0.0).astype(BF16) for h in range(4)]
        sel_b = sel_ref[...].astype(BF16)
        lo, hi = _attn_lo(mode, qi), qi + 1
        m_acc[...] = jnp.full(m_acc.shape, NEG, F32)

        def scores(kb, carry):
            k0 = pl.multiple_of(kb * TQ, TQ)
            kblk = k_ref[pl.ds(k0, TQ), :]
            mask = _attn_mask(mode, qpos, k0, sel_b)
            for h in range(4):
                s = jnp.where(mask, _dot_nt(qm[h], kblk), NEG)
                s_all[h, kb] = s
                m_acc[h] = jnp.maximum(m_acc[h], s)
            return carry

        lax.fori_loop(lo, hi, scores, 0)
        for h in range(4):
            m_acc[h] = jnp.broadcast_to(jnp.max(m_acc[h], axis=-1, keepdims=True), (TQ, TQ))
        l_acc[...] = jnp.zeros_like(l_acc)
        o_acc[...] = jnp.zeros_like(o_acc)

        def weights(kb, carry):
            vblk = v_ref[pl.ds(pl.multiple_of(kb * TQ, TQ), TQ), :]
            for h in range(4):
                p = jnp.exp(s_all[h, kb] - m_acc[h])
                l_acc[h] += p
                o_acc[h] += _dot(p.astype(BF16), vblk)
            return carry

        lax.fori_loop(lo, hi, weights, 0)
        o = jnp.zeros((TQ, GW), F32)
        lse = jnp.zeros((TQ, 128), F32)
        lane = lax.broadcasted_iota(jnp.int32, (1, 128), 1)
        for h in range(4):
            l = jnp.sum(l_acc[h], axis=-1, keepdims=True)
            o = o + jnp.where(head == h, o_acc[h] / l, 0.0)
            lse = jnp.where(lane == h, jnp.max(m_acc[h], axis=-1, keepdims=True) + jnp.log(l), lse)
        o_ref[...] = o
        lse_ref[...] = lse

    kv_spec = pl.BlockSpec((None, S, GW), lambda b, i: (b, 0, 0))
    return pl.pallas_call(
        body, grid=(B, S // TQ),
        in_specs=[_qtile(COL_Q), kv_spec, kv_spec, pl.BlockSpec((None, N_SLC, TQ), lambda b, i: (b, 0, i))],
        out_specs=[_qtile0(), pl.BlockSpec((None, TQ, 128), lambda b, i: (b, i, 0))],
        out_shape=[_sds((B, S, GW), F32), _sds((B, S, 128), F32)],
        scratch_shapes=[pltpu.VMEM((4, S // TQ, TQ, TQ), F32), pltpu.VMEM((4, TQ, TQ), F32), pltpu.VMEM((4, TQ, TQ), F32),
                        pltpu.VMEM((4, TQ, GW), F32)],
        compiler_params=_cp(("parallel", "parallel")), name=mode + "_attn_fwd")(z3, k4, v4, selT)


def attn_bwd(mode, z3, k4, v4, selT, o, lse, do):
    B, S, _ = z3.shape
    nq = S // TQ

    def body(q_ref, k_ref, v_ref, sel_ref, o_ref, lse_ref, do_ref, dq_ref, dk_out, dv_out, dq_s, dk_ref, dv_ref):
        qi = pl.program_id(1)

        @pl.when(qi == 0)
        def _():
            dk_ref[...] = jnp.zeros_like(dk_ref)
            dv_ref[...] = jnp.zeros_like(dv_ref)

        qpos = qi * TQ + lax.broadcasted_iota(jnp.int32, (TQ, 1), 0)
        head = _lane_head()
        lane = lax.broadcasted_iota(jnp.int32, (1, 128), 1)
        q = q_ref[...] * 0.125
        do = do_ref[...]
        doo = do * o_ref[...]
        lse = lse_ref[...]
        qm = [jnp.where(head == h, q, 0.0).astype(BF16) for h in range(4)]
        dom = [jnp.where(head == h, do, 0.0).astype(BF16) for h in range(4)]
        delta = [jnp.sum(jnp.where(head == h, doo, 0.0), axis=-1, keepdims=True) for h in range(4)]
        lse_h = [jnp.max(jnp.where(lane == h, lse, NEG), axis=-1, keepdims=True) for h in range(4)]
        sel_b = sel_ref[...].astype(BF16)
        dq_s[...] = jnp.zeros_like(dq_s)

        def step(kb, carry):
            k0 = pl.multiple_of(kb * TQ, TQ)
            kblk = k_ref[pl.ds(k0, TQ), :]
            vblk = v_ref[pl.ds(k0, TQ), :]
            mask = _attn_mask(mode, qpos, k0, sel_b)
            for h in range(4):
                s = _dot_nt(qm[h], kblk)
                p = jnp.where(mask, jnp.exp(s - lse_h[h]), 0.0)
                dp = _dot_nt(dom[h], vblk)
                ds = (p * (dp - delta[h])).astype(BF16)
                dq_s[...] += jnp.where(head == h, _dot(ds, kblk), 0.0)
                dk_ref[pl.ds(k0, TQ), :] += _dot_tn(ds, qm[h])
                dv_ref[pl.ds(k0, TQ), :] += _dot_tn(p.astype(BF16), dom[h])
            return carry

        lax.fori_loop(_attn_lo(mode, qi), qi + 1, step, 0)
        dq_ref[...] = dq_s[...] * 0.125

        @pl.when(qi == nq - 1)
        def _():
            for r0 in range(0, S, TQ):
                dk_out[r0:r0 + TQ, :] = _fold_heads(dk_ref[r0:r0 + TQ, :])[:, :HEAD_DIM]
                dv_out[r0:r0 + TQ, :] = _fold_heads(dv_ref[r0:r0 + TQ, :])[:, :HEAD_DIM]

    kv_spec = pl.BlockSpec((None, S, GW), lambda b, i: (b, 0, 0))
    return pl.pallas_call(
        body, grid=(B, nq),
        in_specs=[_qtile(COL_Q), kv_spec, kv_spec, pl.BlockSpec((None, N_SLC, TQ), lambda b, i: (b, 0, i)), _qtile0(),
                  pl.BlockSpec((None, TQ, 128), lambda b, i: (b, i, 0)), _qtile0()],
        out_specs=[_qtile0(), pl.BlockSpec((None, S, HEAD_DIM), lambda b, i: (b, 0, 0)),
                   pl.BlockSpec((None, S, HEAD_DIM), lambda b, i: (b, 0, 0))],
        out_shape=[_sds((B, S, GW), F32), _sds((B, S, HEAD_DIM), F32), _sds((B, S, HEAD_DIM), F32)],
        scratch_shapes=[pltpu.VMEM((TQ, GW), F32), pltpu.VMEM((S, GW), F32), pltpu.VMEM((S, GW), F32)],
        compiler_params=_cp(("parallel", "arbitrary")), name=mode + "_attn_bwd")(z3, k4, v4, selT, o, lse, do)


def _gate_expand(g, b):
    head = _lane_head()
    out = jnp.zeros((TQ, GW), F32)
    for h in range(4):
        out = jnp.where(head == h, g[:, 3 * h + b:3 * h + b + 1], out)
    return out


def combine_fwd(z3, o_cmp, o_slc, o_win):
    B, S, _ = z3.shape

    def body(gl_ref, oc_ref, os_ref, ow_ref, y_ref):
        g = _sigmoid(gl_ref[...])
        y = jnp.zeros((TQ, GW), F32)
        for b, o_ref in enumerate((oc_ref, os_ref, ow_ref)):
            y = y + _gate_expand(g, b) * o_ref[...]
        y_ref[...] = y.astype(y_ref.dtype)

    return pl.pallas_call(
        body, grid=(B, S // TQ),
        in_specs=[pl.BlockSpec((None, TQ, 128), lambda b, i: (b, i, COL_GL // 128)), _qtile0(), _qtile0(), _qtile0()],
        out_specs=_qtile0(), out_shape=_sds((B, S, GW), BF16),
        compiler_params=_cp(("parallel", "parallel")), name="combine_fwd")(z3, o_cmp, o_slc, o_win)


def combine_bwd(z3, o_cmp, o_slc, o_win, dycat3):
    B, S, _ = z3.shape

    def body(gl_ref, oc_ref, os_ref, ow_ref, dy_ref, dc_ref, ds_ref, dw_ref, dgl_ref):
        g = _sigmoid(gl_ref[...])
        dy = dy_ref[...]
        head = _lane_head()
        lane = lax.broadcasted_iota(jnp.int32, (1, 128), 1)
        dg = jnp.zeros((TQ, 128), F32)
        for b, (o_ref, d_ref) in enumerate(((oc_ref, dc_ref), (os_ref, ds_ref), (ow_ref, dw_ref))):
            d_ref[...] = _gate_expand(g, b) * dy
            t = dy * o_ref[...]
            for h in range(4):
                dg = jnp.where(lane == 3 * h + b, jnp.sum(jnp.where(head == h, t, 0.0), axis=-1, keepdims=True), dg)
        dgl_ref[...] = dg * g * (1.0 - g)

    gl_spec = pl.BlockSpec((None, TQ, 128), lambda b, i: (b, i, COL_GL // 128))
    return pl.pallas_call(
        body, grid=(B, S // TQ),
        in_specs=[gl_spec, _qtile0(), _qtile0(), _qtile0(), pl.BlockSpec((None, TQ, GW), lambda b, i: (b, i, 2))],
        out_specs=[_qtile0(), _qtile0(), _qtile0(), pl.BlockSpec((None, TQ, 128), lambda b, i: (b, i, 0))],
        out_shape=[_sds((B, S, GW), F32)] * 3 + [_sds((B, S, 128), F32)],
        compiler_params=_cp(("parallel", "parallel")), name="combine_bwd")(z3, o_cmp, o_slc, o_win, dycat3)


def assemble_dz(du, dv, da, dgt, dq_c, dq_s, dq_w, dd, dkvs, dgl):
    T = du.shape[0]

    def body(du_ref, dv_ref, da_ref, dgt_ref, dqc_ref, dqs_ref, dqw_ref, dd_ref, dgl_ref, *rest):
        kv_refs, o_ref = rest[:6], rest[6]
        o_ref[:, COL_U:COL_U + GW] = du_ref[...]
        o_ref[:, COL_V:COL_V + GW] = dv_ref[...]
        o_ref[:, COL_A:COL_A + GW] = da_ref[...]
        o_ref[:, COL_G:COL_G + GW] = dgt_ref[...]
        o_ref[:, COL_Q:COL_Q + GW] = (dqc_ref[...] + dqs_ref[...] + dqw_ref[...]).astype(BF16)
        o_ref[:, COL_D:COL_D + GW] = dd_ref[...].astype(BF16)
        for i, kv_ref in enumerate(kv_refs):
            o_ref[:, COL_KV + i * HEAD_DIM:COL_KV + (i + 1) * HEAD_DIM] = kv_ref[...].astype(BF16)
        o_ref[:, COL_GL:COL_GL + 128] = dgl_ref[...].astype(BF16)

    specs = [_rows(ROW_TILE, GW)] * 8 + [_rows(ROW_TILE, 128)] + [_rows(ROW_TILE, HEAD_DIM)] * 6
    return pl.pallas_call(body, grid=(T // ROW_TILE,), in_specs=specs, out_specs=_rows(ROW_TILE, ZW),
                          out_shape=_sds((T, ZW), BF16), compiler_params=_cp(("parallel",)),
                          name="assemble_dz")(du, dv, da, dgt, dq_c, dq_s, dq_w, dd, dgl, *dkvs)


def _my_pos():
    return lax.axis_index("x"), lax.axis_index("y"), lax.axis_index("c")


def _peer(k):
    x, y, c = _my_pos()
    return ((1 - x) if k & 4 else x, (1 - y) if k & 2 else y, (1 - c) if k & 1 else c)


def _index(pos):
    return 4 * pos[0] + 2 * pos[1] + pos[2]


_HBM = pl.BlockSpec(memory_space=pltpu.HBM)


_SEM = pl.BlockSpec(memory_space=pltpu.SEMAPHORE)
_EFFECT = pltpu.SideEffectType.DATAFLOW_SIDE_EFFECTING


def _exchange_copies(kinds, srcs, lands, send, recv):
    me = _index(_my_pos())
    out = []
    for a, kind in enumerate(kinds):
        for k in range(N_DEV):
            peer = _peer(k)
            if kind == "gather":
                r = srcs[a].shape[1]
                src, dst = srcs[a], lands[a].at[:, pl.ds(me * r, r), :]
            else:
                r = srcs[a].shape[1] // N_DEV
                src, dst = srcs[a].at[:, pl.ds(_index(peer) * r, r), :], lands[a].at[me]
            sem = a * N_DEV + k
            out.append(pltpu.make_async_remote_copy(src_ref=src, dst_ref=dst, send_sem=send.at[sem], recv_sem=recv.at[sem],
                                                    device_id=peer, device_id_type=MESH))
    return out


def _landing_zone(kind, src):
    _, r, C = src.shape
    return lax.empty((1, N_DEV * r, C) if kind == "gather" else (N_DEV, 1, r // N_DEV, C), src.dtype)


def exchange_start(kinds, srcs, name):
    n = len(srcs)
    lands = [_landing_zone(k, s) for k, s in zip(kinds, srcs)]

    def body(*refs):
        s, l = refs[:n], refs[n:2 * n]
        send, recv = refs[2 * n], refs[2 * n + 1]
        for cp in _exchange_copies(kinds, s, l, send, recv):
            cp.start()
        refs[-1][...] = jnp.zeros((8, 128), F32)

    hbm = [pltpu.HBM(a.shape, a.dtype) for a in srcs + lands]
    outs = pl.pallas_call(
        body, name=name,
        out_shape=(pltpu.SemaphoreType.DMA((n * N_DEV,)), pltpu.SemaphoreType.DMA((n * N_DEV,)), *hbm,
                   _sds((8, 128), F32)),
        in_specs=[_HBM] * (2 * n), out_specs=(_SEM, _SEM, *([_HBM] * (2 * n)), pl.BlockSpec(memory_space=pltpu.VMEM)),
        input_output_aliases={i: 2 + i for i in range(2 * n)},
        compiler_params=pltpu.CompilerParams(has_side_effects=_EFFECT),
    )(*[pltpu.with_memory_space_constraint(a, pltpu.HBM) for a in srcs + lands])
    return outs[0], outs[1], list(outs[2:2 + n]), list(outs[2 + n:2 + 2 * n]), outs[-1]


def exchange_wait(kinds, started, after, name):
    send, recv, srcs, lands, _ = started
    n = len(srcs)
    after = list(after) if isinstance(after, (list, tuple)) else [after]

    def body(*refs):
        s, l = refs[:n], refs[n:2 * n]
        for cp in _exchange_copies(kinds, s, l, refs[2 * n], refs[2 * n + 1]):
            cp.wait_send()
            cp.wait_recv()
        refs[-1][...] = jnp.zeros((8, 128), F32)

    outs = pl.pallas_call(
        body, name=name, out_shape=[pltpu.HBM(a.shape, a.dtype) for a in srcs + lands] + [_sds((8, 128), F32)],
        in_specs=[_HBM] * (2 * n) + [_SEM, _SEM] + [pl.BlockSpec(memory_space=pl.ANY)] * len(after),
        out_specs=[_HBM] * (2 * n) + [pl.BlockSpec(memory_space=pltpu.VMEM)],
        input_output_aliases={i: i for i in range(2 * n)},
        compiler_params=pltpu.CompilerParams(has_side_effects=_EFFECT),
    )(*srcs, *lands, send, recv, *after)
    return list(outs[n:2 * n]), outs[-1]


def sum_slots(lands, name):
    L = len(lands)
    _, _, r, C = lands[0].shape
    tr = _tile(r, 256, 16)

    def body(*refs):
        o_ref = refs[L]
        for l in range(L):
            @pl.when(pl.program_id(0) == l)
            def _(x_ref=refs[l]):
                acc = x_ref[0].astype(F32)
                for s in range(1, N_DEV):
                    acc = acc + x_ref[s].astype(F32)
                o_ref[...] = acc

    specs = [pl.BlockSpec((N_DEV, None, tr, C), lambda g, i, l=l: (0, 0, jnp.where(g == l, i, 0), 0)) for l in range(L)]
    return pl.pallas_call(
        body, grid=(L, r // tr), in_specs=specs,
        out_specs=pl.BlockSpec((None, tr, C), lambda g, i: (g, i, 0)), out_shape=_sds((L, r, C), F32),
        compiler_params=_cp(("arbitrary", "arbitrary")), name=name)(*lands)


def pack_flat(arrs):
    flat = jnp.concatenate([a.reshape(-1).astype(F32) for a in arrs])
    n = flat.shape[0]
    total = -(-n // 32768) * 32768
    return jnp.pad(flat, (0, total - n)).reshape(total // 128, 128)


def unpack_flat(flat, shapes):
    v = flat.reshape(-1)
    out, off = [], 0
    for s in shapes:
        n = int(np.prod(s))
        out.append(v[off:off + n].reshape(s))
        off += n
    return out


def adamw(w, g, m, v, name):
    shape = w.shape
    C = shape[-1]
    R = int(np.prod(shape)) // C
    tr = _tile(R, 128, 8)
    c1 = 1.0 - ADAM_B1 ** ADAM_STEP
    c2 = 1.0 - ADAM_B2 ** ADAM_STEP

    def body(w_ref, g_ref, m_ref, v_ref, d_ref, nm_ref, nv_ref):
        g = g_ref[...]
        m2 = ADAM_B1 * m_ref[...] + (1.0 - ADAM_B1) * g
        v2 = ADAM_B2 * v_ref[...] + (1.0 - ADAM_B2) * (g * g)
        nm_ref[...] = m2
        nv_ref[...] = v2
        d_ref[...] = -ADAM_LR * ((m2 / c1) / (jnp.sqrt(v2 / c2) + ADAM_EPS) + ADAM_WD * w_ref[...])

    spec = pl.BlockSpec((tr, C), lambda i: (i, 0))
    outs = pl.pallas_call(body, grid=(R // tr,), in_specs=[spec] * 4, out_specs=[spec] * 3,
                          out_shape=[_sds((R, C), F32)] * 3, compiler_params=_cp(("parallel",)), name=name)(
        w.reshape(R, C), g.reshape(R, C), m.reshape(R, C), v.reshape(R, C))
    return [o.reshape(shape) for o in outs]


def _bexp(sg_b):
    return jnp.repeat(sg_b.T, HEAD_DIM, axis=1)


def _block_diag(pool_w):
    out = jnp.zeros((GW, GW), F32)
    for i in range(4):
        out = out.at[i * 64:(i + 1) * 64, i * 64:(i + 1) * 64].set(pool_w[i])
    return out


def _cmp_rows(t):
    B, S, _ = t.shape
    t2 = t.reshape(B, S // CMP_STRIDE, CMP_STRIDE * HEAD_DIM)
    nxt = jnp.concatenate([t2[:, 1:], jnp.zeros_like(t2[:, :1])], axis=1)
    return jnp.concatenate([t2, nxt], axis=-1)


def _tile4(t):
    return jnp.tile(t, (1, 1, 4)).astype(BF16)


def kv_tiles(z):
    T = z.shape[0]

    def body(x_ref, cv_ref, ks_ref, vs_ref, kw_ref, vw_ref):
        x = x_ref[...]
        cv_ref[...] = x[:, :128]
        xb = x.astype(BF16)
        src = lax.broadcasted_iota(jnp.int32, (384, GW), 0)
        lane = lax.broadcasted_iota(jnp.int32, (384, GW), 1) & 63
        for i, o_ref in enumerate((ks_ref, vs_ref, kw_ref, vw_ref)):
            expand = (src == lane + 64 * (i + 2)).astype(BF16)
            o_ref[...] = _dot(xb, expand).astype(o_ref.dtype)

    return pl.pallas_call(
        body, grid=(T // ROW_TILE,), in_specs=[pl.BlockSpec((ROW_TILE, 384), lambda i: (i, COL_KV // 384))],
        out_specs=[_rows(ROW_TILE, 128)] + [_rows(ROW_TILE, GW)] * 4,
        out_shape=[_sds((T, 128), F32)] + [_sds((T, GW), BF16)] * 4,
        compiler_params=_cp(("parallel",)), name="kv_tiles")(z)


def layer_fwd(x, p, late, B, S):
    T = B * S
    sv = {"x0": x}
    h1, h1t = rms_fwd(x, p["g_pre_mix"], "rms_pre_mix")
    z = mm(h1, p["w_in"], name="mm_in", tn=2048)
    z3 = z.reshape(B, S, ZW)
    ya = mixa_fwd(z, p["sg_ln_g"], p["sg_w"], p["bexp"])
    yb, hc = mixb_fwd(z3, p["cv_w"], p["cv_b"], p["cv_ln_g"], p["cv_ln_b"], p["cv_pw"], p["cv_pw_b"])
    kcv, ks4, vs4, kw4, vw4 = [a.reshape(B, S, -1) for a in kv_tiles(z)]
    tbk, tbv = _cmp_rows(kcv[:, :, :HEAD_DIM]), _cmp_rows(kcv[:, :, HEAD_DIM:])
    kc, vc = cmp_kv_fwd(tbk, tbv, p["cmp_pos_k"], p["cmp_pos_v"], p["cmp_w1_k"], p["cmp_w2_k"], p["cmp_w1_v"], p["cmp_w2_v"])
    kc4, vc4 = _tile4(kc), _tile4(vc)
    o_cmp, selT = cmp_attn_fwd(z3, kc4, vc4)
    o_slc, lse_slc = attn_fwd("slc", z3, ks4, vs4, selT)
    o_win, lse_win = attn_fwd("win", z3, kw4, vw4, selT)
    yc = combine_fwd(z3, o_cmp, o_slc, o_win)
    yd = mixd_fwd(z3, p["pool_bd"], p["pool_scale"])
    ycat = jnp.concatenate([ya, yb.reshape(T, GW), yc.reshape(T, GW), yd.reshape(T, GW)], axis=-1)
    p.update(late(ycat))
    mix = mm(ycat, p["w_out"], name="mm_out")
    x1 = rms_post_fwd(x, mix, p["g_post_mix"], "rms_post_mix")
    h2, h2t = rms_fwd(x1, p["g_pre_ffn"], "rms_pre_ffn")
    gu4, a3 = ffn_up_fwd(h2, p["w_gu"])
    f = mm_kblocks(a3, p["w_down"], tb=False, name="mm_down", tm=1024)
    x2 = rms_post_fwd(x1, f, p["g_post_ffn"], "rms_post_ffn")
    sv.update(h1t=h1t, z=z, hc=hc, tbk=tbk, tbv=tbv, kc4=kc4, vc4=vc4, ks4=ks4, vs4=vs4, kw4=kw4, vw4=vw4, o_cmp=o_cmp, selT=selT,
              o_slc=o_slc, lse_slc=lse_slc, o_win=o_win, lse_win=lse_win, ycat=ycat, mix=mix, x1=x1, h2t=h2t, gu4=gu4, a3=a3, f=f)
    return x2, sv


def layer_bwd_ffn(dx2, p, sv, B, S):
    T = B * S
    gb, gs = {}, {}
    df, gs["g_post_ffn"] = rms_bwd(sv["f"], p["g_post_ffn"], dx2, None, BF16, "rms_post_ffn_bwd")
    dgu = ffn_down_dx(df, p["w_down"], sv["gu4"]).reshape(N_DEV, T, FFN_BLK)
    gb["w_down"] = mm(sv["a3"], df, ta=True, blk="m", out_dtype=BF16, name="mm_down_dw", tk=4096)
    dh2 = mm_kblocks(dgu, p["w_gu"], tb=True, name="mm_gu_dx", tm=512)
    gb["w_gu"] = mm(sv["h2t"], dgu, blk="n", out_dtype=BF16, name="mm_gu_dw", tk=4096)
    dx1, gs["g_pre_ffn"] = rms_bwd(sv["x1"], p["g_pre_ffn"], dh2, dx2, F32, "rms_pre_ffn_bwd")
    gb["w_gu"] = gb["w_gu"].reshape(1, N_DEV * D_MODEL, FFN_BLK)
    gb["w_down"] = gb["w_down"].reshape(1, FFN_HIDDEN, D_MODEL)
    return dx1, gb, gs


def layer_bwd_mix(dx1, p, sv, B, S):
    T = B * S
    gb, gs = {}, {}
    dmix, gs["g_post_mix"] = rms_bwd(sv["mix"], p["g_post_mix"], dx1, None, BF16, "rms_post_mix_bwd")
    dycat = mm(dmix, p["w_out"], tb=True, name="mm_out_dx")
    gb["w_out"] = mm(sv["ycat"], dmix, ta=True, out_dtype=BF16, name="mm_out_dw")
    dycat3 = dycat.reshape(B, S, D_MODEL)
    z = sv["z"]
    z3 = z.reshape(B, S, ZW)
    du, dv, gs["sg_w"], db, gs["sg_ln_g"] = mixa_bwd(z, dycat, p["sg_ln_g"], p["sg_w"], p["bexp"])
    gs["sg_b"] = db[:, :4].T
    (da, dgt, gs["cv_w"], gs["cv_b"], gs["cv_ln_g"], gs["cv_ln_b"], gpw, gs["cv_pw_b"]) = mixb_bwd(
        z3, sv["hc"], dycat3, p["cv_w"], p["cv_ln_g"], p["cv_ln_b"], p["cv_pw"])
    gb["cv_pw"] = gpw.astype(BF16)
    dd, dwbd, gs["pool_scale"] = mixd_bwd(z3, dycat3, p["pool_bd"], p["pool_scale"])
    gs["pool_w"] = jnp.stack([dwbd[i * 64:(i + 1) * 64, i * 64:(i + 1) * 64] for i in range(4)])
    do_c, do_s, do_w, dgl = combine_bwd(z3, sv["o_cmp"], sv["o_slc"], sv["o_win"], dycat3)
    dq_s, dks, dvs = attn_bwd("slc", z3, sv["ks4"], sv["vs4"], sv["selT"], sv["o_slc"], sv["lse_slc"], do_s)
    dq_w, dkw, dvw = attn_bwd("win", z3, sv["kw4"], sv["vw4"], sv["selT"], sv["o_win"], sv["lse_win"], do_w)
    dq_c, dkc, dvc = cmp_attn_bwd(z3, sv["kc4"], sv["vc4"], do_c)
    (dk2, dv2, gs["cmp_pos_k"], gs["cmp_pos_v"], gw1k, gs["cmp_w2_k"], gw1v, gs["cmp_w2_v"]) = cmp_kv_bwd(
        sv["tbk"], sv["tbv"], p["cmp_pos_k"], p["cmp_pos_v"], p["cmp_w1_k"], p["cmp_w2_k"], p["cmp_w1_v"], p["cmp_w2_v"],
        dkc, dvc)
    gb["cmp_w1_k"], gb["cmp_w1_v"] = gw1k.astype(BF16), gw1v.astype(BF16)
    dkvs = [t.reshape(T, HEAD_DIM) for t in (dk2, dv2, dks, dvs, dkw, dvw)]
    dz = assemble_dz(du, dv, da.reshape(T, GW), dgt.reshape(T, GW), dq_c.reshape(T, GW), dq_s.reshape(T, GW),
                     dq_w.reshape(T, GW), dd.reshape(T, GW), dkvs, dgl.reshape(T, 128))
    dh1 = mm(dz, p["w_in"], tb=True, name="mm_in_dx", tk=2048)
    gb["w_in"] = mm(sv["h1t"], dz, out_dtype=BF16, name="mm_in_dw", tk=4096, tn=512)
    dx0, gs["g_pre_mix"] = rms_bwd(sv["x0"], p["g_pre_mix"], dh1, dx1, F32, "rms_pre_mix_bwd")
    return dx0, gb, gs


SMALL = ["g_pre_mix", "g_post_mix", "g_pre_ffn", "g_post_ffn", "sg_ln_g", "sg_w", "sg_b", "cv_w", "cv_b", "cv_ln_g", "cv_ln_b",
         "cv_pw_b", "cmp_pos_k", "cmp_pos_v", "cmp_w2_k", "cmp_w2_v", "pool_w", "pool_scale"]
BIG = ["w_in", "w_out", "w_gu", "w_down", "cmp_w1_k", "cmp_w1_v", "cv_pw"]
NAMES = ["g_pre_mix", "g_post_mix", "g_pre_ffn", "g_post_ffn", "w_in", "sg_ln_g", "sg_w", "sg_b", "cv_w", "cv_b", "cv_ln_g",
         "cv_ln_b", "cv_pw", "cv_pw_b", "cmp_pos_k", "cmp_pos_v", "cmp_w1_k", "cmp_w2_k", "cmp_w1_v", "cmp_w2_v", "pool_w",
         "pool_scale", "w_out", "ffn_w_gu", "ffn_w_down"]


def kernel(x, g_pre_mix, g_post_mix, g_pre_ffn, g_post_ffn, w_in, sg_ln_g, sg_w, sg_b, cv_w, cv_b, cv_ln_g, cv_ln_b, cv_pw, cv_pw_b, cmp_pos_k, cmp_pos_v, cmp_w1_k, cmp_w2_k, cmp_w1_v, cmp_w2_v, pool_w, pool_scale, w_out, ffn_w_gu, ffn_w_down, loss_target, m_g_pre_mix, m_g_post_mix, m_g_pre_ffn, m_g_post_ffn, m_w_in, m_sg_ln_g, m_sg_w, m_sg_b, m_cv_w, m_cv_b, m_cv_ln_g, m_cv_ln_b, m_cv_pw, m_cv_pw_b, m_cmp_pos_k, m_cmp_pos_v, m_cmp_w1_k, m_cmp_w2_k, m_cmp_w1_v, m_cmp_w2_v, m_pool_w, m_pool_scale, m_w_out, m_ffn_w_gu, m_ffn_w_down, v_g_pre_mix, v_g_post_mix, v_g_pre_ffn, v_g_post_ffn, v_w_in, v_sg_ln_g, v_sg_w, v_sg_b, v_cv_w, v_cv_b, v_cv_ln_g, v_cv_ln_b, v_cv_pw, v_cv_pw_b, v_cmp_pos_k, v_cmp_pos_v, v_cmp_w1_k, v_cmp_w2_k, v_cmp_w1_v, v_cmp_w2_v, v_pool_w, v_pool_scale, v_w_out, v_ffn_w_gu, v_ffn_w_down):
    args = dict(locals())
    W = {n: args[n] for n in NAMES}
    M = {n: args["m_" + n] for n in NAMES}
    V = {n: args["v_" + n] for n in NAMES}
    B, S, _ = x.shape
    T = B * S
    L = w_in.shape[0]
    me = _index(_my_pos())
    cpd = GW // N_DEV

    shards = {"w_in": lambda l: pack_cols(w_in[l]).astype(BF16), "w_out": lambda l: w_out[l].astype(BF16),
              "w_gu": lambda l: ffn_w_gu[l].astype(BF16), "w_down": lambda l: ffn_w_down[l].astype(BF16),
              "cmp_w1_k": lambda l: cmp_w1_k[l].astype(BF16), "cmp_w1_v": lambda l: cmp_w1_v[l].astype(BF16),
              "cv_pw": lambda l: cv_pw[l].astype(BF16), "cv_w": lambda l: cv_w[l].T}
    early, later = ["w_in", "cmp_w1_k", "cmp_w1_v", "cv_pw", "cv_w"], ["w_out", "w_gu", "w_down"]

    def start_gather(names, l, tag, srcs=None, behind=None):
        srcs = list(srcs) if srcs is not None else [shards[n](l)[None] for n in names]
        if behind is not None:
            srcs[0] = srcs[0] + behind[0, 0].astype(srcs[0].dtype)
        return exchange_start(["gather"] * len(names), srcs, "gather_%s_start_%d" % (tag, l))

    def wait_gather(names, started, after, l, tag):
        arrived, done = exchange_wait(["gather"] * len(names), started, after, "gather_%s_wait_%d" % (tag, l))
        full = {n: a[0] for n, a in zip(names, arrived)}
        full["done"] = done
        if "w_gu" in full:
            full["w_gu"] = full["w_gu"].reshape(N_DEV, D_MODEL, FFN_BLK)
            full["w_down"] = full["w_down"].reshape(4, FFN_BLK, D_MODEL)
        if "cv_w" in full:
            full["cv_w"] = full["cv_w"].T
        return full

    def layer_params(l, full):
        p = dict(full)
        for n in ("g_pre_mix", "g_post_mix", "g_pre_ffn", "g_post_ffn", "sg_ln_g", "cv_b", "cv_ln_g", "cv_ln_b", "cv_pw_b",
                  "pool_scale"):
            p[n] = W[n][l][None, :]
        p["sg_w"] = sg_w[l]
        p["bexp"] = _bexp(sg_b[l])
        p["cmp_pos_k"] = cmp_pos_k[l].reshape(1, 2048)
        p["cmp_pos_v"] = cmp_pos_v[l].reshape(1, 2048)
        p["cmp_w2_k"], p["cmp_w2_v"] = cmp_w2_k[l], cmp_w2_v[l]
        p["pool_bd"] = _block_diag(pool_w[l])
        return p

    xs = x.reshape(T, D_MODEL)
    params, saved = [], []
    early_st = start_gather(early, 0, "early")
    ahead = {}
    for l in range(L):
        later_srcs = [shards[n](l)[None] for n in later]
        if l == 0:
            later_srcs = [s + early_st[4][0, 0].astype(s.dtype) for s in later_srcs]
        full = wait_gather(early, early_st, later_srcs + ([xs] if l > 0 else []), l, "early")
        later_st = start_gather(later, l, "later", srcs=later_srcs, behind=full["done"])
        p = layer_params(l, full)
        p["g_pre_mix"] = p["g_pre_mix"] + later_st[4][0, 0]

        def late(after, l=l, st=later_st, p=p):
            got = wait_gather(later, st, after, l, "later")
            if l + 1 < L:
                ahead["early"] = start_gather(early, l + 1, "early", behind=got["done"])
                got["g_post_mix"] = p["g_post_mix"] + ahead["early"][4][0, 0]
            return got

        xs, sv = layer_fwd(xs, p, late, B, S)
        params.append(p)
        saved.append(sv)
        early_st = ahead.get("early")
    dy, lpart = loss_fwd_bwd(xs, loss_target.reshape(T, D_MODEL))
    loss = lax.psum(lpart[0, 0], ("x", "y", "c"))

    ffn_big, mix_big = ["w_gu", "w_down"], ["w_in", "w_out", "cmp_w1_k", "cmp_w1_v", "cv_pw"]
    ffn_kinds, mix_kinds = ["scatter"] * len(ffn_big), ["scatter"] * len(mix_big) + ["gather"]
    pending, token = [], None
    for l in reversed(range(L)):
        p = dict(params[l])
        if token is not None:
            p["g_post_ffn"] = p["g_post_ffn"] + token[0, 0]
        dy, gb_ffn, gs = layer_bwd_ffn(dy, p, saved[l], B, S)
        st_ffn = exchange_start(ffn_kinds, [gb_ffn[n] for n in ffn_big], "scatter_ffn_start_%d" % l)
        p["g_post_mix"] = p["g_post_mix"] + st_ffn[4][0, 0]
        dy, gb_mix, gs_mix = layer_bwd_mix(dy, p, saved[l], B, S)
        gs.update(gs_mix)
        small_shapes = [tuple(gs[n].shape) for n in SMALL]
        st_mix = exchange_start(mix_kinds, [gb_mix[n][None] for n in mix_big] + [pack_flat([gs[n] for n in SMALL])[None]],
                                "scatter_mix_start_%d" % l)
        token = st_mix[4]
        pending.append((l, st_ffn, st_mix))
    grad_x = dy.reshape(B, S, D_MODEL)

    delta, new_m, new_v = {}, {}, {}
    lands = {l: {} for l in range(L)}
    for l, st_ffn, _ in pending:
        lands[l].update(zip(ffn_big, exchange_wait(ffn_kinds, st_ffn, token, "scatter_ffn_wait_%d" % l)[0]))
    grads = {}
    for n, name in zip(ffn_big, ("ffn_w_gu", "ffn_w_down")):
        grads[name] = sum_slots([lands[l][n] for l in range(L)], "sum_" + n)
        delta[name], new_m[name], new_v[name] = adamw(W[name], grads[name], M[name], V[name], "adamw_" + name)
    for l, _, st_mix in pending:
        lands[l].update(zip(mix_big + ["small"],
                            exchange_wait(mix_kinds, st_mix, delta["ffn_w_down"], "scatter_mix_wait_%d" % l)[0]))
    for n in mix_big:
        grads[n] = sum_slots([lands[l][n] for l in range(L)], "sum_" + n)
    grads["w_in"] = unpack_cols(grads["w_in"])
    rows = lands[0]["small"].shape[1] // N_DEV
    reduced = sum_slots([lands[l]["small"].reshape(N_DEV, 1, rows, 128) for l in range(L)], "sum_small")
    per_layer = [unpack_flat(reduced[l], small_shapes) for l in range(L)]
    for i, n in enumerate(SMALL):
        g = jnp.stack([per_layer[l][i] for l in range(L)])
        grads[n] = g.reshape(W[n].shape) if n != "cv_w" else g
    grads["cv_w"] = lax.dynamic_slice(grads["cv_w"], (0, 0, me * cpd), (L, CONV_WIDTH, cpd))

    for n in mix_big:
        delta[n], new_m[n], new_v[n] = adamw(W[n], grads[n], M[n], V[n], "adamw_" + n)
    shapes = [W[n].shape for n in SMALL]
    packed = adamw(pack_flat([W[n] for n in SMALL]), pack_flat([grads[n] for n in SMALL]),
                   pack_flat([M[n] for n in SMALL]), pack_flat([V[n] for n in SMALL]), "adamw_small")
    for out, flat in zip((delta, new_m, new_v), packed):
        for n, a in zip(SMALL, unpack_flat(flat, shapes)):
            out[n] = a

    return (loss, grad_x, *[grads[n] for n in NAMES], *[delta[n] for n in NAMES], *[new_m[n] for n in NAMES],
            *[new_v[n] for n in NAMES])
```
